```python
import jax, jax.numpy as jnp
from jax import lax
import numpy as np

D_MODEL = 1024
BATCH = 16
SEQ = 4096
DEPTH = 2

N_EVEN = (DEPTH + 1) // 2
N_ODD = DEPTH // 2

HGRN_KDIM = 128
HGRN_WIDTH = D_MODEL // 2
HGRN_HEADS = HGRN_WIDTH // HGRN_KDIM
HGRN_VDIM = HGRN_WIDTH // HGRN_HEADS
HGRN_CHUNK = 64

FOX_HDIM = 64
FOX_WIDTH = D_MODEL // 2
FOX_HEADS = FOX_WIDTH // FOX_HDIM
FOX_BLOCK = 128

MIX_WIDTH = HGRN_WIDTH + FOX_WIDTH
AB_IN = 4 * HGRN_WIDTH + 4 * FOX_WIDTH + FOX_HEADS

RWKV_HDIM = 64
RWKV_HEADS = D_MODEL // RWKV_HDIM
DECAY_LORA = 64
AAA_LORA = 64
GATE_LORA = 128

D_FF = 4 * D_MODEL

RMS_EPS = 1e-6
GN_EPS = 64e-5

kernel_name = "hgrn2_fox_rwkv7_hybrid"


def rmsnorm(x, g, eps=RMS_EPS):
    xf = x.astype(jnp.float32)
    y = xf * lax.rsqrt(jnp.mean(xf * xf, axis=-1, keepdims=True) + eps)
    return (y * g.astype(jnp.float32)).astype(x.dtype)


def hgrn2_chunkwise(q, k, v, log_f):
    B, S, H, DK = q.shape
    DV = v.shape[-1]
    C = HGRN_CHUNK
    NC = S // C

    def to_chunks(t):
        return t.astype(jnp.float32).reshape(B, NC, C, H, t.shape[-1]).transpose(1, 0, 3, 2, 4)

    qc, kc, vc, gc = to_chunks(q), to_chunks(k), to_chunks(v), to_chunks(log_f)
    causal = jnp.tril(jnp.ones((C, C), dtype=bool))[:, :, None]

    def step(state, inp):
        qb, kb, vb, gb = inp
        b = jnp.cumsum(gb, axis=2)
        diff = b[:, :, :, None, :] - b[:, :, None, :, :]
        decay = jnp.exp(jnp.where(causal, diff, -jnp.inf))
        scores = jnp.einsum('bhtd,bhtsd,bhsd->bhts', qb, decay, kb)
        o = (jnp.einsum('bhts,bhsv->bhtv', scores, vb)
             + jnp.einsum('bhtd,bhdv->bhtv', qb * jnp.exp(b), state))
        b_last = b[:, :, -1:, :]
        state = (state * jnp.exp(b_last[:, :, 0, :])[..., None]
                 + jnp.einsum('bhsd,bhsv->bhdv', kb * jnp.exp(b_last - b), vb))
        return state, o

    s0 = jnp.zeros((B, H, DK, DV), jnp.float32)
    _, o = lax.scan(step, s0, (qc, kc, vc, gc))
    return o.transpose(1, 0, 3, 2, 4).reshape(B, S, H, DV)


def fox_attention(q, k, v, log_f):
    B, S, H, Dh = q.shape
    scale = Dh ** -0.5
    c = jnp.cumsum(log_f.astype(jnp.float32), axis=1).transpose(0, 2, 1)
    qh = q.transpose(0, 2, 1, 3)
    kh = k.transpose(0, 2, 1, 3)
    vh = v.transpose(0, 2, 1, 3)
    q_idx = jnp.arange(FOX_BLOCK)
    outs = []
    for i in range(S // FOX_BLOCK):
        q0 = i * FOX_BLOCK
        kv_len = q0 + FOX_BLOCK
        qb = qh[:, :, q0:kv_len]
        kb = kh[:, :, :kv_len]
        vb = vh[:, :, :kv_len]
        logits = (jnp.einsum('bhqd,bhkd->bhqk', qb, kb).astype(jnp.float32) * scale
                  + c[:, :, q0:kv_len, None] - c[:, :, None, :kv_len])
        mask = (q0 + q_idx)[:, None] >= jnp.arange(kv_len)[None, :]
        p = jax.nn.softmax(jnp.where(mask, logits, -jnp.inf), axis=-1)
        outs.append(jnp.einsum('bhqk,bhkd->bhqd', p.astype(vb.dtype), vb))
    return jnp.concatenate(outs, axis=2).transpose(0, 2, 1, 3)


def hgrn_fox_mixer(h, w_in, lb, hgrn_norm_g, fox_fb, fox_q_g, fox_k_g, w_out):
    B, S, _ = h.shape
    proj = h @ w_in
    sizes = [HGRN_WIDTH] * 4 + [FOX_WIDTH] * 4
    splits = np.cumsum(sizes).tolist()
    a_q, a_f, a_i, a_g, b_q, b_k, b_v, b_g, b_f = jnp.split(proj, splits, axis=-1)

    lbf = lb.astype(jnp.float32)
    f = lbf + (1.0 - lbf) * jax.nn.sigmoid(a_f.astype(jnp.float32))
    hd = (B, S, HGRN_HEADS, HGRN_KDIM)
    o_a = hgrn2_chunkwise(jax.nn.silu(a_q).reshape(hd), (1.0 - f).reshape(hd),
                          a_i.reshape(B, S, HGRN_HEADS, HGRN_VDIM), jnp.log(f).reshape(hd))
    o_a = rmsnorm(o_a, hgrn_norm_g.reshape(HGRN_HEADS, HGRN_VDIM))
    o_a = o_a * jax.nn.silu(a_g.astype(jnp.float32)).reshape(B, S, HGRN_HEADS, HGRN_VDIM)

    fd = (B, S, FOX_HEADS, FOX_HDIM)
    q = rmsnorm(b_q.reshape(fd), fox_q_g)
    k = rmsnorm(b_k.reshape(fd), fox_k_g)
    log_fg = jax.nn.log_sigmoid((b_f + fox_fb).astype(jnp.float32))
    o_b = fox_attention(q, k, b_v.reshape(fd), log_fg)
    o_b = o_b * jax.nn.sigmoid(b_g).reshape(fd)

    y = jnp.concatenate([o_a.reshape(B, S, HGRN_WIDTH).astype(h.dtype),
                         o_b.reshape(B, S, FOX_WIDTH).astype(h.dtype)], axis=-1)
    return y @ w_out


def wkv7_scan(r, w, k, v, a, b):
    B, S, H, N = r.shape

    def step(state, inp):
        r_t, w_t, k_t, v_t, a_t, b_t = inp
        sa = jnp.einsum('bhvk,bhk->bhv', state, a_t)
        state = (state * w_t[:, :, None, :] + sa[..., None] * b_t[:, :, None, :]
                 + v_t[..., None] * k_t[:, :, None, :])
        return state, jnp.einsum('bhvk,bhk->bhv', state, r_t)

    xs = tuple(t.astype(jnp.float32).transpose(1, 0, 2, 3) for t in (r, w, k, v, a, b))
    s0 = jnp.zeros((B, H, N, N), jnp.float32)
    _, y = lax.scan(step, s0, xs)
    return y.transpose(1, 0, 2, 3)


def rwkv7_mixer(h, mu, w_rkv, w0, w1, w2, a0, a1, a2, g1, g2, k_k, k_a, r_k, lnx_g, lnx_b, w_o):
    B, S, D = h.shape
    H, N = RWKV_HEADS, RWKV_HDIM
    xx = jnp.pad(h, ((0, 0), (1, 0), (0, 0)))[:, :-1] - h
    mixed = h[:, :, None, :] + xx[:, :, None, :] * mu
    xr, xw, xk, xv, xa, xg = [mixed[:, :, i] for i in range(6)]

    r = xr @ w_rkv[0]
    k = xk @ w_rkv[1]
    v = xv @ w_rkv[2]
    w_log = -jax.nn.softplus(-(w0 + jnp.tanh(xw @ w1) @ w2).astype(jnp.float32)) - 0.5
    decay = jnp.exp(-jnp.exp(w_log))
    a = jax.nn.sigmoid((a0 + (xa @ a1) @ a2).astype(jnp.float32))
    g = jax.nn.sigmoid(xg @ g1) @ g2

    hs = (B, S, H, N)
    kk = (k * k_k).astype(jnp.float32).reshape(hs)
    kk = kk * lax.rsqrt(jnp.maximum(jnp.sum(kk * kk, axis=-1, keepdims=True), 1e-24))
    k = k.astype(jnp.float32) * (1.0 + (a - 1.0) * k_a.astype(jnp.float32))
    r4, k4, v4 = r.astype(jnp.float32).reshape(hs), k.reshape(hs), v.astype(jnp.float32).reshape(hs)
    a4 = a.reshape(hs)

    y = wkv7_scan(r4, decay.reshape(hs), k4, v4, -kk, kk * a4)
    mean = jnp.mean(y, axis=-1, keepdims=True)
    var = jnp.mean(jnp.square(y - mean), axis=-1, keepdims=True)
    y = ((y - mean) * lax.rsqrt(var + GN_EPS)).reshape(B, S, D) * lnx_g + lnx_b
    bonus = jnp.sum(r4 * k4 * r_k.astype(jnp.float32), axis=-1, keepdims=True) * v4
    y = y + bonus.reshape(B, S, D)
    return (y * g).astype(h.dtype) @ w_o


def sqrelu_mlp(h, w_up, w_down):
    return jnp.square(jax.nn.relu(h @ w_up)) @ w_down


def setup_inputs(seed: int = 0) -> dict:
    key = jax.random.key(seed)
    ks = iter(jax.random.split(key, 40))
    D = D_MODEL

    def nrm(shape, scale):
        return jax.random.normal(next(ks), shape, jnp.float32) * scale

    def unif(shape, lo, hi):
        return jax.random.uniform(next(ks), shape, jnp.float32, lo, hi)

    return {
        "x": nrm((BATCH, SEQ, D), 1.0),
        "norm_mix_g": 1.0 + nrm((DEPTH, D), 0.02),
        "norm_ffn_g": 1.0 + nrm((DEPTH, D), 0.02),
        "ab_w_in": nrm((N_EVEN, D, AB_IN), D ** -0.5),
        "hgrn_lower_bounds": nrm((DEPTH + 1, HGRN_WIDTH), 0.1),
        "hgrn_norm_g": 1.0 + nrm((N_EVEN, HGRN_WIDTH), 0.02),
        "fox_forget_bias": 2.0 + nrm((N_EVEN, FOX_HEADS), 0.1),
        "fox_q_norm_g": 1.0 + nrm((N_EVEN, FOX_HDIM), 0.02),
        "fox_k_norm_g": 1.0 + nrm((N_EVEN, FOX_HDIM), 0.02),
        "ab_w_out": nrm((N_EVEN, MIX_WIDTH, D), MIX_WIDTH ** -0.5),
        "rwkv_mu": unif((N_ODD, 6, D), 0.0, 1.0),
        "rwkv_w_rkv": nrm((N_ODD, 3, D, D), D ** -0.5),
        "rwkv_w0": nrm((N_ODD, D), 0.5),
        "rwkv_w1": nrm((N_ODD, D, DECAY_LORA), D ** -0.5),
        "rwkv_w2": nrm((N_ODD, DECAY_LORA, D), 0.1 * DECAY_LORA ** -0.5),
        "rwkv_a0": nrm((N_ODD, D), 0.1),
        "rwkv_a1": nrm((N_ODD, D, AAA_LORA), D ** -0.5),
        "rwkv_a2": nrm((N_ODD, AAA_LORA, D), 0.1 * AAA_LORA ** -0.5),
        "rwkv_g1": nrm((N_ODD, D, GATE_LORA), D ** -0.5),
        "rwkv_g2": nrm((N_ODD, GATE_LORA, D), GATE_LORA ** -0.5),
        "rwkv_k_k": 0.85 + nrm((N_ODD, D), 0.02),
        "rwkv_k_a": 1.0 + nrm((N_ODD, D), 0.02),
        "rwkv_r_k": nrm((N_ODD, RWKV_HEADS, RWKV_HDIM), 0.1),
        "rwkv_lnx_g": 1.0 + nrm((N_ODD, D), 0.02),
        "rwkv_lnx_b": nrm((N_ODD, D), 0.02),
        "rwkv_w_o": nrm((N_ODD, D, D), D ** -0.5),
        "mlp_w_up": nrm((DEPTH, D, D_FF), D ** -0.5),
        "mlp_w_down": nrm((DEPTH, D_FF, D), D_FF ** -0.5),
    }


def reference(x, norm_mix_g, norm_ffn_g, ab_w_in, hgrn_lower_bounds, hgrn_norm_g,
              fox_forget_bias, fox_q_norm_g, fox_k_norm_g, ab_w_out, rwkv_mu, rwkv_w_rkv,
              rwkv_w0, rwkv_w1, rwkv_w2, rwkv_a0, rwkv_a1, rwkv_a2, rwkv_g1, rwkv_g2,
              rwkv_k_k, rwkv_k_a, rwkv_r_k, rwkv_lnx_g, rwkv_lnx_b, rwkv_w_o,
              mlp_w_up, mlp_w_down):
    lb_all = jnp.cumsum(jax.nn.softmax(hgrn_lower_bounds.astype(jnp.float32), axis=0), axis=0)
    h = x
    for layer in range(DEPTH):
        j = layer // 2
        hn = rmsnorm(h, norm_mix_g[layer])
        if layer % 2 == 0:
            mix = hgrn_fox_mixer(hn, ab_w_in[j], lb_all[layer], hgrn_norm_g[j],
                                 fox_forget_bias[j], fox_q_norm_g[j], fox_k_norm_g[j],
                                 ab_w_out[j])
        else:
            mix = rwkv7_mixer(hn, rwkv_mu[j], rwkv_w_rkv[j], rwkv_w0[j], rwkv_w1[j],
                              rwkv_w2[j], rwkv_a0[j], rwkv_a1[j], rwkv_a2[j], rwkv_g1[j],
                              rwkv_g2[j], rwkv_k_k[j], rwkv_k_a[j], rwkv_r_k[j],
                              rwkv_lnx_g[j], rwkv_lnx_b[j], rwkv_w_o[j])
        h = h + mix
        h = h + sqrelu_mlp(rmsnorm(h, norm_ffn_g[layer]), mlp_w_up[layer], mlp_w_down[layer])
    return h
```

```python
import functools

import jax
import jax.numpy as jnp
from jax import lax
from jax.experimental import pallas as pl
from jax.experimental.pallas import tpu as pltpu

F32 = jnp.float32
BF16 = jnp.bfloat16

RMS_EPS = 1e-6
GN_EPS = 64e-5

HGRN_HEADS = 4
HGRN_DIM = 128
FOX_HEADS = 8
FOX_DIM = 64
RWKV_DIM = 64
RWKV_GROUP = 4
GROUP_LANES = RWKV_GROUP * RWKV_DIM
LANES = 128
NEG_BIG = -1e30

NT_DIMS = (((1,), (1,)), ((), ()))


def _cparams(sem, vmem_mb):
    return pltpu.CompilerParams(dimension_semantics=sem, vmem_limit_bytes=vmem_mb * 1024 * 1024)


def _dot(a, b):
    return jnp.dot(a, b, preferred_element_type=F32)


def _dot_nt(a, b):
    return lax.dot_general(a, b, NT_DIMS, preferred_element_type=F32)


def _rms(x, g):
    return x * lax.rsqrt(jnp.mean(x * x, axis=-1, keepdims=True) + RMS_EPS) * g


def _sigmoid(x):
    return 1.0 / (1.0 + jnp.exp(-x))


def _log_sigmoid(x):
    return jnp.minimum(x, 0.0) - jnp.log(1.0 + jnp.exp(-jnp.abs(x)))


def _tril_mask(n, strict=False):
    r = lax.broadcasted_iota(jnp.int32, (n, n), 0)
    c = lax.broadcasted_iota(jnp.int32, (n, n), 1)
    return (c < r) if strict else (c <= r)


def _split3(x):
    hi = x.astype(BF16)
    r1 = x - hi.astype(F32)
    mid = r1.astype(BF16)
    lo = (r1 - mid.astype(F32)).astype(BF16)
    return hi, mid, lo


def _cumsum_rows(x, tril_bf16):
    hi, mid, lo = _split3(x)
    return _dot(tril_bf16, hi) + _dot(tril_bf16, mid) + _dot(tril_bf16, lo)


def _inproj_kernel(x_ref, g_ref, w_ref, o_ref):
    hn = _rms(x_ref[...], g_ref[...]).astype(BF16)
    o_ref[...] = _dot(hn, w_ref[...])


def _inproj(x2, g, w, tm):
    m, d = x2.shape
    n = w.shape[1]
    return pl.pallas_call(
        _inproj_kernel,
        grid=(m // tm,),
        in_specs=[
            pl.BlockSpec((tm, d), lambda i: (i, 0)),
            pl.BlockSpec((1, d), lambda i: (0, 0)),
            pl.BlockSpec((d, n), lambda i: (0, 0)),
        ],
        out_specs=pl.BlockSpec((tm, n), lambda i: (i, 0)),
        out_shape=jax.ShapeDtypeStruct((m, n), F32),
        compiler_params=_cparams(("parallel",), 48),
        name="inproj",
    )(x2, g, w)


def _hgrn_kernel(q_ref, f_ref, i_ref, g_ref, lb_ref, ng_ref, o_ref, st_ref, *, chunk):
    @pl.when(pl.program_id(1) == 0)
    def _():
        st_ref[...] = jnp.zeros_like(st_ref)

    ts = q_ref.shape[0]
    causal = _tril_mask(chunk)
    tril_b = jnp.where(causal, 1.0, 0.0).astype(BF16)
    mid = chunk // 2

    def body(c, carry):
        rows = pl.ds(pl.multiple_of(c * chunk, chunk), chunk)
        for h in range(HGRN_HEADS):
            cols = slice(h * HGRN_DIM, (h + 1) * HGRN_DIM)
            aq = q_ref[rows, cols]
            af = f_ref[rows, cols]
            v = i_ref[rows, cols]
            ag = g_ref[rows, cols]
            lb = lb_ref[:, cols]
            f = lb + (1.0 - lb) * _sigmoid(af)
            b = _cumsum_rows(jnp.log(f), tril_b)
            b_mid = b[mid - 1:mid, :]
            b_last = b[chunk - 1:chunk, :]
            q = aq * _sigmoid(aq)
            k = 1.0 - f
            s = _dot_nt((q * jnp.exp(b - b_mid)).astype(BF16), (k * jnp.exp(b_mid - b)).astype(BF16))
            s = jnp.where(causal, s, 0.0)
            st = st_ref[h]
            vb = v.astype(BF16)
            o = _dot(s.astype(BF16), vb) + _dot_nt((q * jnp.exp(b)).astype(BF16), st.astype(BF16))
            k_st = (k * jnp.exp(b_last - b)).astype(BF16)
            st_ref[h] = st * jnp.exp(b_last) + _dot(v.T.astype(BF16), k_st)
            on = _rms(o, ng_ref[:, cols])
            o_ref[rows, cols] = (on * (ag * _sigmoid(ag))).astype(o_ref.dtype)
        return carry

    lax.fori_loop(0, ts // chunk, body, 0)


def _hgrn(proj, lb, ng, batch, seq, ts, chunk):
    m = proj.shape[0]
    w = HGRN_HEADS * HGRN_DIM
    nt = seq // ts
    spec = lambda j: pl.BlockSpec((ts, w), lambda b, t, j=j: (b * nt + t, j))
    vec = pl.BlockSpec((1, w), lambda b, t: (0, 0))
    return pl.pallas_call(
        functools.partial(_hgrn_kernel, chunk=chunk),
        grid=(batch, nt),
        in_specs=[spec(0), spec(1), spec(2), spec(3), vec, vec],
        out_specs=pl.BlockSpec((ts, w), lambda b, t: (b * nt + t, 0)),
        out_shape=jax.ShapeDtypeStruct((m, w), BF16),
        scratch_shapes=[pltpu.VMEM((HGRN_HEADS, HGRN_DIM, HGRN_DIM), F32)],
        compiler_params=_cparams(("parallel", "arbitrary"), 32),
        name="hgrn2",
    )(proj, proj, proj, proj, lb, ng)


def _foxprep_kernel(q_ref, k_ref, v_ref, f_ref, fb_ref, qg_ref, kg_ref,
                    qa_ref, ka_ref, vo_ref, carry_ref):
    @pl.when(pl.program_id(1) == 0)
    def _():
        carry_ref[...] = jnp.zeros_like(carry_ref)

    ts = q_ref.shape[0]
    tril_b = jnp.where(_tril_mask(ts), 1.0, 0.0).astype(BF16)
    lf = _log_sigmoid(f_ref[...] + fb_ref[...])
    c = _cumsum_rows(lf, tril_b) + carry_ref[...]
    carry_ref[...] = c[ts - 1:ts, :]
    pieces = [p.astype(F32) for p in _split3(c)]
    lane = lax.broadcasted_iota(jnp.int32, (ts, LANES), 1)
    ones_q = jnp.where((lane >= FOX_DIM + 3) & (lane < FOX_DIM + 6), 1.0, 0.0)
    ones_k = jnp.where((lane >= FOX_DIM) & (lane < FOX_DIM + 3), 1.0, 0.0)
    zpad = jnp.zeros((ts, LANES - FOX_DIM), F32)
    scale = FOX_DIM ** -0.5
    for h in range(FOX_HEADS):
        cols = slice(h * FOX_DIM, (h + 1) * FOX_DIM)
        qn = _rms(q_ref[:, cols], qg_ref[...]) * scale
        kn = _rms(k_ref[:, cols], kg_ref[...])
        qa = jnp.concatenate([qn, zpad], axis=1) + ones_q
        ka = jnp.concatenate([kn, zpad], axis=1) + ones_k
        for p, piece in enumerate(pieces):
            col = jnp.broadcast_to(piece[:, h:h + 1], (ts, LANES))
            qa = qa + jnp.where(lane == FOX_DIM + p, col, 0.0)
            ka = ka - jnp.where(lane == FOX_DIM + 3 + p, col, 0.0)
        out_cols = slice(h * LANES, (h + 1) * LANES)
        qa_ref[:, out_cols] = qa.astype(BF16)
        ka_ref[:, out_cols] = ka.astype(BF16)
    vo_ref[...] = v_ref[...].astype(BF16)


def _foxprep(proj, fb, qg, kg, batch, seq, ts):
    m = proj.shape[0]
    w = FOX_HEADS * FOX_DIM
    nt = seq // ts
    spec = lambda j: pl.BlockSpec((ts, w), lambda b, t, j=j: (b * nt + t, j))
    fcol = (8 * w) // LANES
    return pl.pallas_call(
        _foxprep_kernel,
        grid=(batch, nt),
        in_specs=[spec(4), spec(5), spec(6),
                  pl.BlockSpec((ts, LANES), lambda b, t: (b * nt + t, fcol)),
                  pl.BlockSpec((1, LANES), lambda b, t: (0, 0)),
                  pl.BlockSpec((1, FOX_DIM), lambda b, t: (0, 0)),
                  pl.BlockSpec((1, FOX_DIM), lambda b, t: (0, 0))],
        out_specs=[pl.BlockSpec((ts, FOX_HEADS * LANES), lambda b, t: (b * nt + t, 0)),
                   pl.BlockSpec((ts, FOX_HEADS * LANES), lambda b, t: (b * nt + t, 0)),
                   pl.BlockSpec((ts, w), lambda b, t: (b * nt + t, 0))],
        out_shape=[jax.ShapeDtypeStruct((m, FOX_HEADS * LANES), BF16),
                   jax.ShapeDtypeStruct((m, FOX_HEADS * LANES), BF16),
                   jax.ShapeDtypeStruct((m, w), BF16)],
        scratch_shapes=[pltpu.VMEM((1, LANES), F32)],
        compiler_params=_cparams(("parallel", "arbitrary"), 32),
        name="foxprep",
    )(proj, proj, proj, proj, fb, qg, kg)


def _fox_kernel(q_ref, k_ref, v_ref, g_ref, o_ref, *, tq):
    i = pl.program_id(2)
    lane = lax.broadcasted_iota(jnp.int32, (tq, LANES), 1)
    first = lane < FOX_DIM
    causal = _tril_mask(tq)
    qs = (q_ref[:, :LANES], q_ref[:, LANES:])

    def step(j, carry, masked):
        m0, l0, m1, l1, acc = carry
        rows = pl.ds(pl.multiple_of(j * tq, tq), tq)
        vb = v_ref[rows, :]
        new = []
        for h, (m, l) in enumerate(((m0, l0), (m1, l1))):
            s = _dot_nt(qs[h], k_ref[rows, h * LANES:(h + 1) * LANES])
            if masked:
                s = jnp.where(causal, s, NEG_BIG)
            m_new = jnp.maximum(m, jnp.max(s, axis=-1, keepdims=True))
            alpha = jnp.exp(m - m_new)
            p = jnp.exp(s - m_new)
            l_new = alpha * l + jnp.sum(p, axis=-1, keepdims=True)
            new.append((m_new, l_new, alpha, _dot(p.astype(BF16), vb)))
        (m0, l0, a0, pv0), (m1, l1, a1, pv1) = new
        acc = acc * jnp.where(first, a0, a1) + jnp.where(first, pv0, pv1)
        return m0, l0, m1, l1, acc

    neg = jnp.full((tq, 1), NEG_BIG, F32)
    zero = jnp.zeros((tq, 1), F32)
    init = (neg, zero, neg, zero, jnp.zeros((tq, LANES), F32))
    carry = lax.fori_loop(0, i, lambda j, c: step(j, c, False), init)
    _, l0, _, l1, acc = step(i, carry, True)
    out = acc / jnp.where(first, l0, l1) * _sigmoid(g_ref[...])
    o_ref[...] = out.astype(o_ref.dtype)


def _fox(qa, ka, vb, proj, batch, seq, tq):
    m = qa.shape[0]
    nq = seq // tq
    pairs = FOX_HEADS // 2
    gcol = (7 * FOX_HEADS * FOX_DIM) // LANES
    return pl.pallas_call(
        functools.partial(_fox_kernel, tq=tq),
        grid=(batch, pairs, nq),
        in_specs=[pl.BlockSpec((tq, 2 * LANES), lambda b, p, i: (b * nq + i, p)),
                  pl.BlockSpec((seq, 2 * LANES), lambda b, p, i: (b, p)),
                  pl.BlockSpec((seq, LANES), lambda b, p, i: (b, p)),
                  pl.BlockSpec((tq, LANES), lambda b, p, i: (b * nq + i, gcol + p))],
        out_specs=pl.BlockSpec((tq, LANES), lambda b, p, i: (b * nq + i, p)),
        out_shape=jax.ShapeDtypeStruct((m, FOX_HEADS * FOX_DIM), BF16),
        compiler_params=_cparams(("parallel", "parallel", "arbitrary"), 32),
        name="fox_attention",
    )(qa, ka, vb, proj)


def _outproj2_kernel(ya_ref, yb_ref, wa_ref, wb_ref, h_ref, o_ref):
    o_ref[...] = h_ref[...] + _dot(ya_ref[...], wa_ref[...]) + _dot(yb_ref[...], wb_ref[...])


def _outproj2(ya, yb, wa, wb, h, tm):
    m, d = h.shape
    ka, kb = ya.shape[1], yb.shape[1]
    return pl.pallas_call(
        _outproj2_kernel,
        grid=(m // tm,),
        in_specs=[pl.BlockSpec((tm, ka), lambda i: (i, 0)),
                  pl.BlockSpec((tm, kb), lambda i: (i, 0)),
                  pl.BlockSpec((ka, d), lambda i: (0, 0)),
                  pl.BlockSpec((kb, d), lambda i: (0, 0)),
                  pl.BlockSpec((tm, d), lambda i: (i, 0))],
        out_specs=pl.BlockSpec((tm, d), lambda i: (i, 0)),
        out_shape=jax.ShapeDtypeStruct((m, d), F32),
        compiler_params=_cparams(("parallel",), 32),
        name="outproj_mix",
    )(ya, yb, wa, wb, h)


def _outproj1_kernel(y_ref, w_ref, h_ref, o_ref):
    o_ref[...] = h_ref[...] + _dot(y_ref[...], w_ref[...])


def _outproj1(y, w, h, tm):
    m, d = h.shape
    k = y.shape[1]
    return pl.pallas_call(
        _outproj1_kernel,
        grid=(m // tm,),
        in_specs=[pl.BlockSpec((tm, k), lambda i: (i, 0)),
                  pl.BlockSpec((k, d), lambda i: (0, 0)),
                  pl.BlockSpec((tm, d), lambda i: (i, 0))],
        out_specs=pl.BlockSpec((tm, d), lambda i: (i, 0)),
        out_shape=jax.ShapeDtypeStruct((m, d), F32),
        compiler_params=_cparams(("parallel",), 32),
        name="outproj_rwkv",
    )(y, w, h)


def _mlp_kernel(h_ref, g_ref, wu_ref, wd_ref, o_ref, *, ck):
    x = h_ref[...]
    hn = _rms(x, g_ref[...]).astype(BF16)
    acc = x
    for c in range(wu_ref.shape[1] // ck):
        u = jnp.maximum(_dot(hn, wu_ref[:, c * ck:(c + 1) * ck]), 0.0)
        acc = acc + _dot((u * u).astype(BF16), wd_ref[c * ck:(c + 1) * ck, :])
    o_ref[...] = acc


def _mlp(h, g, wu, wd, tm, ck):
    m, d = h.shape
    dff = wu.shape[1]
    return pl.pallas_call(
        functools.partial(_mlp_kernel, ck=ck),
        grid=(m // tm,),
        in_specs=[pl.BlockSpec((tm, d), lambda i: (i, 0)),
                  pl.BlockSpec((1, d), lambda i: (0, 0)),
                  pl.BlockSpec((d, dff), lambda i: (0, 0)),
                  pl.BlockSpec((dff, d), lambda i: (0, 0))],
        out_specs=pl.BlockSpec((tm, d), lambda i: (i, 0)),
        out_shape=jax.ShapeDtypeStruct((m, d), F32),
        compiler_params=_cparams(("parallel",), 56),
        name="mlp",
    )(h, g, wu, wd)


def _rwkvproj_kernel(h_ref, hp_ref, g_ref, mu_ref, wr_ref, wk_ref, wv_ref, w1_ref, w2_ref,
                     a1_ref, a2_ref, g1_ref, g2_ref, w0_ref, a0_ref, kk_ref, ka_ref,
                     r_ref, lw_ref, km_ref, v_ref, kr_ref, a_ref, go_ref, *, tiles_per_seq):
    i = pl.program_id(0)
    tm = h_ref.shape[0]
    gn = g_ref[...]
    hn = _rms(h_ref[...], gn)
    prev = _rms(hp_ref[7:8, :], gn)
    prev = jnp.where(i % tiles_per_seq == 0, jnp.zeros_like(prev), prev)
    row = lax.broadcasted_iota(jnp.int32, hn.shape, 0)
    shifted = jnp.where(row == 0, jnp.broadcast_to(prev, hn.shape), pltpu.roll(hn, 1, 0))
    xx = shifted - hn
    mix = lambda j: (hn + xx * mu_ref[j:j + 1, :]).astype(BF16)
    r = _dot(mix(0), wr_ref[...])
    k = _dot(mix(2), wk_ref[...])
    v = _dot(mix(3), wv_ref[...])
    z = w0_ref[...] + _dot(jnp.tanh(_dot(mix(1), w1_ref[...])).astype(BF16), w2_ref[...])
    a = _sigmoid(a0_ref[...] + _dot(_dot(mix(4), a1_ref[...]).astype(BF16), a2_ref[...]))
    g = _dot(_sigmoid(_dot(mix(5), g1_ref[...])).astype(BF16), g2_ref[...])
    r_ref[...] = r
    lw_ref[...] = -jnp.exp(_log_sigmoid(z) - 0.5)
    km_ref[...] = k * (1.0 + (a - 1.0) * ka_ref[...])
    v_ref[...] = v
    kr_ref[...] = k * kk_ref[...]
    a_ref[...] = a
    go_ref[...] = g


def _rwkvproj(h, g, mu, wr, wk, wv, w1, w2, a1, a2, g1, g2, w0, a0, k_k, k_a, seq, tm):
    m, d = h.shape
    tiles_per_seq = seq // tm
    full = lambda a: pl.BlockSpec(a.shape, lambda i: (0,) * a.ndim)
    row = pl.BlockSpec((tm, d), lambda i: (i, 0))
    prev = pl.BlockSpec((8, d), lambda i: (jnp.maximum(i * (tm // 8) - 1, 0), 0))
    consts = (g, mu, wr, wk, wv, w1, w2, a1, a2, g1, g2, w0, a0, k_k, k_a)
    return pl.pallas_call(
        functools.partial(_rwkvproj_kernel, tiles_per_seq=tiles_per_seq),
        grid=(m // tm,),
        in_specs=[row, prev] + [full(a) for a in consts],
        out_specs=[row] * 7,
        out_shape=[jax.ShapeDtypeStruct((m, d), F32)] * 7,
        compiler_params=_cparams(("parallel",), 56),
        name="rwkv_proj",
    )(h, h, *consts)


def _wkv_kernel(r_ref, lw_ref, km_ref, v_ref, kr_ref, a_ref, g_ref, rk_ref, lg_ref, lb_ref,
                o_ref, st_ref, y_ref, q1_ref, r2_ref, pc_ref, z0_ref, wc_ref, *, chunk):
    @pl.when(pl.program_id(2) == 0)
    def _():
        st_ref[...] = jnp.zeros_like(st_ref)

    ts = r_ref.shape[0]
    nch = ts // chunk
    gl = GROUP_LANES
    rb = lax.broadcasted_iota(jnp.int32, (gl, gl), 0) // RWKV_DIM
    cb = lax.broadcasted_iota(jnp.int32, (gl, gl), 1) // RWKV_DIM
    blockmask = rb == cb
    ones_bd = jnp.where(blockmask, 1.0, 0.0).astype(BF16)

    def headsum(x):
        hi = x.astype(BF16)
        lo = (x - hi.astype(F32)).astype(BF16)
        return _dot(hi, ones_bd) + _dot(lo, ones_bd)

    def bd(y):
        reps = gl // y.shape[0]
        return jnp.where(blockmask, jnp.concatenate([y] * reps, axis=0), 0.0).astype(BF16)

    def hmm(x, y):
        return _dot(x.astype(BF16), bd(y))

    def tn_blocks(x, y):
        return jnp.where(blockmask, _dot(x.T.astype(BF16), y.astype(BF16)), 0.0)

    t_idx = lax.broadcasted_iota(jnp.int32, (chunk, gl), 0)
    s_idx = lax.broadcasted_iota(jnp.int32, (chunk, gl), 1) % RWKV_DIM
    strict = s_idx < t_idx
    incl = s_idx <= t_idx
    tril_b = jnp.where(_tril_mask(chunk), 1.0, 0.0).astype(BF16)
    zeros_c = jnp.zeros((chunk, gl), F32)

    kr = kr_ref[...]
    kkn_all = kr * lax.rsqrt(jnp.maximum(headsum(kr * kr), 1e-24))

    for c in range(nch):
        rows = slice(c * chunk, (c + 1) * chunk)
        lw = lw_ref[rows, :]
        cum = _cumsum_rows(lw, tril_b)
        c_last = cum[chunk - 1:chunk, :]
        kkn = kkn_all[rows, :]
        a_sig = a_ref[rows, :]
        km = km_ref[rows, :]
        v = v_ref[rows, :]
        e_neg = jnp.exp(-cum)
        e_end = jnp.exp(c_last - cum)
        at = -kkn * jnp.exp(cum - lw)
        bt = kkn * a_sig * e_neg
        kt = km * e_neg
        rt = r_ref[rows, :] * jnp.exp(cum)
        bw = kkn * a_sig * e_end
        kw = km * e_end

        lhs = jnp.concatenate([at, rt], axis=0).astype(BF16)
        pb = _dot_nt(lhs, bd(bt))
        pk = _dot_nt(lhs, bd(kt))
        a_ab = jnp.where(strict, pb[:chunk], 0.0)
        a_rb = jnp.where(incl, pb[chunk:], 0.0)
        a_ak = jnp.where(strict, pk[:chunk], 0.0)
        a_rk = jnp.where(incl, pk[chunk:], 0.0)

        e = jnp.where((t_idx % 2 == 1) & (s_idx == t_idx - 1), a_ab, 0.0)
        size = 2
        while size < chunk:
            off = ((t_idx // size) % 2 == 1) & (s_idx // size == t_idx // size - 1)
            a_off = jnp.where(off, a_ab, 0.0)
            t1 = a_off + hmm(a_off, e)
            e = e + t1 + hmm(e, t1)
            size *= 2

        akv = hmm(a_ak, v)
        p1 = akv + hmm(e, akv)
        q1 = hmm(a_rk, v) + hmm(a_rb, p1)
        gmat = a_rb + hmm(a_rb, e)
        r2 = rt + hmm(gmat, at)
        pad = lambda x: jnp.concatenate([x, zeros_c], axis=0)
        full = _dot(pad(e).T.astype(BF16), pad(bw).astype(BF16))
        full = jnp.where(blockmask, full, 0.0)
        bw2 = bw
        for h in range(RWKV_GROUP):
            bw2 = bw2 + full[h * chunk:(h + 1) * chunk, :]
        pc = tn_blocks(pad(at), pad(bw2))
        z0 = tn_blocks(jnp.concatenate([p1, v], axis=0), jnp.concatenate([bw, kw], axis=0))

        q1_ref[rows, :] = q1
        r2_ref[rows, :] = r2.astype(BF16)
        pc_ref[c] = pc.astype(BF16)
        z0_ref[c] = z0
        wc_ref[c] = jnp.broadcast_to(jnp.exp(c_last), (8, gl))

    def body(c, carry):
        rows = pl.ds(pl.multiple_of(c * chunk, chunk), chunk)
        s = st_ref[...]
        sb = s.astype(BF16)
        y_ref[rows, :] = q1_ref[rows, :] + _dot_nt(r2_ref[rows, :], sb)
        st_ref[...] = s * wc_ref[c][0:1, :] + _dot(sb, pc_ref[c]) + z0_ref[c]
        return carry

    lax.fori_loop(0, nch, body, 0)

    y = y_ref[...]
    inv_n = 1.0 / RWKV_DIM
    mean = headsum(y) * inv_n
    dlt = y - mean
    var = headsum(dlt * dlt) * inv_n
    yn = dlt * lax.rsqrt(var + GN_EPS) * lg_ref[...] + lb_ref[...]
    bonus = headsum(r_ref[...] * km_ref[...] * rk_ref[...]) * v_ref[...]
    o_ref[...] = ((yn + bonus) * g_ref[...]).astype(o_ref.dtype)


def _wkv(r, lw, km, v, kr, a, g, r_k, lnx_g, lnx_b, batch, seq, ts, chunk):
    m, d = r.shape
    gl = GROUP_LANES
    ngroups = d // gl
    nt = seq // ts
    nch = ts // chunk
    row = pl.BlockSpec((ts, gl), lambda b, j, t: (b * nt + t, j))
    vec = pl.BlockSpec((1, gl), lambda b, j, t: (0, j))
    return pl.pallas_call(
        functools.partial(_wkv_kernel, chunk=chunk),
        grid=(batch, ngroups, nt),
        in_specs=[row] * 7 + [vec] * 3,
        out_specs=row,
        out_shape=jax.ShapeDtypeStruct((m, d), BF16),
        scratch_shapes=[pltpu.VMEM((gl, gl), F32),
                        pltpu.VMEM((ts, gl), F32),
                        pltpu.VMEM((ts, gl), F32),
                        pltpu.VMEM((ts, gl), BF16),
                        pltpu.VMEM((nch, gl, gl), BF16),
                        pltpu.VMEM((nch, gl, gl), F32),
                        pltpu.VMEM((nch, 8, gl), F32)],
        compiler_params=_cparams(("parallel", "parallel", "arbitrary"), 48),
        name="wkv7",
    )(r, lw, km, v, kr, a, g, r_k, lnx_g, lnx_b)


def kernel(x, norm_mix_g, norm_ffn_g, ab_w_in, hgrn_lower_bounds, hgrn_norm_g, fox_forget_bias,
           fox_q_norm_g, fox_k_norm_g, ab_w_out, rwkv_mu, rwkv_w_rkv, rwkv_w0, rwkv_w1, rwkv_w2,
           rwkv_a0, rwkv_a1, rwkv_a2, rwkv_g1, rwkv_g2, rwkv_k_k, rwkv_k_a, rwkv_r_k,
           rwkv_lnx_g, rwkv_lnx_b, rwkv_w_o, mlp_w_up, mlp_w_down):
    batch, seq, d = x.shape
    m = batch * seq
    tm = min(256, seq)
    ts = min(512, seq)
    tq = min(256, seq)
    row = lambda a: a.reshape(1, -1).astype(F32)
    bf = lambda a: a.astype(BF16)

    lb_all = jnp.cumsum(jax.nn.softmax(hgrn_lower_bounds.astype(F32), axis=0), axis=0)
    h = x.reshape(m, d)

    n_in = ab_w_in.shape[-1]
    n_pad = (-n_in) % LANES
    w_in = bf(jnp.pad(ab_w_in[0], ((0, 0), (0, n_pad))))
    proj = _inproj(h, row(norm_mix_g[0]), w_in, tm)
    ya = _hgrn(proj, row(lb_all[0]), row(hgrn_norm_g[0]), batch, seq, ts, 64)
    fb = jnp.pad(row(fox_forget_bias[0]), ((0, 0), (0, LANES - FOX_HEADS)))
    qa, ka, vb = _foxprep(proj, fb, row(fox_q_norm_g[0]), row(fox_k_norm_g[0]), batch, seq, ts)
    yb = _fox(qa, ka, vb, proj, batch, seq, tq)
    wa = bf(ab_w_out[0][:HGRN_HEADS * HGRN_DIM])
    wb = bf(ab_w_out[0][HGRN_HEADS * HGRN_DIM:])
    h = _outproj2(ya, yb, wa, wb, h, tm)
    h = _mlp(h, row(norm_ffn_g[0]), bf(mlp_w_up[0]), bf(mlp_w_down[0]), tm, 1024)

    outs = _rwkvproj(h, row(norm_mix_g[1]), rwkv_mu[0].astype(F32),
                     bf(rwkv_w_rkv[0, 0]), bf(rwkv_w_rkv[0, 1]), bf(rwkv_w_rkv[0, 2]),
                     bf(rwkv_w1[0]), bf(rwkv_w2[0]), bf(rwkv_a1[0]), bf(rwkv_a2[0]),
                     bf(rwkv_g1[0]), bf(rwkv_g2[0]), row(rwkv_w0[0]), row(rwkv_a0[0]),
                     row(rwkv_k_k[0]), row(rwkv_k_a[0]), seq, tm)
    z = _wkv(*outs, row(rwkv_r_k[0]), row(rwkv_lnx_g[0]), row(rwkv_lnx_b[0]),
             batch, seq, min(256, seq), 64)
    h = _outproj1(z, bf(rwkv_w_o[0]), h, tm)
    h = _mlp(h, row(norm_ffn_g[1]), bf(mlp_w_up[1]), bf(mlp_w_down[1]), tm, 1024)
    return h.reshape(batch, seq, d)
```

```python
import functools

import jax
import jax.numpy as jnp
from jax import lax
from jax.experimental import pallas as pl
from jax.experimental.pallas import tpu as pltpu

F32 = jnp.float32
BF16 = jnp.bfloat16

RMS_EPS = 1e-6
GN_EPS = 64e-5

HGRN_HEADS = 4
HGRN_DIM = 128
FOX_HEADS = 8
FOX_DIM = 64
RWKV_DIM = 64
RWKV_GROUP = 4
GROUP_LANES = RWKV_GROUP * RWKV_DIM
LANES = 128
NEG_BIG = -1e30

NT_DIMS = (((1,), (1,)), ((), ()))


def _cparams(sem, vmem_mb):
    return pltpu.CompilerParams(dimension_semantics=sem, vmem_limit_bytes=vmem_mb * 1024 * 1024)


def _dot(a, b):
    return jnp.dot(a, b, preferred_element_type=F32)


def _dot_nt(a, b):
    return lax.dot_general(a, b, NT_DIMS, preferred_element_type=F32)


def _rms(x, g):
    return x * lax.rsqrt(jnp.mean(x * x, axis=-1, keepdims=True) + RMS_EPS) * g


def _sigmoid(x):
    return 1.0 / (1.0 + jnp.exp(-x))


def _log_sigmoid(x):
    return jnp.minimum(x, 0.0) - jnp.log(1.0 + jnp.exp(-jnp.abs(x)))


def _tril_mask(n, strict=False):
    r = lax.broadcasted_iota(jnp.int32, (n, n), 0)
    c = lax.broadcasted_iota(jnp.int32, (n, n), 1)
    return (c < r) if strict else (c <= r)


def _split3(x):
    hi = x.astype(BF16)
    r1 = x - hi.astype(F32)
    mid = r1.astype(BF16)
    lo = (r1 - mid.astype(F32)).astype(BF16)
    return hi, mid, lo


def _cumsum_rows(x, tril_bf16):
    hi, mid, lo = _split3(x)
    return _dot(tril_bf16, hi) + _dot(tril_bf16, mid) + _dot(tril_bf16, lo)


def _inproj_kernel(x_ref, g_ref, w_ref, o_ref):
    hn = _rms(x_ref[...], g_ref[...]).astype(BF16)
    o_ref[...] = _dot(hn, w_ref[...])


def _inproj(x2, g, w, tm):
    m, d = x2.shape
    n = w.shape[1]
    return pl.pallas_call(
        _inproj_kernel,
        grid=(m // tm,),
        in_specs=[
            pl.BlockSpec((tm, d), lambda i: (i, 0)),
            pl.BlockSpec((1, d), lambda i: (0, 0)),
            pl.BlockSpec((d, n), lambda i: (0, 0)),
        ],
        out_specs=pl.BlockSpec((tm, n), lambda i: (i, 0)),
        out_shape=jax.ShapeDtypeStruct((m, n), F32),
        compiler_params=_cparams(("parallel",), 48),
        name="inproj",
    )(x2, g, w)


def _hgrn_kernel(q_ref, f_ref, i_ref, g_ref, lb_ref, ng_ref, o_ref, st_ref, *, chunk):
    @pl.when(pl.program_id(1) == 0)
    def _():
        st_ref[...] = jnp.zeros_like(st_ref)

    ts = q_ref.shape[0]
    causal = _tril_mask(chunk)
    tril_b = jnp.where(causal, 1.0, 0.0).astype(BF16)
    mid = chunk // 2

    def body(c, carry):
        rows = pl.ds(pl.multiple_of(c * chunk, chunk), chunk)
        for h in range(HGRN_HEADS):
            cols = slice(h * HGRN_DIM, (h + 1) * HGRN_DIM)
            aq = q_ref[rows, cols]
            af = f_ref[rows, cols]
            v = i_ref[rows, cols]
            ag = g_ref[rows, cols]
            lb = lb_ref[:, cols]
            f = lb + (1.0 - lb) * _sigmoid(af)
            b = _cumsum_rows(jnp.log(f), tril_b)
            b_mid = b[mid - 1:mid, :]
            b_last = b[chunk - 1:chunk, :]
            q = aq * _sigmoid(aq)
            k = 1.0 - f
            s = _dot_nt((q * jnp.exp(b - b_mid)).astype(BF16), (k * jnp.exp(b_mid - b)).astype(BF16))
            s = jnp.where(causal, s, 0.0)
            st = st_ref[h]
            vb = v.astype(BF16)
            o = _dot(s.astype(BF16), vb) + _dot_nt((q * jnp.exp(b)).astype(BF16), st.astype(BF16))
            k_st = (k * jnp.exp(b_last - b)).astype(BF16)
            st_ref[h] = st * jnp.exp(b_last) + _dot(v.T.astype(BF16), k_st)
            on = _rms(o, ng_ref[:, cols])
            o_ref[rows, cols] = (on * (ag * _sigmoid(ag))).astype(o_ref.dtype)
        return carry

    lax.fori_loop(0, ts // chunk, body, 0)


def _hgrn(proj, lb, ng, batch, seq, ts, chunk):
    m = proj.shape[0]
    w = HGRN_HEADS * HGRN_DIM
    nt = seq // ts
    spec = lambda j: pl.BlockSpec((ts, w), lambda b, t, j=j: (b * nt + t, j))
    vec = pl.BlockSpec((1, w), lambda b, t: (0, 0))
    return pl.pallas_call(
        functools.partial(_hgrn_kernel, chunk=chunk),
        grid=(batch, nt),
        in_specs=[spec(0), spec(1), spec(2), spec(3), vec, vec],
        out_specs=pl.BlockSpec((ts, w), lambda b, t: (b * nt + t, 0)),
        out_shape=jax.ShapeDtypeStruct((m, w), BF16),
        scratch_shapes=[pltpu.VMEM((HGRN_HEADS, HGRN_DIM, HGRN_DIM), F32)],
        compiler_params=_cparams(("parallel", "arbitrary"), 32),
        name="hgrn2",
    )(proj, proj, proj, proj, lb, ng)


def _foxprep_kernel(q_ref, k_ref, v_ref, f_ref, fb_ref, qg_ref, kg_ref,
                    qa_ref, ka_ref, vo_ref, carry_ref):
    @pl.when(pl.program_id(1) == 0)
    def _():
        carry_ref[...] = jnp.zeros_like(carry_ref)

    ts = q_ref.shape[0]
    tril_b = jnp.where(_tril_mask(ts), 1.0, 0.0).astype(BF16)
    lf = _log_sigmoid(f_ref[...] + fb_ref[...])
    c = _cumsum_rows(lf, tril_b) + carry_ref[...]
    carry_ref[...] = c[ts - 1:ts, :]
    pieces = [p.astype(F32) for p in _split3(c)]
    lane = lax.broadcasted_iota(jnp.int32, (ts, LANES), 1)
    ones_q = jnp.where((lane >= FOX_DIM + 3) & (lane < FOX_DIM + 6), 1.0, 0.0)
    ones_k = jnp.where((lane >= FOX_DIM) & (lane < FOX_DIM + 3), 1.0, 0.0)
    zpad = jnp.zeros((ts, LANES - FOX_DIM), F32)
    scale = FOX_DIM ** -0.5
    for h in range(FOX_HEADS):
        cols = slice(h * FOX_DIM, (h + 1) * FOX_DIM)
        qn = _rms(q_ref[:, cols], qg_ref[...]) * scale
        kn = _rms(k_ref[:, cols], kg_ref[...])
        qa = jnp.concatenate([qn, zpad], axis=1) + ones_q
        ka = jnp.concatenate([kn, zpad], axis=1) + ones_k
        for p, piece in enumerate(pieces):
            col = jnp.broadcast_to(piece[:, h:h + 1], (ts, LANES))
            qa = qa + jnp.where(lane == FOX_DIM + p, col, 0.0)
            ka = ka - jnp.where(lane == FOX_DIM + 3 + p, col, 0.0)
        out_cols = slice(h * LANES, (h + 1) * LANES)
        qa_ref[:, out_cols] = qa.astype(BF16)
        ka_ref[:, out_cols] = ka.astype(BF16)
    vo_ref[...] = v_ref[...].astype(BF16)


def _foxprep(proj, fb, qg, kg, batch, seq, ts):
    m = proj.shape[0]
    w = FOX_HEADS * FOX_DIM
    nt = seq // ts
    spec = lambda j: pl.BlockSpec((ts, w), lambda b, t, j=j: (b * nt + t, j))
    fcol = (8 * w) // LANES
    return pl.pallas_call(
        _foxprep_kernel,
        grid=(batch, nt),
        in_specs=[spec(4), spec(5), spec(6),
                  pl.BlockSpec((ts, LANES), lambda b, t: (b * nt + t, fcol)),
                  pl.BlockSpec((1, LANES), lambda b, t: (0, 0)),
                  pl.BlockSpec((1, FOX_DIM), lambda b, t: (0, 0)),
                  pl.BlockSpec((1, FOX_DIM), lambda b, t: (0, 0))],
        out_specs=[pl.BlockSpec((ts, FOX_HEADS * LANES), lambda b, t: (b * nt + t, 0)),
                   pl.BlockSpec((ts, FOX_HEADS * LANES), lambda b, t: (b * nt + t, 0)),
                   pl.BlockSpec((ts, w), lambda b, t: (b * nt + t, 0))],
        out_shape=[jax.ShapeDtypeStruct((m, FOX_HEADS * LANES), BF16),
                   jax.ShapeDtypeStruct((m, FOX_HEADS * LANES), BF16),
                   jax.ShapeDtypeStruct((m, w), BF16)],
        scratch_shapes=[pltpu.VMEM((1, LANES), F32)],
        compiler_params=_cparams(("parallel", "arbitrary"), 32),
        name="foxprep",
    )(proj, proj, proj, proj, fb, qg, kg)


def _fox_kernel(q_ref, k_ref, v_ref, g_ref, o_ref, *, tq, heads):
    i = pl.program_id(2)
    pairs = heads // 2
    hs = range(heads)
    first = lax.broadcasted_iota(jnp.int32, (tq, LANES), 1) < FOX_DIM
    causal = _tril_mask(tq)
    qs = [q_ref[:, h * LANES:(h + 1) * LANES] for h in hs]

    def step(j, carry, masked):
        ms, ls, accs = carry
        rows = pl.ds(pl.multiple_of(j * tq, tq), tq)
        s = [_dot_nt(qs[h], k_ref[rows, h * LANES:(h + 1) * LANES]) for h in hs]
        if masked:
            s = [jnp.where(causal, x, NEG_BIG) for x in s]
        m_new = [jnp.maximum(ms[h], jnp.max(s[h], axis=-1, keepdims=True)) for h in hs]
        alpha = [jnp.exp(ms[h] - m_new[h]) for h in hs]
        p = [jnp.exp(s[h] - m_new[h]) for h in hs]
        l_new = [alpha[h] * ls[h] + jnp.sum(p[h], axis=-1, keepdims=True) for h in hs]
        pv = [_dot(jnp.concatenate([p[2 * r].astype(BF16), p[2 * r + 1].astype(BF16)], axis=0),
                   v_ref[rows, r * LANES:(r + 1) * LANES]) for r in range(pairs)]
        acc_new = [accs[r] * jnp.where(first, alpha[2 * r], alpha[2 * r + 1])
                   + jnp.where(first, pv[r][:tq], pv[r][tq:]) for r in range(pairs)]
        return tuple(m_new), tuple(l_new), tuple(acc_new)

    neg = jnp.full((tq, 1), NEG_BIG, F32)
    zero = jnp.zeros((tq, 1), F32)
    init = ((neg,) * heads, (zero,) * heads, (jnp.zeros((tq, LANES), F32),) * pairs)
    carry = lax.fori_loop(0, i, lambda j, c: step(j, c, False), init)
    _, ls, accs = step(i, carry, True)
    for r in range(pairs):
        cols = slice(r * LANES, (r + 1) * LANES)
        out = accs[r] / jnp.where(first, ls[2 * r], ls[2 * r + 1]) * _sigmoid(g_ref[:, cols])
        o_ref[:, cols] = out.astype(o_ref.dtype)


def _fox(qa, ka, vb, proj, batch, seq, tq, heads):
    m = qa.shape[0]
    nq = seq // tq
    groups = FOX_HEADS // heads
    wq = heads * LANES
    wv = heads * FOX_DIM
    gcol = (7 * FOX_HEADS * FOX_DIM) // wv
    return pl.pallas_call(
        functools.partial(_fox_kernel, tq=tq, heads=heads),
        grid=(batch, groups, nq),
        in_specs=[pl.BlockSpec((tq, wq), lambda b, p, i: (b * nq + i, p)),
                  pl.BlockSpec((seq, wq), lambda b, p, i: (b, p)),
                  pl.BlockSpec((seq, wv), lambda b, p, i: (b, p)),
                  pl.BlockSpec((tq, wv), lambda b, p, i: (b * nq + i, gcol + p))],
        out_specs=pl.BlockSpec((tq, wv), lambda b, p, i: (b * nq + i, p)),
        out_shape=jax.ShapeDtypeStruct((m, FOX_HEADS * FOX_DIM), BF16),
        compiler_params=_cparams(("parallel", "parallel", "arbitrary"), 48),
        name="fox_attention",
    )(qa, ka, vb, proj)


def _outproj2_kernel(ya_ref, yb_ref, wa_ref, wb_ref, h_ref, o_ref):
    o_ref[...] = h_ref[...] + _dot(ya_ref[...], wa_ref[...]) + _dot(yb_ref[...], wb_ref[...])


def _outproj2(ya, yb, wa, wb, h, tm):
    m, d = h.shape
    ka, kb = ya.shape[1], yb.shape[1]
    return pl.pallas_call(
        _outproj2_kernel,
        grid=(m // tm,),
        in_specs=[pl.BlockSpec((tm, ka), lambda i: (i, 0)),
                  pl.BlockSpec((tm, kb), lambda i: (i, 0)),
                  pl.BlockSpec((ka, d), lambda i: (0, 0)),
                  pl.BlockSpec((kb, d), lambda i: (0, 0)),
                  pl.BlockSpec((tm, d), lambda i: (i, 0))],
        out_specs=pl.BlockSpec((tm, d), lambda i: (i, 0)),
        out_shape=jax.ShapeDtypeStruct((m, d), F32),
        compiler_params=_cparams(("parallel",), 32),
        name="outproj_mix",
    )(ya, yb, wa, wb, h)


def _outproj1_kernel(y_ref, w_ref, h_ref, o_ref):
    o_ref[...] = h_ref[...] + _dot(y_ref[...], w_ref[...])


def _outproj1(y, w, h, tm):
    m, d = h.shape
    k = y.shape[1]
    return pl.pallas_call(
        _outproj1_kernel,
        grid=(m // tm,),
        in_specs=[pl.BlockSpec((tm, k), lambda i: (i, 0)),
                  pl.BlockSpec((k, d), lambda i: (0, 0)),
                  pl.BlockSpec((tm, d), lambda i: (i, 0))],
        out_specs=pl.BlockSpec((tm, d), lambda i: (i, 0)),
        out_shape=jax.ShapeDtypeStruct((m, d), F32),
        compiler_params=_cparams(("parallel",), 32),
        name="outproj_rwkv",
    )(y, w, h)


def _mlp_kernel(h_ref, g_ref, wu_ref, wd_ref, o_ref, *, ck):
    x = h_ref[...]
    hn = _rms(x, g_ref[...]).astype(BF16)
    acc = x
    for c in range(wu_ref.shape[1] // ck):
        u = jnp.maximum(_dot(hn, wu_ref[:, c * ck:(c + 1) * ck]), 0.0)
        acc = acc + _dot((u * u).astype(BF16), wd_ref[c * ck:(c + 1) * ck, :])
    o_ref[...] = acc


def _mlp(h, g, wu, wd, tm, ck):
    m, d = h.shape
    dff = wu.shape[1]
    return pl.pallas_call(
        functools.partial(_mlp_kernel, ck=ck),
        grid=(m // tm,),
        in_specs=[pl.BlockSpec((tm, d), lambda i: (i, 0)),
                  pl.BlockSpec((1, d), lambda i: (0, 0)),
                  pl.BlockSpec((d, dff), lambda i: (0, 0)),
                  pl.BlockSpec((dff, d), lambda i: (0, 0))],
        out_specs=pl.BlockSpec((tm, d), lambda i: (i, 0)),
        out_shape=jax.ShapeDtypeStruct((m, d), F32),
        compiler_params=_cparams(("parallel",), 56),
        name="mlp",
    )(h, g, wu, wd)


def _rwkvproj_kernel(h_ref, hp_ref, g_ref, mu_ref, wr_ref, wk_ref, wv_ref, w1_ref, w2_ref,
                     a1_ref, a2_ref, g1_ref, g2_ref, w0_ref, a0_ref, kk_ref, ka_ref,
                     r_ref, lw_ref, km_ref, v_ref, kr_ref, a_ref, go_ref, *, tiles_per_seq):
    i = pl.program_id(0)
    tm = h_ref.shape[0]
    gn = g_ref[...]
    hn = _rms(h_ref[...], gn)
    prev = _rms(hp_ref[7:8, :], gn)
    prev = jnp.where(i % tiles_per_seq == 0, jnp.zeros_like(prev), prev)
    row = lax.broadcasted_iota(jnp.int32, hn.shape, 0)
    shifted = jnp.where(row == 0, jnp.broadcast_to(prev, hn.shape), pltpu.roll(hn, 1, 0))
    xx = shifted - hn
    mix = lambda j: (hn + xx * mu_ref[j:j + 1, :]).astype(BF16)
    r = _dot(mix(0), wr_ref[...])
    k = _dot(mix(2), wk_ref[...])
    v = _dot(mix(3), wv_ref[...])
    z = w0_ref[...] + _dot(jnp.tanh(_dot(mix(1), w1_ref[...])).astype(BF16), w2_ref[...])
    a = _sigmoid(a0_ref[...] + _dot(_dot(mix(4), a1_ref[...]).astype(BF16), a2_ref[...]))
    g = _dot(_sigmoid(_dot(mix(5), g1_ref[...])).astype(BF16), g2_ref[...])
    r_ref[...] = r
    lw_ref[...] = -jnp.exp(_log_sigmoid(z) - 0.5)
    km_ref[...] = k * (1.0 + (a - 1.0) * ka_ref[...])
    v_ref[...] = v
    kr_ref[...] = k * kk_ref[...]
    a_ref[...] = a
    go_ref[...] = g


def _rwkvproj(h, g, mu, wr, wk, wv, w1, w2, a1, a2, g1, g2, w0, a0, k_k, k_a, seq, tm):
    m, d = h.shape
    tiles_per_seq = seq // tm
    full = lambda a: pl.BlockSpec(a.shape, lambda i: (0,) * a.ndim)
    row = pl.BlockSpec((tm, d), lambda i: (i, 0))
    prev = pl.BlockSpec((8, d), lambda i: (jnp.maximum(i * (tm // 8) - 1, 0), 0))
    consts = (g, mu, wr, wk, wv, w1, w2, a1, a2, g1, g2, w0, a0, k_k, k_a)
    return pl.pallas_call(
        functools.partial(_rwkvproj_kernel, tiles_per_seq=tiles_per_seq),
        grid=(m // tm,),
        in_specs=[row, prev] + [full(a) for a in consts],
        out_specs=[row] * 7,
        out_shape=[jax.ShapeDtypeStruct((m, d), F32)] * 7,
        compiler_params=_cparams(("parallel",), 56),
        name="rwkv_proj",
    )(h, h, *consts)


def _wkv_kernel(r_ref, lw_ref, km_ref, v_ref, kr_ref, a_ref, g_ref, rk_ref, lg_ref, lb_ref,
                o_ref, st_ref, y_ref, q1_ref, r2_ref, pc_ref, z0_ref, wc_ref, *, chunk):
    @pl.when(pl.program_id(2) == 0)
    def _():
        st_ref[...] = jnp.zeros_like(st_ref)

    ts = r_ref.shape[0]
    nch = ts // chunk
    gl = GROUP_LANES
    rb = lax.broadcasted_iota(jnp.int32, (gl, gl), 0) // RWKV_DIM
    cb = lax.broadcasted_iota(jnp.int32, (gl, gl), 1) // RWKV_DIM
    blockmask = rb == cb
    ones_bd = jnp.where(blockmask, 1.0, 0.0).astype(BF16)

    def headsum(x):
        hi = x.astype(BF16)
        lo = (x - hi.astype(F32)).astype(BF16)
        return _dot(hi, ones_bd) + _dot(lo, ones_bd)

    def bd(y):
        reps = gl // y.shape[0]
        return jnp.where(blockmask, jnp.concatenate([y] * reps, axis=0), 0.0).astype(BF16)

    def hmm(x, y):
        return _dot(x.astype(BF16), bd(y))

    def tn_blocks(x, y):
        return jnp.where(blockmask, _dot(x.T.astype(BF16), y.astype(BF16)), 0.0)

    t_idx = lax.broadcasted_iota(jnp.int32, (chunk, gl), 0)
    s_idx = lax.broadcasted_iota(jnp.int32, (chunk, gl), 1) % RWKV_DIM
    strict = s_idx < t_idx
    incl = s_idx <= t_idx
    tril_b = jnp.where(_tril_mask(chunk), 1.0, 0.0).astype(BF16)
    zeros_c = jnp.zeros((chunk, gl), F32)

    kr = kr_ref[...]
    kkn_all = kr * lax.rsqrt(jnp.maximum(headsum(kr * kr), 1e-24))

    chunks = range(nch)
    rows_of = lambda c: slice(c * chunk, (c + 1) * chunk)
    pad = lambda x: jnp.concatenate([x, zeros_c], axis=0)
    each = lambda fn, *lists: [fn(*args) for args in zip(*lists)]

    lw = [lw_ref[rows_of(c), :] for c in chunks]
    cum = each(lambda x: _cumsum_rows(x, tril_b), lw)
    c_last = [x[chunk - 1:chunk, :] for x in cum]
    kkn = [kkn_all[rows_of(c), :] for c in chunks]
    kka = [kkn[c] * a_ref[rows_of(c), :] for c in chunks]
    km = [km_ref[rows_of(c), :] for c in chunks]
    v = [v_ref[rows_of(c), :] for c in chunks]
    e_neg = [jnp.exp(-x) for x in cum]
    e_end = each(lambda cl, x: jnp.exp(cl - x), c_last, cum)
    at = each(lambda k, x, l: -k * jnp.exp(x - l), kkn, cum, lw)
    bt = each(jnp.multiply, kka, e_neg)
    kt = each(jnp.multiply, km, e_neg)
    rt = [r_ref[rows_of(c), :] * jnp.exp(cum[c]) for c in chunks]
    bw = each(jnp.multiply, kka, e_end)
    kw = each(jnp.multiply, km, e_end)
    for c in chunks:
        wc_ref[c] = jnp.broadcast_to(jnp.exp(c_last[c]), (8, gl))

    lhs = each(lambda a, r: jnp.concatenate([a, r], axis=0).astype(BF16), at, rt)
    pb = each(lambda l, b: _dot_nt(l, bd(b)), lhs, bt)
    pk = each(lambda l, k: _dot_nt(l, bd(k)), lhs, kt)
    a_ab = [jnp.where(strict, x[:chunk], 0.0) for x in pb]
    a_rb = [jnp.where(incl, x[chunk:], 0.0) for x in pb]
    a_ak = [jnp.where(strict, x[:chunk], 0.0) for x in pk]
    a_rk = [jnp.where(incl, x[chunk:], 0.0) for x in pk]

    e = [jnp.where((t_idx % 2 == 1) & (s_idx == t_idx - 1), x, 0.0) for x in a_ab]
    size = 2
    while size < chunk:
        off = ((t_idx // size) % 2 == 1) & (s_idx // size == t_idx // size - 1)
        a_off = [jnp.where(off, x, 0.0) for x in a_ab]
        t1 = each(lambda ao, ee: ao + hmm(ao, ee), a_off, e)
        e = each(lambda ee, tt: ee + tt + hmm(ee, tt), e, t1)
        size *= 2

    akv = each(hmm, a_ak, v)
    p1 = each(lambda x, ee: x + hmm(ee, x), akv, e)
    mat = each(lambda x, ee: x + hmm(ee, x), at, e)
    q1 = each(lambda ark, vv, arb, pp: hmm(ark, vv) + hmm(arb, pp), a_rk, v, a_rb, p1)
    r2 = each(lambda r, arb, mm: r + hmm(arb, mm), rt, a_rb, mat)
    pc = each(lambda mm, b: tn_blocks(pad(mm), pad(b)), mat, bw)
    z0 = each(lambda pp, vv, b, k: tn_blocks(jnp.concatenate([pp, vv], axis=0),
                                             jnp.concatenate([b, k], axis=0)), p1, v, bw, kw)
    for c in chunks:
        q1_ref[rows_of(c), :] = q1[c]
        r2_ref[rows_of(c), :] = r2[c].astype(BF16)
        pc_ref[c] = pc[c].astype(BF16)
        z0_ref[c] = z0[c]

    def body(c, carry):
        rows = pl.ds(pl.multiple_of(c * chunk, chunk), chunk)
        s = st_ref[...]
        sb = s.astype(BF16)
        y_ref[rows, :] = q1_ref[rows, :] + _dot_nt(r2_ref[rows, :], sb)
        st_ref[...] = s * wc_ref[c][0:1, :] + _dot(sb, pc_ref[c]) + z0_ref[c]
        return carry

    lax.fori_loop(0, nch, body, 0)

    y = y_ref[...]
    inv_n = 1.0 / RWKV_DIM
    mean = headsum(y) * inv_n
    dlt = y - mean
    var = headsum(dlt * dlt) * inv_n
    yn = dlt * lax.rsqrt(var + GN_EPS) * lg_ref[...] + lb_ref[...]
    bonus = headsum(r_ref[...] * km_ref[...] * rk_ref[...]) * v_ref[...]
    o_ref[...] = ((yn + bonus) * g_ref[...]).astype(o_ref.dtype)


def _wkv(r, lw, km, v, kr, a, g, r_k, lnx_g, lnx_b, batch, seq, ts, chunk):
    m, d = r.shape
    gl = GROUP_LANES
    ngroups = d // gl
    nt = seq // ts
    nch = ts // chunk
    row = pl.BlockSpec((ts, gl), lambda b, j, t: (b * nt + t, j))
    vec = pl.BlockSpec((1, gl), lambda b, j, t: (0, j))
    return pl.pallas_call(
        functools.partial(_wkv_kernel, chunk=chunk),
        grid=(batch, ngroups, nt),
        in_specs=[row] * 7 + [vec] * 3,
        out_specs=row,
        out_shape=jax.ShapeDtypeStruct((m, d), BF16),
        scratch_shapes=[pltpu.VMEM((gl, gl), F32),
                        pltpu.VMEM((ts, gl), F32),
                        pltpu.VMEM((ts, gl), F32),
                        pltpu.VMEM((ts, gl), BF16),
                        pltpu.VMEM((nch, gl, gl), BF16),
                        pltpu.VMEM((nch, gl, gl), F32),
                        pltpu.VMEM((nch, 8, gl), F32)],
        compiler_params=_cparams(("parallel", "parallel", "arbitrary"), 48),
        name="wkv7",
    )(r, lw, km, v, kr, a, g, r_k, lnx_g, lnx_b)


def kernel(x, norm_mix_g, norm_ffn_g, ab_w_in, hgrn_lower_bounds, hgrn_norm_g, fox_forget_bias,
           fox_q_norm_g, fox_k_norm_g, ab_w_out, rwkv_mu, rwkv_w_rkv, rwkv_w0, rwkv_w1, rwkv_w2,
           rwkv_a0, rwkv_a1, rwkv_a2, rwkv_g1, rwkv_g2, rwkv_k_k, rwkv_k_a, rwkv_r_k,
           rwkv_lnx_g, rwkv_lnx_b, rwkv_w_o, mlp_w_up, mlp_w_down):
    batch, seq, d = x.shape
    m = batch * seq
    tm = min(256, seq)
    ts = min(512, seq)
    tq = min(512, seq)
    row = lambda a: a.reshape(1, -1).astype(F32)
    bf = lambda a: a.astype(BF16)

    lb_all = jnp.cumsum(jax.nn.softmax(hgrn_lower_bounds.astype(F32), axis=0), axis=0)
    h = x.reshape(m, d)

    n_in = ab_w_in.shape[-1]
    n_pad = (-n_in) % LANES
    w_in = bf(jnp.pad(ab_w_in[0], ((0, 0), (0, n_pad))))
    proj = _inproj(h, row(norm_mix_g[0]), w_in, tm)
    ya = _hgrn(proj, row(lb_all[0]), row(hgrn_norm_g[0]), batch, seq, ts, 64)
    fb = jnp.pad(row(fox_forget_bias[0]), ((0, 0), (0, LANES - FOX_HEADS)))
    qa, ka, vb = _foxprep(proj, fb, row(fox_q_norm_g[0]), row(fox_k_norm_g[0]), batch, seq, ts)
    yb = _fox(qa, ka, vb, proj, batch, seq, tq, 4)
    wa = bf(ab_w_out[0][:HGRN_HEADS * HGRN_DIM])
    wb = bf(ab_w_out[0][HGRN_HEADS * HGRN_DIM:])
    h = _outproj2(ya, yb, wa, wb, h, tm)
    h = _mlp(h, row(norm_ffn_g[0]), bf(mlp_w_up[0]), bf(mlp_w_down[0]), tm, 1024)

    outs = _rwkvproj(h, row(norm_mix_g[1]), rwkv_mu[0].astype(F32),
                     bf(rwkv_w_rkv[0, 0]), bf(rwkv_w_rkv[0, 1]), bf(rwkv_w_rkv[0, 2]),
                     bf(rwkv_w1[0]), bf(rwkv_w2[0]), bf(rwkv_a1[0]), bf(rwkv_a2[0]),
                     bf(rwkv_g1[0]), bf(rwkv_g2[0]), row(rwkv_w0[0]), row(rwkv_a0[0]),
                     row(rwkv_k_k[0]), row(rwkv_k_a[0]), seq, tm)
    z = _wkv(*outs, row(rwkv_r_k[0]), row(rwkv_lnx_g[0]), row(rwkv_lnx_b[0]),
             batch, seq, min(512, seq), 64)
    h = _outproj1(z, bf(rwkv_w_o[0]), h, tm)
    h = _mlp(h, row(norm_ffn_g[1]), bf(mlp_w_up[1]), bf(mlp_w_down[1]), tm, 1024)
    return h.reshape(batch, seq, d)
```

```python
import functools

import jax
import jax.numpy as jnp
from jax import lax
from jax.experimental import pallas as pl
from jax.experimental.pallas import tpu as pltpu

F32 = jnp.float32
BF16 = jnp.bfloat16

RMS_EPS = 1e-6
GN_EPS = 64e-5

HGRN_HEADS = 4
HGRN_DIM = 128
FOX_HEADS = 8
FOX_DIM = 64
RWKV_DIM = 64
RWKV_GROUP = 4
GROUP_LANES = RWKV_GROUP * RWKV_DIM
LANES = 128
NEG_BIG = -1e30

NT_DIMS = (((1,), (1,)), ((), ()))


def _cparams(sem, vmem_mb):
    return pltpu.CompilerParams(dimension_semantics=sem, vmem_limit_bytes=vmem_mb * 1024 * 1024)


def _dot(a, b):
    return jnp.dot(a, b, preferred_element_type=F32)


def _dot_nt(a, b):
    return lax.dot_general(a, b, NT_DIMS, preferred_element_type=F32)


def _rms(x, g):
    return x * lax.rsqrt(jnp.mean(x * x, axis=-1, keepdims=True) + RMS_EPS) * g


def _sigmoid(x):
    return 1.0 / (1.0 + jnp.exp(-x))


def _log_sigmoid(x):
    return jnp.minimum(x, 0.0) - jnp.log(1.0 + jnp.exp(-jnp.abs(x)))


def _tril_mask(n, strict=False):
    r = lax.broadcasted_iota(jnp.int32, (n, n), 0)
    c = lax.broadcasted_iota(jnp.int32, (n, n), 1)
    return (c < r) if strict else (c <= r)


def _split3(x):
    hi = x.astype(BF16)
    r1 = x - hi.astype(F32)
    mid = r1.astype(BF16)
    lo = (r1 - mid.astype(F32)).astype(BF16)
    return hi, mid, lo


def _cumsum_rows(x, tril_bf16):
    hi, mid, lo = _split3(x)
    return _dot(tril_bf16, hi) + _dot(tril_bf16, mid) + _dot(tril_bf16, lo)


def _inproj_kernel(x_ref, g_ref, w_ref, o_ref):
    hn = _rms(x_ref[...], g_ref[...]).astype(BF16)
    o_ref[...] = _dot(hn, w_ref[...])


def _inproj(x2, g, w, tm):
    m, d = x2.shape
    n = w.shape[1]
    return pl.pallas_call(
        _inproj_kernel,
        grid=(m // tm,),
        in_specs=[
            pl.BlockSpec((tm, d), lambda i: (i, 0)),
            pl.BlockSpec((1, d), lambda i: (0, 0)),
            pl.BlockSpec((d, n), lambda i: (0, 0)),
        ],
        out_specs=pl.BlockSpec((tm, n), lambda i: (i, 0)),
        out_shape=jax.ShapeDtypeStruct((m, n), F32),
        compiler_params=_cparams(("parallel",), 48),
        name="inproj",
    )(x2, g, w)


def _hgrn_kernel(q_ref, f_ref, i_ref, g_ref, lb_ref, ng_ref, o_ref, st_ref, *, chunk):
    @pl.when(pl.program_id(1) == 0)
    def _():
        st_ref[...] = jnp.zeros_like(st_ref)

    ts = q_ref.shape[0]
    causal = _tril_mask(chunk)
    tril_b = jnp.where(causal, 1.0, 0.0).astype(BF16)
    mid = chunk // 2

    hs = range(HGRN_HEADS)
    nch = ts // chunk
    tiles = [(c, h) for c in range(nch) for h in hs]
    blk = lambda ref, c, h: ref[c * chunk:(c + 1) * chunk, h * HGRN_DIM:(h + 1) * HGRN_DIM]
    lbs = [lb_ref[:, h * HGRN_DIM:(h + 1) * HGRN_DIM] for h in hs]

    f = [lbs[h] + (1.0 - lbs[h]) * _sigmoid(blk(f_ref, c, h)) for c, h in tiles]
    b = [_cumsum_rows(jnp.log(x), tril_b) for x in f]
    b_mid = [x[mid - 1:mid, :] for x in b]
    b_last = [x[chunk - 1:chunk, :] for x in b]
    q = [blk(q_ref, c, h) * _sigmoid(blk(q_ref, c, h)) for c, h in tiles]
    k = [1.0 - x for x in f]
    vb = [blk(i_ref, c, h).astype(BF16) for c, h in tiles]
    n = range(len(tiles))
    s = [_dot_nt((q[i] * jnp.exp(b[i] - b_mid[i])).astype(BF16),
                 (k[i] * jnp.exp(b_mid[i] - b[i])).astype(BF16)) for i in n]
    o = [_dot(jnp.where(causal, s[i], 0.0).astype(BF16), vb[i]) for i in n]
    inc = [_dot(blk(i_ref, c, h).T.astype(BF16), (k[i] * jnp.exp(b_last[i] - b[i])).astype(BF16))
           for i, (c, h) in enumerate(tiles)]
    dec = [jnp.exp(x) for x in b_last]

    st = [st_ref[h] for h in hs]
    st_in = []
    for i, (c, h) in enumerate(tiles):
        st_in.append(st[h].astype(BF16))
        st[h] = st[h] * dec[i] + inc[i]
    for h in hs:
        st_ref[h] = st[h]

    for i, (c, h) in enumerate(tiles):
        oi = o[i] + _dot_nt((q[i] * jnp.exp(b[i])).astype(BF16), st_in[i])
        ag = blk(g_ref, c, h)
        on = _rms(oi, ng_ref[:, h * HGRN_DIM:(h + 1) * HGRN_DIM])
        o_ref[c * chunk:(c + 1) * chunk, h * HGRN_DIM:(h + 1) * HGRN_DIM] = (
            on * (ag * _sigmoid(ag))).astype(o_ref.dtype)


def _hgrn(proj, lb, ng, batch, seq, ts, chunk):
    m = proj.shape[0]
    w = HGRN_HEADS * HGRN_DIM
    nt = seq // ts
    spec = lambda j: pl.BlockSpec((ts, w), lambda b, t, j=j: (b * nt + t, j))
    vec = pl.BlockSpec((1, w), lambda b, t: (0, 0))
    return pl.pallas_call(
        functools.partial(_hgrn_kernel, chunk=chunk),
        grid=(batch, nt),
        in_specs=[spec(0), spec(1), spec(2), spec(3), vec, vec],
        out_specs=pl.BlockSpec((ts, w), lambda b, t: (b * nt + t, 0)),
        out_shape=jax.ShapeDtypeStruct((m, w), BF16),
        scratch_shapes=[pltpu.VMEM((HGRN_HEADS, HGRN_DIM, HGRN_DIM), F32)],
        compiler_params=_cparams(("parallel", "arbitrary"), 32),
        name="hgrn2",
    )(proj, proj, proj, proj, lb, ng)


def _foxprep_kernel(q_ref, k_ref, v_ref, f_ref, fb_ref, qg_ref, kg_ref,
                    qt_ref, ka_ref, vt_ref, carry_ref):
    @pl.when(pl.program_id(1) == 0)
    def _():
        carry_ref[...] = jnp.zeros_like(carry_ref)

    ts = q_ref.shape[0]
    tril_b = jnp.where(_tril_mask(ts), 1.0, 0.0).astype(BF16)
    lf = _log_sigmoid(f_ref[...] + fb_ref[...])
    c = _cumsum_rows(lf, tril_b) + carry_ref[...]
    carry_ref[...] = c[ts - 1:ts, :]
    pieces = [p.astype(F32) for p in _split3(c)]
    lane = lax.broadcasted_iota(jnp.int32, (ts, LANES), 1)
    ones_q = jnp.where((lane >= FOX_DIM + 3) & (lane < FOX_DIM + 6), 1.0, 0.0)
    ones_k = jnp.where((lane >= FOX_DIM) & (lane < FOX_DIM + 3), 1.0, 0.0)
    zpad = jnp.zeros((ts, LANES - FOX_DIM), F32)
    scale = FOX_DIM ** -0.5
    for h in range(FOX_HEADS):
        cols = slice(h * FOX_DIM, (h + 1) * FOX_DIM)
        qn = _rms(q_ref[:, cols], qg_ref[...]) * scale
        kn = _rms(k_ref[:, cols], kg_ref[...])
        qa = jnp.concatenate([qn, zpad], axis=1) + ones_q
        ka = jnp.concatenate([kn, zpad], axis=1) + ones_k
        for p, piece in enumerate(pieces):
            col = jnp.broadcast_to(piece[:, h:h + 1], (ts, LANES))
            qa = qa + jnp.where(lane == FOX_DIM + p, col, 0.0)
            ka = ka - jnp.where(lane == FOX_DIM + 3 + p, col, 0.0)
        qt_ref[0, h] = qa.T.astype(BF16)
        ka_ref[:, h * LANES:(h + 1) * LANES] = ka.astype(BF16)
    for r in range(FOX_HEADS // 2):
        vt_ref[0, r, 0] = v_ref[:, r * LANES:(r + 1) * LANES].T.astype(BF16)


def _foxprep(proj, fb, qg, kg, batch, seq, ts):
    m = proj.shape[0]
    w = FOX_HEADS * FOX_DIM
    nt = seq // ts
    pairs = FOX_HEADS // 2
    spec = lambda j: pl.BlockSpec((ts, w), lambda b, t, j=j: (b * nt + t, j))
    fcol = (8 * w) // LANES
    return pl.pallas_call(
        _foxprep_kernel,
        grid=(batch, nt),
        in_specs=[spec(4), spec(5), spec(6),
                  pl.BlockSpec((ts, LANES), lambda b, t: (b * nt + t, fcol)),
                  pl.BlockSpec((1, LANES), lambda b, t: (0, 0)),
                  pl.BlockSpec((1, FOX_DIM), lambda b, t: (0, 0)),
                  pl.BlockSpec((1, FOX_DIM), lambda b, t: (0, 0))],
        out_specs=[pl.BlockSpec((1, FOX_HEADS, LANES, ts), lambda b, t: (b, 0, 0, t)),
                   pl.BlockSpec((ts, FOX_HEADS * LANES), lambda b, t: (b * nt + t, 0)),
                   pl.BlockSpec((1, pairs, 1, LANES, ts), lambda b, t: (b, 0, t, 0, 0))],
        out_shape=[jax.ShapeDtypeStruct((batch, FOX_HEADS, LANES, seq), BF16),
                   jax.ShapeDtypeStruct((m, FOX_HEADS * LANES), BF16),
                   jax.ShapeDtypeStruct((batch, pairs, nt, LANES, ts), BF16)],
        scratch_shapes=[pltpu.VMEM((1, LANES), F32)],
        compiler_params=_cparams(("parallel", "arbitrary"), 32),
        name="foxprep",
    )(proj, proj, proj, proj, fb, qg, kg)


def _fox_kernel(qt_ref, k_ref, vt_ref, g_ref, o_ref, *, tq, heads):
    i = pl.program_id(2)
    pairs = heads // 2
    hs = range(heads)
    upper = lax.broadcasted_iota(jnp.int32, (LANES, tq), 0) < FOX_DIM
    row = lax.broadcasted_iota(jnp.int32, (tq, tq), 0)
    col = lax.broadcasted_iota(jnp.int32, (tq, tq), 1)
    visible = col >= row
    qts = [qt_ref[0, h] for h in hs]

    def step(j, carry, masked):
        ms, ls, accs = carry
        rows = pl.ds(pl.multiple_of(j * tq, tq), tq)
        s = [_dot(k_ref[rows, h * LANES:(h + 1) * LANES], qts[h]) for h in hs]
        if masked:
            s = [jnp.where(visible, x, NEG_BIG) for x in s]
        m_new = [jnp.maximum(ms[h], jnp.max(s[h], axis=0, keepdims=True)) for h in hs]
        alpha = [jnp.exp(ms[h] - m_new[h]) for h in hs]
        p = [jnp.exp(s[h] - m_new[h]) for h in hs]
        l_new = [alpha[h] * ls[h] + jnp.sum(p[h], axis=0, keepdims=True) for h in hs]
        pv = [_dot(vt_ref[0, h // 2, j], p[h].astype(BF16)) for h in hs]
        acc_new = [accs[r] * jnp.where(upper, alpha[2 * r], alpha[2 * r + 1])
                   + jnp.where(upper, pv[2 * r], pv[2 * r + 1]) for r in range(pairs)]
        return tuple(m_new), tuple(l_new), tuple(acc_new)

    neg = jnp.full((1, tq), NEG_BIG, F32)
    zero = jnp.zeros((1, tq), F32)
    init = ((neg,) * heads, (zero,) * heads, (jnp.zeros((LANES, tq), F32),) * pairs)
    carry = lax.fori_loop(0, i, lambda j, c: step(j, c, False), init)
    _, ls, accs = step(i, carry, True)
    for r in range(pairs):
        cols = slice(r * LANES, (r + 1) * LANES)
        out = (accs[r] / jnp.where(upper, ls[2 * r], ls[2 * r + 1])).T * _sigmoid(g_ref[:, cols])
        o_ref[:, cols] = out.astype(o_ref.dtype)


def _fox(qt, ka, vt, proj, batch, seq, tq, heads):
    m = ka.shape[0]
    nq = seq // tq
    groups = FOX_HEADS // heads
    pairs = heads // 2
    wv = heads * FOX_DIM
    gcol = (7 * FOX_HEADS * FOX_DIM) // wv
    return pl.pallas_call(
        functools.partial(_fox_kernel, tq=tq, heads=heads),
        grid=(batch, groups, nq),
        in_specs=[pl.BlockSpec((1, heads, LANES, tq), lambda b, p, i: (b, p, 0, i)),
                  pl.BlockSpec((seq, heads * LANES), lambda b, p, i: (b, p)),
                  pl.BlockSpec((1, pairs, nq, LANES, tq), lambda b, p, i: (b, p, 0, 0, 0)),
                  pl.BlockSpec((tq, wv), lambda b, p, i: (b * nq + i, gcol + p))],
        out_specs=pl.BlockSpec((tq, wv), lambda b, p, i: (b * nq + i, p)),
        out_shape=jax.ShapeDtypeStruct((m, FOX_HEADS * FOX_DIM), BF16),
        compiler_params=_cparams(("parallel", "parallel", "arbitrary"), 48),
        name="fox_attention",
    )(qt, ka, vt, proj)


def _outproj2_kernel(ya_ref, yb_ref, wa_ref, wb_ref, h_ref, o_ref):
    o_ref[...] = h_ref[...] + _dot(ya_ref[...], wa_ref[...]) + _dot(yb_ref[...], wb_ref[...])


def _outproj2(ya, yb, wa, wb, h, tm):
    m, d = h.shape
    ka, kb = ya.shape[1], yb.shape[1]
    return pl.pallas_call(
        _outproj2_kernel,
        grid=(m // tm,),
        in_specs=[pl.BlockSpec((tm, ka), lambda i: (i, 0)),
                  pl.BlockSpec((tm, kb), lambda i: (i, 0)),
                  pl.BlockSpec((ka, d), lambda i: (0, 0)),
                  pl.BlockSpec((kb, d), lambda i: (0, 0)),
                  pl.BlockSpec((tm, d), lambda i: (i, 0))],
        out_specs=pl.BlockSpec((tm, d), lambda i: (i, 0)),
        out_shape=jax.ShapeDtypeStruct((m, d), F32),
        compiler_params=_cparams(("parallel",), 32),
        name="outproj_mix",
    )(ya, yb, wa, wb, h)


def _outproj1_kernel(y_ref, w_ref, h_ref, o_ref):
    o_ref[...] = h_ref[...] + _dot(y_ref[...], w_ref[...])


def _outproj1(y, w, h, tm):
    m, d = h.shape
    k = y.shape[1]
    return pl.pallas_call(
        _outproj1_kernel,
        grid=(m // tm,),
        in_specs=[pl.BlockSpec((tm, k), lambda i: (i, 0)),
                  pl.BlockSpec((k, d), lambda i: (0, 0)),
                  pl.BlockSpec((tm, d), lambda i: (i, 0))],
        out_specs=pl.BlockSpec((tm, d), lambda i: (i, 0)),
        out_shape=jax.ShapeDtypeStruct((m, d), F32),
        compiler_params=_cparams(("parallel",), 32),
        name="outproj_rwkv",
    )(y, w, h)


def _mlp_kernel(h_ref, g_ref, wu_ref, wd_ref, o_ref, *, ck):
    x = h_ref[...]
    hn = _rms(x, g_ref[...]).astype(BF16)
    acc = x
    for c in range(wu_ref.shape[1] // ck):
        u = jnp.maximum(_dot(hn, wu_ref[:, c * ck:(c + 1) * ck]), 0.0)
        acc = acc + _dot((u * u).astype(BF16), wd_ref[c * ck:(c + 1) * ck, :])
    o_ref[...] = acc


def _mlp(h, g, wu, wd, tm, ck):
    m, d = h.shape
    dff = wu.shape[1]
    return pl.pallas_call(
        functools.partial(_mlp_kernel, ck=ck),
        grid=(m // tm,),
        in_specs=[pl.BlockSpec((tm, d), lambda i: (i, 0)),
                  pl.BlockSpec((1, d), lambda i: (0, 0)),
                  pl.BlockSpec((d, dff), lambda i: (0, 0)),
                  pl.BlockSpec((dff, d), lambda i: (0, 0))],
        out_specs=pl.BlockSpec((tm, d), lambda i: (i, 0)),
        out_shape=jax.ShapeDtypeStruct((m, d), F32),
        compiler_params=_cparams(("parallel",), 56),
        name="mlp",
    )(h, g, wu, wd)


def _rwkvproj_kernel(h_ref, hp_ref, g_ref, mu_ref, wr_ref, wk_ref, wv_ref, w1_ref, w2_ref,
                     a1_ref, a2_ref, g1_ref, g2_ref, w0_ref, a0_ref, kk_ref, ka_ref,
                     r_ref, lw_ref, km_ref, v_ref, kr_ref, a_ref, go_ref, *, tiles_per_seq):
    i = pl.program_id(0)
    tm = h_ref.shape[0]
    gn = g_ref[...]
    hn = _rms(h_ref[...], gn)
    prev = _rms(hp_ref[7:8, :], gn)
    prev = jnp.where(i % tiles_per_seq == 0, jnp.zeros_like(prev), prev)
    row = lax.broadcasted_iota(jnp.int32, hn.shape, 0)
    shifted = jnp.where(row == 0, jnp.broadcast_to(prev, hn.shape), pltpu.roll(hn, 1, 0))
    xx = shifted - hn
    mix = lambda j: (hn + xx * mu_ref[j:j + 1, :]).astype(BF16)
    r = _dot(mix(0), wr_ref[...])
    k = _dot(mix(2), wk_ref[...])
    v = _dot(mix(3), wv_ref[...])
    z = w0_ref[...] + _dot(jnp.tanh(_dot(mix(1), w1_ref[...])).astype(BF16), w2_ref[...])
    a = _sigmoid(a0_ref[...] + _dot(_dot(mix(4), a1_ref[...]).astype(BF16), a2_ref[...]))
    g = _dot(_sigmoid(_dot(mix(5), g1_ref[...])).astype(BF16), g2_ref[...])
    r_ref[...] = r
    lw_ref[...] = -jnp.exp(_log_sigmoid(z) - 0.5)
    km_ref[...] = k * (1.0 + (a - 1.0) * ka_ref[...])
    v_ref[...] = v
    kr_ref[...] = k * kk_ref[...]
    a_ref[...] = a
    go_ref[...] = g


def _rwkvproj(h, g, mu, wr, wk, wv, w1, w2, a1, a2, g1, g2, w0, a0, k_k, k_a, seq, tm):
    m, d = h.shape
    tiles_per_seq = seq // tm
    full = lambda a: pl.BlockSpec(a.shape, lambda i: (0,) * a.ndim)
    row = pl.BlockSpec((tm, d), lambda i: (i, 0))
    prev = pl.BlockSpec((8, d), lambda i: (jnp.maximum(i * (tm // 8) - 1, 0), 0))
    consts = (g, mu, wr, wk, wv, w1, w2, a1, a2, g1, g2, w0, a0, k_k, k_a)
    return pl.pallas_call(
        functools.partial(_rwkvproj_kernel, tiles_per_seq=tiles_per_seq),
        grid=(m // tm,),
        in_specs=[row, prev] + [full(a) for a in consts],
        out_specs=[row] * 7,
        out_shape=[jax.ShapeDtypeStruct((m, d), F32)] * 7,
        compiler_params=_cparams(("parallel",), 56),
        name="rwkv_proj",
    )(h, h, *consts)


def _wkv_kernel(r_ref, lw_ref, km_ref, v_ref, kr_ref, a_ref, g_ref, rk_ref, lg_ref, lb_ref,
                o_ref, st_ref, y_ref, q1_ref, r2_ref, pc_ref, z0_ref, wc_ref, *, chunk):
    @pl.when(pl.program_id(2) == 0)
    def _():
        st_ref[...] = jnp.zeros_like(st_ref)

    ts = r_ref.shape[0]
    nch = ts // chunk
    gl = GROUP_LANES
    rb = lax.broadcasted_iota(jnp.int32, (gl, gl), 0) // RWKV_DIM
    cb = lax.broadcasted_iota(jnp.int32, (gl, gl), 1) // RWKV_DIM
    blockmask = rb == cb
    ones_bd = jnp.where(blockmask, 1.0, 0.0).astype(BF16)

    def headsum(x):
        hi = x.astype(BF16)
        lo = (x - hi.astype(F32)).astype(BF16)
        return _dot(hi, ones_bd) + _dot(lo, ones_bd)

    def bd(y):
        reps = gl // y.shape[0]
        return jnp.where(blockmask, jnp.concatenate([y] * reps, axis=0), 0.0).astype(BF16)

    def hmm(x, y):
        return _dot(x.astype(BF16), bd(y))

    def tn_blocks(x, y):
        return jnp.where(blockmask, _dot(x.T.astype(BF16), y.astype(BF16)), 0.0)

    t_idx = lax.broadcasted_iota(jnp.int32, (chunk, gl), 0)
    s_idx = lax.broadcasted_iota(jnp.int32, (chunk, gl), 1) % RWKV_DIM
    strict = s_idx < t_idx
    incl = s_idx <= t_idx
    tril_b = jnp.where(_tril_mask(chunk), 1.0, 0.0).astype(BF16)
    zeros_c = jnp.zeros((chunk, gl), F32)

    kr = kr_ref[...]
    kkn_all = kr * lax.rsqrt(jnp.maximum(headsum(kr * kr), 1e-24))

    chunks = range(nch)
    rows_of = lambda c: slice(c * chunk, (c + 1) * chunk)
    pad = lambda x: jnp.concatenate([x, zeros_c], axis=0)
    each = lambda fn, *lists: [fn(*args) for args in zip(*lists)]

    lw = [lw_ref[rows_of(c), :] for c in chunks]
    cum = each(lambda x: _cumsum_rows(x, tril_b), lw)
    c_last = [x[chunk - 1:chunk, :] for x in cum]
    kkn = [kkn_all[rows_of(c), :] for c in chunks]
    kka = [kkn[c] * a_ref[rows_of(c), :] for c in chunks]
    km = [km_ref[rows_of(c), :] for c in chunks]
    v = [v_ref[rows_of(c), :] for c in chunks]
    e_neg = [jnp.exp(-x) for x in cum]
    e_end = each(lambda cl, x: jnp.exp(cl - x), c_last, cum)
    at = each(lambda k, x, l: -k * jnp.exp(x - l), kkn, cum, lw)
    bt = each(jnp.multiply, kka, e_neg)
    kt = each(jnp.multiply, km, e_neg)
    rt = [r_ref[rows_of(c), :] * jnp.exp(cum[c]) for c in chunks]
    bw = each(jnp.multiply, kka, e_end)
    kw = each(jnp.multiply, km, e_end)
    for c in chunks:
        wc_ref[c] = jnp.broadcast_to(jnp.exp(c_last[c]), (8, gl))

    lhs = each(lambda a, r: jnp.concatenate([a, r], axis=0).astype(BF16), at, rt)
    pb = each(lambda l, b: _dot_nt(l, bd(b)), lhs, bt)
    pk = each(lambda l, k: _dot_nt(l, bd(k)), lhs, kt)
    a_ab = [jnp.where(strict, x[:chunk], 0.0) for x in pb]
    a_rb = [jnp.where(incl, x[chunk:], 0.0) for x in pb]
    a_ak = [jnp.where(strict, x[:chunk], 0.0) for x in pk]
    a_rk = [jnp.where(incl, x[chunk:], 0.0) for x in pk]

    e = [jnp.where((t_idx % 2 == 1) & (s_idx == t_idx - 1), x, 0.0) for x in a_ab]
    size = 2
    while size < chunk:
        off = ((t_idx // size) % 2 == 1) & (s_idx // size == t_idx // size - 1)
        a_off = [jnp.where(off, x, 0.0) for x in a_ab]
        t1 = each(lambda ao, ee: ao + hmm(ao, ee), a_off, e)
        e = each(lambda ee, tt: ee + tt + hmm(ee, tt), e, t1)
        size *= 2

    akv = each(hmm, a_ak, v)
    p1 = each(lambda x, ee: x + hmm(ee, x), akv, e)
    mat = each(lambda x, ee: x + hmm(ee, x), at, e)
    q1 = each(lambda ark, vv, arb, pp: hmm(ark, vv) + hmm(arb, pp), a_rk, v, a_rb, p1)
    r2 = each(lambda r, arb, mm: r + hmm(arb, mm), rt, a_rb, mat)
    pc = each(lambda mm, b: tn_blocks(pad(mm), pad(b)), mat, bw)
    z0 = each(lambda pp, vv, b, k: tn_blocks(jnp.concatenate([pp, vv], axis=0),
                                             jnp.concatenate([b, k], axis=0)), p1, v, bw, kw)
    for c in chunks:
        q1_ref[rows_of(c), :] = q1[c]
        r2_ref[rows_of(c), :] = r2[c].astype(BF16)
        pc_ref[c] = pc[c].astype(BF16)
        z0_ref[c] = z0[c]

    def body(c, carry):
        rows = pl.ds(pl.multiple_of(c * chunk, chunk), chunk)
        s = st_ref[...]
        sb = s.astype(BF16)
        y_ref[rows, :] = q1_ref[rows, :] + _dot_nt(r2_ref[rows, :], sb)
        st_ref[...] = s * wc_ref[c][0:1, :] + _dot(sb, pc_ref[c]) + z0_ref[c]
        return carry

    lax.fori_loop(0, nch, body, 0)

    y = y_ref[...]
    inv_n = 1.0 / RWKV_DIM
    mean = headsum(y) * inv_n
    dlt = y - mean
    var = headsum(dlt * dlt) * inv_n
    yn = dlt * lax.rsqrt(var + GN_EPS) * lg_ref[...] + lb_ref[...]
    bonus = headsum(r_ref[...] * km_ref[...] * rk_ref[...]) * v_ref[...]
    o_ref[...] = ((yn + bonus) * g_ref[...]).astype(o_ref.dtype)


def _wkv(r, lw, km, v, kr, a, g, r_k, lnx_g, lnx_b, batch, seq, ts, chunk):
    m, d = r.shape
    gl = GROUP_LANES
    ngroups = d // gl
    nt = seq // ts
    nch = ts // chunk
    row = pl.BlockSpec((ts, gl), lambda b, j, t: (b * nt + t, j))
    vec = pl.BlockSpec((1, gl), lambda b, j, t: (0, j))
    return pl.pallas_call(
        functools.partial(_wkv_kernel, chunk=chunk),
        grid=(batch, ngroups, nt),
        in_specs=[row] * 7 + [vec] * 3,
        out_specs=row,
        out_shape=jax.ShapeDtypeStruct((m, d), BF16),
        scratch_shapes=[pltpu.VMEM((gl, gl), F32),
                        pltpu.VMEM((ts, gl), F32),
                        pltpu.VMEM((ts, gl), F32),
                        pltpu.VMEM((ts, gl), BF16),
                        pltpu.VMEM((nch, gl, gl), BF16),
                        pltpu.VMEM((nch, gl, gl), F32),
                        pltpu.VMEM((nch, 8, gl), F32)],
        compiler_params=_cparams(("parallel", "parallel", "arbitrary"), 48),
        name="wkv7",
    )(r, lw, km, v, kr, a, g, r_k, lnx_g, lnx_b)


def kernel(x, norm_mix_g, norm_ffn_g, ab_w_in, hgrn_lower_bounds, hgrn_norm_g, fox_forget_bias,
           fox_q_norm_g, fox_k_norm_g, ab_w_out, rwkv_mu, rwkv_w_rkv, rwkv_w0, rwkv_w1, rwkv_w2,
           rwkv_a0, rwkv_a1, rwkv_a2, rwkv_g1, rwkv_g2, rwkv_k_k, rwkv_k_a, rwkv_r_k,
           rwkv_lnx_g, rwkv_lnx_b, rwkv_w_o, mlp_w_up, mlp_w_down):
    batch, seq, d = x.shape
    m = batch * seq
    tm = min(256, seq)
    ts = min(512, seq)
    tq = min(512, seq)
    row = lambda a: a.reshape(1, -1).astype(F32)
    bf = lambda a: a.astype(BF16)

    lb_all = jnp.cumsum(jax.nn.softmax(hgrn_lower_bounds.astype(F32), axis=0), axis=0)
    h = x.reshape(m, d)

    n_in = ab_w_in.shape[-1]
    n_pad = (-n_in) % LANES
    w_in = bf(jnp.pad(ab_w_in[0], ((0, 0), (0, n_pad))))
    proj = _inproj(h, row(norm_mix_g[0]), w_in, tm)
    ya = _hgrn(proj, row(lb_all[0]), row(hgrn_norm_g[0]), batch, seq, ts, 64)
    fb = jnp.pad(row(fox_forget_bias[0]), ((0, 0), (0, LANES - FOX_HEADS)))
    qa, ka, vb = _foxprep(proj, fb, row(fox_q_norm_g[0]), row(fox_k_norm_g[0]), batch, seq, ts)
    yb = _fox(qa, ka, vb, proj, batch, seq, tq, 4)
    wa = bf(ab_w_out[0][:HGRN_HEADS * HGRN_DIM])
    wb = bf(ab_w_out[0][HGRN_HEADS * HGRN_DIM:])
    h = _outproj2(ya, yb, wa, wb, h, tm)
    h = _mlp(h, row(norm_ffn_g[0]), bf(mlp_w_up[0]), bf(mlp_w_down[0]), tm, 1024)

    outs = _rwkvproj(h, row(norm_mix_g[1]), rwkv_mu[0].astype(F32),
                     bf(rwkv_w_rkv[0, 0]), bf(rwkv_w_rkv[0, 1]), bf(rwkv_w_rkv[0, 2]),
                     bf(rwkv_w1[0]), bf(rwkv_w2[0]), bf(rwkv_a1[0]), bf(rwkv_a2[0]),
                     bf(rwkv_g1[0]), bf(rwkv_g2[0]), row(rwkv_w0[0]), row(rwkv_a0[0]),
                     row(rwkv_k_k[0]), row(rwkv_k_a[0]), seq, tm)
    z = _wkv(*outs, row(rwkv_r_k[0]), row(rwkv_lnx_g[0]), row(rwkv_lnx_b[0]),
             batch, seq, min(512, seq), 64)
    h = _outproj1(z, bf(rwkv_w_o[0]), h, tm)
    h = _mlp(h, row(norm_ffn_g[1]), bf(mlp_w_up[1]), bf(mlp_w_down[1]), tm, 1024)
    return h.reshape(batch, seq, d)
```

```python
import functools

import jax
import jax.numpy as jnp
from jax import lax
from jax.experimental import pallas as pl
from jax.experimental.pallas import tpu as pltpu

F32 = jnp.float32
BF16 = jnp.bfloat16

RMS_EPS = 1e-6
GN_EPS = 64e-5

HGRN_HEADS = 4
HGRN_DIM = 128
FOX_HEADS = 8
FOX_DIM = 64
RWKV_DIM = 64
RWKV_GROUP = 4
GROUP_LANES = RWKV_GROUP * RWKV_DIM
LANES = 128
NEG_BIG = -1e30

NT_DIMS = (((1,), (1,)), ((), ()))


def _cparams(sem, vmem_mb):
    return pltpu.CompilerParams(dimension_semantics=sem, vmem_limit_bytes=vmem_mb * 1024 * 1024)


def _dot(a, b):
    return jnp.dot(a, b, preferred_element_type=F32)


def _dot_nt(a, b):
    return lax.dot_general(a, b, NT_DIMS, preferred_element_type=F32)


def _rms(x, g):
    return x * lax.rsqrt(jnp.mean(x * x, axis=-1, keepdims=True) + RMS_EPS) * g


def _sigmoid(x):
    return 1.0 / (1.0 + jnp.exp(-x))


def _log_sigmoid(x):
    return jnp.minimum(x, 0.0) - jnp.log(1.0 + jnp.exp(-jnp.abs(x)))


def _tril_mask(n, strict=False):
    r = lax.broadcasted_iota(jnp.int32, (n, n), 0)
    c = lax.broadcasted_iota(jnp.int32, (n, n), 1)
    return (c < r) if strict else (c <= r)


def _split3(x):
    hi = x.astype(BF16)
    r1 = x - hi.astype(F32)
    mid = r1.astype(BF16)
    lo = (r1 - mid.astype(F32)).astype(BF16)
    return hi, mid, lo


def _cumsum_rows(x, tril_bf16):
    hi, mid, lo = _split3(x)
    return _dot(tril_bf16, hi) + _dot(tril_bf16, mid) + _dot(tril_bf16, lo)


def _inproj_kernel(x_ref, g_ref, w_ref, o_ref):
    hn = _rms(x_ref[...], g_ref[...]).astype(BF16)
    o_ref[...] = _dot(hn, w_ref[...])


def _inproj(x2, g, w, tm):
    m, d = x2.shape
    n = w.shape[1]
    return pl.pallas_call(
        _inproj_kernel,
        grid=(m // tm,),
        in_specs=[
            pl.BlockSpec((tm, d), lambda i: (i, 0)),
            pl.BlockSpec((1, d), lambda i: (0, 0)),
            pl.BlockSpec((d, n), lambda i: (0, 0)),
        ],
        out_specs=pl.BlockSpec((tm, n), lambda i: (i, 0)),
        out_shape=jax.ShapeDtypeStruct((m, n), F32),
        compiler_params=_cparams(("parallel",), 48),
        name="inproj",
    )(x2, g, w)


def _hgrn_kernel(q_ref, f_ref, i_ref, g_ref, lb_ref, ng_ref, o_ref, st_ref, *, chunk):
    @pl.when(pl.program_id(1) == 0)
    def _():
        st_ref[...] = jnp.zeros_like(st_ref)

    ts = q_ref.shape[0]
    causal = _tril_mask(chunk)
    tril_b = jnp.where(causal, 1.0, 0.0).astype(BF16)
    mid = chunk // 2

    hs = range(HGRN_HEADS)
    nch = ts // chunk
    tiles = [(c, h) for c in range(nch) for h in hs]
    blk = lambda ref, c, h: ref[c * chunk:(c + 1) * chunk, h * HGRN_DIM:(h + 1) * HGRN_DIM]
    lbs = [lb_ref[:, h * HGRN_DIM:(h + 1) * HGRN_DIM] for h in hs]

    f = [lbs[h] + (1.0 - lbs[h]) * _sigmoid(blk(f_ref, c, h)) for c, h in tiles]
    b = [_cumsum_rows(jnp.log(x), tril_b) for x in f]
    b_mid = [x[mid - 1:mid, :] for x in b]
    b_last = [x[chunk - 1:chunk, :] for x in b]
    q = [blk(q_ref, c, h) * _sigmoid(blk(q_ref, c, h)) for c, h in tiles]
    k = [1.0 - x for x in f]
    vb = [blk(i_ref, c, h).astype(BF16) for c, h in tiles]
    n = range(len(tiles))
    s = [_dot_nt((q[i] * jnp.exp(b[i] - b_mid[i])).astype(BF16),
                 (k[i] * jnp.exp(b_mid[i] - b[i])).astype(BF16)) for i in n]
    o = [_dot(jnp.where(causal, s[i], 0.0).astype(BF16), vb[i]) for i in n]
    inc = [_dot(blk(i_ref, c, h).T.astype(BF16), (k[i] * jnp.exp(b_last[i] - b[i])).astype(BF16))
           for i, (c, h) in enumerate(tiles)]
    dec = [jnp.exp(x) for x in b_last]

    st = [st_ref[h] for h in hs]
    st_in = []
    for i, (c, h) in enumerate(tiles):
        st_in.append(st[h].astype(BF16))
        st[h] = st[h] * dec[i] + inc[i]
    for h in hs:
        st_ref[h] = st[h]

    for i, (c, h) in enumerate(tiles):
        oi = o[i] + _dot_nt((q[i] * jnp.exp(b[i])).astype(BF16), st_in[i])
        ag = blk(g_ref, c, h)
        on = _rms(oi, ng_ref[:, h * HGRN_DIM:(h + 1) * HGRN_DIM])
        o_ref[c * chunk:(c + 1) * chunk, h * HGRN_DIM:(h + 1) * HGRN_DIM] = (
            on * (ag * _sigmoid(ag))).astype(o_ref.dtype)


def _hgrn(proj, lb, ng, batch, seq, ts, chunk):
    m = proj.shape[0]
    w = HGRN_HEADS * HGRN_DIM
    nt = seq // ts
    spec = lambda j: pl.BlockSpec((ts, w), lambda b, t, j=j: (b * nt + t, j))
    vec = pl.BlockSpec((1, w), lambda b, t: (0, 0))
    return pl.pallas_call(
        functools.partial(_hgrn_kernel, chunk=chunk),
        grid=(batch, nt),
        in_specs=[spec(0), spec(1), spec(2), spec(3), vec, vec],
        out_specs=pl.BlockSpec((ts, w), lambda b, t: (b * nt + t, 0)),
        out_shape=jax.ShapeDtypeStruct((m, w), BF16),
        scratch_shapes=[pltpu.VMEM((HGRN_HEADS, HGRN_DIM, HGRN_DIM), F32)],
        compiler_params=_cparams(("parallel", "arbitrary"), 32),
        name="hgrn2",
    )(proj, proj, proj, proj, lb, ng)


def _foxprep_kernel(q_ref, k_ref, v_ref, f_ref, fb_ref, qg_ref, kg_ref,
                    qt_ref, ka_ref, vt_ref, carry_ref):
    @pl.when(pl.program_id(1) == 0)
    def _():
        carry_ref[...] = jnp.zeros_like(carry_ref)

    ts = q_ref.shape[0]
    tril_b = jnp.where(_tril_mask(ts), 1.0, 0.0).astype(BF16)
    lf = _log_sigmoid(f_ref[...] + fb_ref[...])
    c = _cumsum_rows(lf, tril_b) + carry_ref[...]
    carry_ref[...] = c[ts - 1:ts, :]
    pieces = [p.astype(F32) for p in _split3(c)]
    lane = lax.broadcasted_iota(jnp.int32, (ts, LANES), 1)
    ones_q = jnp.where((lane >= FOX_DIM + 3) & (lane < FOX_DIM + 6), 1.0, 0.0)
    ones_k = jnp.where((lane >= FOX_DIM) & (lane < FOX_DIM + 3), 1.0, 0.0)
    zpad = jnp.zeros((ts, LANES - FOX_DIM), F32)
    scale = FOX_DIM ** -0.5
    for h in range(FOX_HEADS):
        cols = slice(h * FOX_DIM, (h + 1) * FOX_DIM)
        qn = _rms(q_ref[:, cols], qg_ref[...]) * scale
        kn = _rms(k_ref[:, cols], kg_ref[...])
        qa = jnp.concatenate([qn, zpad], axis=1) + ones_q
        ka = jnp.concatenate([kn, zpad], axis=1) + ones_k
        for p, piece in enumerate(pieces):
            col = jnp.broadcast_to(piece[:, h:h + 1], (ts, LANES))
            qa = qa + jnp.where(lane == FOX_DIM + p, col, 0.0)
            ka = ka - jnp.where(lane == FOX_DIM + 3 + p, col, 0.0)
        qt_ref[0, h] = qa.T.astype(BF16)
        ka_ref[:, h * LANES:(h + 1) * LANES] = ka.astype(BF16)
    for r in range(FOX_HEADS // 2):
        vt_ref[0, r, 0] = v_ref[:, r * LANES:(r + 1) * LANES].T.astype(BF16)


def _foxprep(proj, fb, qg, kg, batch, seq, ts):
    m = proj.shape[0]
    w = FOX_HEADS * FOX_DIM
    nt = seq // ts
    pairs = FOX_HEADS // 2
    spec = lambda j: pl.BlockSpec((ts, w), lambda b, t, j=j: (b * nt + t, j))
    fcol = (8 * w) // LANES
    return pl.pallas_call(
        _foxprep_kernel,
        grid=(batch, nt),
        in_specs=[spec(4), spec(5), spec(6),
                  pl.BlockSpec((ts, LANES), lambda b, t: (b * nt + t, fcol)),
                  pl.BlockSpec((1, LANES), lambda b, t: (0, 0)),
                  pl.BlockSpec((1, FOX_DIM), lambda b, t: (0, 0)),
                  pl.BlockSpec((1, FOX_DIM), lambda b, t: (0, 0))],
        out_specs=[pl.BlockSpec((1, FOX_HEADS, LANES, ts), lambda b, t: (b, 0, 0, t)),
                   pl.BlockSpec((ts, FOX_HEADS * LANES), lambda b, t: (b * nt + t, 0)),
                   pl.BlockSpec((1, pairs, 1, LANES, ts), lambda b, t: (b, 0, t, 0, 0))],
        out_shape=[jax.ShapeDtypeStruct((batch, FOX_HEADS, LANES, seq), BF16),
                   jax.ShapeDtypeStruct((m, FOX_HEADS * LANES), BF16),
                   jax.ShapeDtypeStruct((batch, pairs, nt, LANES, ts), BF16)],
        scratch_shapes=[pltpu.VMEM((1, LANES), F32)],
        compiler_params=_cparams(("parallel", "arbitrary"), 32),
        name="foxprep",
    )(proj, proj, proj, proj, fb, qg, kg)


def _fox_kernel(qt_ref, k_ref, vt_ref, g_ref, o_ref, *, tq, heads):
    i = pl.program_id(2)
    pairs = heads // 2
    hs = range(heads)
    upper = lax.broadcasted_iota(jnp.int32, (LANES, tq), 0) < FOX_DIM
    row = lax.broadcasted_iota(jnp.int32, (tq, tq), 0)
    col = lax.broadcasted_iota(jnp.int32, (tq, tq), 1)
    visible = col >= row
    qts = [qt_ref[0, h] for h in hs]

    def step(j, carry, masked):
        ms, ls, accs = carry
        rows = pl.ds(pl.multiple_of(j * tq, tq), tq)
        s = [_dot(k_ref[rows, h * LANES:(h + 1) * LANES], qts[h]) for h in hs]
        if masked:
            s = [jnp.where(visible, x, NEG_BIG) for x in s]
        m_new = [jnp.maximum(ms[h], jnp.max(s[h], axis=0, keepdims=True)) for h in hs]
        alpha = [jnp.exp(ms[h] - m_new[h]) for h in hs]
        p = [jnp.exp(s[h] - m_new[h]) for h in hs]
        l_new = [alpha[h] * ls[h] + jnp.sum(p[h], axis=0, keepdims=True) for h in hs]
        pv = [_dot(vt_ref[0, h // 2, j], p[h].astype(BF16)) for h in hs]
        acc_new = [accs[r] * jnp.where(upper, alpha[2 * r], alpha[2 * r + 1])
                   + jnp.where(upper, pv[2 * r], pv[2 * r + 1]) for r in range(pairs)]
        return tuple(m_new), tuple(l_new), tuple(acc_new)

    neg = jnp.full((1, tq), NEG_BIG, F32)
    zero = jnp.zeros((1, tq), F32)
    init = ((neg,) * heads, (zero,) * heads, (jnp.zeros((LANES, tq), F32),) * pairs)
    carry = lax.fori_loop(0, i, lambda j, c: step(j, c, False), init)
    _, ls, accs = step(i, carry, True)
    for r in range(pairs):
        cols = slice(r * LANES, (r + 1) * LANES)
        out = (accs[r] / jnp.where(upper, ls[2 * r], ls[2 * r + 1])).T * _sigmoid(g_ref[:, cols])
        o_ref[:, cols] = out.astype(o_ref.dtype)


def _fox(qt, ka, vt, proj, batch, seq, tq, heads):
    m = ka.shape[0]
    nq = seq // tq
    groups = FOX_HEADS // heads
    pairs = heads // 2
    wv = heads * FOX_DIM
    gcol = (7 * FOX_HEADS * FOX_DIM) // wv
    return pl.pallas_call(
        functools.partial(_fox_kernel, tq=tq, heads=heads),
        grid=(batch, groups, nq),
        in_specs=[pl.BlockSpec((1, heads, LANES, tq), lambda b, p, i: (b, p, 0, i)),
                  pl.BlockSpec((seq, heads * LANES), lambda b, p, i: (b, p)),
                  pl.BlockSpec((1, pairs, nq, LANES, tq), lambda b, p, i: (b, p, 0, 0, 0)),
                  pl.BlockSpec((tq, wv), lambda b, p, i: (b * nq + i, gcol + p))],
        out_specs=pl.BlockSpec((tq, wv), lambda b, p, i: (b * nq + i, p)),
        out_shape=jax.ShapeDtypeStruct((m, FOX_HEADS * FOX_DIM), BF16),
        compiler_params=_cparams(("parallel", "parallel", "arbitrary"), 48),
        name="fox_attention",
    )(qt, ka, vt, proj)


def _outproj2_kernel(ya_ref, yb_ref, wa_ref, wb_ref, h_ref, o_ref):
    o_ref[...] = h_ref[...] + _dot(ya_ref[...], wa_ref[...]) + _dot(yb_ref[...], wb_ref[...])


def _outproj2(ya, yb, wa, wb, h, tm):
    m, d = h.shape
    ka, kb = ya.shape[1], yb.shape[1]
    return pl.pallas_call(
        _outproj2_kernel,
        grid=(m // tm,),
        in_specs=[pl.BlockSpec((tm, ka), lambda i: (i, 0)),
                  pl.BlockSpec((tm, kb), lambda i: (i, 0)),
                  pl.BlockSpec((ka, d), lambda i: (0, 0)),
                  pl.BlockSpec((kb, d), lambda i: (0, 0)),
                  pl.BlockSpec((tm, d), lambda i: (i, 0))],
        out_specs=pl.BlockSpec((tm, d), lambda i: (i, 0)),
        out_shape=jax.ShapeDtypeStruct((m, d), F32),
        compiler_params=_cparams(("parallel",), 32),
        name="outproj_mix",
    )(ya, yb, wa, wb, h)


def _outproj1_kernel(y_ref, w_ref, h_ref, o_ref):
    o_ref[...] = h_ref[...] + _dot(y_ref[...], w_ref[...])


def _outproj1(y, w, h, tm):
    m, d = h.shape
    k = y.shape[1]
    return pl.pallas_call(
        _outproj1_kernel,
        grid=(m // tm,),
        in_specs=[pl.BlockSpec((tm, k), lambda i: (i, 0)),
                  pl.BlockSpec((k, d), lambda i: (0, 0)),
                  pl.BlockSpec((tm, d), lambda i: (i, 0))],
        out_specs=pl.BlockSpec((tm, d), lambda i: (i, 0)),
        out_shape=jax.ShapeDtypeStruct((m, d), F32),
        compiler_params=_cparams(("parallel",), 32),
        name="outproj_rwkv",
    )(y, w, h)


def _mlp_kernel(h_ref, g_ref, wu_ref, wd_ref, o_ref, *, ck):
    x = h_ref[...]
    hn = _rms(x, g_ref[...]).astype(BF16)
    acc = x
    for c in range(wu_ref.shape[1] // ck):
        u = jnp.maximum(_dot(hn, wu_ref[:, c * ck:(c + 1) * ck]), 0.0)
        acc = acc + _dot((u * u).astype(BF16), wd_ref[c * ck:(c + 1) * ck, :])
    o_ref[...] = acc


def _mlp(h, g, wu, wd, tm, ck):
    m, d = h.shape
    dff = wu.shape[1]
    return pl.pallas_call(
        functools.partial(_mlp_kernel, ck=ck),
        grid=(m // tm,),
        in_specs=[pl.BlockSpec((tm, d), lambda i: (i, 0)),
                  pl.BlockSpec((1, d), lambda i: (0, 0)),
                  pl.BlockSpec((d, dff), lambda i: (0, 0)),
                  pl.BlockSpec((dff, d), lambda i: (0, 0))],
        out_specs=pl.BlockSpec((tm, d), lambda i: (i, 0)),
        out_shape=jax.ShapeDtypeStruct((m, d), F32),
        compiler_params=_cparams(("parallel",), 56),
        name="mlp",
    )(h, g, wu, wd)


def _rwkvproj_kernel(h_ref, hp_ref, g_ref, mu_ref, wr_ref, wk_ref, wv_ref, w1_ref, w2_ref,
                     a1_ref, a2_ref, g1_ref, g2_ref, w0_ref, a0_ref, kk_ref, ka_ref,
                     r_ref, lw_ref, km_ref, v_ref, kr_ref, a_ref, go_ref, *, tiles_per_seq):
    i = pl.program_id(0)
    tm = h_ref.shape[0]
    gn = g_ref[...]
    hn = _rms(h_ref[...], gn)
    prev = _rms(hp_ref[7:8, :], gn)
    prev = jnp.where(i % tiles_per_seq == 0, jnp.zeros_like(prev), prev)
    row = lax.broadcasted_iota(jnp.int32, hn.shape, 0)
    shifted = jnp.where(row == 0, jnp.broadcast_to(prev, hn.shape), pltpu.roll(hn, 1, 0))
    xx = shifted - hn
    mix = lambda j: (hn + xx * mu_ref[j:j + 1, :]).astype(BF16)
    r = _dot(mix(0), wr_ref[...])
    k = _dot(mix(2), wk_ref[...])
    v = _dot(mix(3), wv_ref[...])
    z = w0_ref[...] + _dot(jnp.tanh(_dot(mix(1), w1_ref[...])).astype(BF16), w2_ref[...])
    a = _sigmoid(a0_ref[...] + _dot(_dot(mix(4), a1_ref[...]).astype(BF16), a2_ref[...]))
    g = _dot(_sigmoid(_dot(mix(5), g1_ref[...])).astype(BF16), g2_ref[...])
    r_ref[...] = r
    lw_ref[...] = -jnp.exp(_log_sigmoid(z) - 0.5)
    km_ref[...] = k * (1.0 + (a - 1.0) * ka_ref[...])
    v_ref[...] = v
    kr_ref[...] = k * kk_ref[...]
    a_ref[...] = a
    go_ref[...] = g


def _rwkvproj(h, g, mu, wr, wk, wv, w1, w2, a1, a2, g1, g2, w0, a0, k_k, k_a, seq, tm):
    m, d = h.shape
    tiles_per_seq = seq // tm
    full = lambda a: pl.BlockSpec(a.shape, lambda i: (0,) * a.ndim)
    row = pl.BlockSpec((tm, d), lambda i: (i, 0))
    prev = pl.BlockSpec((8, d), lambda i: (jnp.maximum(i * (tm // 8) - 1, 0), 0))
    consts = (g, mu, wr, wk, wv, w1, w2, a1, a2, g1, g2, w0, a0, k_k, k_a)
    return pl.pallas_call(
        functools.partial(_rwkvproj_kernel, tiles_per_seq=tiles_per_seq),
        grid=(m // tm,),
        in_specs=[row, prev] + [full(a) for a in consts],
        out_specs=[row] * 7,
        out_shape=[jax.ShapeDtypeStruct((m, d), F32)] * 7,
        compiler_params=_cparams(("parallel",), 56),
        name="rwkv_proj",
    )(h, h, *consts)


def _wkv_kernel(r_ref, lw_ref, km_ref, v_ref, kr_ref, a_ref, g_ref, rk_ref, lg_ref, lb_ref,
                o_ref, st_ref, y_ref, q1_ref, lhs_ref, z0_ref, wc_ref, bonus_ref, gate_ref,
                *, chunk, groups):
    grp = pl.program_id(2)

    @pl.when(pl.program_id(1) == 0)
    def _():
        st_ref[grp] = jnp.zeros(st_ref.shape[1:], F32)

    ts = r_ref.shape[0]
    nch = ts // chunk
    gl = GROUP_LANES
    rb = lax.broadcasted_iota(jnp.int32, (gl, gl), 0) // RWKV_DIM
    cb = lax.broadcasted_iota(jnp.int32, (gl, gl), 1) // RWKV_DIM
    blockmask = rb == cb
    ones_bd = jnp.where(blockmask, 1.0, 0.0).astype(BF16)

    def headsum(x):
        hi = x.astype(BF16)
        lo = (x - hi.astype(F32)).astype(BF16)
        return _dot(hi, ones_bd) + _dot(lo, ones_bd)

    def bd(y):
        reps = gl // y.shape[0]
        return jnp.where(blockmask, jnp.concatenate([y] * reps, axis=0), 0.0).astype(BF16)

    def hmm(x, y):
        return _dot(x.astype(BF16), bd(y))

    def tn_blocks(x, y):
        return jnp.where(blockmask, _dot(x.T.astype(BF16), y.astype(BF16)), 0.0)

    t_idx = lax.broadcasted_iota(jnp.int32, (chunk, gl), 0)
    s_idx = lax.broadcasted_iota(jnp.int32, (chunk, gl), 1) % RWKV_DIM
    strict = s_idx < t_idx
    incl = s_idx <= t_idx
    tril_b = jnp.where(_tril_mask(chunk), 1.0, 0.0).astype(BF16)
    zeros_c = jnp.zeros((chunk, gl), F32)

    kr = kr_ref[...]
    kkn_all = kr * lax.rsqrt(jnp.maximum(headsum(kr * kr), 1e-24))

    chunks = range(nch)
    rows_of = lambda c: slice(c * chunk, (c + 1) * chunk)
    pad = lambda x: jnp.concatenate([x, zeros_c], axis=0)
    each = lambda fn, *lists: [fn(*args) for args in zip(*lists)]

    lw = [lw_ref[rows_of(c), :] for c in chunks]
    cum = each(lambda x: _cumsum_rows(x, tril_b), lw)
    c_last = [x[chunk - 1:chunk, :] for x in cum]
    kkn = [kkn_all[rows_of(c), :] for c in chunks]
    kka = [kkn[c] * a_ref[rows_of(c), :] for c in chunks]
    km = [km_ref[rows_of(c), :] for c in chunks]
    v = [v_ref[rows_of(c), :] for c in chunks]
    e_neg = [jnp.exp(-x) for x in cum]
    e_end = each(lambda cl, x: jnp.exp(cl - x), c_last, cum)
    at = each(lambda k, x, l: -k * jnp.exp(x - l), kkn, cum, lw)
    bt = each(jnp.multiply, kka, e_neg)
    kt = each(jnp.multiply, km, e_neg)
    rt = [r_ref[rows_of(c), :] * jnp.exp(cum[c]) for c in chunks]
    bw = each(jnp.multiply, kka, e_end)
    kw = each(jnp.multiply, km, e_end)
    for c in chunks:
        wc_ref[grp, c] = jnp.broadcast_to(jnp.exp(c_last[c]), (LANES, gl)).T

    lhs = each(lambda a, r: jnp.concatenate([a, r], axis=0).astype(BF16), at, rt)
    pb = each(lambda l, b: _dot_nt(l, bd(b)), lhs, bt)
    pk = each(lambda l, k: _dot_nt(l, bd(k)), lhs, kt)
    a_ab = [jnp.where(strict, x[:chunk], 0.0) for x in pb]
    a_rb = [jnp.where(incl, x[chunk:], 0.0) for x in pb]
    a_ak = [jnp.where(strict, x[:chunk], 0.0) for x in pk]
    a_rk = [jnp.where(incl, x[chunk:], 0.0) for x in pk]

    e = [jnp.where((t_idx % 2 == 1) & (s_idx == t_idx - 1), x, 0.0) for x in a_ab]
    size = 2
    while size < chunk:
        off = ((t_idx // size) % 2 == 1) & (s_idx // size == t_idx // size - 1)
        a_off = [jnp.where(off, x, 0.0) for x in a_ab]
        t1 = each(lambda ao, ee: ao + hmm(ao, ee), a_off, e)
        e = each(lambda ee, tt: ee + tt + hmm(ee, tt), e, t1)
        size *= 2

    akv = each(hmm, a_ak, v)
    p1 = each(lambda x, ee: x + hmm(ee, x), akv, e)
    mat = each(lambda x, ee: x + hmm(ee, x), at, e)
    q1 = each(lambda ark, vv, arb, pp: hmm(ark, vv) + hmm(arb, pp), a_rk, v, a_rb, p1)
    r2 = each(lambda r, arb, mm: r + hmm(arb, mm), rt, a_rb, mat)
    pct = each(lambda b, mm: tn_blocks(pad(b), pad(mm)), bw, mat)
    z0 = each(lambda b, k, pp, vv: tn_blocks(jnp.concatenate([b, k], axis=0),
                                             jnp.concatenate([pp, vv], axis=0)), bw, kw, p1, v)
    for c in chunks:
        q1_ref[grp, rows_of(c), :] = q1[c]
        lhs_ref[grp, c, :chunk, :] = r2[c].astype(BF16)
        lhs_ref[grp, c, chunk:, :] = pct[c].astype(BF16)
        z0_ref[grp, c] = z0[c]
    bonus_ref[grp] = headsum(r_ref[...] * km_ref[...] * rk_ref[...]) * v_ref[...]
    gate_ref[grp] = g_ref[...]

    @pl.when(grp == groups - 1)
    def _():
        gs = range(groups)

        def body(c, carry):
            rows = pl.ds(pl.multiple_of(c * chunk, chunk), chunk)
            st = [st_ref[j] for j in gs]
            res = [_dot(lhs_ref[j, c], st[j].astype(BF16)) for j in gs]
            for j in gs:
                y_ref[j, rows, :] = q1_ref[j, rows, :] + res[j][:chunk]
                wc = wc_ref[j, c]
                st_ref[j] = (st[j] * jnp.concatenate([wc, wc], axis=1) + res[j][chunk:]
                             + z0_ref[j, c])
            return carry

        lax.fori_loop(0, nch, body, 0)

        inv_n = 1.0 / RWKV_DIM
        for j in gs:
            cols = slice(j * gl, (j + 1) * gl)
            y = y_ref[j]
            mean = headsum(y) * inv_n
            dlt = y - mean
            var = headsum(dlt * dlt) * inv_n
            yn = dlt * lax.rsqrt(var + GN_EPS) * lg_ref[:, cols] + lb_ref[:, cols]
            o_ref[:, cols] = ((yn + bonus_ref[j]) * gate_ref[j]).astype(o_ref.dtype)


def _wkv(r, lw, km, v, kr, a, g, r_k, lnx_g, lnx_b, batch, seq, ts, chunk):
    m, d = r.shape
    gl = GROUP_LANES
    groups = d // gl
    nt = seq // ts
    nch = ts // chunk
    row = pl.BlockSpec((ts, gl), lambda b, t, j: (b * nt + t, j))
    vec = pl.BlockSpec((1, gl), lambda b, t, j: (0, j))
    full = pl.BlockSpec((1, d), lambda b, t, j: (0, 0))
    return pl.pallas_call(
        functools.partial(_wkv_kernel, chunk=chunk, groups=groups),
        grid=(batch, nt, groups),
        in_specs=[row] * 7 + [vec, full, full],
        out_specs=pl.BlockSpec((ts, d), lambda b, t, j: (b * nt + t, 0)),
        out_shape=jax.ShapeDtypeStruct((m, d), BF16),
        scratch_shapes=[pltpu.VMEM((groups, gl, gl), F32),
                        pltpu.VMEM((groups, ts, gl), F32),
                        pltpu.VMEM((groups, ts, gl), F32),
                        pltpu.VMEM((groups, nch, chunk + gl, gl), BF16),
                        pltpu.VMEM((groups, nch, gl, gl), F32),
                        pltpu.VMEM((groups, nch, gl, LANES), F32),
                        pltpu.VMEM((groups, ts, gl), F32),
                        pltpu.VMEM((groups, ts, gl), F32)],
        compiler_params=_cparams(("parallel", "arbitrary", "arbitrary"), 56),
        name="wkv7",
    )(r, lw, km, v, kr, a, g, r_k, lnx_g, lnx_b)


def kernel(x, norm_mix_g, norm_ffn_g, ab_w_in, hgrn_lower_bounds, hgrn_norm_g, fox_forget_bias,
           fox_q_norm_g, fox_k_norm_g, ab_w_out, rwkv_mu, rwkv_w_rkv, rwkv_w0, rwkv_w1, rwkv_w2,
           rwkv_a0, rwkv_a1, rwkv_a2, rwkv_g1, rwkv_g2, rwkv_k_k, rwkv_k_a, rwkv_r_k,
           rwkv_lnx_g, rwkv_lnx_b, rwkv_w_o, mlp_w_up, mlp_w_down):
    batch, seq, d = x.shape
    m = batch * seq
    tm = min(256, seq)
    ts = min(512, seq)
    tq = min(512, seq)
    row = lambda a: a.reshape(1, -1).astype(F32)
    bf = lambda a: a.astype(BF16)

    lb_all = jnp.cumsum(jax.nn.softmax(hgrn_lower_bounds.astype(F32), axis=0), axis=0)
    h = x.reshape(m, d)

    n_in = ab_w_in.shape[-1]
    n_pad = (-n_in) % LANES
    w_in = bf(jnp.pad(ab_w_in[0], ((0, 0), (0, n_pad))))
    proj = _inproj(h, row(norm_mix_g[0]), w_in, tm)
    ya = _hgrn(proj, row(lb_all[0]), row(hgrn_norm_g[0]), batch, seq, ts, 64)
    fb = jnp.pad(row(fox_forget_bias[0]), ((0, 0), (0, LANES - FOX_HEADS)))
    qa, ka, vb = _foxprep(proj, fb, row(fox_q_norm_g[0]), row(fox_k_norm_g[0]), batch, seq, ts)
    yb = _fox(qa, ka, vb, proj, batch, seq, tq, 4)
    wa = bf(ab_w_out[0][:HGRN_HEADS * HGRN_DIM])
    wb = bf(ab_w_out[0][HGRN_HEADS * HGRN_DIM:])
    h = _outproj2(ya, yb, wa, wb, h, tm)
    h = _mlp(h, row(norm_ffn_g[0]), bf(mlp_w_up[0]), bf(mlp_w_down[0]), tm, 1024)

    outs = _rwkvproj(h, row(norm_mix_g[1]), rwkv_mu[0].astype(F32),
                     bf(rwkv_w_rkv[0, 0]), bf(rwkv_w_rkv[0, 1]), bf(rwkv_w_rkv[0, 2]),
                     bf(rwkv_w1[0]), bf(rwkv_w2[0]), bf(rwkv_a1[0]), bf(rwkv_a2[0]),
                     bf(rwkv_g1[0]), bf(rwkv_g2[0]), row(rwkv_w0[0]), row(rwkv_a0[0]),
                     row(rwkv_k_k[0]), row(rwkv_k_a[0]), seq, tm)
    z = _wkv(*outs, row(rwkv_r_k[0]), row(rwkv_lnx_g[0]), row(rwkv_lnx_b[0]),
             batch, seq, min(512, seq), 64)
    h = _outproj1(z, bf(rwkv_w_o[0]), h, tm)
    h = _mlp(h, row(norm_ffn_g[1]), bf(mlp_w_up[1]), bf(mlp_w_down[1]), tm, 1024)
    return h.reshape(batch, seq, d)
```

```python
import functools

import jax
import jax.numpy as jnp
import numpy as np
from jax import lax
from jax.experimental import pallas as pl
from jax.experimental.pallas import tpu as pltpu

F32 = jnp.float32
BF16 = jnp.bfloat16

RMS_EPS = 1e-6
GN_EPS = 64e-5

HGRN_HEADS = 4
HGRN_DIM = 128
FOX_HEADS = 8
FOX_DIM = 64
RWKV_DIM = 64
RWKV_GROUP = 4
GROUP_LANES = RWKV_GROUP * RWKV_DIM
LANES = 128
NEG_BIG = -1e30
LOG2E = 1.4426950408889634

NT_DIMS = (((1,), (1,)), ((), ()))


def _cparams(sem, vmem_mb):
    return pltpu.CompilerParams(dimension_semantics=sem, vmem_limit_bytes=vmem_mb * 1024 * 1024)


def _dot(a, b):
    return jnp.dot(a, b, preferred_element_type=F32)


def _dot_nt(a, b):
    return lax.dot_general(a, b, NT_DIMS, preferred_element_type=F32)


def _rms(x, g):
    return x * lax.rsqrt(jnp.mean(x * x, axis=-1, keepdims=True) + RMS_EPS) * g


def _sigmoid(x):
    return 1.0 / (1.0 + jnp.exp(-x))


def _log_sigmoid(x):
    return jnp.minimum(x, 0.0) - jnp.log(1.0 + jnp.exp(-jnp.abs(x)))


def _tril_mask(n, strict=False):
    r = lax.broadcasted_iota(jnp.int32, (n, n), 0)
    c = lax.broadcasted_iota(jnp.int32, (n, n), 1)
    return (c < r) if strict else (c <= r)


def _split3(x):
    hi = x.astype(BF16)
    r1 = x - hi.astype(F32)
    mid = r1.astype(BF16)
    lo = (r1 - mid.astype(F32)).astype(BF16)
    return hi, mid, lo


def _cumsum_rows(x, tril_bf16, pieces=3):
    parts = _split3(x)[:pieces]
    out = _dot(tril_bf16, parts[0])
    for part in parts[1:]:
        out = out + _dot(tril_bf16, part)
    return out


def _inproj_kernel(x_ref, g_ref, w_ref, o_ref):
    hn = _rms(x_ref[...], g_ref[...]).astype(BF16)
    o_ref[...] = _dot(hn, w_ref[...])


def _inproj(x2, g, w, tm):
    m, d = x2.shape
    n = w.shape[1]
    return pl.pallas_call(
        _inproj_kernel,
        grid=(m // tm,),
        in_specs=[
            pl.BlockSpec((tm, d), lambda i: (i, 0)),
            pl.BlockSpec((1, d), lambda i: (0, 0)),
            pl.BlockSpec((d, n), lambda i: (0, 0)),
        ],
        out_specs=pl.BlockSpec((tm, n), lambda i: (i, 0)),
        out_shape=jax.ShapeDtypeStruct((m, n), F32),
        compiler_params=_cparams(("parallel",), 48),
        name="inproj",
    )(x2, g, w)


def _hgrn_kernel(q_ref, f_ref, i_ref, g_ref, lb_ref, ng_ref, o_ref, st_ref, *, chunk):
    @pl.when(pl.program_id(1) == 0)
    def _():
        st_ref[...] = jnp.zeros_like(st_ref)

    ts = q_ref.shape[0]
    causal = _tril_mask(chunk)
    tril_b = jnp.where(causal, 1.0, 0.0).astype(BF16)
    mid = chunk // 2

    hs = range(HGRN_HEADS)
    nch = ts // chunk
    tiles = [(c, h) for c in range(nch) for h in hs]
    blk = lambda ref, c, h: ref[c * chunk:(c + 1) * chunk, h * HGRN_DIM:(h + 1) * HGRN_DIM]
    lbs = [lb_ref[:, h * HGRN_DIM:(h + 1) * HGRN_DIM] for h in hs]

    f = [lbs[h] + (1.0 - lbs[h]) * _sigmoid(blk(f_ref, c, h)) for c, h in tiles]
    b = [_cumsum_rows(jnp.log(x), tril_b) for x in f]
    b_mid = [x[mid - 1:mid, :] for x in b]
    b_last = [x[chunk - 1:chunk, :] for x in b]
    q = [blk(q_ref, c, h) * _sigmoid(blk(q_ref, c, h)) for c, h in tiles]
    k = [1.0 - x for x in f]
    vb = [blk(i_ref, c, h).astype(BF16) for c, h in tiles]
    n = range(len(tiles))
    s = [_dot_nt((q[i] * jnp.exp(b[i] - b_mid[i])).astype(BF16),
                 (k[i] * jnp.exp(b_mid[i] - b[i])).astype(BF16)) for i in n]
    o = [_dot(jnp.where(causal, s[i], 0.0).astype(BF16), vb[i]) for i in n]
    inc = [_dot(blk(i_ref, c, h).T.astype(BF16), (k[i] * jnp.exp(b_last[i] - b[i])).astype(BF16))
           for i, (c, h) in enumerate(tiles)]
    dec = [jnp.exp(x) for x in b_last]

    st = [st_ref[h] for h in hs]
    st_in = []
    for i, (c, h) in enumerate(tiles):
        st_in.append(st[h].astype(BF16))
        st[h] = st[h] * dec[i] + inc[i]
    for h in hs:
        st_ref[h] = st[h]

    for i, (c, h) in enumerate(tiles):
        oi = o[i] + _dot_nt((q[i] * jnp.exp(b[i])).astype(BF16), st_in[i])
        ag = blk(g_ref, c, h)
        on = _rms(oi, ng_ref[:, h * HGRN_DIM:(h + 1) * HGRN_DIM])
        o_ref[c * chunk:(c + 1) * chunk, h * HGRN_DIM:(h + 1) * HGRN_DIM] = (
            on * (ag * _sigmoid(ag))).astype(o_ref.dtype)


def _hgrn(proj, lb, ng, batch, seq, ts, chunk):
    m = proj.shape[0]
    w = HGRN_HEADS * HGRN_DIM
    nt = seq // ts
    spec = lambda j: pl.BlockSpec((ts, w), lambda b, t, j=j: (b * nt + t, j))
    vec = pl.BlockSpec((1, w), lambda b, t: (0, 0))
    return pl.pallas_call(
        functools.partial(_hgrn_kernel, chunk=chunk),
        grid=(batch, nt),
        in_specs=[spec(0), spec(1), spec(2), spec(3), vec, vec],
        out_specs=pl.BlockSpec((ts, w), lambda b, t: (b * nt + t, 0)),
        out_shape=jax.ShapeDtypeStruct((m, w), BF16),
        scratch_shapes=[pltpu.VMEM((HGRN_HEADS, HGRN_DIM, HGRN_DIM), F32)],
        compiler_params=_cparams(("parallel", "arbitrary"), 32),
        name="hgrn2",
    )(proj, proj, proj, proj, lb, ng)


def _foxprep_kernel(q_ref, k_ref, v_ref, f_ref, fb_ref, qg_ref, kg_ref, hsum_ref, place_ref,
                    cq_ref, ck_ref, oq_ref, ok_ref, qt_ref, ka_ref, vt_ref, carry_ref):
    @pl.when(pl.program_id(1) == 0)
    def _():
        carry_ref[...] = jnp.zeros_like(carry_ref)

    ts = q_ref.shape[0]
    tril_b = jnp.where(_tril_mask(ts), 1.0, 0.0).astype(BF16)
    lf = _log_sigmoid(f_ref[...] + fb_ref[...])
    c = _cumsum_rows(lf, tril_b) + carry_ref[...]
    carry_ref[...] = c[ts - 1:ts, :]
    pieces = jnp.concatenate(_split3(c * LOG2E), axis=1)

    def headnorm(x, g):
        xx = x * x
        hi = xx.astype(BF16)
        lo = (xx - hi.astype(F32)).astype(BF16)
        ss = _dot(hi, hsum_ref[...]) + _dot(lo, hsum_ref[...])
        return x * lax.rsqrt(ss * (1.0 / FOX_DIM) + RMS_EPS) * g

    qn = (headnorm(q_ref[...], qg_ref[...]) * (FOX_DIM ** -0.5 * LOG2E)).astype(BF16)
    kn = headnorm(k_ref[...], kg_ref[...]).astype(BF16)
    qa = _dot(qn, place_ref[...]) + _dot(pieces, cq_ref[...]) + oq_ref[...]
    ka = _dot(kn, place_ref[...]) + _dot(pieces, ck_ref[...]) + ok_ref[...]
    ka_ref[...] = ka.astype(BF16)
    for h in range(FOX_HEADS):
        qt_ref[0, h] = qa[:, h * LANES:(h + 1) * LANES].T.astype(BF16)
    for r in range(FOX_HEADS // 2):
        vt_ref[0, r, 0] = v_ref[:, r * LANES:(r + 1) * LANES].T.astype(BF16)


def _foxprep(proj, fb, qg, kg, batch, seq, ts):
    m = proj.shape[0]
    w = FOX_HEADS * FOX_DIM
    nt = seq // ts
    pairs = FOX_HEADS // 2
    spec = lambda j: pl.BlockSpec((ts, w), lambda b, t, j=j: (b * nt + t, j))
    fcol = (8 * w) // LANES
    const = lambda a: pl.BlockSpec(a.shape, lambda b, t: (0, 0))

    wa = FOX_HEADS * LANES
    ch = np.arange(w)
    head, dim = ch // FOX_DIM, ch % FOX_DIM
    hsum = (head[:, None] == head[None, :]).astype(np.float32)
    place = np.zeros((w, wa), np.float32)
    place[ch, head * LANES + dim] = 1.0
    cq = np.zeros((3 * LANES, wa), np.float32)
    ck = np.zeros((3 * LANES, wa), np.float32)
    oq = np.zeros((1, wa), np.float32)
    ok = np.zeros((1, wa), np.float32)
    for h in range(FOX_HEADS):
        for p in range(3):
            cq[p * LANES + h, h * LANES + FOX_DIM + p] = 1.0
            ck[p * LANES + h, h * LANES + FOX_DIM + 3 + p] = -1.0
            oq[0, h * LANES + FOX_DIM + 3 + p] = 1.0
            ok[0, h * LANES + FOX_DIM + p] = 1.0
    consts = [jnp.asarray(a, BF16) for a in (hsum, place, cq, ck)] + [jnp.asarray(oq), jnp.asarray(ok)]
    qg = jnp.tile(qg, (1, FOX_HEADS))
    kg = jnp.tile(kg, (1, FOX_HEADS))
    return pl.pallas_call(
        _foxprep_kernel,
        grid=(batch, nt),
        in_specs=[spec(4), spec(5), spec(6),
                  pl.BlockSpec((ts, LANES), lambda b, t: (b * nt + t, fcol)),
                  const(fb), const(qg), const(kg)] + [const(a) for a in consts],
        out_specs=[pl.BlockSpec((1, FOX_HEADS, LANES, ts), lambda b, t: (b, 0, 0, t)),
                   pl.BlockSpec((ts, FOX_HEADS * LANES), lambda b, t: (b * nt + t, 0)),
                   pl.BlockSpec((1, pairs, 1, LANES, ts), lambda b, t: (b, 0, t, 0, 0))],
        out_shape=[jax.ShapeDtypeStruct((batch, FOX_HEADS, LANES, seq), BF16),
                   jax.ShapeDtypeStruct((m, FOX_HEADS * LANES), BF16),
                   jax.ShapeDtypeStruct((batch, pairs, nt, LANES, ts), BF16)],
        scratch_shapes=[pltpu.VMEM((1, LANES), F32)],
        compiler_params=_cparams(("parallel", "arbitrary"), 32),
        name="foxprep",
    )(proj, proj, proj, proj, fb, qg, kg, *consts)


def _fox_kernel(qt_ref, k_ref, vt_ref, g_ref, o_ref, *, tq, heads):
    i = pl.program_id(2)
    hs = range(heads)
    row = lax.broadcasted_iota(jnp.int32, (tq, tq), 0)
    col = lax.broadcasted_iota(jnp.int32, (tq, tq), 1)
    visible = col >= row
    qts = [qt_ref[0, h] for h in hs]
    vrows = [slice((h % 2) * FOX_DIM, (h % 2 + 1) * FOX_DIM) for h in hs]

    def step(j, carry, masked):
        ms, ls, accs = carry
        rows = pl.ds(pl.multiple_of(j * tq, tq), tq)
        s = [_dot(k_ref[rows, h * LANES:(h + 1) * LANES], qts[h]) for h in hs]
        if masked:
            s = [jnp.where(visible, x, NEG_BIG) for x in s]
        m_new = [jnp.maximum(ms[h], jnp.max(s[h], axis=0, keepdims=True)) for h in hs]
        alpha = [jnp.exp2(ms[h] - m_new[h]) for h in hs]
        p = [jnp.exp2(s[h] - m_new[h]) for h in hs]
        l_new = [alpha[h] * ls[h] + jnp.sum(p[h], axis=0, keepdims=True) for h in hs]
        pv = [_dot(vt_ref[0, h // 2, j, vrows[h], :], p[h].astype(BF16)) for h in hs]
        acc_new = [accs[h] * alpha[h] + pv[h] for h in hs]
        return tuple(m_new), tuple(l_new), tuple(acc_new)

    neg = jnp.full((1, tq), NEG_BIG, F32)
    zero = jnp.zeros((1, tq), F32)
    init = ((neg,) * heads, (zero,) * heads, (jnp.zeros((FOX_DIM, tq), F32),) * heads)
    carry = lax.fori_loop(0, i, lambda j, c: step(j, c, False), init)
    _, ls, accs = step(i, carry, True)
    for r in range(heads // 2):
        cols = slice(r * LANES, (r + 1) * LANES)
        out = jnp.concatenate([accs[2 * r] / ls[2 * r], accs[2 * r + 1] / ls[2 * r + 1]], axis=0)
        o_ref[:, cols] = (out.T * _sigmoid(g_ref[:, cols])).astype(o_ref.dtype)


def _fox(qt, ka, vt, proj, batch, seq, tq, heads):
    m = ka.shape[0]
    nq = seq // tq
    groups = FOX_HEADS // heads
    pairs = heads // 2
    wv = heads * FOX_DIM
    gcol = (7 * FOX_HEADS * FOX_DIM) // wv
    return pl.pallas_call(
        functools.partial(_fox_kernel, tq=tq, heads=heads),
        grid=(batch, groups, nq),
        in_specs=[pl.BlockSpec((1, heads, LANES, tq), lambda b, p, i: (b, p, 0, i)),
                  pl.BlockSpec((seq, heads * LANES), lambda b, p, i: (b, p)),
                  pl.BlockSpec((1, pairs, nq, LANES, tq), lambda b, p, i: (b, p, 0, 0, 0)),
                  pl.BlockSpec((tq, wv), lambda b, p, i: (b * nq + i, gcol + p))],
        out_specs=pl.BlockSpec((tq, wv), lambda b, p, i: (b * nq + i, p)),
        out_shape=jax.ShapeDtypeStruct((m, FOX_HEADS * FOX_DIM), BF16),
        compiler_params=_cparams(("parallel", "parallel", "arbitrary"), 48),
        name="fox_attention",
    )(qt, ka, vt, proj)


def _mix_mlp_kernel(*refs, n_mix, ck):
    ys = refs[:n_mix]
    w_ref, h_ref, g_ref, wu_ref, wd_ref, o_ref = refs[n_mix:]
    y = ys[0][...] if n_mix == 1 else jnp.concatenate([r[...] for r in ys], axis=1)
    x = h_ref[...] + _dot(y, w_ref[...])
    hn = _rms(x, g_ref[...]).astype(BF16)
    acc = x
    for c in range(wu_ref.shape[1] // ck):
        u = jnp.maximum(_dot(hn, wu_ref[:, c * ck:(c + 1) * ck]), 0.0)
        acc = acc + _dot((u * u).astype(BF16), wd_ref[c * ck:(c + 1) * ck, :])
    o_ref[...] = acc


def _mix_mlp(ys, w, h, g, wu, wd, tm, ck, name):
    m, d = h.shape
    const = lambda a: pl.BlockSpec(a.shape, lambda i: (0, 0))
    return pl.pallas_call(
        functools.partial(_mix_mlp_kernel, n_mix=len(ys), ck=ck),
        grid=(m // tm,),
        in_specs=([pl.BlockSpec((tm, y.shape[1]), lambda i: (i, 0)) for y in ys]
                  + [const(w), pl.BlockSpec((tm, d), lambda i: (i, 0)), const(g), const(wu), const(wd)]),
        out_specs=pl.BlockSpec((tm, d), lambda i: (i, 0)),
        out_shape=jax.ShapeDtypeStruct((m, d), F32),
        compiler_params=_cparams(("parallel",), 56),
        name=name,
    )(*ys, w, h, g, wu, wd)


def _rwkvproj_kernel(h_ref, hp_ref, g_ref, mu_ref, wr_ref, wk_ref, wv_ref, w1_ref, w2_ref,
                     a1_ref, a2_ref, g1_ref, g2_ref, w0_ref, a0_ref, kk_ref, ka_ref,
                     r_ref, lw_ref, km_ref, v_ref, kr_ref, a_ref, go_ref, *, tiles_per_seq):
    i = pl.program_id(0)
    tm = h_ref.shape[0]
    gn = g_ref[...]
    hn = _rms(h_ref[...], gn)
    prev = _rms(hp_ref[7:8, :], gn)
    prev = jnp.where(i % tiles_per_seq == 0, jnp.zeros_like(prev), prev)
    row = lax.broadcasted_iota(jnp.int32, hn.shape, 0)
    shifted = jnp.where(row == 0, jnp.broadcast_to(prev, hn.shape), pltpu.roll(hn, 1, 0))
    xx = shifted - hn
    mix = lambda j: (hn + xx * mu_ref[j:j + 1, :]).astype(BF16)
    r = _dot(mix(0), wr_ref[...])
    k = _dot(mix(2), wk_ref[...])
    v = _dot(mix(3), wv_ref[...])
    z = w0_ref[...] + _dot(jnp.tanh(_dot(mix(1), w1_ref[...])).astype(BF16), w2_ref[...])
    a = _sigmoid(a0_ref[...] + _dot(_dot(mix(4), a1_ref[...]).astype(BF16), a2_ref[...]))
    g = _dot(_sigmoid(_dot(mix(5), g1_ref[...])).astype(BF16), g2_ref[...])
    r_ref[...] = r
    lw_ref[...] = -jnp.exp(_log_sigmoid(z) - 0.5)
    km_ref[...] = k * (1.0 + (a - 1.0) * ka_ref[...])
    v_ref[...] = v
    kr_ref[...] = k * kk_ref[...]
    a_ref[...] = a
    go_ref[...] = g


def _rwkvproj(h, g, mu, wr, wk, wv, w1, w2, a1, a2, g1, g2, w0, a0, k_k, k_a, seq, tm):
    m, d = h.shape
    tiles_per_seq = seq // tm
    full = lambda a: pl.BlockSpec(a.shape, lambda i: (0,) * a.ndim)
    row = pl.BlockSpec((tm, d), lambda i: (i, 0))
    prev = pl.BlockSpec((8, d), lambda i: (jnp.maximum(i * (tm // 8) - 1, 0), 0))
    consts = (g, mu, wr, wk, wv, w1, w2, a1, a2, g1, g2, w0, a0, k_k, k_a)
    return pl.pallas_call(
        functools.partial(_rwkvproj_kernel, tiles_per_seq=tiles_per_seq),
        grid=(m // tm,),
        in_specs=[row, prev] + [full(a) for a in consts],
        out_specs=[row] * 7,
        out_shape=[jax.ShapeDtypeStruct((m, d), F32)] * 7,
        compiler_params=_cparams(("parallel",), 56),
        name="rwkv_proj",
    )(h, h, *consts)


def _wkv_kernel(r_ref, lw_ref, km_ref, v_ref, kr_ref, a_ref, g_ref, rk_ref, lg_ref, lb_ref,
                o_ref, st_ref, y_ref, q1_ref, lhs_ref, z0_ref, wc_ref, bonus_ref, gate_ref,
                *, chunk, groups):
    grp = pl.program_id(2)

    @pl.when(pl.program_id(1) == 0)
    def _():
        st_ref[grp] = jnp.zeros(st_ref.shape[1:], F32)

    ts = r_ref.shape[0]
    nch = ts // chunk
    gl = GROUP_LANES
    rb = lax.broadcasted_iota(jnp.int32, (gl, gl), 0) // RWKV_DIM
    cb = lax.broadcasted_iota(jnp.int32, (gl, gl), 1) // RWKV_DIM
    blockmask = rb == cb
    ones_bd = jnp.where(blockmask, 1.0, 0.0).astype(BF16)

    def headsum(x):
        hi = x.astype(BF16)
        lo = (x - hi.astype(F32)).astype(BF16)
        return _dot(hi, ones_bd) + _dot(lo, ones_bd)

    def bd(y):
        reps = gl // y.shape[0]
        return jnp.where(blockmask, jnp.concatenate([y] * reps, axis=0), 0.0).astype(BF16)

    def hmm(x, y):
        return _dot(x.astype(BF16), bd(y))

    def tn_blocks(x, y):
        return jnp.where(blockmask, _dot(x.T.astype(BF16), y.astype(BF16)), 0.0)

    t_idx = lax.broadcasted_iota(jnp.int32, (chunk, gl), 0)
    s_idx = lax.broadcasted_iota(jnp.int32, (chunk, gl), 1) % RWKV_DIM
    strict = s_idx < t_idx
    incl = s_idx <= t_idx
    tril_b = jnp.where(_tril_mask(chunk), 1.0, 0.0).astype(BF16)
    zeros_c = jnp.zeros((chunk, gl), F32)

    kr = kr_ref[...]
    kkn_all = kr * lax.rsqrt(jnp.maximum(headsum(kr * kr), 1e-24))

    chunks = range(nch)
    rows_of = lambda c: slice(c * chunk, (c + 1) * chunk)
    pad = lambda x: jnp.concatenate([x, zeros_c], axis=0)
    each = lambda fn, *lists: [fn(*args) for args in zip(*lists)]

    lw = [lw_ref[rows_of(c), :] for c in chunks]
    cum = each(lambda x: _cumsum_rows(x, tril_b, pieces=2), lw)
    c_last = [x[chunk - 1:chunk, :] for x in cum]
    kkn = [kkn_all[rows_of(c), :] for c in chunks]
    kka = [kkn[c] * a_ref[rows_of(c), :] for c in chunks]
    km = [km_ref[rows_of(c), :] for c in chunks]
    v = [v_ref[rows_of(c), :] for c in chunks]
    e_neg = [jnp.exp(-x) for x in cum]
    e_end = each(lambda cl, x: jnp.exp(cl - x), c_last, cum)
    at = each(lambda k, x, l: -k * jnp.exp(x - l), kkn, cum, lw)
    bt = each(jnp.multiply, kka, e_neg)
    kt = each(jnp.multiply, km, e_neg)
    rt = [r_ref[rows_of(c), :] * jnp.exp(cum[c]) for c in chunks]
    bw = each(jnp.multiply, kka, e_end)
    kw = each(jnp.multiply, km, e_end)
    for c in chunks:
        wc_ref[grp, c] = jnp.broadcast_to(jnp.exp(c_last[c]), (LANES, gl)).T

    lhs = each(lambda a, r: jnp.concatenate([a, r], axis=0).astype(BF16), at, rt)
    pb = each(lambda l, b: _dot_nt(l, bd(b)), lhs, bt)
    pk = each(lambda l, k: _dot_nt(l, bd(k)), lhs, kt)
    a_ab = [jnp.where(strict, x[:chunk], 0.0) for x in pb]
    a_rb = [jnp.where(incl, x[chunk:], 0.0) for x in pb]
    a_ak = [jnp.where(strict, x[:chunk], 0.0) for x in pk]
    a_rk = [jnp.where(incl, x[chunk:], 0.0) for x in pk]

    e = [jnp.where((t_idx % 2 == 1) & (s_idx == t_idx - 1), x, 0.0) for x in a_ab]
    size = 2
    while size < chunk:
        off = ((t_idx // size) % 2 == 1) & (s_idx // size == t_idx // size - 1)
        a_off = [jnp.where(off, x, 0.0) for x in a_ab]
        t1 = each(lambda ao, ee: ao + hmm(ao, ee), a_off, e)
        e = each(lambda ee, tt: ee + tt + hmm(ee, tt), e, t1)
        size *= 2

    akv = each(hmm, a_ak, v)
    p1 = each(lambda x, ee: x + hmm(ee, x), akv, e)
    mat = each(lambda x, ee: x + hmm(ee, x), at, e)
    q1 = each(lambda ark, vv, arb, pp: hmm(ark, vv) + hmm(arb, pp), a_rk, v, a_rb, p1)
    r2 = each(lambda r, arb, mm: r + hmm(arb, mm), rt, a_rb, mat)
    pct = each(lambda b, mm: tn_blocks(pad(b), pad(mm)), bw, mat)
    z0 = each(lambda b, k, pp, vv: tn_blocks(jnp.concatenate([b, k], axis=0),
                                             jnp.concatenate([pp, vv], axis=0)), bw, kw, p1, v)
    for c in chunks:
        q1_ref[grp, rows_of(c), :] = q1[c]
        lhs_ref[grp, c, :chunk, :] = r2[c].astype(BF16)
        lhs_ref[grp, c, chunk:, :] = pct[c].astype(BF16)
        z0_ref[grp, c] = z0[c]
    bonus_ref[grp] = headsum(r_ref[...] * km_ref[...] * rk_ref[...]) * v_ref[...]
    gate_ref[grp] = g_ref[...]

    @pl.when(grp == groups - 1)
    def _():
        gs = range(groups)

        def body(c, carry):
            rows = pl.ds(pl.multiple_of(c * chunk, chunk), chunk)
            st = [st_ref[j] for j in gs]
            res = [_dot(lhs_ref[j, c], st[j].astype(BF16)) for j in gs]
            for j in gs:
                y_ref[j, rows, :] = q1_ref[j, rows, :] + res[j][:chunk]
                wc = wc_ref[j, c]
                st_ref[j] = (st[j] * jnp.concatenate([wc, wc], axis=1) + res[j][chunk:]
                             + z0_ref[j, c])
            return carry

        lax.fori_loop(0, nch, body, 0)

        inv_n = 1.0 / RWKV_DIM
        for j in gs:
            cols = slice(j * gl, (j + 1) * gl)
            y = y_ref[j]
            mean = headsum(y) * inv_n
            dlt = y - mean
            var = headsum(dlt * dlt) * inv_n
            yn = dlt * lax.rsqrt(var + GN_EPS) * lg_ref[:, cols] + lb_ref[:, cols]
            o_ref[:, cols] = ((yn + bonus_ref[j]) * gate_ref[j]).astype(o_ref.dtype)


def _wkv(r, lw, km, v, kr, a, g, r_k, lnx_g, lnx_b, batch, seq, ts, chunk):
    m, d = r.shape
    gl = GROUP_LANES
    groups = d // gl
    nt = seq // ts
    nch = ts // chunk
    row = pl.BlockSpec((ts, gl), lambda b, t, j: (b * nt + t, j))
    vec = pl.BlockSpec((1, gl), lambda b, t, j: (0, j))
    full = pl.BlockSpec((1, d), lambda b, t, j: (0, 0))
    return pl.pallas_call(
        functools.partial(_wkv_kernel, chunk=chunk, groups=groups),
        grid=(batch, nt, groups),
        in_specs=[row] * 7 + [vec, full, full],
        out_specs=pl.BlockSpec((ts, d), lambda b, t, j: (b * nt + t, 0)),
        out_shape=jax.ShapeDtypeStruct((m, d), BF16),
        scratch_shapes=[pltpu.VMEM((groups, gl, gl), F32),
                        pltpu.VMEM((groups, ts, gl), F32),
                        pltpu.VMEM((groups, ts, gl), F32),
                        pltpu.VMEM((groups, nch, chunk + gl, gl), BF16),
                        pltpu.VMEM((groups, nch, gl, gl), F32),
                        pltpu.VMEM((groups, nch, gl, LANES), F32),
                        pltpu.VMEM((groups, ts, gl), F32),
                        pltpu.VMEM((groups, ts, gl), F32)],
        compiler_params=_cparams(("parallel", "arbitrary", "arbitrary"), 56),
        name="wkv7",
    )(r, lw, km, v, kr, a, g, r_k, lnx_g, lnx_b)


def kernel(x, norm_mix_g, norm_ffn_g, ab_w_in, hgrn_lower_bounds, hgrn_norm_g, fox_forget_bias,
           fox_q_norm_g, fox_k_norm_g, ab_w_out, rwkv_mu, rwkv_w_rkv, rwkv_w0, rwkv_w1, rwkv_w2,
           rwkv_a0, rwkv_a1, rwkv_a2, rwkv_g1, rwkv_g2, rwkv_k_k, rwkv_k_a, rwkv_r_k,
           rwkv_lnx_g, rwkv_lnx_b, rwkv_w_o, mlp_w_up, mlp_w_down):
    batch, seq, d = x.shape
    m = batch * seq
    tm = min(256, seq)
    ts = min(512, seq)
    tq = min(512, seq)
    row = lambda a: a.reshape(1, -1).astype(F32)
    bf = lambda a: a.astype(BF16)

    lb_all = jnp.cumsum(jax.nn.softmax(hgrn_lower_bounds.astype(F32), axis=0), axis=0)
    h = x.reshape(m, d)

    n_in = ab_w_in.shape[-1]
    n_pad = (-n_in) % LANES
    w_in = bf(jnp.pad(ab_w_in[0], ((0, 0), (0, n_pad))))
    proj = _inproj(h, row(norm_mix_g[0]), w_in, tm)
    ya = _hgrn(proj, row(lb_all[0]), row(hgrn_norm_g[0]), batch, seq, ts, 64)
    fb = jnp.pad(row(fox_forget_bias[0]), ((0, 0), (0, LANES - FOX_HEADS)))
    qa, ka, vb = _foxprep(proj, fb, row(fox_q_norm_g[0]), row(fox_k_norm_g[0]), batch, seq, ts)
    yb = _fox(qa, ka, vb, proj, batch, seq, tq, 4)
    h = _mix_mlp([ya, yb], bf(ab_w_out[0]), h, row(norm_ffn_g[0]), bf(mlp_w_up[0]), bf(mlp_w_down[0]),
                 tm, 1024, "mix_mlp0")

    outs = _rwkvproj(h, row(norm_mix_g[1]), rwkv_mu[0].astype(F32),
                     bf(rwkv_w_rkv[0, 0]), bf(rwkv_w_rkv[0, 1]), bf(rwkv_w_rkv[0, 2]),
                     bf(rwkv_w1[0]), bf(rwkv_w2[0]), bf(rwkv_a1[0]), bf(rwkv_a2[0]),
                     bf(rwkv_g1[0]), bf(rwkv_g2[0]), row(rwkv_w0[0]), row(rwkv_a0[0]),
                     row(rwkv_k_k[0]), row(rwkv_k_a[0]), seq, tm)
    z = _wkv(*outs, row(rwkv_r_k[0]), row(rwkv_lnx_g[0]), row(rwkv_lnx_b[0]),
             batch, seq, min(512, seq), 64)
    h = _mix_mlp([z], bf(rwkv_w_o[0]), h, row(norm_ffn_g[1]), bf(mlp_w_up[1]), bf(mlp_w_down[1]),
                 tm, 1024, "mix_mlp1")
    return h.reshape(batch, seq, d)
```

```python
import functools

import jax
import jax.numpy as jnp
import numpy as np
from jax import lax
from jax.experimental import pallas as pl
from jax.experimental.pallas import tpu as pltpu

F32 = jnp.float32
BF16 = jnp.bfloat16

RMS_EPS = 1e-6
GN_EPS = 64e-5

HGRN_HEADS = 4
HGRN_DIM = 128
FOX_HEADS = 8
FOX_DIM = 64
RWKV_DIM = 64
RWKV_GROUP = 4
GROUP_LANES = RWKV_GROUP * RWKV_DIM
LANES = 128
NEG_BIG = -1e30
LOG2E = 1.4426950408889634

NT_DIMS = (((1,), (1,)), ((), ()))


def _cparams(sem, vmem_mb):
    return pltpu.CompilerParams(dimension_semantics=sem, vmem_limit_bytes=vmem_mb * 1024 * 1024)


def _dot(a, b):
    return jnp.dot(a, b, preferred_element_type=F32)


def _dot_nt(a, b):
    return lax.dot_general(a, b, NT_DIMS, preferred_element_type=F32)


def _rms(x, g):
    return x * lax.rsqrt(jnp.mean(x * x, axis=-1, keepdims=True) + RMS_EPS) * g


def _sigmoid(x):
    return 1.0 / (1.0 + jnp.exp(-x))


def _log_sigmoid(x):
    return jnp.minimum(x, 0.0) - jnp.log(1.0 + jnp.exp(-jnp.abs(x)))


def _tril_mask(n, strict=False):
    r = lax.broadcasted_iota(jnp.int32, (n, n), 0)
    c = lax.broadcasted_iota(jnp.int32, (n, n), 1)
    return (c < r) if strict else (c <= r)


def _split3(x):
    hi = x.astype(BF16)
    r1 = x - hi.astype(F32)
    mid = r1.astype(BF16)
    lo = (r1 - mid.astype(F32)).astype(BF16)
    return hi, mid, lo


def _cumsum_rows(x, tril_bf16, pieces=3):
    parts = _split3(x)[:pieces]
    out = _dot(tril_bf16, parts[0])
    for part in parts[1:]:
        out = out + _dot(tril_bf16, part)
    return out


def _inproj_kernel(x_ref, g_ref, w_ref, o_ref):
    hn = _rms(x_ref[...], g_ref[...]).astype(BF16)
    o_ref[...] = _dot(hn, w_ref[...])


def _inproj(x2, g, w, tm):
    m, d = x2.shape
    n = w.shape[1]
    return pl.pallas_call(
        _inproj_kernel,
        grid=(m // tm,),
        in_specs=[
            pl.BlockSpec((tm, d), lambda i: (i, 0)),
            pl.BlockSpec((1, d), lambda i: (0, 0)),
            pl.BlockSpec((d, n), lambda i: (0, 0)),
        ],
        out_specs=pl.BlockSpec((tm, n), lambda i: (i, 0)),
        out_shape=jax.ShapeDtypeStruct((m, n), F32),
        compiler_params=_cparams(("parallel",), 48),
        name="inproj",
    )(x2, g, w)


def _hgrn_kernel(q_ref, f_ref, i_ref, g_ref, lb_ref, ng_ref, o_ref, st_ref, *, chunk):
    @pl.when(pl.program_id(1) == 0)
    def _():
        st_ref[...] = jnp.zeros_like(st_ref)

    ts = q_ref.shape[0]
    causal = _tril_mask(chunk)
    tril_b = jnp.where(causal, 1.0, 0.0).astype(BF16)
    mid = chunk // 2

    hs = range(HGRN_HEADS)
    nch = ts // chunk
    tiles = [(c, h) for c in range(nch) for h in hs]
    blk = lambda ref, c, h: ref[c * chunk:(c + 1) * chunk, h * HGRN_DIM:(h + 1) * HGRN_DIM]
    lbs = [lb_ref[:, h * HGRN_DIM:(h + 1) * HGRN_DIM] for h in hs]

    f = [lbs[h] + (1.0 - lbs[h]) * _sigmoid(blk(f_ref, c, h)) for c, h in tiles]
    b = [_cumsum_rows(jnp.log(x), tril_b) for x in f]
    b_mid = [x[mid - 1:mid, :] for x in b]
    b_last = [x[chunk - 1:chunk, :] for x in b]
    q = [blk(q_ref, c, h) * _sigmoid(blk(q_ref, c, h)) for c, h in tiles]
    k = [1.0 - x for x in f]
    vb = [blk(i_ref, c, h).astype(BF16) for c, h in tiles]
    n = range(len(tiles))
    s = [_dot_nt((q[i] * jnp.exp(b[i] - b_mid[i])).astype(BF16),
                 (k[i] * jnp.exp(b_mid[i] - b[i])).astype(BF16)) for i in n]
    o = [_dot(jnp.where(causal, s[i], 0.0).astype(BF16), vb[i]) for i in n]
    inc = [_dot(blk(i_ref, c, h).T.astype(BF16), (k[i] * jnp.exp(b_last[i] - b[i])).astype(BF16))
           for i, (c, h) in enumerate(tiles)]
    dec = [jnp.exp(x) for x in b_last]

    st = [st_ref[h] for h in hs]
    st_in = []
    for i, (c, h) in enumerate(tiles):
        st_in.append(st[h].astype(BF16))
        st[h] = st[h] * dec[i] + inc[i]
    for h in hs:
        st_ref[h] = st[h]

    for i, (c, h) in enumerate(tiles):
        oi = o[i] + _dot_nt((q[i] * jnp.exp(b[i])).astype(BF16), st_in[i])
        ag = blk(g_ref, c, h)
        on = _rms(oi, ng_ref[:, h * HGRN_DIM:(h + 1) * HGRN_DIM])
        o_ref[c * chunk:(c + 1) * chunk, h * HGRN_DIM:(h + 1) * HGRN_DIM] = (
            on * (ag * _sigmoid(ag))).astype(o_ref.dtype)


def _hgrn(proj, lb, ng, batch, seq, ts, chunk):
    m = proj.shape[0]
    w = HGRN_HEADS * HGRN_DIM
    nt = seq // ts
    spec = lambda j: pl.BlockSpec((ts, w), lambda b, t, j=j: (b * nt + t, j))
    vec = pl.BlockSpec((1, w), lambda b, t: (0, 0))
    return pl.pallas_call(
        functools.partial(_hgrn_kernel, chunk=chunk),
        grid=(batch, nt),
        in_specs=[spec(0), spec(1), spec(2), spec(3), vec, vec],
        out_specs=pl.BlockSpec((ts, w), lambda b, t: (b * nt + t, 0)),
        out_shape=jax.ShapeDtypeStruct((m, w), BF16),
        scratch_shapes=[pltpu.VMEM((HGRN_HEADS, HGRN_DIM, HGRN_DIM), F32)],
        compiler_params=_cparams(("parallel", "arbitrary"), 32),
        name="hgrn2",
    )(proj, proj, proj, proj, lb, ng)


def _foxprep_kernel(q_ref, k_ref, v_ref, f_ref, fb_ref, qg_ref, kg_ref, hsum_ref, place_ref,
                    cq_ref, ck_ref, oq_ref, ok_ref, qt_ref, ka_ref, vt_ref, carry_ref):
    @pl.when(pl.program_id(1) == 0)
    def _():
        carry_ref[...] = jnp.zeros_like(carry_ref)

    ts = q_ref.shape[0]
    tril_b = jnp.where(_tril_mask(ts), 1.0, 0.0).astype(BF16)
    lf = _log_sigmoid(f_ref[...] + fb_ref[...])
    c = _cumsum_rows(lf, tril_b) + carry_ref[...]
    carry_ref[...] = c[ts - 1:ts, :]
    pieces = jnp.concatenate(_split3(c * LOG2E), axis=1)

    def headnorm(x, g):
        xx = x * x
        hi = xx.astype(BF16)
        lo = (xx - hi.astype(F32)).astype(BF16)
        ss = _dot(hi, hsum_ref[...]) + _dot(lo, hsum_ref[...])
        return x * lax.rsqrt(ss * (1.0 / FOX_DIM) + RMS_EPS) * g

    qn = (headnorm(q_ref[...], qg_ref[...]) * (FOX_DIM ** -0.5 * LOG2E)).astype(BF16)
    kn = headnorm(k_ref[...], kg_ref[...]).astype(BF16)
    qa = _dot(qn, place_ref[...]) + _dot(pieces, cq_ref[...]) + oq_ref[...]
    ka = _dot(kn, place_ref[...]) + _dot(pieces, ck_ref[...]) + ok_ref[...]
    ka_ref[...] = ka.astype(BF16)
    for h in range(FOX_HEADS):
        qt_ref[0, h] = qa[:, h * LANES:(h + 1) * LANES].T.astype(BF16)
    for r in range(FOX_HEADS // 2):
        vt_ref[0, r, 0] = v_ref[:, r * LANES:(r + 1) * LANES].T.astype(BF16)


def _foxprep(proj, fb, qg, kg, batch, seq, ts):
    m = proj.shape[0]
    w = FOX_HEADS * FOX_DIM
    nt = seq // ts
    pairs = FOX_HEADS // 2
    spec = lambda j: pl.BlockSpec((ts, w), lambda b, t, j=j: (b * nt + t, j))
    fcol = (8 * w) // LANES
    const = lambda a: pl.BlockSpec(a.shape, lambda b, t: (0, 0))

    wa = FOX_HEADS * LANES
    ch = np.arange(w)
    head, dim = ch // FOX_DIM, ch % FOX_DIM
    hsum = (head[:, None] == head[None, :]).astype(np.float32)
    place = np.zeros((w, wa), np.float32)
    place[ch, head * LANES + dim] = 1.0
    cq = np.zeros((3 * LANES, wa), np.float32)
    ck = np.zeros((3 * LANES, wa), np.float32)
    oq = np.zeros((1, wa), np.float32)
    ok = np.zeros((1, wa), np.float32)
    for h in range(FOX_HEADS):
        for p in range(3):
            cq[p * LANES + h, h * LANES + FOX_DIM + p] = 1.0
            ck[p * LANES + h, h * LANES + FOX_DIM + 3 + p] = -1.0
            oq[0, h * LANES + FOX_DIM + 3 + p] = 1.0
            ok[0, h * LANES + FOX_DIM + p] = 1.0
    consts = [jnp.asarray(a, BF16) for a in (hsum, place, cq, ck)] + [jnp.asarray(oq), jnp.asarray(ok)]
    qg = jnp.tile(qg, (1, FOX_HEADS))
    kg = jnp.tile(kg, (1, FOX_HEADS))
    return pl.pallas_call(
        _foxprep_kernel,
        grid=(batch, nt),
        in_specs=[spec(4), spec(5), spec(6),
                  pl.BlockSpec((ts, LANES), lambda b, t: (b * nt + t, fcol)),
                  const(fb), const(qg), const(kg)] + [const(a) for a in consts],
        out_specs=[pl.BlockSpec((1, FOX_HEADS, LANES, ts), lambda b, t: (b, 0, 0, t)),
                   pl.BlockSpec((ts, FOX_HEADS * LANES), lambda b, t: (b * nt + t, 0)),
                   pl.BlockSpec((1, pairs, 1, LANES, ts), lambda b, t: (b, 0, t, 0, 0))],
        out_shape=[jax.ShapeDtypeStruct((batch, FOX_HEADS, LANES, seq), BF16),
                   jax.ShapeDtypeStruct((m, FOX_HEADS * LANES), BF16),
                   jax.ShapeDtypeStruct((batch, pairs, nt, LANES, ts), BF16)],
        scratch_shapes=[pltpu.VMEM((1, LANES), F32)],
        compiler_params=_cparams(("parallel", "arbitrary"), 32),
        name="foxprep",
    )(proj, proj, proj, proj, fb, qg, kg, *consts)


def _fox_kernel(qt_ref, k_ref, vt_ref, g_ref, o_ref, *, tq, heads):
    i = pl.program_id(2)
    hs = range(heads)
    row = lax.broadcasted_iota(jnp.int32, (tq, tq), 0)
    col = lax.broadcasted_iota(jnp.int32, (tq, tq), 1)
    visible = col >= row
    qts = [qt_ref[0, h] for h in hs]
    vrows = [slice((h % 2) * FOX_DIM, (h % 2 + 1) * FOX_DIM) for h in hs]

    def step(j, carry, masked):
        ms, ls, accs = carry
        rows = pl.ds(pl.multiple_of(j * tq, tq), tq)
        def scores(h):
            s = _dot(k_ref[rows, h * LANES:(h + 1) * LANES], qts[h])
            return jnp.where(visible, s, NEG_BIG) if masked else s

        def softmax(h, s):
            m_new = jnp.maximum(ms[h], jnp.max(s, axis=0, keepdims=True))
            alpha = jnp.exp2(ms[h] - m_new)
            p = jnp.exp2(s - m_new)
            return m_new, alpha, alpha * ls[h] + jnp.sum(p, axis=0, keepdims=True), p.astype(BF16)

        def values(h, alpha, p):
            return accs[h] * alpha + _dot(vt_ref[0, h // 2, j, vrows[h], :], p)

        s, sm, out = {}, {}, {}
        for t in range(heads + 2):
            if t < heads:
                s[t] = scores(t)
            if 0 <= t - 1 < heads:
                sm[t - 1] = softmax(t - 1, s[t - 1])
            if 0 <= t - 2 < heads:
                out[t - 2] = values(t - 2, sm[t - 2][1], sm[t - 2][3])
        return (tuple(sm[h][0] for h in hs), tuple(sm[h][2] for h in hs), tuple(out[h] for h in hs))

    neg = jnp.full((1, tq), NEG_BIG, F32)
    zero = jnp.zeros((1, tq), F32)
    init = ((neg,) * heads, (zero,) * heads, (jnp.zeros((FOX_DIM, tq), F32),) * heads)
    carry = lax.fori_loop(0, i, lambda j, c: step(j, c, False), init)
    _, ls, accs = step(i, carry, True)
    for r in range(heads // 2):
        cols = slice(r * LANES, (r + 1) * LANES)
        out = jnp.concatenate([accs[2 * r] / ls[2 * r], accs[2 * r + 1] / ls[2 * r + 1]], axis=0)
        o_ref[:, cols] = (out.T * _sigmoid(g_ref[:, cols])).astype(o_ref.dtype)


def _fox(qt, ka, vt, proj, batch, seq, tq, heads):
    m = ka.shape[0]
    nq = seq // tq
    groups = FOX_HEADS // heads
    pairs = heads // 2
    wv = heads * FOX_DIM
    gcol = (7 * FOX_HEADS * FOX_DIM) // wv
    return pl.pallas_call(
        functools.partial(_fox_kernel, tq=tq, heads=heads),
        grid=(batch, groups, nq),
        in_specs=[pl.BlockSpec((1, heads, LANES, tq), lambda b, p, i: (b, p, 0, i)),
                  pl.BlockSpec((seq, heads * LANES), lambda b, p, i: (b, p)),
                  pl.BlockSpec((1, pairs, nq, LANES, tq), lambda b, p, i: (b, p, 0, 0, 0)),
                  pl.BlockSpec((tq, wv), lambda b, p, i: (b * nq + i, gcol + p))],
        out_specs=pl.BlockSpec((tq, wv), lambda b, p, i: (b * nq + i, p)),
        out_shape=jax.ShapeDtypeStruct((m, FOX_HEADS * FOX_DIM), BF16),
        compiler_params=_cparams(("parallel", "parallel", "arbitrary"), 48),
        name="fox_attention",
    )(qt, ka, vt, proj)


def _mix_mlp_kernel(*refs, n_mix, ck):
    ys = refs[:n_mix]
    w_ref, h_ref, g_ref, wu_ref, wd_ref, o_ref = refs[n_mix:]
    y = ys[0][...] if n_mix == 1 else jnp.concatenate([r[...] for r in ys], axis=1)
    x = h_ref[...] + _dot(y, w_ref[...])
    hn = _rms(x, g_ref[...]).astype(BF16)
    acc = x
    for c in range(wu_ref.shape[1] // ck):
        u = jnp.maximum(_dot(hn, wu_ref[:, c * ck:(c + 1) * ck]), 0.0)
        acc = acc + _dot((u * u).astype(BF16), wd_ref[c * ck:(c + 1) * ck, :])
    o_ref[...] = acc


def _mix_mlp(ys, w, h, g, wu, wd, tm, ck, name):
    m, d = h.shape
    const = lambda a: pl.BlockSpec(a.shape, lambda i: (0, 0))
    return pl.pallas_call(
        functools.partial(_mix_mlp_kernel, n_mix=len(ys), ck=ck),
        grid=(m // tm,),
        in_specs=([pl.BlockSpec((tm, y.shape[1]), lambda i: (i, 0)) for y in ys]
                  + [const(w), pl.BlockSpec((tm, d), lambda i: (i, 0)), const(g), const(wu), const(wd)]),
        out_specs=pl.BlockSpec((tm, d), lambda i: (i, 0)),
        out_shape=jax.ShapeDtypeStruct((m, d), F32),
        compiler_params=_cparams(("parallel",), 56),
        name=name,
    )(*ys, w, h, g, wu, wd)


def _rwkvproj_kernel(h_ref, hp_ref, g_ref, mu_ref, wr_ref, wk_ref, wv_ref, w1_ref, w2_ref,
                     a1_ref, a2_ref, g1_ref, g2_ref, w0_ref, a0_ref, kk_ref, ka_ref,
                     r_ref, lw_ref, km_ref, v_ref, kr_ref, a_ref, go_ref, *, tiles_per_seq):
    i = pl.program_id(0)
    tm = h_ref.shape[0]
    gn = g_ref[...]
    hn = _rms(h_ref[...], gn)
    prev = _rms(hp_ref[7:8, :], gn)
    prev = jnp.where(i % tiles_per_seq == 0, jnp.zeros_like(prev), prev)
    row = lax.broadcasted_iota(jnp.int32, hn.shape, 0)
    shifted = jnp.where(row == 0, jnp.broadcast_to(prev, hn.shape), pltpu.roll(hn, 1, 0))
    xx = shifted - hn
    mix = lambda j: (hn + xx * mu_ref[j:j + 1, :]).astype(BF16)
    r = _dot(mix(0), wr_ref[...])
    k = _dot(mix(2), wk_ref[...])
    v = _dot(mix(3), wv_ref[...])
    z = w0_ref[...] + _dot(jnp.tanh(_dot(mix(1), w1_ref[...])).astype(BF16), w2_ref[...])
    a = _sigmoid(a0_ref[...] + _dot(_dot(mix(4), a1_ref[...]).astype(BF16), a2_ref[...]))
    g = _dot(_sigmoid(_dot(mix(5), g1_ref[...])).astype(BF16), g2_ref[...])
    r_ref[...] = r
    lw_ref[...] = -jnp.exp(_log_sigmoid(z) - 0.5)
    km_ref[...] = k * (1.0 + (a - 1.0) * ka_ref[...])
    v_ref[...] = v
    kr_ref[...] = k * kk_ref[...]
    a_ref[...] = a
    go_ref[...] = g


def _rwkvproj(h, g, mu, wr, wk, wv, w1, w2, a1, a2, g1, g2, w0, a0, k_k, k_a, seq, tm):
    m, d = h.shape
    tiles_per_seq = seq // tm
    full = lambda a: pl.BlockSpec(a.shape, lambda i: (0,) * a.ndim)
    row = pl.BlockSpec((tm, d), lambda i: (i, 0))
    prev = pl.BlockSpec((8, d), lambda i: (jnp.maximum(i * (tm // 8) - 1, 0), 0))
    consts = (g, mu, wr, wk, wv, w1, w2, a1, a2, g1, g2, w0, a0, k_k, k_a)
    return pl.pallas_call(
        functools.partial(_rwkvproj_kernel, tiles_per_seq=tiles_per_seq),
        grid=(m // tm,),
        in_specs=[row, prev] + [full(a) for a in consts],
        out_specs=[row] * 7,
        out_shape=[jax.ShapeDtypeStruct((m, d), F32)] * 7,
        compiler_params=_cparams(("parallel",), 56),
        name="rwkv_proj",
    )(h, h, *consts)


def _wkv_kernel(r_ref, lw_ref, km_ref, v_ref, kr_ref, a_ref, g_ref, rk_ref, lg_ref, lb_ref,
                o_ref, st_ref, y_ref, q1_ref, lhs_ref, z0_ref, wc_ref, bonus_ref, gate_ref,
                *, chunk, groups, gps):
    step = pl.program_id(2)

    @pl.when(pl.program_id(1) == 0)
    def _():
        for gi in range(gps):
            st_ref[step * gps + gi] = jnp.zeros(st_ref.shape[1:], F32)

    ts = r_ref.shape[0]
    nch = ts // chunk
    gl = GROUP_LANES
    rb = lax.broadcasted_iota(jnp.int32, (gl, gl), 0) // RWKV_DIM
    cb = lax.broadcasted_iota(jnp.int32, (gl, gl), 1) // RWKV_DIM
    blockmask = rb == cb
    ones_bd = jnp.where(blockmask, 1.0, 0.0).astype(BF16)

    def headsum(x, pieces=2):
        hi = x.astype(BF16)
        out = _dot(hi, ones_bd)
        if pieces == 2:
            out = out + _dot((x - hi.astype(F32)).astype(BF16), ones_bd)
        return out

    def bd(y):
        reps = gl // y.shape[0]
        return jnp.where(blockmask, jnp.concatenate([y] * reps, axis=0), 0.0).astype(BF16)

    def hmm(x, y):
        return _dot(x.astype(BF16), bd(y))

    def tn_blocks(x, y):
        return jnp.where(blockmask, _dot(x.T.astype(BF16), y.astype(BF16)), 0.0)

    t_idx = lax.broadcasted_iota(jnp.int32, (chunk, gl), 0)
    s_idx = lax.broadcasted_iota(jnp.int32, (chunk, gl), 1) % RWKV_DIM
    strict = s_idx < t_idx
    incl = s_idx <= t_idx
    tril_b = jnp.where(_tril_mask(chunk), 1.0, 0.0).astype(BF16)
    zeros_c = jnp.zeros((chunk, gl), F32)

    tiles = [(gi, c) for gi in range(gps) for c in range(nch)]
    lanes_of = lambda gi: slice(gi * gl, (gi + 1) * gl)
    blk = lambda ref, gi, c: ref[c * chunk:(c + 1) * chunk, lanes_of(gi)]
    pad = lambda x: jnp.concatenate([x, zeros_c], axis=0)
    each = lambda fn, *lists: [fn(*args) for args in zip(*lists)]

    kkn_all = []
    for gi in range(gps):
        kr = kr_ref[:, lanes_of(gi)]
        kkn_all.append(kr * lax.rsqrt(jnp.maximum(headsum(kr * kr), 1e-24)))

    lw = [blk(lw_ref, gi, c) for gi, c in tiles]
    cum = each(lambda x: _cumsum_rows(x, tril_b, pieces=2), lw)
    c_last = [x[chunk - 1:chunk, :] for x in cum]
    kkn = [kkn_all[gi][c * chunk:(c + 1) * chunk, :] for gi, c in tiles]
    kka = [kkn[i] * blk(a_ref, gi, c) for i, (gi, c) in enumerate(tiles)]
    km = [blk(km_ref, gi, c) for gi, c in tiles]
    v = [blk(v_ref, gi, c) for gi, c in tiles]
    e_neg = [jnp.exp(-x) for x in cum]
    e_end = each(lambda cl, x: jnp.exp(cl - x), c_last, cum)
    at = each(lambda k, x, l: -k * jnp.exp(x - l), kkn, cum, lw)
    bt = each(jnp.multiply, kka, e_neg)
    kt = each(jnp.multiply, km, e_neg)
    rt = [blk(r_ref, gi, c) * jnp.exp(cum[i]) for i, (gi, c) in enumerate(tiles)]
    bw = each(jnp.multiply, kka, e_end)
    kw = each(jnp.multiply, km, e_end)
    for i, (gi, c) in enumerate(tiles):
        wc_ref[step * gps + gi, c] = jnp.broadcast_to(jnp.exp(c_last[i]), (LANES, gl)).T

    lhs = each(lambda a, r: jnp.concatenate([a, r], axis=0).astype(BF16), at, rt)
    pb = each(lambda l, b: _dot_nt(l, bd(b)), lhs, bt)
    pk = each(lambda l, k: _dot_nt(l, bd(k)), lhs, kt)
    a_ab = [jnp.where(strict, x[:chunk], 0.0) for x in pb]
    a_rb = [jnp.where(incl, x[chunk:], 0.0) for x in pb]
    a_ak = [jnp.where(strict, x[:chunk], 0.0) for x in pk]
    a_rk = [jnp.where(incl, x[chunk:], 0.0) for x in pk]

    e = [jnp.where((t_idx % 2 == 1) & (s_idx == t_idx - 1), x, 0.0) for x in a_ab]
    size = 2
    while size < chunk:
        off = ((t_idx // size) % 2 == 1) & (s_idx // size == t_idx // size - 1)
        a_off = [jnp.where(off, x, 0.0) for x in a_ab]
        t1 = each(lambda ao, ee: ao + hmm(ao, ee), a_off, e)
        e = each(lambda ee, tt: ee + tt + hmm(ee, tt), e, t1)
        size *= 2

    akv = each(hmm, a_ak, v)
    p1 = each(lambda x, ee: x + hmm(ee, x), akv, e)
    mat = each(lambda x, ee: x + hmm(ee, x), at, e)
    q1 = each(lambda ark, vv, arb, pp: hmm(ark, vv) + hmm(arb, pp), a_rk, v, a_rb, p1)
    r2 = each(lambda r, arb, mm: r + hmm(arb, mm), rt, a_rb, mat)
    pct = each(lambda b, mm: tn_blocks(pad(b), pad(mm)), bw, mat)
    z0 = each(lambda b, k, pp, vv: tn_blocks(jnp.concatenate([b, k], axis=0),
                                             jnp.concatenate([pp, vv], axis=0)), bw, kw, p1, v)
    for i, (gi, c) in enumerate(tiles):
        g = step * gps + gi
        q1_ref[g, c * chunk:(c + 1) * chunk, :] = q1[i]
        lhs_ref[g, c, :chunk, :] = r2[i].astype(BF16)
        lhs_ref[g, c, chunk:, :] = pct[i].astype(BF16)
        z0_ref[g, c] = z0[i]
    for gi in range(gps):
        cols = lanes_of(gi)
        bonus_ref[step * gps + gi] = headsum(
            r_ref[:, cols] * km_ref[:, cols] * rk_ref[:, cols], pieces=1) * v_ref[:, cols]
        gate_ref[step * gps + gi] = g_ref[:, cols]

    @pl.when(step == groups // gps - 1)
    def _():
        gs = range(groups)

        def body(c, carry):
            rows = pl.ds(pl.multiple_of(c * chunk, chunk), chunk)
            st = [st_ref[j] for j in gs]
            res = [_dot(lhs_ref[j, c], st[j].astype(BF16)) for j in gs]
            for j in gs:
                y_ref[j, rows, :] = q1_ref[j, rows, :] + res[j][:chunk]
                wc = wc_ref[j, c]
                st_ref[j] = (st[j] * jnp.concatenate([wc, wc], axis=1) + res[j][chunk:]
                             + z0_ref[j, c])
            return carry

        lax.fori_loop(0, nch, body, 0)

        inv_n = 1.0 / RWKV_DIM
        for j in gs:
            cols = slice(j * gl, (j + 1) * gl)
            y = y_ref[j]
            mean = headsum(y) * inv_n
            dlt = y - mean
            var = headsum(dlt * dlt, pieces=1) * inv_n
            yn = dlt * lax.rsqrt(var + GN_EPS) * lg_ref[:, cols] + lb_ref[:, cols]
            o_ref[:, cols] = ((yn + bonus_ref[j]) * gate_ref[j]).astype(o_ref.dtype)


def _wkv(r, lw, km, v, kr, a, g, r_k, lnx_g, lnx_b, batch, seq, ts, chunk, gps):
    m, d = r.shape
    gl = GROUP_LANES
    groups = d // gl
    nt = seq // ts
    nch = ts // chunk
    row = pl.BlockSpec((ts, gps * gl), lambda b, t, j: (b * nt + t, j))
    vec = pl.BlockSpec((1, gps * gl), lambda b, t, j: (0, j))
    full = pl.BlockSpec((1, d), lambda b, t, j: (0, 0))
    return pl.pallas_call(
        functools.partial(_wkv_kernel, chunk=chunk, groups=groups, gps=gps),
        grid=(batch, nt, groups // gps),
        in_specs=[row] * 7 + [vec, full, full],
        out_specs=pl.BlockSpec((ts, d), lambda b, t, j: (b * nt + t, 0)),
        out_shape=jax.ShapeDtypeStruct((m, d), BF16),
        scratch_shapes=[pltpu.VMEM((groups, gl, gl), F32),
                        pltpu.VMEM((groups, ts, gl), F32),
                        pltpu.VMEM((groups, ts, gl), F32),
                        pltpu.VMEM((groups, nch, chunk + gl, gl), BF16),
                        pltpu.VMEM((groups, nch, gl, gl), F32),
                        pltpu.VMEM((groups, nch, gl, LANES), F32),
                        pltpu.VMEM((groups, ts, gl), F32),
                        pltpu.VMEM((groups, ts, gl), F32)],
        compiler_params=_cparams(("parallel", "arbitrary", "arbitrary"), 56),
        name="wkv7",
    )(r, lw, km, v, kr, a, g, r_k, lnx_g, lnx_b)


def kernel(x, norm_mix_g, norm_ffn_g, ab_w_in, hgrn_lower_bounds, hgrn_norm_g, fox_forget_bias,
           fox_q_norm_g, fox_k_norm_g, ab_w_out, rwkv_mu, rwkv_w_rkv, rwkv_w0, rwkv_w1, rwkv_w2,
           rwkv_a0, rwkv_a1, rwkv_a2, rwkv_g1, rwkv_g2, rwkv_k_k, rwkv_k_a, rwkv_r_k,
           rwkv_lnx_g, rwkv_lnx_b, rwkv_w_o, mlp_w_up, mlp_w_down):
    batch, seq, d = x.shape
    m = batch * seq
    tm = min(256, seq)
    ts = min(512, seq)
    tq = min(512, seq)
    row = lambda a: a.reshape(1, -1).astype(F32)
    bf = lambda a: a.astype(BF16)

    lb_all = jnp.cumsum(jax.nn.softmax(hgrn_lower_bounds.astype(F32), axis=0), axis=0)
    h = x.reshape(m, d)

    n_in = ab_w_in.shape[-1]
    n_pad = (-n_in) % LANES
    w_in = bf(jnp.pad(ab_w_in[0], ((0, 0), (0, n_pad))))
    proj = _inproj(h, row(norm_mix_g[0]), w_in, tm)
    ya = _hgrn(proj, row(lb_all[0]), row(hgrn_norm_g[0]), batch, seq, ts, 64)
    fb = jnp.pad(row(fox_forget_bias[0]), ((0, 0), (0, LANES - FOX_HEADS)))
    qa, ka, vb = _foxprep(proj, fb, row(fox_q_norm_g[0]), row(fox_k_norm_g[0]), batch, seq, ts)
    yb = _fox(qa, ka, vb, proj, batch, seq, tq, 8)
    h = _mix_mlp([ya, yb], bf(ab_w_out[0]), h, row(norm_ffn_g[0]), bf(mlp_w_up[0]), bf(mlp_w_down[0]),
                 tm, 1024, "mix_mlp0")

    outs = _rwkvproj(h, row(norm_mix_g[1]), rwkv_mu[0].astype(F32),
                     bf(rwkv_w_rkv[0, 0]), bf(rwkv_w_rkv[0, 1]), bf(rwkv_w_rkv[0, 2]),
                     bf(rwkv_w1[0]), bf(rwkv_w2[0]), bf(rwkv_a1[0]), bf(rwkv_a2[0]),
                     bf(rwkv_g1[0]), bf(rwkv_g2[0]), row(rwkv_w0[0]), row(rwkv_a0[0]),
                     row(rwkv_k_k[0]), row(rwkv_k_a[0]), seq, tm)
    z = _wkv(*outs, row(rwkv_r_k[0]), row(rwkv_lnx_g[0]), row(rwkv_lnx_b[0]),
             batch, seq, min(512, seq), 64, 2)
    h = _mix_mlp([z], bf(rwkv_w_o[0]), h, row(norm_ffn_g[1]), bf(mlp_w_up[1]), bf(mlp_w_down[1]),
                 tm, 1024, "mix_mlp1")
    return h.reshape(batch, seq, d)
```

```python
import functools
from typing import NamedTuple

import jax
import jax.numpy as jnp
import numpy as np
from jax import lax
from jax.experimental import pallas as pl
from jax.experimental.pallas import tpu as pltpu

F32 = jnp.float32
BF16 = jnp.bfloat16

RMS_EPS = 1e-6
GN_EPS = 64e-5

HGRN_HEADS = 4
HGRN_DIM = 128
FOX_HEADS = 8
FOX_DIM = 64
RWKV_DIM = 64
RWKV_GROUP = 4
GROUP_LANES = RWKV_GROUP * RWKV_DIM
LANES = 128
NEG_BIG = -1e30
LOG2E = 1.4426950408889634

NT_DIMS = (((1,), (1,)), ((), ()))


class _Tiles(NamedTuple):
    rows: int
    mlp_rows: int
    time: int
    chunk: int
    ff_chunk: int


def _tiles(seq):
    return _Tiles(rows=min(256, seq), mlp_rows=min(512, seq), time=min(512, seq), chunk=64,
                  ff_chunk=1024)


def _cparams(sem, vmem_mb):
    return pltpu.CompilerParams(dimension_semantics=sem, vmem_limit_bytes=vmem_mb * 1024 * 1024)


def _dot(a, b):
    return jnp.dot(a, b, preferred_element_type=F32)


def _dot_nt(a, b):
    return lax.dot_general(a, b, NT_DIMS, preferred_element_type=F32)


def _rms(x, g):
    return x * lax.rsqrt(jnp.mean(x * x, axis=-1, keepdims=True) + RMS_EPS) * g


def _sigmoid(x):
    return 1.0 / (1.0 + jnp.exp(-x))


def _log_sigmoid(x):
    return jnp.minimum(x, 0.0) - jnp.log(1.0 + jnp.exp(-jnp.abs(x)))


def _tril_mask(n, strict=False):
    r = lax.broadcasted_iota(jnp.int32, (n, n), 0)
    c = lax.broadcasted_iota(jnp.int32, (n, n), 1)
    return (c < r) if strict else (c <= r)


def _split3(x):
    hi = x.astype(BF16)
    r1 = x - hi.astype(F32)
    mid = r1.astype(BF16)
    lo = (r1 - mid.astype(F32)).astype(BF16)
    return hi, mid, lo


def _cumsum_rows(x, tril_bf16, pieces=3):
    parts = _split3(x)[:pieces]
    out = _dot(tril_bf16, parts[0])
    for part in parts[1:]:
        out = out + _dot(tril_bf16, part)
    return out


def _inproj_kernel(x_ref, g_ref, w_ref, o_ref):
    hn = _rms(x_ref[...], g_ref[...]).astype(BF16)
    o_ref[...] = _dot(hn, w_ref[...])


def _inproj(x2, g, w, tm):
    m, d = x2.shape
    n = w.shape[1]
    return pl.pallas_call(
        _inproj_kernel,
        grid=(m // tm,),
        in_specs=[
            pl.BlockSpec((tm, d), lambda i: (i, 0)),
            pl.BlockSpec((1, d), lambda i: (0, 0)),
            pl.BlockSpec((d, n), lambda i: (0, 0)),
        ],
        out_specs=pl.BlockSpec((tm, n), lambda i: (i, 0)),
        out_shape=jax.ShapeDtypeStruct((m, n), F32),
        compiler_params=_cparams(("parallel",), 48),
        name="inproj",
    )(x2, g, w)


def _hgrn_kernel(q_ref, f_ref, i_ref, g_ref, lb_ref, ng_ref, o_ref, st_ref, *, chunk):
    @pl.when(pl.program_id(1) == 0)
    def _():
        st_ref[...] = jnp.zeros_like(st_ref)

    ts = q_ref.shape[0]
    causal = _tril_mask(chunk)
    tril_b = jnp.where(causal, 1.0, 0.0).astype(BF16)
    mid = chunk // 2

    hs = range(HGRN_HEADS)
    nch = ts // chunk
    tiles = [(c, h) for c in range(nch) for h in hs]
    blk = lambda ref, c, h: ref[c * chunk:(c + 1) * chunk, h * HGRN_DIM:(h + 1) * HGRN_DIM]
    lbs = [lb_ref[:, h * HGRN_DIM:(h + 1) * HGRN_DIM] for h in hs]

    f = [lbs[h] + (1.0 - lbs[h]) * _sigmoid(blk(f_ref, c, h)) for c, h in tiles]
    b = [_cumsum_rows(jnp.log(x), tril_b) for x in f]
    b_mid = [x[mid - 1:mid, :] for x in b]
    b_last = [x[chunk - 1:chunk, :] for x in b]
    q = [blk(q_ref, c, h) * _sigmoid(blk(q_ref, c, h)) for c, h in tiles]
    k = [1.0 - x for x in f]
    vb = [blk(i_ref, c, h).astype(BF16) for c, h in tiles]
    n = range(len(tiles))
    s = [_dot_nt((q[i] * jnp.exp(b[i] - b_mid[i])).astype(BF16),
                 (k[i] * jnp.exp(b_mid[i] - b[i])).astype(BF16)) for i in n]
    o = [_dot(jnp.where(causal, s[i], 0.0).astype(BF16), vb[i]) for i in n]
    inc = [_dot(blk(i_ref, c, h).T.astype(BF16), (k[i] * jnp.exp(b_last[i] - b[i])).astype(BF16))
           for i, (c, h) in enumerate(tiles)]
    dec = [jnp.exp(x) for x in b_last]

    st = [st_ref[h] for h in hs]
    st_in = []
    for i, (c, h) in enumerate(tiles):
        st_in.append(st[h].astype(BF16))
        st[h] = st[h] * dec[i] + inc[i]
    for h in hs:
        st_ref[h] = st[h]

    for i, (c, h) in enumerate(tiles):
        oi = o[i] + _dot_nt((q[i] * jnp.exp(b[i])).astype(BF16), st_in[i])
        ag = blk(g_ref, c, h)
        on = _rms(oi, ng_ref[:, h * HGRN_DIM:(h + 1) * HGRN_DIM])
        o_ref[c * chunk:(c + 1) * chunk, h * HGRN_DIM:(h + 1) * HGRN_DIM] = (
            on * (ag * _sigmoid(ag))).astype(o_ref.dtype)


def _hgrn(proj, lb, ng, batch, seq, ts, chunk):
    m = proj.shape[0]
    w = HGRN_HEADS * HGRN_DIM
    nt = seq // ts
    spec = lambda j: pl.BlockSpec((ts, w), lambda b, t, j=j: (b * nt + t, j))
    vec = pl.BlockSpec((1, w), lambda b, t: (0, 0))
    return pl.pallas_call(
        functools.partial(_hgrn_kernel, chunk=chunk),
        grid=(batch, nt),
        in_specs=[spec(0), spec(1), spec(2), spec(3), vec, vec],
        out_specs=pl.BlockSpec((ts, w), lambda b, t: (b * nt + t, 0)),
        out_shape=jax.ShapeDtypeStruct((m, w), BF16),
        scratch_shapes=[pltpu.VMEM((HGRN_HEADS, HGRN_DIM, HGRN_DIM), F32)],
        compiler_params=_cparams(("parallel", "arbitrary"), 32),
        name="hgrn2",
    )(proj, proj, proj, proj, lb, ng)


def _foxprep_kernel(q_ref, k_ref, v_ref, f_ref, fb_ref, qg_ref, kg_ref, hsum_ref, place_ref,
                    cq_ref, ck_ref, oq_ref, ok_ref, qt_ref, ka_ref, vt_ref, carry_ref):
    @pl.when(pl.program_id(1) == 0)
    def _():
        carry_ref[...] = jnp.zeros_like(carry_ref)

    ts = q_ref.shape[0]
    tril_b = jnp.where(_tril_mask(ts), 1.0, 0.0).astype(BF16)
    lf = _log_sigmoid(f_ref[...] + fb_ref[...])
    c = _cumsum_rows(lf, tril_b) + carry_ref[...]
    carry_ref[...] = c[ts - 1:ts, :]
    pieces = jnp.concatenate(_split3(c * LOG2E), axis=1)

    def headnorm(x, g):
        xx = x * x
        hi = xx.astype(BF16)
        lo = (xx - hi.astype(F32)).astype(BF16)
        ss = _dot(hi, hsum_ref[...]) + _dot(lo, hsum_ref[...])
        return x * lax.rsqrt(ss * (1.0 / FOX_DIM) + RMS_EPS) * g

    qn = (headnorm(q_ref[...], qg_ref[...]) * (FOX_DIM ** -0.5 * LOG2E)).astype(BF16)
    kn = headnorm(k_ref[...], kg_ref[...]).astype(BF16)
    qa = _dot(qn, place_ref[...]) + _dot(pieces, cq_ref[...]) + oq_ref[...]
    ka = _dot(kn, place_ref[...]) + _dot(pieces, ck_ref[...]) + ok_ref[...]
    ka_ref[...] = ka.astype(BF16)
    for h in range(FOX_HEADS):
        qt_ref[0, h] = qa[:, h * LANES:(h + 1) * LANES].T.astype(BF16)
    for r in range(FOX_HEADS // 2):
        vt_ref[0, r, 0] = v_ref[:, r * LANES:(r + 1) * LANES].T.astype(BF16)


def _foxprep(proj, fb, qg, kg, batch, seq, ts):
    m = proj.shape[0]
    w = FOX_HEADS * FOX_DIM
    nt = seq // ts
    pairs = FOX_HEADS // 2
    spec = lambda j: pl.BlockSpec((ts, w), lambda b, t, j=j: (b * nt + t, j))
    fcol = (8 * w) // LANES
    const = lambda a: pl.BlockSpec(a.shape, lambda b, t: (0, 0))

    wa = FOX_HEADS * LANES
    ch = np.arange(w)
    head, dim = ch // FOX_DIM, ch % FOX_DIM
    hsum = (head[:, None] == head[None, :]).astype(np.float32)
    place = np.zeros((w, wa), np.float32)
    place[ch, head * LANES + dim] = 1.0
    cq = np.zeros((3 * LANES, wa), np.float32)
    ck = np.zeros((3 * LANES, wa), np.float32)
    oq = np.zeros((1, wa), np.float32)
    ok = np.zeros((1, wa), np.float32)
    for h in range(FOX_HEADS):
        for p in range(3):
            cq[p * LANES + h, h * LANES + FOX_DIM + p] = 1.0
            ck[p * LANES + h, h * LANES + FOX_DIM + 3 + p] = -1.0
            oq[0, h * LANES + FOX_DIM + 3 + p] = 1.0
            ok[0, h * LANES + FOX_DIM + p] = 1.0
    consts = [jnp.asarray(a, BF16) for a in (hsum, place, cq, ck)] + [jnp.asarray(oq), jnp.asarray(ok)]
    qg = jnp.tile(qg, (1, FOX_HEADS))
    kg = jnp.tile(kg, (1, FOX_HEADS))
    return pl.pallas_call(
        _foxprep_kernel,
        grid=(batch, nt),
        in_specs=[spec(4), spec(5), spec(6),
                  pl.BlockSpec((ts, LANES), lambda b, t: (b * nt + t, fcol)),
                  const(fb), const(qg), const(kg)] + [const(a) for a in consts],
        out_specs=[pl.BlockSpec((1, FOX_HEADS, LANES, ts), lambda b, t: (b, 0, 0, t)),
                   pl.BlockSpec((ts, FOX_HEADS * LANES), lambda b, t: (b * nt + t, 0)),
                   pl.BlockSpec((1, pairs, 1, LANES, ts), lambda b, t: (b, 0, t, 0, 0))],
        out_shape=[jax.ShapeDtypeStruct((batch, FOX_HEADS, LANES, seq), BF16),
                   jax.ShapeDtypeStruct((m, FOX_HEADS * LANES), BF16),
                   jax.ShapeDtypeStruct((batch, pairs, nt, LANES, ts), BF16)],
        scratch_shapes=[pltpu.VMEM((1, LANES), F32)],
        compiler_params=_cparams(("parallel", "arbitrary"), 32),
        name="foxprep",
    )(proj, proj, proj, proj, fb, qg, kg, *consts)


def _fox_kernel(qt_ref, k_ref, vt_ref, g_ref, o_ref, *, tq, heads):
    i = pl.program_id(2)
    hs = range(heads)
    row = lax.broadcasted_iota(jnp.int32, (tq, tq), 0)
    col = lax.broadcasted_iota(jnp.int32, (tq, tq), 1)
    visible = col >= row
    qts = [qt_ref[0, h] for h in hs]
    vrows = [slice((h % 2) * FOX_DIM, (h % 2 + 1) * FOX_DIM) for h in hs]

    def step(j, carry, masked):
        ms, ls, accs = carry
        rows = pl.ds(pl.multiple_of(j * tq, tq), tq)
        def scores(h):
            s = _dot(k_ref[rows, h * LANES:(h + 1) * LANES], qts[h])
            return jnp.where(visible, s, NEG_BIG) if masked else s

        def softmax(h, s):
            m_new = jnp.maximum(ms[h], jnp.max(s, axis=0, keepdims=True))
            alpha = jnp.exp2(ms[h] - m_new)
            p = jnp.exp2(s - m_new)
            return m_new, alpha, alpha * ls[h] + jnp.sum(p, axis=0, keepdims=True), p.astype(BF16)

        def values(h, alpha, p):
            return accs[h] * alpha + _dot(vt_ref[0, h // 2, j, vrows[h], :], p)

        s, sm, out = {}, {}, {}
        for t in range(heads + 2):
            if t < heads:
                s[t] = scores(t)
            if 0 <= t - 1 < heads:
                sm[t - 1] = softmax(t - 1, s[t - 1])
            if 0 <= t - 2 < heads:
                out[t - 2] = values(t - 2, sm[t - 2][1], sm[t - 2][3])
        return (tuple(sm[h][0] for h in hs), tuple(sm[h][2] for h in hs), tuple(out[h] for h in hs))

    neg = jnp.full((1, tq), NEG_BIG, F32)
    zero = jnp.zeros((1, tq), F32)
    init = ((neg,) * heads, (zero,) * heads, (jnp.zeros((FOX_DIM, tq), F32),) * heads)
    carry = lax.fori_loop(0, i, lambda j, c: step(j, c, False), init)
    _, ls, accs = step(i, carry, True)
    for r in range(heads // 2):
        cols = slice(r * LANES, (r + 1) * LANES)
        out = jnp.concatenate([accs[2 * r] / ls[2 * r], accs[2 * r + 1] / ls[2 * r + 1]], axis=0)
        o_ref[:, cols] = (out.T * _sigmoid(g_ref[:, cols])).astype(o_ref.dtype)


def _fox(qt, ka, vt, proj, batch, seq, tq, heads):
    m = ka.shape[0]
    nq = seq // tq
    groups = FOX_HEADS // heads
    pairs = heads // 2
    wv = heads * FOX_DIM
    gcol = (7 * FOX_HEADS * FOX_DIM) // wv
    return pl.pallas_call(
        functools.partial(_fox_kernel, tq=tq, heads=heads),
        grid=(batch, groups, nq),
        in_specs=[pl.BlockSpec((1, heads, LANES, tq), lambda b, p, i: (b, p, 0, i)),
                  pl.BlockSpec((seq, heads * LANES), lambda b, p, i: (b, p)),
                  pl.BlockSpec((1, pairs, nq, LANES, tq), lambda b, p, i: (b, p, 0, 0, 0)),
                  pl.BlockSpec((tq, wv), lambda b, p, i: (b * nq + i, gcol + p))],
        out_specs=pl.BlockSpec((tq, wv), lambda b, p, i: (b * nq + i, p)),
        out_shape=jax.ShapeDtypeStruct((m, FOX_HEADS * FOX_DIM), BF16),
        compiler_params=_cparams(("parallel", "parallel", "arbitrary"), 48),
        name="fox_attention",
    )(qt, ka, vt, proj)


def _mix_mlp_kernel(*refs, n_mix, ck):
    ys = refs[:n_mix]
    w_ref, h_ref, g_ref, wu_ref, wd_ref, o_ref = refs[n_mix:]
    y = ys[0][...] if n_mix == 1 else jnp.concatenate([r[...] for r in ys], axis=1)
    x = h_ref[...] + _dot(y, w_ref[...])
    hn = _rms(x, g_ref[...]).astype(BF16)
    acc = x
    for c in range(wu_ref.shape[1] // ck):
        u = jnp.maximum(_dot(hn, wu_ref[:, c * ck:(c + 1) * ck]), 0.0)
        acc = acc + _dot((u * u).astype(BF16), wd_ref[c * ck:(c + 1) * ck, :])
    o_ref[...] = acc


def _mix_mlp(ys, w, h, g, wu, wd, tm, ck, name):
    m, d = h.shape
    const = lambda a: pl.BlockSpec(a.shape, lambda i: (0, 0), pipeline_mode=pl.Buffered(1))
    return pl.pallas_call(
        functools.partial(_mix_mlp_kernel, n_mix=len(ys), ck=ck),
        grid=(m // tm,),
        in_specs=([pl.BlockSpec((tm, y.shape[1]), lambda i: (i, 0)) for y in ys]
                  + [const(w), pl.BlockSpec((tm, d), lambda i: (i, 0)), const(g), const(wu), const(wd)]),
        out_specs=pl.BlockSpec((tm, d), lambda i: (i, 0)),
        out_shape=jax.ShapeDtypeStruct((m, d), F32),
        compiler_params=_cparams(("parallel",), 56),
        name=name,
    )(*ys, w, h, g, wu, wd)


def _rwkvproj_kernel(h_ref, hp_ref, g_ref, mu_ref, wr_ref, wk_ref, wv_ref, w1_ref, w2_ref,
                     a1_ref, a2_ref, g1_ref, g2_ref, w0_ref, a0_ref, kk_ref, ka_ref,
                     r_ref, lw_ref, km_ref, v_ref, kr_ref, a_ref, go_ref, *, tiles_per_seq):
    i = pl.program_id(0)
    tm = h_ref.shape[0]
    gn = g_ref[...]
    hn = _rms(h_ref[...], gn)
    prev = _rms(hp_ref[7:8, :], gn)
    prev = jnp.where(i % tiles_per_seq == 0, jnp.zeros_like(prev), prev)
    row = lax.broadcasted_iota(jnp.int32, hn.shape, 0)
    shifted = jnp.where(row == 0, jnp.broadcast_to(prev, hn.shape), pltpu.roll(hn, 1, 0))
    xx = shifted - hn
    mix = lambda j: (hn + xx * mu_ref[j:j + 1, :]).astype(BF16)
    r = _dot(mix(0), wr_ref[...])
    k = _dot(mix(2), wk_ref[...])
    v = _dot(mix(3), wv_ref[...])
    z = w0_ref[...] + _dot(jnp.tanh(_dot(mix(1), w1_ref[...])).astype(BF16), w2_ref[...])
    a = _sigmoid(a0_ref[...] + _dot(_dot(mix(4), a1_ref[...]).astype(BF16), a2_ref[...]))
    g = _dot(_sigmoid(_dot(mix(5), g1_ref[...])).astype(BF16), g2_ref[...])
    r_ref[...] = r
    lw_ref[...] = -jnp.exp(_log_sigmoid(z) - 0.5)
    km_ref[...] = k * (1.0 + (a - 1.0) * ka_ref[...])
    v_ref[...] = v
    kr_ref[...] = k * kk_ref[...]
    a_ref[...] = a
    go_ref[...] = g


def _rwkvproj(h, g, mu, wr, wk, wv, w1, w2, a1, a2, g1, g2, w0, a0, k_k, k_a, seq, tm):
    m, d = h.shape
    tiles_per_seq = seq // tm
    full = lambda a: pl.BlockSpec(a.shape, lambda i: (0,) * a.ndim)
    row = pl.BlockSpec((tm, d), lambda i: (i, 0))
    prev = pl.BlockSpec((8, d), lambda i: (jnp.maximum(i * (tm // 8) - 1, 0), 0))
    consts = (g, mu, wr, wk, wv, w1, w2, a1, a2, g1, g2, w0, a0, k_k, k_a)
    return pl.pallas_call(
        functools.partial(_rwkvproj_kernel, tiles_per_seq=tiles_per_seq),
        grid=(m // tm,),
        in_specs=[row, prev] + [full(a) for a in consts],
        out_specs=[row] * 7,
        out_shape=[jax.ShapeDtypeStruct((m, d), F32)] * 7,
        compiler_params=_cparams(("parallel",), 56),
        name="rwkv_proj",
    )(h, h, *consts)


def _wkv_kernel(r_ref, lw_ref, km_ref, v_ref, kr_ref, a_ref, g_ref, rk_ref, lg_ref, lb_ref,
                o_ref, st_ref, y_ref, q1_ref, lhs_ref, z0_ref, wc_ref, bonus_ref, gate_ref,
                *, chunk, groups, gps):
    step = pl.program_id(2)

    @pl.when(pl.program_id(1) == 0)
    def _():
        for gi in range(gps):
            st_ref[step * gps + gi] = jnp.zeros(st_ref.shape[1:], F32)

    ts = r_ref.shape[0]
    nch = ts // chunk
    gl = GROUP_LANES
    rb = lax.broadcasted_iota(jnp.int32, (gl, gl), 0) // RWKV_DIM
    cb = lax.broadcasted_iota(jnp.int32, (gl, gl), 1) // RWKV_DIM
    blockmask = rb == cb
    ones_bd = jnp.where(blockmask, 1.0, 0.0).astype(BF16)

    def headsum(x, pieces=2):
        hi = x.astype(BF16)
        out = _dot(hi, ones_bd)
        if pieces == 2:
            out = out + _dot((x - hi.astype(F32)).astype(BF16), ones_bd)
        return out

    def bd(y):
        reps = gl // y.shape[0]
        return jnp.where(blockmask, jnp.concatenate([y] * reps, axis=0), 0.0).astype(BF16)

    def hmm(x, y):
        return _dot(x.astype(BF16), bd(y))

    def tn_blocks(x, y):
        return jnp.where(blockmask, _dot(x.T.astype(BF16), y.astype(BF16)), 0.0)

    t_idx = lax.broadcasted_iota(jnp.int32, (chunk, gl), 0)
    s_idx = lax.broadcasted_iota(jnp.int32, (chunk, gl), 1) % RWKV_DIM
    strict = s_idx < t_idx
    incl = s_idx <= t_idx
    tril_b = jnp.where(_tril_mask(chunk), 1.0, 0.0).astype(BF16)
    zeros_c = jnp.zeros((chunk, gl), F32)

    tiles = [(gi, c) for gi in range(gps) for c in range(nch)]
    lanes_of = lambda gi: slice(gi * gl, (gi + 1) * gl)
    blk = lambda ref, gi, c: ref[c * chunk:(c + 1) * chunk, lanes_of(gi)]
    pad = lambda x: jnp.concatenate([x, zeros_c], axis=0)
    each = lambda fn, *lists: [fn(*args) for args in zip(*lists)]

    kkn_all = []
    for gi in range(gps):
        kr = kr_ref[:, lanes_of(gi)]
        kkn_all.append(kr * lax.rsqrt(jnp.maximum(headsum(kr * kr), 1e-24)))

    lw = [blk(lw_ref, gi, c) for gi, c in tiles]
    cum = each(lambda x: _cumsum_rows(x, tril_b, pieces=2), lw)
    c_last = [x[chunk - 1:chunk, :] for x in cum]
    kkn = [kkn_all[gi][c * chunk:(c + 1) * chunk, :] for gi, c in tiles]
    kka = [kkn[i] * blk(a_ref, gi, c) for i, (gi, c) in enumerate(tiles)]
    km = [blk(km_ref, gi, c) for gi, c in tiles]
    v = [blk(v_ref, gi, c) for gi, c in tiles]
    e_neg = [jnp.exp(-x) for x in cum]
    e_end = each(lambda cl, x: jnp.exp(cl - x), c_last, cum)
    at = each(lambda k, x, l: -k * jnp.exp(x - l), kkn, cum, lw)
    bt = each(jnp.multiply, kka, e_neg)
    kt = each(jnp.multiply, km, e_neg)
    rt = [blk(r_ref, gi, c) * jnp.exp(cum[i]) for i, (gi, c) in enumerate(tiles)]
    bw = each(jnp.multiply, kka, e_end)
    kw = each(jnp.multiply, km, e_end)
    for i, (gi, c) in enumerate(tiles):
        wc_ref[step * gps + gi, c] = jnp.broadcast_to(jnp.exp(c_last[i]), (LANES, gl)).T

    lhs = each(lambda a, r: jnp.concatenate([a, r], axis=0).astype(BF16), at, rt)
    pb = each(lambda l, b: _dot_nt(l, bd(b)), lhs, bt)
    pk = each(lambda l, k: _dot_nt(l, bd(k)), lhs, kt)
    a_ab = [jnp.where(strict, x[:chunk], 0.0) for x in pb]
    a_rb = [jnp.where(incl, x[chunk:], 0.0) for x in pb]
    a_ak = [jnp.where(strict, x[:chunk], 0.0) for x in pk]
    a_rk = [jnp.where(incl, x[chunk:], 0.0) for x in pk]

    e = [jnp.where((t_idx % 2 == 1) & (s_idx == t_idx - 1), x, 0.0) for x in a_ab]
    size = 2
    while size < chunk:
        off = ((t_idx // size) % 2 == 1) & (s_idx // size == t_idx // size - 1)
        a_off = [jnp.where(off, x, 0.0) for x in a_ab]
        t1 = each(lambda ao, ee: ao + hmm(ao, ee), a_off, e)
        e = each(lambda ee, tt: ee + tt + hmm(ee, tt), e, t1)
        size *= 2

    akv = each(hmm, a_ak, v)
    p1 = each(lambda x, ee: x + hmm(ee, x), akv, e)
    mat = each(lambda x, ee: x + hmm(ee, x), at, e)
    q1 = each(lambda ark, vv, arb, pp: hmm(ark, vv) + hmm(arb, pp), a_rk, v, a_rb, p1)
    r2 = each(lambda r, arb, mm: r + hmm(arb, mm), rt, a_rb, mat)
    pct = each(lambda b, mm: tn_blocks(pad(b), pad(mm)), bw, mat)
    z0 = each(lambda b, k, pp, vv: tn_blocks(jnp.concatenate([b, k], axis=0),
                                             jnp.concatenate([pp, vv], axis=0)), bw, kw, p1, v)
    for i, (gi, c) in enumerate(tiles):
        g = step * gps + gi
        q1_ref[g, c * chunk:(c + 1) * chunk, :] = q1[i]
        lhs_ref[g, c, :chunk, :] = r2[i].astype(BF16)
        lhs_ref[g, c, chunk:, :] = pct[i].astype(BF16)
        z0_ref[g, c] = z0[i]
    for gi in range(gps):
        cols = lanes_of(gi)
        bonus_ref[step * gps + gi] = headsum(
            r_ref[:, cols] * km_ref[:, cols] * rk_ref[:, cols], pieces=1) * v_ref[:, cols]
        gate_ref[step * gps + gi] = g_ref[:, cols]

    @pl.when(step == groups // gps - 1)
    def _():
        gs = range(groups)

        def body(c, carry):
            rows = pl.ds(pl.multiple_of(c * chunk, chunk), chunk)
            st = [st_ref[j] for j in gs]
            res = [_dot(lhs_ref[j, c], st[j].astype(BF16)) for j in gs]
            for j in gs:
                y_ref[j, rows, :] = q1_ref[j, rows, :] + res[j][:chunk]
                wc = wc_ref[j, c]
                st_ref[j] = (st[j] * jnp.concatenate([wc, wc], axis=1) + res[j][chunk:]
                             + z0_ref[j, c])
            return carry

        lax.fori_loop(0, nch, body, 0)

        inv_n = 1.0 / RWKV_DIM
        for j in gs:
            cols = slice(j * gl, (j + 1) * gl)
            y = y_ref[j]
            mean = headsum(y) * inv_n
            dlt = y - mean
            var = headsum(dlt * dlt, pieces=1) * inv_n
            yn = dlt * lax.rsqrt(var + GN_EPS) * lg_ref[:, cols] + lb_ref[:, cols]
            o_ref[:, cols] = ((yn + bonus_ref[j]) * gate_ref[j]).astype(o_ref.dtype)


def _wkv(r, lw, km, v, kr, a, g, r_k, lnx_g, lnx_b, batch, seq, ts, chunk, gps):
    m, d = r.shape
    gl = GROUP_LANES
    groups = d // gl
    nt = seq // ts
    nch = ts // chunk
    row = pl.BlockSpec((ts, gps * gl), lambda b, t, j: (b * nt + t, j))
    vec = pl.BlockSpec((1, gps * gl), lambda b, t, j: (0, j))
    full = pl.BlockSpec((1, d), lambda b, t, j: (0, 0))
    return pl.pallas_call(
        functools.partial(_wkv_kernel, chunk=chunk, groups=groups, gps=gps),
        grid=(batch, nt, groups // gps),
        in_specs=[row] * 7 + [vec, full, full],
        out_specs=pl.BlockSpec((ts, d), lambda b, t, j: (b * nt + t, 0)),
        out_shape=jax.ShapeDtypeStruct((m, d), BF16),
        scratch_shapes=[pltpu.VMEM((groups, gl, gl), F32),
                        pltpu.VMEM((groups, ts, gl), F32),
                        pltpu.VMEM((groups, ts, gl), F32),
                        pltpu.VMEM((groups, nch, chunk + gl, gl), BF16),
                        pltpu.VMEM((groups, nch, gl, gl), F32),
                        pltpu.VMEM((groups, nch, gl, LANES), F32),
                        pltpu.VMEM((groups, ts, gl), F32),
                        pltpu.VMEM((groups, ts, gl), F32)],
        compiler_params=_cparams(("parallel", "arbitrary", "arbitrary"), 56),
        name="wkv7",
    )(r, lw, km, v, kr, a, g, r_k, lnx_g, lnx_b)


def kernel(x, norm_mix_g, norm_ffn_g, ab_w_in, hgrn_lower_bounds, hgrn_norm_g, fox_forget_bias,
           fox_q_norm_g, fox_k_norm_g, ab_w_out, rwkv_mu, rwkv_w_rkv, rwkv_w0, rwkv_w1, rwkv_w2,
           rwkv_a0, rwkv_a1, rwkv_a2, rwkv_g1, rwkv_g2, rwkv_k_k, rwkv_k_a, rwkv_r_k,
           rwkv_lnx_g, rwkv_lnx_b, rwkv_w_o, mlp_w_up, mlp_w_down):
    batch, seq, d = x.shape
    m = batch * seq
    t = _tiles(seq)
    row = lambda a: a.reshape(1, -1).astype(F32)
    bf = lambda a: a.astype(BF16)

    lb_all = jnp.cumsum(jax.nn.softmax(hgrn_lower_bounds.astype(F32), axis=0), axis=0)
    h = x.reshape(m, d)

    n_in = ab_w_in.shape[-1]
    n_pad = (-n_in) % LANES
    w_in = bf(jnp.pad(ab_w_in[0], ((0, 0), (0, n_pad))))
    proj = _inproj(h, row(norm_mix_g[0]), w_in, t.rows)
    ya = _hgrn(proj, row(lb_all[0]), row(hgrn_norm_g[0]), batch, seq, t.time, t.chunk)
    fb = jnp.pad(row(fox_forget_bias[0]), ((0, 0), (0, LANES - FOX_HEADS)))
    qt, ka, vt = _foxprep(proj, fb, row(fox_q_norm_g[0]), row(fox_k_norm_g[0]), batch, seq, t.time)
    yb = _fox(qt, ka, vt, proj, batch, seq, t.time, FOX_HEADS)
    h = _mix_mlp([ya, yb], bf(ab_w_out[0]), h, row(norm_ffn_g[0]), bf(mlp_w_up[0]), bf(mlp_w_down[0]),
                 t.mlp_rows, t.ff_chunk, "mix_mlp0")

    outs = _rwkvproj(h, row(norm_mix_g[1]), rwkv_mu[0].astype(F32),
                     bf(rwkv_w_rkv[0, 0]), bf(rwkv_w_rkv[0, 1]), bf(rwkv_w_rkv[0, 2]),
                     bf(rwkv_w1[0]), bf(rwkv_w2[0]), bf(rwkv_a1[0]), bf(rwkv_a2[0]),
                     bf(rwkv_g1[0]), bf(rwkv_g2[0]), row(rwkv_w0[0]), row(rwkv_a0[0]),
                     row(rwkv_k_k[0]), row(rwkv_k_a[0]), seq, t.rows)
    z = _wkv(*outs, row(rwkv_r_k[0]), row(rwkv_lnx_g[0]), row(rwkv_lnx_b[0]),
             batch, seq, t.time, t.chunk, 1)
    h = _mix_mlp([z], bf(rwkv_w_o[0]), h, row(norm_ffn_g[1]), bf(mlp_w_up[1]), bf(mlp_w_down[1]),
                 t.mlp_rows, t.ff_chunk, "mix_mlp1")
    return h.reshape(batch, seq, d)
```

```python
import functools
import math
from typing import NamedTuple

import jax
import jax.numpy as jnp
import numpy as np
from jax import lax
from jax.experimental import pallas as pl
from jax.experimental.pallas import tpu as pltpu

F32 = jnp.float32
BF16 = jnp.bfloat16

RMS_EPS = 1e-6
GN_EPS = 64e-5

HGRN_HEADS = 4
HGRN_DIM = 128
FOX_HEADS = 8
FOX_DIM = 64
RWKV_DIM = 64
RWKV_GROUP = 4
GROUP_LANES = RWKV_GROUP * RWKV_DIM
LANES = 128
NEG_BIG = -1e30
LOG2E = 1.4426950408889634

NT_DIMS = (((1,), (1,)), ((), ()))


class _Tiles(NamedTuple):
    rows: int
    mlp_rows: int
    time: int
    chunk: int
    ff_chunk: int


def _tiles(seq):
    return _Tiles(rows=min(512, seq), mlp_rows=min(512, seq), time=min(512, seq), chunk=64,
                  ff_chunk=1024)


def _cparams(sem, vmem_mb):
    return pltpu.CompilerParams(dimension_semantics=sem, vmem_limit_bytes=vmem_mb * 1024 * 1024)


def _dot(a, b):
    return jnp.dot(a, b, preferred_element_type=F32)


def _dot_nt(a, b):
    return lax.dot_general(a, b, NT_DIMS, preferred_element_type=F32)


def _rms(x, g):
    return x * lax.rsqrt(jnp.mean(x * x, axis=-1, keepdims=True) + RMS_EPS) * g


def _sigmoid(x):
    return 1.0 / (1.0 + jnp.exp(-x))


def _log_sigmoid(x):
    return jnp.minimum(x, 0.0) - jnp.log(1.0 + jnp.exp(-jnp.abs(x)))


def _tril_mask(n, strict=False):
    r = lax.broadcasted_iota(jnp.int32, (n, n), 0)
    c = lax.broadcasted_iota(jnp.int32, (n, n), 1)
    return (c < r) if strict else (c <= r)


def _split3(x):
    hi = x.astype(BF16)
    r1 = x - hi.astype(F32)
    mid = r1.astype(BF16)
    lo = (r1 - mid.astype(F32)).astype(BF16)
    return hi, mid, lo


def _cumsum_rows(x, tril_bf16, pieces=3):
    parts = _split3(x)[:pieces]
    out = _dot(tril_bf16, parts[0])
    for part in parts[1:]:
        out = out + _dot(tril_bf16, part)
    return out


def _inproj_kernel(x_ref, g_ref, w_ref, o_ref):
    hn = _rms(x_ref[...], g_ref[...]).astype(BF16)
    o_ref[...] = _dot(hn, w_ref[...])


def _inproj(x2, g, w, tm):
    m, d = x2.shape
    n = w.shape[1]
    return pl.pallas_call(
        _inproj_kernel,
        grid=(m // tm,),
        in_specs=[
            pl.BlockSpec((tm, d), lambda i: (i, 0)),
            pl.BlockSpec((1, d), lambda i: (0, 0)),
            pl.BlockSpec((d, n), lambda i: (0, 0), pipeline_mode=pl.Buffered(1)),
        ],
        out_specs=pl.BlockSpec((tm, n), lambda i: (i, 0)),
        out_shape=jax.ShapeDtypeStruct((m, n), F32),
        compiler_params=_cparams(("parallel",), 48),
        name="inproj",
    )(x2, g, w)


def _hgrn_kernel(q_ref, f_ref, i_ref, g_ref, lb_ref, ng_ref, o_ref, st_ref, *, chunk):
    @pl.when(pl.program_id(1) == 0)
    def _():
        st_ref[...] = jnp.zeros_like(st_ref)

    ts = q_ref.shape[0]
    causal = _tril_mask(chunk)
    tril_b = jnp.where(causal, 1.0, 0.0).astype(BF16)
    mid = chunk // 2

    hs = range(HGRN_HEADS)
    nch = ts // chunk
    tiles = [(c, h) for c in range(nch) for h in hs]
    blk = lambda ref, c, h: ref[c * chunk:(c + 1) * chunk, h * HGRN_DIM:(h + 1) * HGRN_DIM]
    lbs = [lb_ref[:, h * HGRN_DIM:(h + 1) * HGRN_DIM] for h in hs]

    f = [lbs[h] + (1.0 - lbs[h]) * _sigmoid(blk(f_ref, c, h)) for c, h in tiles]
    b = [_cumsum_rows(jnp.log(x), tril_b) for x in f]
    b_mid = [x[mid - 1:mid, :] for x in b]
    b_last = [x[chunk - 1:chunk, :] for x in b]
    q = [blk(q_ref, c, h) * _sigmoid(blk(q_ref, c, h)) for c, h in tiles]
    k = [1.0 - x for x in f]
    vb = [blk(i_ref, c, h).astype(BF16) for c, h in tiles]
    n = range(len(tiles))
    s = [_dot_nt((q[i] * jnp.exp(b[i] - b_mid[i])).astype(BF16),
                 (k[i] * jnp.exp(b_mid[i] - b[i])).astype(BF16)) for i in n]
    o = [_dot(jnp.where(causal, s[i], 0.0).astype(BF16), vb[i]) for i in n]
    inc = [_dot(blk(i_ref, c, h).T.astype(BF16), (k[i] * jnp.exp(b_last[i] - b[i])).astype(BF16))
           for i, (c, h) in enumerate(tiles)]
    dec = [jnp.exp(x) for x in b_last]

    st = [st_ref[h] for h in hs]
    st_in = []
    for i, (c, h) in enumerate(tiles):
        st_in.append(st[h].astype(BF16))
        st[h] = st[h] * dec[i] + inc[i]
    for h in hs:
        st_ref[h] = st[h]

    for i, (c, h) in enumerate(tiles):
        oi = o[i] + _dot_nt((q[i] * jnp.exp(b[i])).astype(BF16), st_in[i])
        ag = blk(g_ref, c, h)
        on = _rms(oi, ng_ref[:, h * HGRN_DIM:(h + 1) * HGRN_DIM])
        o_ref[c * chunk:(c + 1) * chunk, h * HGRN_DIM:(h + 1) * HGRN_DIM] = (
            on * (ag * _sigmoid(ag))).astype(o_ref.dtype)


def _hgrn(proj, lb, ng, batch, seq, ts, chunk):
    m = proj.shape[0]
    w = HGRN_HEADS * HGRN_DIM
    nt = seq // ts
    spec = lambda j: pl.BlockSpec((ts, w), lambda b, t, j=j: (b * nt + t, j))
    vec = pl.BlockSpec((1, w), lambda b, t: (0, 0))
    return pl.pallas_call(
        functools.partial(_hgrn_kernel, chunk=chunk),
        grid=(batch, nt),
        in_specs=[spec(0), spec(1), spec(2), spec(3), vec, vec],
        out_specs=pl.BlockSpec((ts, w), lambda b, t: (b * nt + t, 0)),
        out_shape=jax.ShapeDtypeStruct((m, w), BF16),
        scratch_shapes=[pltpu.VMEM((HGRN_HEADS, HGRN_DIM, HGRN_DIM), F32)],
        compiler_params=_cparams(("parallel", "arbitrary"), 32),
        name="hgrn2",
    )(proj, proj, proj, proj, lb, ng)


def _foxprep_kernel(q_ref, k_ref, v_ref, f_ref, fb_ref, qg_ref, kg_ref, hsum_ref, wq_ref, wk_ref,
                    oq_ref, ok_ref, qt_ref, ka_ref, vt_ref, carry_ref):
    @pl.when(pl.program_id(1) == 0)
    def _():
        carry_ref[...] = jnp.zeros_like(carry_ref)

    ts = q_ref.shape[0]
    pairs = FOX_HEADS // 2
    tril_b = jnp.where(_tril_mask(ts), 1.0, 0.0).astype(BF16)
    lf = _log_sigmoid(f_ref[...] + fb_ref[...])
    c = _cumsum_rows(lf, tril_b) + carry_ref[...]
    carry_ref[...] = c[ts - 1:ts, :]
    c2 = c * LOG2E
    hi = c2.astype(BF16).astype(F32)
    rest = c2 - hi
    mid = rest.astype(BF16).astype(F32)
    lane = lax.broadcasted_iota(jnp.int32, c.shape, 1)
    pieces = jnp.where(lane < FOX_HEADS, hi,
                       jnp.where(lane < 2 * FOX_HEADS, mid, rest - mid)).astype(BF16)

    def headnorm(x_ref, g_ref, r, scale):
        cols = slice(r * 2 * LANES, (r + 1) * 2 * LANES)
        x = x_ref[:, cols]
        xx = x * x
        xh = xx.astype(BF16)
        ss = _dot(xh, hsum_ref[...]) + _dot((xx - xh.astype(F32)).astype(BF16), hsum_ref[...])
        return (x * lax.rsqrt(ss * (1.0 / FOX_DIM) + RMS_EPS) * (g_ref[:, cols] * scale)).astype(BF16)

    qn = [headnorm(q_ref, qg_ref, r, FOX_DIM ** -0.5 * LOG2E) for r in range(pairs // 2)]
    kn = [headnorm(k_ref, kg_ref, r, 1.0) for r in range(pairs // 2)]
    for r in range(pairs):
        src = slice((r % 2) * LANES, (r % 2 + 1) * LANES)
        out = slice(r * 2 * LANES, (r + 1) * 2 * LANES)
        qa = _dot(jnp.concatenate([qn[r // 2][:, src], pieces], axis=1), wq_ref[r]) + oq_ref[:, out]
        ka = _dot(jnp.concatenate([kn[r // 2][:, src], pieces], axis=1), wk_ref[r]) + ok_ref[:, out]
        ka_ref[:, out] = ka.astype(BF16)
        for hh in range(2):
            qt_ref[0, 2 * r + hh] = qa[:, hh * LANES:(hh + 1) * LANES].T.astype(BF16)
        vt_ref[0, r, 0] = v_ref[:, r * LANES:(r + 1) * LANES].T.astype(BF16)


def _foxprep(proj, fb, qg, kg, batch, seq, ts):
    m = proj.shape[0]
    w = FOX_HEADS * FOX_DIM
    nt = seq // ts
    pairs = FOX_HEADS // 2
    spec = lambda j: pl.BlockSpec((ts, w), lambda b, t, j=j: (b * nt + t, j))
    fcol = (8 * w) // LANES
    const = lambda a: pl.BlockSpec(a.shape, lambda b, t: (0,) * a.ndim)

    wa = FOX_HEADS * LANES
    ch = np.arange(2 * LANES)
    hsum = (ch[:, None] // FOX_DIM == ch[None, :] // FOX_DIM).astype(np.float32)
    wq = np.zeros((pairs, 2 * LANES, 2 * LANES), np.float32)
    wk = np.zeros((pairs, 2 * LANES, 2 * LANES), np.float32)
    oq = np.zeros((1, wa), np.float32)
    ok = np.zeros((1, wa), np.float32)
    for r in range(pairs):
        for hh in range(2):
            h = 2 * r + hh
            d = np.arange(FOX_DIM)
            wq[r, hh * FOX_DIM + d, hh * LANES + d] = 1.0
            wk[r, hh * FOX_DIM + d, hh * LANES + d] = 1.0
            for p in range(3):
                wq[r, LANES + p * FOX_HEADS + h, hh * LANES + FOX_DIM + p] = 1.0
                wk[r, LANES + p * FOX_HEADS + h, hh * LANES + FOX_DIM + 3 + p] = -1.0
                oq[0, h * LANES + FOX_DIM + 3 + p] = 1.0
                ok[0, h * LANES + FOX_DIM + p] = 1.0
    consts = [jnp.asarray(a, BF16) for a in (hsum, wq, wk)] + [jnp.asarray(oq), jnp.asarray(ok)]
    qg = jnp.tile(qg, (1, FOX_HEADS))
    kg = jnp.tile(kg, (1, FOX_HEADS))
    return pl.pallas_call(
        _foxprep_kernel,
        grid=(batch, nt),
        in_specs=[spec(4), spec(5), spec(6),
                  pl.BlockSpec((ts, LANES), lambda b, t: (b * nt + t, fcol)),
                  const(fb), const(qg), const(kg)] + [const(a) for a in consts],
        out_specs=[pl.BlockSpec((1, FOX_HEADS, LANES, ts), lambda b, t: (b, 0, 0, t)),
                   pl.BlockSpec((ts, FOX_HEADS * LANES), lambda b, t: (b * nt + t, 0)),
                   pl.BlockSpec((1, pairs, 1, LANES, ts), lambda b, t: (b, 0, t, 0, 0))],
        out_shape=[jax.ShapeDtypeStruct((batch, FOX_HEADS, LANES, seq), BF16),
                   jax.ShapeDtypeStruct((m, FOX_HEADS * LANES), BF16),
                   jax.ShapeDtypeStruct((batch, pairs, nt, LANES, ts), BF16)],
        scratch_shapes=[pltpu.VMEM((1, LANES), F32)],
        compiler_params=_cparams(("parallel", "arbitrary"), 32),
        name="foxprep",
    )(proj, proj, proj, proj, fb, qg, kg, *consts)


def _fox_kernel(qt_ref, k_ref, vt_ref, g_ref, o_ref, *, tq, heads):
    i = pl.program_id(2)
    hs = range(heads)
    row = lax.broadcasted_iota(jnp.int32, (tq, tq), 0)
    col = lax.broadcasted_iota(jnp.int32, (tq, tq), 1)
    visible = col >= row
    qts = [qt_ref[0, h] for h in hs]
    vrows = [slice((h % 2) * FOX_DIM, (h % 2 + 1) * FOX_DIM) for h in hs]

    def step(j, carry, masked):
        ms, ls, accs = carry
        rows = pl.ds(pl.multiple_of(j * tq, tq), tq)
        def scores(h):
            s = _dot(k_ref[rows, h * LANES:(h + 1) * LANES], qts[h])
            return jnp.where(visible, s, NEG_BIG) if masked else s

        def softmax(h, s):
            m_new = jnp.maximum(ms[h], jnp.max(s, axis=0, keepdims=True))
            alpha = jnp.exp2(ms[h] - m_new)
            p = jnp.exp2(s - m_new)
            return m_new, alpha, alpha * ls[h] + jnp.sum(p, axis=0, keepdims=True), p.astype(BF16)

        def values(h, alpha, p):
            return accs[h] * alpha + _dot(vt_ref[0, h // 2, j, vrows[h], :], p)

        s, sm, out = {}, {}, {}
        for t in range(heads + 2):
            if t < heads:
                s[t] = scores(t)
            if 0 <= t - 1 < heads:
                sm[t - 1] = softmax(t - 1, s[t - 1])
            if 0 <= t - 2 < heads:
                out[t - 2] = values(t - 2, sm[t - 2][1], sm[t - 2][3])
        return (tuple(sm[h][0] for h in hs), tuple(sm[h][2] for h in hs), tuple(out[h] for h in hs))

    neg = jnp.full((1, tq), NEG_BIG, F32)
    zero = jnp.zeros((1, tq), F32)
    init = ((neg,) * heads, (zero,) * heads, (jnp.zeros((FOX_DIM, tq), F32),) * heads)
    carry = lax.fori_loop(0, i, lambda j, c: step(j, c, False), init)
    _, ls, accs = step(i, carry, True)
    for r in range(heads // 2):
        cols = slice(r * LANES, (r + 1) * LANES)
        out = jnp.concatenate([accs[2 * r] / ls[2 * r], accs[2 * r + 1] / ls[2 * r + 1]], axis=0)
        o_ref[:, cols] = (out.T * _sigmoid(g_ref[:, cols])).astype(o_ref.dtype)


def _fox(qt, ka, vt, proj, batch, seq, tq, heads):
    m = ka.shape[0]
    nq = seq // tq
    groups = FOX_HEADS // heads
    pairs = heads // 2
    wv = heads * FOX_DIM
    gcol = (7 * FOX_HEADS * FOX_DIM) // wv
    return pl.pallas_call(
        functools.partial(_fox_kernel, tq=tq, heads=heads),
        grid=(batch, groups, nq),
        in_specs=[pl.BlockSpec((1, heads, LANES, tq), lambda b, p, i: (b, p, 0, i)),
                  pl.BlockSpec((seq, heads * LANES), lambda b, p, i: (b, p)),
                  pl.BlockSpec((1, pairs, nq, LANES, tq), lambda b, p, i: (b, p, 0, 0, 0)),
                  pl.BlockSpec((tq, wv), lambda b, p, i: (b * nq + i, gcol + p))],
        out_specs=pl.BlockSpec((tq, wv), lambda b, p, i: (b * nq + i, p)),
        out_shape=jax.ShapeDtypeStruct((m, FOX_HEADS * FOX_DIM), BF16),
        compiler_params=_cparams(("parallel", "parallel", "arbitrary"), 48),
        name="fox_attention",
    )(qt, ka, vt, proj)


def _mix_mlp_kernel(*refs, n_mix, ck):
    ys = refs[:n_mix]
    w_ref, h_ref, g_ref, wu_ref, wd_ref, o_ref = refs[n_mix:]
    y = ys[0][...] if n_mix == 1 else jnp.concatenate([r[...] for r in ys], axis=1)
    x = h_ref[...] + _dot(y, w_ref[...])
    hn = _rms(x, g_ref[...]).astype(BF16)
    acc = x
    for c in range(wu_ref.shape[1] // ck):
        u = jnp.maximum(_dot(hn, wu_ref[:, c * ck:(c + 1) * ck]), 0.0)
        acc = acc + _dot((u * u).astype(BF16), wd_ref[c * ck:(c + 1) * ck, :])
    o_ref[...] = acc


def _mix_mlp(ys, w, h, g, wu, wd, tm, ck, name):
    m, d = h.shape
    const = lambda a: pl.BlockSpec(a.shape, lambda i: (0, 0), pipeline_mode=pl.Buffered(1))
    return pl.pallas_call(
        functools.partial(_mix_mlp_kernel, n_mix=len(ys), ck=ck),
        grid=(m // tm,),
        in_specs=([pl.BlockSpec((tm, y.shape[1]), lambda i: (i, 0)) for y in ys]
                  + [const(w), pl.BlockSpec((tm, d), lambda i: (i, 0)), const(g), const(wu), const(wd)]),
        out_specs=pl.BlockSpec((tm, d), lambda i: (i, 0)),
        out_shape=jax.ShapeDtypeStruct((m, d), F32),
        compiler_params=_cparams(("parallel",), 56),
        name=name,
    )(*ys, w, h, g, wu, wd)


def _rwkvproj_kernel(h_ref, hp_ref, g_ref, mu_ref, wr_ref, wk_ref, wv_ref, w1_ref, w2_ref,
                     a1_ref, a2_ref, g1_ref, g2_ref, w0_ref, a0_ref, kk_ref, ka_ref,
                     r_ref, lw_ref, km_ref, v_ref, kr_ref, a_ref, go_ref, *, tiles_per_seq):
    i = pl.program_id(0)
    tm = h_ref.shape[0]
    gn = g_ref[...]
    hn = _rms(h_ref[...], gn)
    prev = _rms(hp_ref[7:8, :], gn)
    prev = jnp.where(i % tiles_per_seq == 0, jnp.zeros_like(prev), prev)
    row = lax.broadcasted_iota(jnp.int32, hn.shape, 0)
    shifted = jnp.where(row == 0, jnp.broadcast_to(prev, hn.shape), pltpu.roll(hn, 1, 0))
    xx = shifted - hn
    hn_b = hn.astype(BF16)
    xx_b = xx.astype(BF16)
    mix = lambda j: hn_b + xx_b * mu_ref[j:j + 1, :].astype(BF16)
    r = _dot(mix(0), wr_ref[...])
    k = _dot(mix(2), wk_ref[...])
    v = _dot(mix(3), wv_ref[...])
    z = w0_ref[...] + _dot(jnp.tanh(_dot(mix(1), w1_ref[...])).astype(BF16), w2_ref[...])
    a = _sigmoid(a0_ref[...] + _dot(_dot(mix(4), a1_ref[...]).astype(BF16), a2_ref[...]))
    g = _dot(_sigmoid(_dot(mix(5), g1_ref[...])).astype(BF16), g2_ref[...])
    r_ref[...] = r
    lw_ref[...] = _sigmoid(z) * (-math.exp(-0.5))
    km_ref[...] = k * (1.0 + (a - 1.0) * ka_ref[...])
    v_ref[...] = v
    kr_ref[...] = k * kk_ref[...]
    a_ref[...] = a
    go_ref[...] = g


def _rwkvproj(h, g, mu, wr, wk, wv, w1, w2, a1, a2, g1, g2, w0, a0, k_k, k_a, seq, tm):
    m, d = h.shape
    tiles_per_seq = seq // tm
    full = lambda a: pl.BlockSpec(a.shape, lambda i: (0,) * a.ndim, pipeline_mode=pl.Buffered(1))
    row = pl.BlockSpec((tm, d), lambda i: (i, 0))
    prev = pl.BlockSpec((8, d), lambda i: (jnp.maximum(i * (tm // 8) - 1, 0), 0))
    consts = (g, mu, wr, wk, wv, w1, w2, a1, a2, g1, g2, w0, a0, k_k, k_a)
    return pl.pallas_call(
        functools.partial(_rwkvproj_kernel, tiles_per_seq=tiles_per_seq),
        grid=(m // tm,),
        in_specs=[row, prev] + [full(a) for a in consts],
        out_specs=[row] * 7,
        out_shape=[jax.ShapeDtypeStruct((m, d), F32)] * 7,
        compiler_params=_cparams(("parallel",), 56),
        name="rwkv_proj",
    )(h, h, *consts)


def _wkv_kernel(r_ref, lw_ref, km_ref, v_ref, kr_ref, a_ref, g_ref, rk_ref, lg_ref, lb_ref,
                o_ref, st_ref, y_ref, q1_ref, lhs_ref, z0_ref, wc_ref, bonus_ref, gate_ref,
                *, chunk, groups, gps):
    step = pl.program_id(2)

    @pl.when(pl.program_id(1) == 0)
    def _():
        for gi in range(gps):
            st_ref[step * gps + gi] = jnp.zeros(st_ref.shape[1:], F32)

    ts = r_ref.shape[0]
    nch = ts // chunk
    gl = GROUP_LANES
    rb = lax.broadcasted_iota(jnp.int32, (gl, gl), 0) // RWKV_DIM
    cb = lax.broadcasted_iota(jnp.int32, (gl, gl), 1) // RWKV_DIM
    blockmask = rb == cb
    ones_bd = jnp.where(blockmask, 1.0, 0.0).astype(BF16)

    def headsum(x, pieces=2):
        hi = x.astype(BF16)
        out = _dot(hi, ones_bd)
        if pieces == 2:
            out = out + _dot((x - hi.astype(F32)).astype(BF16), ones_bd)
        return out

    def bd(y):
        reps = gl // y.shape[0]
        return jnp.where(blockmask, jnp.concatenate([y] * reps, axis=0), 0.0).astype(BF16)

    def hmm(x, y):
        return _dot(x.astype(BF16), bd(y))

    def tn_blocks(x, y):
        return jnp.where(blockmask, _dot(x.T.astype(BF16), y.astype(BF16)), 0.0)

    t_idx = lax.broadcasted_iota(jnp.int32, (chunk, gl), 0)
    s_idx = lax.broadcasted_iota(jnp.int32, (chunk, gl), 1) % RWKV_DIM
    strict = s_idx < t_idx
    incl = s_idx <= t_idx
    tril_b = jnp.where(_tril_mask(chunk), 1.0, 0.0).astype(BF16)
    zeros_c = jnp.zeros((chunk, gl), F32)

    tiles = [(gi, c) for gi in range(gps) for c in range(nch)]
    lanes_of = lambda gi: slice(gi * gl, (gi + 1) * gl)
    blk = lambda ref, gi, c: ref[c * chunk:(c + 1) * chunk, lanes_of(gi)]
    pad = lambda x: jnp.concatenate([x, zeros_c], axis=0)
    each = lambda fn, *lists: [fn(*args) for args in zip(*lists)]

    kkn_all = []
    for gi in range(gps):
        kr = kr_ref[:, lanes_of(gi)]
        kkn_all.append(kr * lax.rsqrt(jnp.maximum(headsum(kr * kr), 1e-24)))

    lw = [blk(lw_ref, gi, c) for gi, c in tiles]
    cum = each(lambda x: _cumsum_rows(x, tril_b, pieces=2), lw)
    c_last = [x[chunk - 1:chunk, :] for x in cum]
    kkn = [kkn_all[gi][c * chunk:(c + 1) * chunk, :] for gi, c in tiles]
    kka = [kkn[i] * blk(a_ref, gi, c) for i, (gi, c) in enumerate(tiles)]
    km = [blk(km_ref, gi, c) for gi, c in tiles]
    v = [blk(v_ref, gi, c) for gi, c in tiles]
    e_neg = [jnp.exp(-x) for x in cum]
    e_end = each(lambda cl, x: jnp.exp(cl - x), c_last, cum)
    at = each(lambda k, x, l: -k * jnp.exp(x - l), kkn, cum, lw)
    bt = each(jnp.multiply, kka, e_neg)
    kt = each(jnp.multiply, km, e_neg)
    rt = [blk(r_ref, gi, c) * jnp.exp(cum[i]) for i, (gi, c) in enumerate(tiles)]
    bw = each(jnp.multiply, kka, e_end)
    kw = each(jnp.multiply, km, e_end)
    for i, (gi, c) in enumerate(tiles):
        wc_ref[step * gps + gi, c] = jnp.broadcast_to(jnp.exp(c_last[i]), (LANES, gl)).T

    lhs = each(lambda a, r: jnp.concatenate([a, r], axis=0).astype(BF16), at, rt)
    pb = each(lambda l, b: _dot_nt(l, bd(b)), lhs, bt)
    pk = each(lambda l, k: _dot_nt(l, bd(k)), lhs, kt)
    a_ab = [jnp.where(strict, x[:chunk], 0.0) for x in pb]
    a_rb = [jnp.where(incl, x[chunk:], 0.0) for x in pb]
    a_ak = [jnp.where(strict, x[:chunk], 0.0) for x in pk]
    a_rk = [jnp.where(incl, x[chunk:], 0.0) for x in pk]

    e = [jnp.where((t_idx % 2 == 1) & (s_idx == t_idx - 1), x, 0.0) for x in a_ab]
    size = 2
    while size < chunk:
        off = ((t_idx // size) % 2 == 1) & (s_idx // size == t_idx // size - 1)
        a_off = [jnp.where(off, x, 0.0) for x in a_ab]
        t1 = each(lambda ao, ee: ao + hmm(ao, ee), a_off, e)
        e = each(lambda ee, tt: ee + tt + hmm(ee, tt), e, t1)
        size *= 2

    akv = each(hmm, a_ak, v)
    p1 = each(lambda x, ee: x + hmm(ee, x), akv, e)
    mat = each(lambda x, ee: x + hmm(ee, x), at, e)
    q1 = each(lambda ark, vv, arb, pp: hmm(ark, vv) + hmm(arb, pp), a_rk, v, a_rb, p1)
    r2 = each(lambda r, arb, mm: r + hmm(arb, mm), rt, a_rb, mat)
    pct = each(lambda b, mm: tn_blocks(pad(b), pad(mm)), bw, mat)
    z0 = each(lambda b, k, pp, vv: tn_blocks(jnp.concatenate([b, k], axis=0),
                                             jnp.concatenate([pp, vv], axis=0)), bw, kw, p1, v)
    for i, (gi, c) in enumerate(tiles):
        g = step * gps + gi
        q1_ref[g, c * chunk:(c + 1) * chunk, :] = q1[i]
        lhs_ref[g, c, :chunk, :] = r2[i].astype(BF16)
        lhs_ref[g, c, chunk:, :] = pct[i].astype(BF16)
        z0_ref[g, c] = z0[i]
    for gi in range(gps):
        cols = lanes_of(gi)
        bonus_ref[step * gps + gi] = headsum(
            r_ref[:, cols] * km_ref[:, cols] * rk_ref[:, cols], pieces=1) * v_ref[:, cols]
        gate_ref[step * gps + gi] = g_ref[:, cols]

    @pl.when(step == groups // gps - 1)
    def _():
        gs = range(groups)

        def body(c, carry):
            rows = pl.ds(pl.multiple_of(c * chunk, chunk), chunk)
            st = [st_ref[j] for j in gs]
            res = [_dot(lhs_ref[j, c], st[j].astype(BF16)) for j in gs]
            for j in gs:
                y_ref[j, rows, :] = q1_ref[j, rows, :] + res[j][:chunk]
                wc = wc_ref[j, c]
                st_ref[j] = (st[j] * jnp.concatenate([wc, wc], axis=1) + res[j][chunk:]
                             + z0_ref[j, c])
            return carry

        lax.fori_loop(0, nch, body, 0)

        inv_n = 1.0 / RWKV_DIM
        for j in gs:
            cols = slice(j * gl, (j + 1) * gl)
            y = y_ref[j]
            mean = headsum(y) * inv_n
            dlt = y - mean
            var = headsum(dlt * dlt, pieces=1) * inv_n
            yn = dlt * lax.rsqrt(var + GN_EPS) * lg_ref[:, cols] + lb_ref[:, cols]
            o_ref[:, cols] = ((yn + bonus_ref[j]) * gate_ref[j]).astype(o_ref.dtype)


def _wkv(r, lw, km, v, kr, a, g, r_k, lnx_g, lnx_b, batch, seq, ts, chunk, gps):
    m, d = r.shape
    gl = GROUP_LANES
    groups = d // gl
    nt = seq // ts
    nch = ts // chunk
    row = pl.BlockSpec((ts, gps * gl), lambda b, t, j: (b * nt + t, j))
    vec = pl.BlockSpec((1, gps * gl), lambda b, t, j: (0, j))
    full = pl.BlockSpec((1, d), lambda b, t, j: (0, 0))
    return pl.pallas_call(
        functools.partial(_wkv_kernel, chunk=chunk, groups=groups, gps=gps),
        grid=(batch, nt, groups // gps),
        in_specs=[row] * 7 + [vec, full, full],
        out_specs=pl.BlockSpec((ts, d), lambda b, t, j: (b * nt + t, 0)),
        out_shape=jax.ShapeDtypeStruct((m, d), BF16),
        scratch_shapes=[pltpu.VMEM((groups, gl, gl), F32),
                        pltpu.VMEM((groups, ts, gl), F32),
                        pltpu.VMEM((groups, ts, gl), F32),
                        pltpu.VMEM((groups, nch, chunk + gl, gl), BF16),
                        pltpu.VMEM((groups, nch, gl, gl), F32),
                        pltpu.VMEM((groups, nch, gl, LANES), F32),
                        pltpu.VMEM((groups, ts, gl), F32),
                        pltpu.VMEM((groups, ts, gl), F32)],
        compiler_params=_cparams(("parallel", "arbitrary", "arbitrary"), 56),
        name="wkv7",
    )(r, lw, km, v, kr, a, g, r_k, lnx_g, lnx_b)


def kernel(x, norm_mix_g, norm_ffn_g, ab_w_in, hgrn_lower_bounds, hgrn_norm_g, fox_forget_bias,
           fox_q_norm_g, fox_k_norm_g, ab_w_out, rwkv_mu, rwkv_w_rkv, rwkv_w0, rwkv_w1, rwkv_w2,
           rwkv_a0, rwkv_a1, rwkv_a2, rwkv_g1, rwkv_g2, rwkv_k_k, rwkv_k_a, rwkv_r_k,
           rwkv_lnx_g, rwkv_lnx_b, rwkv_w_o, mlp_w_up, mlp_w_down):
    batch, seq, d = x.shape
    m = batch * seq
    t = _tiles(seq)
    row = lambda a: a.reshape(1, -1).astype(F32)
    bf = lambda a: a.astype(BF16)

    lb_all = jnp.cumsum(jax.nn.softmax(hgrn_lower_bounds.astype(F32), axis=0), axis=0)
    h = x.reshape(m, d)

    n_wide = ab_w_in.shape[-1] - FOX_HEADS
    gate_w = jnp.tile(ab_w_in[0][:, n_wide:], (1, 3))
    w_in = bf(jnp.pad(jnp.concatenate([ab_w_in[0][:, :n_wide], gate_w], axis=1),
                      ((0, 0), (0, LANES - 3 * FOX_HEADS))))
    proj = _inproj(h, row(norm_mix_g[0]), w_in, t.rows)
    ya = _hgrn(proj, row(lb_all[0]), row(hgrn_norm_g[0]), batch, seq, t.time, t.chunk)
    fb = jnp.pad(jnp.tile(row(fox_forget_bias[0]), (1, 3)), ((0, 0), (0, LANES - 3 * FOX_HEADS)))
    qt, ka, vt = _foxprep(proj, fb, row(fox_q_norm_g[0]), row(fox_k_norm_g[0]), batch, seq, t.time)
    yb = _fox(qt, ka, vt, proj, batch, seq, t.time, FOX_HEADS)
    h = _mix_mlp([ya, yb], bf(ab_w_out[0]), h, row(norm_ffn_g[0]), bf(mlp_w_up[0]), bf(mlp_w_down[0]),
                 t.mlp_rows, t.ff_chunk, "mix_mlp0")

    outs = _rwkvproj(h, row(norm_mix_g[1]), rwkv_mu[0].astype(F32),
                     bf(rwkv_w_rkv[0, 0]), bf(rwkv_w_rkv[0, 1]), bf(rwkv_w_rkv[0, 2]),
                     bf(rwkv_w1[0]), bf(rwkv_w2[0]), bf(rwkv_a1[0]), bf(rwkv_a2[0]),
                     bf(rwkv_g1[0]), bf(rwkv_g2[0]), row(rwkv_w0[0]), row(rwkv_a0[0]),
                     row(rwkv_k_k[0]), row(rwkv_k_a[0]), seq, t.rows)
    z = _wkv(*outs, row(rwkv_r_k[0]), row(rwkv_lnx_g[0]), row(rwkv_lnx_b[0]),
             batch, seq, t.time, t.chunk, 1)
    h = _mix_mlp([z], bf(rwkv_w_o[0]), h, row(norm_ffn_g[1]), bf(mlp_w_up[1]), bf(mlp_w_down[1]),
                 t.mlp_rows, t.ff_chunk, "mix_mlp1")
    return h.reshape(batch, seq, d)
```

```python
import functools
import math
from typing import NamedTuple

import jax
import jax.numpy as jnp
import numpy as np
from jax import lax
from jax.experimental import pallas as pl
from jax.experimental.pallas import tpu as pltpu

F32 = jnp.float32
BF16 = jnp.bfloat16

RMS_EPS = 1e-6
GN_EPS = 64e-5

HGRN_HEADS = 4
HGRN_DIM = 128
HGRN_SUB = 16
FOX_HEADS = 8
FOX_DIM = 64
RWKV_DIM = 64
RWKV_GROUP = 4
GROUP_LANES = RWKV_GROUP * RWKV_DIM
LANES = 128
NEG_BIG = -1e30
LOG2E = 1.4426950408889634

NT_DIMS = (((1,), (1,)), ((), ()))


class _Tiles(NamedTuple):
    rows: int
    mlp_rows: int
    time: int
    chunk: int
    ff_chunk: int


def _tiles(seq):
    return _Tiles(rows=min(512, seq), mlp_rows=min(512, seq), time=min(512, seq), chunk=64,
                  ff_chunk=1024)


def _cparams(sem, vmem_mb):
    return pltpu.CompilerParams(dimension_semantics=sem, vmem_limit_bytes=vmem_mb * 1024 * 1024)


def _dot(a, b):
    return jnp.dot(a, b, preferred_element_type=F32)


def _dot_nt(a, b):
    return lax.dot_general(a, b, NT_DIMS, preferred_element_type=F32)


def _rms(x, g):
    return x * lax.rsqrt(jnp.mean(x * x, axis=-1, keepdims=True) + RMS_EPS) * g


def _sigmoid(x):
    return 1.0 / (1.0 + jnp.exp(-x))


def _log_sigmoid(x):
    return jnp.minimum(x, 0.0) - jnp.log(1.0 + jnp.exp(-jnp.abs(x)))


def _tril_mask(n, strict=False):
    r = lax.broadcasted_iota(jnp.int32, (n, n), 0)
    c = lax.broadcasted_iota(jnp.int32, (n, n), 1)
    return (c < r) if strict else (c <= r)


def _split3(x):
    hi = x.astype(BF16)
    r1 = x - hi.astype(F32)
    mid = r1.astype(BF16)
    lo = (r1 - mid.astype(F32)).astype(BF16)
    return hi, mid, lo


def _cumsum_rows(x, tril_bf16, pieces=3):
    parts = _split3(x)[:pieces]
    out = _dot(tril_bf16, parts[0])
    for part in parts[1:]:
        out = out + _dot(tril_bf16, part)
    return out


def _inproj_kernel(x_ref, g_ref, w_ref, o_ref):
    hn = _rms(x_ref[...], g_ref[...]).astype(BF16)
    o_ref[...] = _dot(hn, w_ref[...])


def _inproj(x2, g, w, tm):
    m, d = x2.shape
    n = w.shape[1]
    return pl.pallas_call(
        _inproj_kernel,
        grid=(m // tm,),
        in_specs=[
            pl.BlockSpec((tm, d), lambda i: (i, 0)),
            pl.BlockSpec((1, d), lambda i: (0, 0)),
            pl.BlockSpec((d, n), lambda i: (0, 0), pipeline_mode=pl.Buffered(1)),
        ],
        out_specs=pl.BlockSpec((tm, n), lambda i: (i, 0)),
        out_shape=jax.ShapeDtypeStruct((m, n), F32),
        compiler_params=_cparams(("parallel",), 48),
        name="inproj",
    )(x2, g, w)


def _hgrn_kernel(q_ref, f_ref, i_ref, g_ref, lb_ref, ng_ref, o_ref, st_ref, *, chunk):
    @pl.when(pl.program_id(1) == 0)
    def _():
        st_ref[...] = jnp.zeros_like(st_ref)

    ts = q_ref.shape[0]
    tril_b = jnp.where(_tril_mask(chunk), 1.0, 0.0).astype(BF16)

    hs = range(HGRN_HEADS)
    nch = ts // chunk
    tiles = [(c, h) for c in range(nch) for h in hs]
    blk = lambda ref, c, h: ref[c * chunk:(c + 1) * chunk, h * HGRN_DIM:(h + 1) * HGRN_DIM]
    lbs = [lb_ref[:, h * HGRN_DIM:(h + 1) * HGRN_DIM] for h in hs]

    f = [lbs[h] + (1.0 - lbs[h]) * _sigmoid(blk(f_ref, c, h)) for c, h in tiles]
    b = [_cumsum_rows(jnp.log(x), tril_b) for x in f]
    b_last = [x[chunk - 1:chunk, :] for x in b]
    q = [blk(q_ref, c, h) * _sigmoid(blk(q_ref, c, h)) for c, h in tiles]
    k = [1.0 - x for x in f]
    vb = [blk(i_ref, c, h).astype(BF16) for c, h in tiles]
    n = range(len(tiles))

    sub = HGRN_SUB
    tril_sub = _tril_mask(sub)
    score_cols = [[] for _ in n]
    for j in range(chunk // sub):
        lo, hi = j * sub, (j + 1) * sub
        for i in n:
            bj = b[i][lo:hi]
            b_mid = b[i][lo + sub // 2 - 1:lo + sub // 2]
            diag = _dot_nt((q[i][lo:hi] * jnp.exp(bj - b_mid)).astype(BF16),
                           (k[i][lo:hi] * jnp.exp(b_mid - bj)).astype(BF16))
            parts = [jnp.where(tril_sub, diag, 0.0)]
            if lo > 0:
                parts.insert(0, jnp.zeros((lo, sub), F32))
            if hi < chunk:
                b_end = b[i][hi - 1:hi]
                parts.append(_dot_nt((q[i][hi:] * jnp.exp(b[i][hi:] - b_end)).astype(BF16),
                                     (k[i][lo:hi] * jnp.exp(b_end - bj)).astype(BF16)))
            score_cols[i].append(jnp.concatenate(parts, axis=0).astype(BF16))
    o = []
    for i in n:
        acc = _dot(score_cols[i][0], vb[i][:sub])
        for j in range(1, chunk // sub):
            acc = acc + _dot(score_cols[i][j], vb[i][j * sub:(j + 1) * sub])
        o.append(acc)
    inc = [_dot(blk(i_ref, c, h).T.astype(BF16), (k[i] * jnp.exp(b_last[i] - b[i])).astype(BF16))
           for i, (c, h) in enumerate(tiles)]
    dec = [jnp.exp(x) for x in b_last]

    st = [st_ref[h] for h in hs]
    st_in = []
    for i, (c, h) in enumerate(tiles):
        st_in.append(st[h].astype(BF16))
        st[h] = st[h] * dec[i] + inc[i]
    for h in hs:
        st_ref[h] = st[h]

    for i, (c, h) in enumerate(tiles):
        oi = o[i] + _dot_nt((q[i] * jnp.exp(b[i])).astype(BF16), st_in[i])
        ag = blk(g_ref, c, h)
        on = _rms(oi, ng_ref[:, h * HGRN_DIM:(h + 1) * HGRN_DIM])
        o_ref[c * chunk:(c + 1) * chunk, h * HGRN_DIM:(h + 1) * HGRN_DIM] = (
            on * (ag * _sigmoid(ag))).astype(o_ref.dtype)


def _hgrn(proj, lb, ng, batch, seq, ts, chunk):
    m = proj.shape[0]
    w = HGRN_HEADS * HGRN_DIM
    nt = seq // ts
    spec = lambda j: pl.BlockSpec((ts, w), lambda b, t, j=j: (b * nt + t, j))
    vec = pl.BlockSpec((1, w), lambda b, t: (0, 0))
    return pl.pallas_call(
        functools.partial(_hgrn_kernel, chunk=chunk),
        grid=(batch, nt),
        in_specs=[spec(0), spec(1), spec(2), spec(3), vec, vec],
        out_specs=pl.BlockSpec((ts, w), lambda b, t: (b * nt + t, 0)),
        out_shape=jax.ShapeDtypeStruct((m, w), BF16),
        scratch_shapes=[pltpu.VMEM((HGRN_HEADS, HGRN_DIM, HGRN_DIM), F32)],
        compiler_params=_cparams(("parallel", "arbitrary"), 32),
        name="hgrn2",
    )(proj, proj, proj, proj, lb, ng)


def _foxprep_kernel(q_ref, k_ref, v_ref, f_ref, fb_ref, qg_ref, kg_ref, hsum_ref, wq_ref, wk_ref,
                    oq_ref, ok_ref, qt_ref, ka_ref, vt_ref, carry_ref):
    @pl.when(pl.program_id(1) == 0)
    def _():
        carry_ref[...] = jnp.zeros_like(carry_ref)

    ts = q_ref.shape[0]
    pairs = FOX_HEADS // 2
    tril_b = jnp.where(_tril_mask(ts), 1.0, 0.0).astype(BF16)
    lf = _log_sigmoid(f_ref[...] + fb_ref[...])
    c = _cumsum_rows(lf, tril_b) + carry_ref[...]
    carry_ref[...] = c[ts - 1:ts, :]
    c2 = c * LOG2E
    hi = c2.astype(BF16).astype(F32)
    rest = c2 - hi
    mid = rest.astype(BF16).astype(F32)
    lane = lax.broadcasted_iota(jnp.int32, c.shape, 1)
    pieces = jnp.where(lane < FOX_HEADS, hi,
                       jnp.where(lane < 2 * FOX_HEADS, mid, rest - mid)).astype(BF16)

    def headnorm(x_ref, g_ref, r, scale):
        cols = slice(r * 2 * LANES, (r + 1) * 2 * LANES)
        x = x_ref[:, cols]
        xx = x * x
        xh = xx.astype(BF16)
        ss = _dot(xh, hsum_ref[...]) + _dot((xx - xh.astype(F32)).astype(BF16), hsum_ref[...])
        return (x * lax.rsqrt(ss * (1.0 / FOX_DIM) + RMS_EPS) * (g_ref[:, cols] * scale)).astype(BF16)

    qn = [headnorm(q_ref, qg_ref, r, FOX_DIM ** -0.5 * LOG2E) for r in range(pairs // 2)]
    kn = [headnorm(k_ref, kg_ref, r, 1.0) for r in range(pairs // 2)]
    for r in range(pairs):
        src = slice((r % 2) * LANES, (r % 2 + 1) * LANES)
        out = slice(r * 2 * LANES, (r + 1) * 2 * LANES)
        qa = _dot(jnp.concatenate([qn[r // 2][:, src], pieces], axis=1), wq_ref[r]) + oq_ref[:, out]
        ka = _dot(jnp.concatenate([kn[r // 2][:, src], pieces], axis=1), wk_ref[r]) + ok_ref[:, out]
        ka_ref[:, out] = ka.astype(BF16)
        for hh in range(2):
            qt_ref[0, 2 * r + hh] = qa[:, hh * LANES:(hh + 1) * LANES].T.astype(BF16)
        vt_ref[0, r, 0] = v_ref[:, r * LANES:(r + 1) * LANES].T.astype(BF16)


def _foxprep(proj, fb, qg, kg, batch, seq, ts):
    m = proj.shape[0]
    w = FOX_HEADS * FOX_DIM
    nt = seq // ts
    pairs = FOX_HEADS // 2
    spec = lambda j: pl.BlockSpec((ts, w), lambda b, t, j=j: (b * nt + t, j))
    fcol = (8 * w) // LANES
    const = lambda a: pl.BlockSpec(a.shape, lambda b, t: (0,) * a.ndim)

    wa = FOX_HEADS * LANES
    ch = np.arange(2 * LANES)
    hsum = (ch[:, None] // FOX_DIM == ch[None, :] // FOX_DIM).astype(np.float32)
    wq = np.zeros((pairs, 2 * LANES, 2 * LANES), np.float32)
    wk = np.zeros((pairs, 2 * LANES, 2 * LANES), np.float32)
    oq = np.zeros((1, wa), np.float32)
    ok = np.zeros((1, wa), np.float32)
    for r in range(pairs):
        for hh in range(2):
            h = 2 * r + hh
            d = np.arange(FOX_DIM)
            wq[r, hh * FOX_DIM + d, hh * LANES + d] = 1.0
            wk[r, hh * FOX_DIM + d, hh * LANES + d] = 1.0
            for p in range(3):
                wq[r, LANES + p * FOX_HEADS + h, hh * LANES + FOX_DIM + p] = 1.0
                wk[r, LANES + p * FOX_HEADS + h, hh * LANES + FOX_DIM + 3 + p] = -1.0
                oq[0, h * LANES + FOX_DIM + 3 + p] = 1.0
                ok[0, h * LANES + FOX_DIM + p] = 1.0
    consts = [jnp.asarray(a, BF16) for a in (hsum, wq, wk)] + [jnp.asarray(oq), jnp.asarray(ok)]
    qg = jnp.tile(qg, (1, FOX_HEADS))
    kg = jnp.tile(kg, (1, FOX_HEADS))
    return pl.pallas_call(
        _foxprep_kernel,
        grid=(batch, nt),
        in_specs=[spec(4), spec(5), spec(6),
                  pl.BlockSpec((ts, LANES), lambda b, t: (b * nt + t, fcol)),
                  const(fb), const(qg), const(kg)] + [const(a) for a in consts],
        out_specs=[pl.BlockSpec((1, FOX_HEADS, LANES, ts), lambda b, t: (b, 0, 0, t)),
                   pl.BlockSpec((ts, FOX_HEADS * LANES), lambda b, t: (b * nt + t, 0)),
                   pl.BlockSpec((1, pairs, 1, LANES, ts), lambda b, t: (b, 0, t, 0, 0))],
        out_shape=[jax.ShapeDtypeStruct((batch, FOX_HEADS, LANES, seq), BF16),
                   jax.ShapeDtypeStruct((m, FOX_HEADS * LANES), BF16),
                   jax.ShapeDtypeStruct((batch, pairs, nt, LANES, ts), BF16)],
        scratch_shapes=[pltpu.VMEM((1, LANES), F32)],
        compiler_params=_cparams(("parallel", "arbitrary"), 32),
        name="foxprep",
    )(proj, proj, proj, proj, fb, qg, kg, *consts)


def _fox_kernel(qt_ref, k_ref, vt_ref, g_ref, o_ref, *, tq, heads):
    i = pl.program_id(2)
    hs = range(heads)
    row = lax.broadcasted_iota(jnp.int32, (tq, tq), 0)
    col = lax.broadcasted_iota(jnp.int32, (tq, tq), 1)
    visible = col >= row
    qts = [qt_ref[0, h] for h in hs]
    vrows = [slice((h % 2) * FOX_DIM, (h % 2 + 1) * FOX_DIM) for h in hs]

    def step(j, carry, masked):
        ms, ls, accs = carry
        rows = pl.ds(pl.multiple_of(j * tq, tq), tq)
        def scores(h):
            s = _dot(k_ref[rows, h * LANES:(h + 1) * LANES], qts[h])
            return jnp.where(visible, s, NEG_BIG) if masked else s

        def softmax(h, s):
            m_new = jnp.maximum(ms[h], jnp.max(s, axis=0, keepdims=True))
            alpha = jnp.exp2(ms[h] - m_new)
            p = jnp.exp2(s - m_new)
            return m_new, alpha, alpha * ls[h] + jnp.sum(p, axis=0, keepdims=True), p.astype(BF16)

        def values(h, alpha, p):
            return accs[h] * alpha + _dot(vt_ref[0, h // 2, j, vrows[h], :], p)

        s, sm, out = {}, {}, {}
        for t in range(heads + 2):
            if t < heads:
                s[t] = scores(t)
            if 0 <= t - 1 < heads:
                sm[t - 1] = softmax(t - 1, s[t - 1])
            if 0 <= t - 2 < heads:
                out[t - 2] = values(t - 2, sm[t - 2][1], sm[t - 2][3])
        return (tuple(sm[h][0] for h in hs), tuple(sm[h][2] for h in hs), tuple(out[h] for h in hs))

    neg = jnp.full((1, tq), NEG_BIG, F32)
    zero = jnp.zeros((1, tq), F32)
    init = ((neg,) * heads, (zero,) * heads, (jnp.zeros((FOX_DIM, tq), F32),) * heads)
    carry = lax.fori_loop(0, i, lambda j, c: step(j, c, False), init)
    _, ls, accs = step(i, carry, True)
    for r in range(heads // 2):
        cols = slice(r * LANES, (r + 1) * LANES)
        out = jnp.concatenate([accs[2 * r] / ls[2 * r], accs[2 * r + 1] / ls[2 * r + 1]], axis=0)
        o_ref[:, cols] = (out.T * _sigmoid(g_ref[:, cols])).astype(o_ref.dtype)


def _fox(qt, ka, vt, proj, batch, seq, tq, heads):
    m = ka.shape[0]
    nq = seq // tq
    groups = FOX_HEADS // heads
    pairs = heads // 2
    wv = heads * FOX_DIM
    gcol = (7 * FOX_HEADS * FOX_DIM) // wv
    return pl.pallas_call(
        functools.partial(_fox_kernel, tq=tq, heads=heads),
        grid=(batch, groups, nq),
        in_specs=[pl.BlockSpec((1, heads, LANES, tq), lambda b, p, i: (b, p, 0, i)),
                  pl.BlockSpec((seq, heads * LANES), lambda b, p, i: (b, p)),
                  pl.BlockSpec((1, pairs, nq, LANES, tq), lambda b, p, i: (b, p, 0, 0, 0)),
                  pl.BlockSpec((tq, wv), lambda b, p, i: (b * nq + i, gcol + p))],
        out_specs=pl.BlockSpec((tq, wv), lambda b, p, i: (b * nq + i, p)),
        out_shape=jax.ShapeDtypeStruct((m, FOX_HEADS * FOX_DIM), BF16),
        compiler_params=_cparams(("parallel", "parallel", "arbitrary"), 48),
        name="fox_attention",
    )(qt, ka, vt, proj)


def _mix_mlp_kernel(*refs, n_mix, ck):
    ys = refs[:n_mix]
    w_ref, h_ref, g_ref, wu_ref, wd_ref, o_ref = refs[n_mix:]
    y = ys[0][...] if n_mix == 1 else jnp.concatenate([r[...] for r in ys], axis=1)
    x = h_ref[...] + _dot(y, w_ref[...])
    hn = _rms(x, g_ref[...]).astype(BF16)
    acc = x
    for c in range(wu_ref.shape[1] // ck):
        u = jnp.maximum(_dot(hn, wu_ref[:, c * ck:(c + 1) * ck]), 0.0)
        acc = acc + _dot((u * u).astype(BF16), wd_ref[c * ck:(c + 1) * ck, :])
    o_ref[...] = acc


def _mix_mlp(ys, w, h, g, wu, wd, tm, ck, name):
    m, d = h.shape
    const = lambda a: pl.BlockSpec(a.shape, lambda i: (0, 0), pipeline_mode=pl.Buffered(1))
    return pl.pallas_call(
        functools.partial(_mix_mlp_kernel, n_mix=len(ys), ck=ck),
        grid=(m // tm,),
        in_specs=([pl.BlockSpec((tm, y.shape[1]), lambda i: (i, 0)) for y in ys]
                  + [const(w), pl.BlockSpec((tm, d), lambda i: (i, 0)), const(g), const(wu), const(wd)]),
        out_specs=pl.BlockSpec((tm, d), lambda i: (i, 0)),
        out_shape=jax.ShapeDtypeStruct((m, d), F32),
        compiler_params=_cparams(("parallel",), 56),
        name=name,
    )(*ys, w, h, g, wu, wd)


def _rwkvproj_kernel(h_ref, hp_ref, g_ref, mu_ref, wr_ref, wk_ref, wv_ref, w1_ref, w2_ref,
                     a1_ref, a2_ref, g1_ref, g2_ref, w0_ref, a0_ref, kk_ref, ka_ref,
                     r_ref, lw_ref, km_ref, v_ref, kr_ref, a_ref, go_ref, *, tiles_per_seq):
    i = pl.program_id(0)
    tm = h_ref.shape[0]
    gn = g_ref[...]
    hn = _rms(h_ref[...], gn)
    prev = _rms(hp_ref[7:8, :], gn)
    prev = jnp.where(i % tiles_per_seq == 0, jnp.zeros_like(prev), prev)
    row = lax.broadcasted_iota(jnp.int32, hn.shape, 0)
    shifted = jnp.where(row == 0, jnp.broadcast_to(prev, hn.shape), pltpu.roll(hn, 1, 0))
    xx = shifted - hn
    hn_b = hn.astype(BF16)
    xx_b = xx.astype(BF16)
    mix = lambda j: hn_b + xx_b * mu_ref[j:j + 1, :].astype(BF16)
    r = _dot(mix(0), wr_ref[...])
    k = _dot(mix(2), wk_ref[...])
    v = _dot(mix(3), wv_ref[...])
    z = w0_ref[...] + _dot(jnp.tanh(_dot(mix(1), w1_ref[...])).astype(BF16), w2_ref[...])
    a = _sigmoid(a0_ref[...] + _dot(_dot(mix(4), a1_ref[...]).astype(BF16), a2_ref[...]))
    g = _dot(_sigmoid(_dot(mix(5), g1_ref[...])).astype(BF16), g2_ref[...])
    r_ref[...] = r
    lw_ref[...] = _sigmoid(z) * (-math.exp(-0.5))
    km_ref[...] = k * (1.0 + (a - 1.0) * ka_ref[...])
    v_ref[...] = v
    kr_ref[...] = k * kk_ref[...]
    a_ref[...] = a
    go_ref[...] = g


def _rwkvproj(h, g, mu, wr, wk, wv, w1, w2, a1, a2, g1, g2, w0, a0, k_k, k_a, seq, tm):
    m, d = h.shape
    tiles_per_seq = seq // tm
    full = lambda a: pl.BlockSpec(a.shape, lambda i: (0,) * a.ndim, pipeline_mode=pl.Buffered(1))
    row = pl.BlockSpec((tm, d), lambda i: (i, 0))
    prev = pl.BlockSpec((8, d), lambda i: (jnp.maximum(i * (tm // 8) - 1, 0), 0))
    consts = (g, mu, wr, wk, wv, w1, w2, a1, a2, g1, g2, w0, a0, k_k, k_a)
    return pl.pallas_call(
        functools.partial(_rwkvproj_kernel, tiles_per_seq=tiles_per_seq),
        grid=(m // tm,),
        in_specs=[row, prev] + [full(a) for a in consts],
        out_specs=[row] * 7,
        out_shape=[jax.ShapeDtypeStruct((m, d), F32)] * 7,
        compiler_params=_cparams(("parallel",), 56),
        name="rwkv_proj",
    )(h, h, *consts)


def _wkv_kernel(r_ref, lw_ref, km_ref, v_ref, kr_ref, a_ref, g_ref, rk_ref, lg_ref, lb_ref,
                o_ref, st_ref, y_ref, q1_ref, lhs_ref, z0_ref, wc_ref, bonus_ref, gate_ref,
                *, chunk, groups, gps):
    step = pl.program_id(2)

    @pl.when(pl.program_id(1) == 0)
    def _():
        for gi in range(gps):
            st_ref[step * gps + gi] = jnp.zeros(st_ref.shape[1:], F32)

    ts = r_ref.shape[0]
    nch = ts // chunk
    gl = GROUP_LANES
    rb = lax.broadcasted_iota(jnp.int32, (gl, gl), 0) // RWKV_DIM
    cb = lax.broadcasted_iota(jnp.int32, (gl, gl), 1) // RWKV_DIM
    blockmask = rb == cb
    ones_bd = jnp.where(blockmask, 1.0, 0.0).astype(BF16)

    def headsum(x, pieces=2):
        hi = x.astype(BF16)
        out = _dot(hi, ones_bd)
        if pieces == 2:
            out = out + _dot((x - hi.astype(F32)).astype(BF16), ones_bd)
        return out

    def bd(y):
        reps = gl // y.shape[0]
        return jnp.where(blockmask, jnp.concatenate([y] * reps, axis=0), 0.0).astype(BF16)

    def hmm(x, y):
        return _dot(x.astype(BF16), bd(y))

    def tn_blocks(x, y):
        return jnp.where(blockmask, _dot(x.T.astype(BF16), y.astype(BF16)), 0.0)

    t_idx = lax.broadcasted_iota(jnp.int32, (chunk, gl), 0)
    s_idx = lax.broadcasted_iota(jnp.int32, (chunk, gl), 1) % RWKV_DIM
    strict = s_idx < t_idx
    incl = s_idx <= t_idx
    tril_b = jnp.where(_tril_mask(chunk), 1.0, 0.0).astype(BF16)
    zeros_c = jnp.zeros((chunk, gl), F32)

    tiles = [(gi, c) for gi in range(gps) for c in range(nch)]
    lanes_of = lambda gi: slice(gi * gl, (gi + 1) * gl)
    blk = lambda ref, gi, c: ref[c * chunk:(c + 1) * chunk, lanes_of(gi)]
    pad = lambda x: jnp.concatenate([x, zeros_c], axis=0)
    each = lambda fn, *lists: [fn(*args) for args in zip(*lists)]

    kkn_all = []
    for gi in range(gps):
        kr = kr_ref[:, lanes_of(gi)]
        kkn_all.append(kr * lax.rsqrt(jnp.maximum(headsum(kr * kr), 1e-24)))

    lw = [blk(lw_ref, gi, c) for gi, c in tiles]
    cum = each(lambda x: _cumsum_rows(x, tril_b, pieces=2), lw)
    c_last = [x[chunk - 1:chunk, :] for x in cum]
    kkn = [kkn_all[gi][c * chunk:(c + 1) * chunk, :] for gi, c in tiles]
    kka = [kkn[i] * blk(a_ref, gi, c) for i, (gi, c) in enumerate(tiles)]
    km = [blk(km_ref, gi, c) for gi, c in tiles]
    v = [blk(v_ref, gi, c) for gi, c in tiles]
    e_neg = [jnp.exp(-x) for x in cum]
    e_end = each(lambda cl, x: jnp.exp(cl - x), c_last, cum)
    at = each(lambda k, x, l: -k * jnp.exp(x - l), kkn, cum, lw)
    bt = each(jnp.multiply, kka, e_neg)
    kt = each(jnp.multiply, km, e_neg)
    rt = [blk(r_ref, gi, c) * jnp.exp(cum[i]) for i, (gi, c) in enumerate(tiles)]
    bw = each(jnp.multiply, kka, e_end)
    kw = each(jnp.multiply, km, e_end)
    for i, (gi, c) in enumerate(tiles):
        wc_ref[step * gps + gi, c] = jnp.broadcast_to(jnp.exp(c_last[i]), (LANES, gl)).T

    lhs = each(lambda a, r: jnp.concatenate([a, r], axis=0).astype(BF16), at, rt)
    pb = each(lambda l, b: _dot_nt(l, bd(b)), lhs, bt)
    pk = each(lambda l, k: _dot_nt(l, bd(k)), lhs, kt)
    a_ab = [jnp.where(strict, x[:chunk], 0.0) for x in pb]
    a_rb = [jnp.where(incl, x[chunk:], 0.0) for x in pb]
    a_ak = [jnp.where(strict, x[:chunk], 0.0) for x in pk]
    a_rk = [jnp.where(incl, x[chunk:], 0.0) for x in pk]

    e = [jnp.where((t_idx % 2 == 1) & (s_idx == t_idx - 1), x, 0.0) for x in a_ab]
    size = 2
    while size < chunk:
        off = ((t_idx // size) % 2 == 1) & (s_idx // size == t_idx // size - 1)
        a_off = [jnp.where(off, x, 0.0) for x in a_ab]
        t1 = each(lambda ao, ee: ao + hmm(ao, ee), a_off, e)
        e = each(lambda ee, tt: ee + tt + hmm(ee, tt), e, t1)
        size *= 2

    av = each(lambda ak, rk, vv: hmm(jnp.concatenate([ak, rk], axis=0), vv), a_ak, a_rk, v)
    akv = [x[:chunk] for x in av]
    p1 = each(lambda x, ee: x + hmm(ee, x), akv, e)
    mat = each(lambda x, ee: x + hmm(ee, x), at, e)
    q1 = each(lambda x, arb, pp: x[chunk:] + hmm(arb, pp), av, a_rb, p1)
    r2 = each(lambda r, arb, mm: r + hmm(arb, mm), rt, a_rb, mat)
    pct = each(lambda b, mm: tn_blocks(pad(b), pad(mm)), bw, mat)
    z0 = each(lambda b, k, pp, vv: tn_blocks(jnp.concatenate([b, k], axis=0),
                                             jnp.concatenate([pp, vv], axis=0)), bw, kw, p1, v)
    for i, (gi, c) in enumerate(tiles):
        g = step * gps + gi
        q1_ref[g, c * chunk:(c + 1) * chunk, :] = q1[i]
        lhs_ref[g, c, :chunk, :] = r2[i].astype(BF16)
        lhs_ref[g, c, chunk:, :] = pct[i].astype(BF16)
        z0_ref[g, c] = z0[i]
    for gi in range(gps):
        cols = lanes_of(gi)
        bonus_ref[step * gps + gi] = headsum(
            r_ref[:, cols] * km_ref[:, cols] * rk_ref[:, cols], pieces=1) * v_ref[:, cols]
        gate_ref[step * gps + gi] = g_ref[:, cols]

    @pl.when(step == groups // gps - 1)
    def _():
        gs = range(groups)

        def body(c, carry):
            rows = pl.ds(pl.multiple_of(c * chunk, chunk), chunk)
            st = [st_ref[j] for j in gs]
            res = [_dot(lhs_ref[j, c], st[j].astype(BF16)) for j in gs]
            for j in gs:
                y_ref[j, rows, :] = q1_ref[j, rows, :] + res[j][:chunk]
                wc = wc_ref[j, c]
                st_ref[j] = (st[j] * jnp.concatenate([wc, wc], axis=1) + res[j][chunk:]
                             + z0_ref[j, c])
            return carry

        lax.fori_loop(0, nch, body, 0)

        inv_n = 1.0 / RWKV_DIM
        for j in gs:
            cols = slice(j * gl, (j + 1) * gl)
            y = y_ref[j]
            mean = headsum(y) * inv_n
            dlt = y - mean
            var = headsum(dlt * dlt, pieces=1) * inv_n
            yn = dlt * lax.rsqrt(var + GN_EPS) * lg_ref[:, cols] + lb_ref[:, cols]
            o_ref[:, cols] = ((yn + bonus_ref[j]) * gate_ref[j]).astype(o_ref.dtype)


def _wkv(r, lw, km, v, kr, a, g, r_k, lnx_g, lnx_b, batch, seq, ts, chunk, gps):
    m, d = r.shape
    gl = GROUP_LANES
    groups = d // gl
    nt = seq // ts
    nch = ts // chunk
    row = pl.BlockSpec((ts, gps * gl), lambda b, t, j: (b * nt + t, j))
    vec = pl.BlockSpec((1, gps * gl), lambda b, t, j: (0, j))
    full = pl.BlockSpec((1, d), lambda b, t, j: (0, 0))
    return pl.pallas_call(
        functools.partial(_wkv_kernel, chunk=chunk, groups=groups, gps=gps),
        grid=(batch, nt, groups // gps),
        in_specs=[row] * 7 + [vec, full, full],
        out_specs=pl.BlockSpec((ts, d), lambda b, t, j: (b * nt + t, 0)),
        out_shape=jax.ShapeDtypeStruct((m, d), BF16),
        scratch_shapes=[pltpu.VMEM((groups, gl, gl), F32),
                        pltpu.VMEM((groups, ts, gl), F32),
                        pltpu.VMEM((groups, ts, gl), F32),
                        pltpu.VMEM((groups, nch, chunk + gl, gl), BF16),
                        pltpu.VMEM((groups, nch, gl, gl), F32),
                        pltpu.VMEM((groups, nch, gl, LANES), F32),
                        pltpu.VMEM((groups, ts, gl), F32),
                        pltpu.VMEM((groups, ts, gl), F32)],
        compiler_params=_cparams(("parallel", "arbitrary", "arbitrary"), 56),
        name="wkv7",
    )(r, lw, km, v, kr, a, g, r_k, lnx_g, lnx_b)


def kernel(x, norm_mix_g, norm_ffn_g, ab_w_in, hgrn_lower_bounds, hgrn_norm_g, fox_forget_bias,
           fox_q_norm_g, fox_k_norm_g, ab_w_out, rwkv_mu, rwkv_w_rkv, rwkv_w0, rwkv_w1, rwkv_w2,
           rwkv_a0, rwkv_a1, rwkv_a2, rwkv_g1, rwkv_g2, rwkv_k_k, rwkv_k_a, rwkv_r_k,
           rwkv_lnx_g, rwkv_lnx_b, rwkv_w_o, mlp_w_up, mlp_w_down):
    batch, seq, d = x.shape
    m = batch * seq
    t = _tiles(seq)
    row = lambda a: a.reshape(1, -1).astype(F32)
    bf = lambda a: a.astype(BF16)

    lb_all = jnp.cumsum(jax.nn.softmax(hgrn_lower_bounds.astype(F32), axis=0), axis=0)
    h = x.reshape(m, d)

    n_wide = ab_w_in.shape[-1] - FOX_HEADS
    gate_w = jnp.tile(ab_w_in[0][:, n_wide:], (1, 3))
    w_in = bf(jnp.pad(jnp.concatenate([ab_w_in[0][:, :n_wide], gate_w], axis=1),
                      ((0, 0), (0, LANES - 3 * FOX_HEADS))))
    proj = _inproj(h, row(norm_mix_g[0]), w_in, t.rows)
    ya = _hgrn(proj, row(lb_all[0]), row(hgrn_norm_g[0]), batch, seq, t.time, t.chunk)
    fb = jnp.pad(jnp.tile(row(fox_forget_bias[0]), (1, 3)), ((0, 0), (0, LANES - 3 * FOX_HEADS)))
    qt, ka, vt = _foxprep(proj, fb, row(fox_q_norm_g[0]), row(fox_k_norm_g[0]), batch, seq, t.time)
    yb = _fox(qt, ka, vt, proj, batch, seq, t.time, FOX_HEADS)
    h = _mix_mlp([ya, yb], bf(ab_w_out[0]), h, row(norm_ffn_g[0]), bf(mlp_w_up[0]), bf(mlp_w_down[0]),
                 t.mlp_rows, t.ff_chunk, "mix_mlp0")

    outs = _rwkvproj(h, row(norm_mix_g[1]), rwkv_mu[0].astype(F32),
                     bf(rwkv_w_rkv[0, 0]), bf(rwkv_w_rkv[0, 1]), bf(rwkv_w_rkv[0, 2]),
                     bf(rwkv_w1[0]), bf(rwkv_w2[0]), bf(rwkv_a1[0]), bf(rwkv_a2[0]),
                     bf(rwkv_g1[0]), bf(rwkv_g2[0]), row(rwkv_w0[0]), row(rwkv_a0[0]),
                     row(rwkv_k_k[0]), row(rwkv_k_a[0]), seq, t.rows)
    z = _wkv(*outs, row(rwkv_r_k[0]), row(rwkv_lnx_g[0]), row(rwkv_lnx_b[0]),
             batch, seq, t.time, t.chunk, 1)
    h = _mix_mlp([z], bf(rwkv_w_o[0]), h, row(norm_ffn_g[1]), bf(mlp_w_up[1]), bf(mlp_w_down[1]),
                 t.mlp_rows, t.ff_chunk, "mix_mlp1")
    return h.reshape(batch, seq, d)
```

```python
import functools
import math
from typing import NamedTuple

import jax
import jax.numpy as jnp
import numpy as np
from jax import lax
from jax.experimental import pallas as pl
from jax.experimental.pallas import tpu as pltpu

F32 = jnp.float32
BF16 = jnp.bfloat16

RMS_EPS = 1e-6
GN_EPS = 64e-5

HGRN_HEADS = 4
HGRN_DIM = 128
HGRN_SUB = 16
FOX_HEADS = 8
FOX_DIM = 64
RWKV_DIM = 64
RWKV_GROUP = 4
GROUP_LANES = RWKV_GROUP * RWKV_DIM
LANES = 128
NEG_BIG = -1e30
LOG2E = 1.4426950408889634

NT_DIMS = (((1,), (1,)), ((), ()))


class _Tiles(NamedTuple):
    rows: int
    mlp_rows: int
    time: int
    fox_keys: int
    chunk: int
    ff_chunk: int


def _tiles(seq):
    return _Tiles(rows=min(512, seq), mlp_rows=min(512, seq), time=min(512, seq),
                  fox_keys=min(256, seq), chunk=64, ff_chunk=1024)


def _cparams(sem, vmem_mb):
    return pltpu.CompilerParams(dimension_semantics=sem, vmem_limit_bytes=vmem_mb * 1024 * 1024)


def _dot(a, b):
    return jnp.dot(a, b, preferred_element_type=F32)


def _dot_nt(a, b):
    return lax.dot_general(a, b, NT_DIMS, preferred_element_type=F32)


def _rms(x, g):
    return x * lax.rsqrt(jnp.mean(x * x, axis=-1, keepdims=True) + RMS_EPS) * g


def _sigmoid(x):
    return 1.0 / (1.0 + jnp.exp(-x))


def _log_sigmoid(x):
    return jnp.minimum(x, 0.0) - jnp.log(1.0 + jnp.exp(-jnp.abs(x)))


def _tril_mask(n, strict=False):
    r = lax.broadcasted_iota(jnp.int32, (n, n), 0)
    c = lax.broadcasted_iota(jnp.int32, (n, n), 1)
    return (c < r) if strict else (c <= r)


def _split3(x):
    hi = x.astype(BF16)
    r1 = x - hi.astype(F32)
    mid = r1.astype(BF16)
    lo = (r1 - mid.astype(F32)).astype(BF16)
    return hi, mid, lo


def _cumsum_rows(x, tril_bf16, pieces=3):
    parts = _split3(x)[:pieces]
    out = _dot(tril_bf16, parts[0])
    for part in parts[1:]:
        out = out + _dot(tril_bf16, part)
    return out


def _inproj_kernel(x_ref, g_ref, w_ref, o_ref):
    hn = _rms(x_ref[...], g_ref[...]).astype(BF16)
    o_ref[...] = _dot(hn, w_ref[...])


def _inproj(x2, g, w, tm):
    m, d = x2.shape
    n = w.shape[1]
    return pl.pallas_call(
        _inproj_kernel,
        grid=(m // tm,),
        in_specs=[
            pl.BlockSpec((tm, d), lambda i: (i, 0)),
            pl.BlockSpec((1, d), lambda i: (0, 0)),
            pl.BlockSpec((d, n), lambda i: (0, 0), pipeline_mode=pl.Buffered(1)),
        ],
        out_specs=pl.BlockSpec((tm, n), lambda i: (i, 0)),
        out_shape=jax.ShapeDtypeStruct((m, n), F32),
        compiler_params=_cparams(("parallel",), 48),
        name="inproj",
    )(x2, g, w)


def _hgrn_kernel(q_ref, f_ref, i_ref, g_ref, lb_ref, ng_ref, o_ref, st_ref, *, chunk):
    @pl.when(pl.program_id(1) == 0)
    def _():
        st_ref[...] = jnp.zeros_like(st_ref)

    ts = q_ref.shape[0]
    tril_b = jnp.where(_tril_mask(chunk), 1.0, 0.0).astype(BF16)

    hs = range(HGRN_HEADS)
    nch = ts // chunk
    tiles = [(c, h) for c in range(nch) for h in hs]
    blk = lambda ref, c, h: ref[c * chunk:(c + 1) * chunk, h * HGRN_DIM:(h + 1) * HGRN_DIM]
    lbs = [lb_ref[:, h * HGRN_DIM:(h + 1) * HGRN_DIM] for h in hs]

    f = [lbs[h] + (1.0 - lbs[h]) * _sigmoid(blk(f_ref, c, h)) for c, h in tiles]
    b = [_cumsum_rows(jnp.log(x), tril_b) for x in f]
    b_last = [x[chunk - 1:chunk, :] for x in b]
    q = [blk(q_ref, c, h) * _sigmoid(blk(q_ref, c, h)) for c, h in tiles]
    k = [1.0 - x for x in f]
    vb = [blk(i_ref, c, h).astype(BF16) for c, h in tiles]
    n = range(len(tiles))

    sub = HGRN_SUB
    tril_sub = _tril_mask(sub)
    score_cols = [[] for _ in n]
    for j in range(chunk // sub):
        lo, hi = j * sub, (j + 1) * sub
        for i in n:
            bj = b[i][lo:hi]
            b_mid = b[i][lo + sub // 2 - 1:lo + sub // 2]
            diag = _dot_nt((q[i][lo:hi] * jnp.exp(bj - b_mid)).astype(BF16),
                           (k[i][lo:hi] * jnp.exp(b_mid - bj)).astype(BF16))
            parts = [jnp.where(tril_sub, diag, 0.0)]
            if lo > 0:
                parts.insert(0, jnp.zeros((lo, sub), F32))
            if hi < chunk:
                b_end = b[i][hi - 1:hi]
                parts.append(_dot_nt((q[i][hi:] * jnp.exp(b[i][hi:] - b_end)).astype(BF16),
                                     (k[i][lo:hi] * jnp.exp(b_end - bj)).astype(BF16)))
            score_cols[i].append(jnp.concatenate(parts, axis=0).astype(BF16))
    o = []
    for i in n:
        acc = _dot(score_cols[i][0], vb[i][:sub])
        for j in range(1, chunk // sub):
            acc = acc + _dot(score_cols[i][j], vb[i][j * sub:(j + 1) * sub])
        o.append(acc)
    inc = [_dot(blk(i_ref, c, h).T.astype(BF16), (k[i] * jnp.exp(b_last[i] - b[i])).astype(BF16))
           for i, (c, h) in enumerate(tiles)]
    dec = [jnp.exp(x) for x in b_last]

    st = [st_ref[h] for h in hs]
    st_in = []
    for i, (c, h) in enumerate(tiles):
        st_in.append(st[h].astype(BF16))
        st[h] = st[h] * dec[i] + inc[i]
    for h in hs:
        st_ref[h] = st[h]

    for i, (c, h) in enumerate(tiles):
        oi = o[i] + _dot_nt((q[i] * jnp.exp(b[i])).astype(BF16), st_in[i])
        ag = blk(g_ref, c, h)
        on = _rms(oi, ng_ref[:, h * HGRN_DIM:(h + 1) * HGRN_DIM])
        o_ref[c * chunk:(c + 1) * chunk, h * HGRN_DIM:(h + 1) * HGRN_DIM] = (
            on * (ag * _sigmoid(ag))).astype(o_ref.dtype)


def _hgrn(proj, lb, ng, batch, seq, ts, chunk):
    m = proj.shape[0]
    w = HGRN_HEADS * HGRN_DIM
    nt = seq // ts
    spec = lambda j: pl.BlockSpec((ts, w), lambda b, t, j=j: (b * nt + t, j))
    vec = pl.BlockSpec((1, w), lambda b, t: (0, 0))
    return pl.pallas_call(
        functools.partial(_hgrn_kernel, chunk=chunk),
        grid=(batch, nt),
        in_specs=[spec(0), spec(1), spec(2), spec(3), vec, vec],
        out_specs=pl.BlockSpec((ts, w), lambda b, t: (b * nt + t, 0)),
        out_shape=jax.ShapeDtypeStruct((m, w), BF16),
        scratch_shapes=[pltpu.VMEM((HGRN_HEADS, HGRN_DIM, HGRN_DIM), F32)],
        compiler_params=_cparams(("parallel", "arbitrary"), 32),
        name="hgrn2",
    )(proj, proj, proj, proj, lb, ng)


def _foxprep_kernel(q_ref, k_ref, v_ref, f_ref, fb_ref, qg_ref, kg_ref, hsum_ref, wq_ref, wk_ref,
                    oq_ref, ok_ref, qt_ref, ka_ref, vt_ref, carry_ref):
    @pl.when(pl.program_id(1) == 0)
    def _():
        carry_ref[...] = jnp.zeros_like(carry_ref)

    ts = q_ref.shape[0]
    pairs = FOX_HEADS // 2
    tril_b = jnp.where(_tril_mask(ts), 1.0, 0.0).astype(BF16)
    lf = _log_sigmoid(f_ref[...] + fb_ref[...])
    c = _cumsum_rows(lf, tril_b) + carry_ref[...]
    carry_ref[...] = c[ts - 1:ts, :]
    c2 = c * LOG2E
    hi = c2.astype(BF16).astype(F32)
    rest = c2 - hi
    mid = rest.astype(BF16).astype(F32)
    lane = lax.broadcasted_iota(jnp.int32, c.shape, 1)
    pieces = jnp.where(lane < FOX_HEADS, hi,
                       jnp.where(lane < 2 * FOX_HEADS, mid, rest - mid)).astype(BF16)

    def headnorm(x_ref, g_ref, r, scale):
        cols = slice(r * 2 * LANES, (r + 1) * 2 * LANES)
        x = x_ref[:, cols]
        xx = x * x
        xh = xx.astype(BF16)
        ss = _dot(xh, hsum_ref[...]) + _dot((xx - xh.astype(F32)).astype(BF16), hsum_ref[...])
        return (x * lax.rsqrt(ss * (1.0 / FOX_DIM) + RMS_EPS) * (g_ref[:, cols] * scale)).astype(BF16)

    qn = [headnorm(q_ref, qg_ref, r, FOX_DIM ** -0.5 * LOG2E) for r in range(pairs // 2)]
    kn = [headnorm(k_ref, kg_ref, r, 1.0) for r in range(pairs // 2)]
    for r in range(pairs):
        src = slice((r % 2) * LANES, (r % 2 + 1) * LANES)
        out = slice(r * 2 * LANES, (r + 1) * 2 * LANES)
        qa = _dot(jnp.concatenate([qn[r // 2][:, src], pieces], axis=1), wq_ref[r]) + oq_ref[:, out]
        ka = _dot(jnp.concatenate([kn[r // 2][:, src], pieces], axis=1), wk_ref[r]) + ok_ref[:, out]
        ka_ref[:, out] = ka.astype(BF16)
        for hh in range(2):
            qt_ref[0, 2 * r + hh] = qa[:, hh * LANES:(hh + 1) * LANES].T.astype(BF16)
        vt_ref[0, r, 0] = v_ref[:, r * LANES:(r + 1) * LANES].T.astype(BF16)


def _foxprep(proj, fb, qg, kg, batch, seq, ts):
    m = proj.shape[0]
    w = FOX_HEADS * FOX_DIM
    nt = seq // ts
    pairs = FOX_HEADS // 2
    spec = lambda j: pl.BlockSpec((ts, w), lambda b, t, j=j: (b * nt + t, j))
    fcol = (8 * w) // LANES
    const = lambda a: pl.BlockSpec(a.shape, lambda b, t: (0,) * a.ndim)

    wa = FOX_HEADS * LANES
    ch = np.arange(2 * LANES)
    hsum = (ch[:, None] // FOX_DIM == ch[None, :] // FOX_DIM).astype(np.float32)
    wq = np.zeros((pairs, 2 * LANES, 2 * LANES), np.float32)
    wk = np.zeros((pairs, 2 * LANES, 2 * LANES), np.float32)
    oq = np.zeros((1, wa), np.float32)
    ok = np.zeros((1, wa), np.float32)
    for r in range(pairs):
        for hh in range(2):
            h = 2 * r + hh
            d = np.arange(FOX_DIM)
            wq[r, hh * FOX_DIM + d, hh * LANES + d] = 1.0
            wk[r, hh * FOX_DIM + d, hh * LANES + d] = 1.0
            for p in range(3):
                wq[r, LANES + p * FOX_HEADS + h, hh * LANES + FOX_DIM + p] = 1.0
                wk[r, LANES + p * FOX_HEADS + h, hh * LANES + FOX_DIM + 3 + p] = -1.0
                oq[0, h * LANES + FOX_DIM + 3 + p] = 1.0
                ok[0, h * LANES + FOX_DIM + p] = 1.0
    consts = [jnp.asarray(a, BF16) for a in (hsum, wq, wk)] + [jnp.asarray(oq), jnp.asarray(ok)]
    qg = jnp.tile(qg, (1, FOX_HEADS))
    kg = jnp.tile(kg, (1, FOX_HEADS))
    return pl.pallas_call(
        _foxprep_kernel,
        grid=(batch, nt),
        in_specs=[spec(4), spec(5), spec(6),
                  pl.BlockSpec((ts, LANES), lambda b, t: (b * nt + t, fcol)),
                  const(fb), const(qg), const(kg)] + [const(a) for a in consts],
        out_specs=[pl.BlockSpec((1, FOX_HEADS, LANES, ts), lambda b, t: (b, 0, 0, t)),
                   pl.BlockSpec((ts, FOX_HEADS * LANES), lambda b, t: (b * nt + t, 0)),
                   pl.BlockSpec((1, pairs, 1, LANES, ts), lambda b, t: (b, 0, t, 0, 0))],
        out_shape=[jax.ShapeDtypeStruct((batch, FOX_HEADS, LANES, seq), BF16),
                   jax.ShapeDtypeStruct((m, FOX_HEADS * LANES), BF16),
                   jax.ShapeDtypeStruct((batch, pairs, nt, LANES, ts), BF16)],
        scratch_shapes=[pltpu.VMEM((1, LANES), F32)],
        compiler_params=_cparams(("parallel", "arbitrary"), 32),
        name="foxprep",
    )(proj, proj, proj, proj, fb, qg, kg, *consts)


def _fox_kernel(qt_ref, k_ref, vt_ref, g_ref, o_ref, *, tq, tk, heads):
    i = pl.program_id(2)
    hs = range(heads)
    ratio = tq // tk
    vrows = [slice((h % 2) * FOX_DIM, (h % 2 + 1) * FOX_DIM) for h in hs]

    def step(j, carry, q0):
        ms, ls, accs = carry
        diagonal = q0 is not None
        q0 = q0 or 0
        nq = tq - q0
        nkeys = tk if diagonal else tq
        rows = pl.ds(pl.multiple_of(j * tq, tq) + q0, nkeys)
        keys = slice(q0, q0 + nkeys)
        if diagonal:
            visible = (lax.broadcasted_iota(jnp.int32, (tk, nq), 1)
                       >= lax.broadcasted_iota(jnp.int32, (tk, nq), 0))

        def scores(h):
            s = _dot(k_ref[rows, h * LANES:(h + 1) * LANES], qt_ref[0, h, :, q0:])
            return jnp.where(visible, s, NEG_BIG) if diagonal else s

        def softmax(h, s):
            m_old = ms[h][:, q0:]
            m_new = jnp.maximum(m_old, jnp.max(s, axis=0, keepdims=True))
            alpha = jnp.exp2(m_old - m_new)
            p = jnp.exp2(s - m_new)
            l_new = alpha * ls[h][:, q0:] + jnp.sum(p, axis=0, keepdims=True)
            return m_new, alpha, l_new, p.astype(BF16)

        def values(h, alpha, p):
            return accs[h][:, q0:] * alpha + _dot(vt_ref[0, h // 2, j, vrows[h], keys], p)

        s, sm, out = {}, {}, {}
        for t in range(heads + 2):
            if t < heads:
                s[t] = scores(t)
            if 0 <= t - 1 < heads:
                sm[t - 1] = softmax(t - 1, s[t - 1])
            if 0 <= t - 2 < heads:
                out[t - 2] = values(t - 2, sm[t - 2][1], sm[t - 2][3])
        keep = lambda old, new: new if q0 == 0 else jnp.concatenate([old[:, :q0], new], axis=1)
        return (tuple(keep(ms[h], sm[h][0]) for h in hs), tuple(keep(ls[h], sm[h][2]) for h in hs),
                tuple(keep(accs[h], out[h]) for h in hs))

    neg = jnp.full((1, tq), NEG_BIG, F32)
    zero = jnp.zeros((1, tq), F32)
    carry = ((neg,) * heads, (zero,) * heads, (jnp.zeros((FOX_DIM, tq), F32),) * heads)
    carry = lax.fori_loop(0, i, lambda j, c: step(j, c, None), carry)
    for d in range(ratio):
        carry = step(i, carry, d * tk)
    _, ls, accs = carry
    for r in range(heads // 2):
        cols = slice(r * LANES, (r + 1) * LANES)
        out = jnp.concatenate([accs[2 * r] / ls[2 * r], accs[2 * r + 1] / ls[2 * r + 1]], axis=0)
        o_ref[:, cols] = (out.T * _sigmoid(g_ref[:, cols])).astype(o_ref.dtype)


def _fox(qt, ka, vt, proj, batch, seq, tq, tk, heads):
    m = ka.shape[0]
    nq = seq // tq
    groups = FOX_HEADS // heads
    pairs = heads // 2
    wv = heads * FOX_DIM
    gcol = (7 * FOX_HEADS * FOX_DIM) // wv
    return pl.pallas_call(
        functools.partial(_fox_kernel, tq=tq, tk=tk, heads=heads),
        grid=(batch, groups, nq),
        in_specs=[pl.BlockSpec((1, heads, LANES, tq), lambda b, p, i: (b, p, 0, i)),
                  pl.BlockSpec((seq, heads * LANES), lambda b, p, i: (b, p)),
                  pl.BlockSpec((1, pairs, nq, LANES, tq), lambda b, p, i: (b, p, 0, 0, 0)),
                  pl.BlockSpec((tq, wv), lambda b, p, i: (b * nq + i, gcol + p))],
        out_specs=pl.BlockSpec((tq, wv), lambda b, p, i: (b * nq + i, p)),
        out_shape=jax.ShapeDtypeStruct((m, FOX_HEADS * FOX_DIM), BF16),
        compiler_params=_cparams(("parallel", "parallel", "arbitrary"), 48),
        name="fox_attention",
    )(qt, ka, vt, proj)


def _mix_mlp_kernel(*refs, n_mix, ck):
    ys = refs[:n_mix]
    w_ref, h_ref, g_ref, wu_ref, wd_ref, o_ref = refs[n_mix:]
    y = ys[0][...] if n_mix == 1 else jnp.concatenate([r[...] for r in ys], axis=1)
    x = h_ref[...] + _dot(y, w_ref[...])
    hn = _rms(x, g_ref[...]).astype(BF16)
    acc = x
    for c in range(wu_ref.shape[1] // ck):
        u = jnp.maximum(_dot(hn, wu_ref[:, c * ck:(c + 1) * ck]), 0.0)
        acc = acc + _dot((u * u).astype(BF16), wd_ref[c * ck:(c + 1) * ck, :])
    o_ref[...] = acc


def _mix_mlp(ys, w, h, g, wu, wd, tm, ck, name):
    m, d = h.shape
    const = lambda a: pl.BlockSpec(a.shape, lambda i: (0, 0), pipeline_mode=pl.Buffered(1))
    return pl.pallas_call(
        functools.partial(_mix_mlp_kernel, n_mix=len(ys), ck=ck),
        grid=(m // tm,),
        in_specs=([pl.BlockSpec((tm, y.shape[1]), lambda i: (i, 0)) for y in ys]
                  + [const(w), pl.BlockSpec((tm, d), lambda i: (i, 0)), const(g), const(wu), const(wd)]),
        out_specs=pl.BlockSpec((tm, d), lambda i: (i, 0)),
        out_shape=jax.ShapeDtypeStruct((m, d), F32),
        compiler_params=_cparams(("parallel",), 56),
        name=name,
    )(*ys, w, h, g, wu, wd)


def _rwkvproj_kernel(h_ref, hp_ref, g_ref, mu_ref, wr_ref, wk_ref, wv_ref, w1_ref, w2_ref,
                     a1_ref, a2_ref, g1_ref, g2_ref, w0_ref, a0_ref, kk_ref, ka_ref,
                     r_ref, lw_ref, km_ref, v_ref, kr_ref, a_ref, go_ref, *, tiles_per_seq):
    i = pl.program_id(0)
    tm = h_ref.shape[0]
    gn = g_ref[...]
    hn = _rms(h_ref[...], gn)
    prev = _rms(hp_ref[7:8, :], gn)
    prev = jnp.where(i % tiles_per_seq == 0, jnp.zeros_like(prev), prev)
    row = lax.broadcasted_iota(jnp.int32, hn.shape, 0)
    shifted = jnp.where(row == 0, jnp.broadcast_to(prev, hn.shape), pltpu.roll(hn, 1, 0))
    xx = shifted - hn
    hn_b = hn.astype(BF16)
    xx_b = xx.astype(BF16)
    mix = lambda j: hn_b + xx_b * mu_ref[j:j + 1, :].astype(BF16)
    r = _dot(mix(0), wr_ref[...])
    k = _dot(mix(2), wk_ref[...])
    v = _dot(mix(3), wv_ref[...])
    z = w0_ref[...] + _dot(jnp.tanh(_dot(mix(1), w1_ref[...])).astype(BF16), w2_ref[...])
    a = _sigmoid(a0_ref[...] + _dot(_dot(mix(4), a1_ref[...]).astype(BF16), a2_ref[...]))
    g = _dot(_sigmoid(_dot(mix(5), g1_ref[...])).astype(BF16), g2_ref[...])
    r_ref[...] = r
    lw_ref[...] = _sigmoid(z) * (-math.exp(-0.5))
    km_ref[...] = k * (1.0 + (a - 1.0) * ka_ref[...])
    v_ref[...] = v
    kr_ref[...] = k * kk_ref[...]
    a_ref[...] = a
    go_ref[...] = g


def _rwkvproj(h, g, mu, wr, wk, wv, w1, w2, a1, a2, g1, g2, w0, a0, k_k, k_a, seq, tm):
    m, d = h.shape
    tiles_per_seq = seq // tm
    full = lambda a: pl.BlockSpec(a.shape, lambda i: (0,) * a.ndim, pipeline_mode=pl.Buffered(1))
    row = pl.BlockSpec((tm, d), lambda i: (i, 0))
    prev = pl.BlockSpec((8, d), lambda i: (jnp.maximum(i * (tm // 8) - 1, 0), 0))
    consts = (g, mu, wr, wk, wv, w1, w2, a1, a2, g1, g2, w0, a0, k_k, k_a)
    return pl.pallas_call(
        functools.partial(_rwkvproj_kernel, tiles_per_seq=tiles_per_seq),
        grid=(m // tm,),
        in_specs=[row, prev] + [full(a) for a in consts],
        out_specs=[row] * 7,
        out_shape=[jax.ShapeDtypeStruct((m, d), F32)] * 7,
        compiler_params=_cparams(("parallel",), 56),
        name="rwkv_proj",
    )(h, h, *consts)


def _wkv_kernel(r_ref, lw_ref, km_ref, v_ref, kr_ref, a_ref, g_ref, rk_ref, lg_ref, lb_ref,
                o_ref, st_ref, y_ref, q1_ref, lhs_ref, z0_ref, wc_ref, bonus_ref, gate_ref,
                *, chunk, groups, gps):
    step = pl.program_id(2)

    @pl.when(pl.program_id(1) == 0)
    def _():
        for gi in range(gps):
            st_ref[step * gps + gi] = jnp.zeros(st_ref.shape[1:], F32)

    ts = r_ref.shape[0]
    nch = ts // chunk
    gl = GROUP_LANES
    rb = lax.broadcasted_iota(jnp.int32, (gl, gl), 0) // RWKV_DIM
    cb = lax.broadcasted_iota(jnp.int32, (gl, gl), 1) // RWKV_DIM
    blockmask = rb == cb
    ones_bd = jnp.where(blockmask, 1.0, 0.0).astype(BF16)

    def headsum(x, pieces=2):
        hi = x.astype(BF16)
        out = _dot(hi, ones_bd)
        if pieces == 2:
            out = out + _dot((x - hi.astype(F32)).astype(BF16), ones_bd)
        return out

    def bd(y):
        reps = gl // y.shape[0]
        return jnp.where(blockmask, jnp.concatenate([y] * reps, axis=0), 0.0).astype(BF16)

    def hmm(x, y):
        return _dot(x.astype(BF16), bd(y))

    def tn_blocks(x, y):
        return jnp.where(blockmask, _dot(x.T.astype(BF16), y.astype(BF16)), 0.0)

    t_idx = lax.broadcasted_iota(jnp.int32, (chunk, gl), 0)
    s_idx = lax.broadcasted_iota(jnp.int32, (chunk, gl), 1) % RWKV_DIM
    strict = s_idx < t_idx
    incl = s_idx <= t_idx
    tril_b = jnp.where(_tril_mask(chunk), 1.0, 0.0).astype(BF16)
    zeros_c = jnp.zeros((chunk, gl), F32)

    tiles = [(gi, c) for gi in range(gps) for c in range(nch)]
    lanes_of = lambda gi: slice(gi * gl, (gi + 1) * gl)
    blk = lambda ref, gi, c: ref[c * chunk:(c + 1) * chunk, lanes_of(gi)]
    pad = lambda x: jnp.concatenate([x, zeros_c], axis=0)
    each = lambda fn, *lists: [fn(*args) for args in zip(*lists)]

    kkn_all = []
    for gi in range(gps):
        kr = kr_ref[:, lanes_of(gi)]
        kkn_all.append(kr * lax.rsqrt(jnp.maximum(headsum(kr * kr), 1e-24)))

    lw = [blk(lw_ref, gi, c) for gi, c in tiles]
    cum = each(lambda x: _cumsum_rows(x, tril_b, pieces=2), lw)
    c_last = [x[chunk - 1:chunk, :] for x in cum]
    kkn = [kkn_all[gi][c * chunk:(c + 1) * chunk, :] for gi, c in tiles]
    kka = [kkn[i] * blk(a_ref, gi, c) for i, (gi, c) in enumerate(tiles)]
    km = [blk(km_ref, gi, c) for gi, c in tiles]
    v = [blk(v_ref, gi, c) for gi, c in tiles]
    e_neg = [jnp.exp(-x) for x in cum]
    e_end = each(lambda cl, x: jnp.exp(cl - x), c_last, cum)
    at = each(lambda k, x, l: -k * jnp.exp(x - l), kkn, cum, lw)
    bt = each(jnp.multiply, kka, e_neg)
    kt = each(jnp.multiply, km, e_neg)
    rt = [blk(r_ref, gi, c) * jnp.exp(cum[i]) for i, (gi, c) in enumerate(tiles)]
    bw = each(jnp.multiply, kka, e_end)
    kw = each(jnp.multiply, km, e_end)
    first_head = lax.broadcasted_iota(jnp.int32, (RWKV_DIM, LANES), 1) < RWKV_DIM
    for i, (gi, c) in enumerate(tiles):
        wt = jnp.broadcast_to(jnp.exp(c_last[i]), (LANES, gl)).T
        wc_ref[step * gps + gi, c] = jnp.concatenate(
            [jnp.where(first_head, wt[2 * p * RWKV_DIM:(2 * p + 1) * RWKV_DIM],
                       wt[(2 * p + 1) * RWKV_DIM:(2 * p + 2) * RWKV_DIM])
             for p in range(RWKV_GROUP // 2)], axis=1)

    lhs = each(lambda a, r: jnp.concatenate([a, r], axis=0).astype(BF16), at, rt)
    pb = each(lambda l, b: _dot_nt(l, bd(b)), lhs, bt)
    pk = each(lambda l, k: _dot_nt(l, bd(k)), lhs, kt)
    a_ab = [jnp.where(strict, x[:chunk], 0.0) for x in pb]
    a_rb = [jnp.where(incl, x[chunk:], 0.0) for x in pb]
    a_ak = [jnp.where(strict, x[:chunk], 0.0) for x in pk]
    a_rk = [jnp.where(incl, x[chunk:], 0.0) for x in pk]

    e = [jnp.where((t_idx % 2 == 1) & (s_idx == t_idx - 1), x, 0.0) for x in a_ab]
    size = 2
    while size < chunk:
        off = ((t_idx // size) % 2 == 1) & (s_idx // size == t_idx // size - 1)
        a_off = [jnp.where(off, x, 0.0) for x in a_ab]
        t1 = each(lambda ao, ee: ao + hmm(ao, ee), a_off, e)
        e = each(lambda ee, tt: ee + tt + hmm(ee, tt), e, t1)
        size *= 2

    av = each(lambda ak, rk, vv: hmm(jnp.concatenate([ak, rk], axis=0), vv), a_ak, a_rk, v)
    akv = [x[:chunk] for x in av]
    p1 = each(lambda x, ee: x + hmm(ee, x), akv, e)
    mat = each(lambda x, ee: x + hmm(ee, x), at, e)
    q1 = each(lambda x, arb, pp: x[chunk:] + hmm(arb, pp), av, a_rb, p1)
    r2 = each(lambda r, arb, mm: r + hmm(arb, mm), rt, a_rb, mat)
    pct = each(lambda b, mm: tn_blocks(pad(b), pad(mm)), bw, mat)
    z0 = each(lambda b, k, pp, vv: tn_blocks(jnp.concatenate([b, k], axis=0),
                                             jnp.concatenate([pp, vv], axis=0)), bw, kw, p1, v)
    def fold(x):
        out = x[:RWKV_DIM]
        for h in range(1, RWKV_GROUP):
            out = out + x[h * RWKV_DIM:(h + 1) * RWKV_DIM]
        return out

    for i, (gi, c) in enumerate(tiles):
        g = step * gps + gi
        q1_ref[g, c * chunk:(c + 1) * chunk, :] = q1[i]
        lhs_ref[g, c, :chunk, :] = r2[i].astype(BF16)
        lhs_ref[g, c, chunk:, :] = fold(pct[i]).astype(BF16)
        z0_ref[g, c] = fold(z0[i])
    for gi in range(gps):
        cols = lanes_of(gi)
        bonus_ref[step * gps + gi] = headsum(
            r_ref[:, cols] * km_ref[:, cols] * rk_ref[:, cols], pieces=1) * v_ref[:, cols]
        gate_ref[step * gps + gi] = g_ref[:, cols]

    @pl.when(step == groups // gps - 1)
    def _():
        gs = range(groups)

        def body(c, carry):
            rows = pl.ds(pl.multiple_of(c * chunk, chunk), chunk)
            st = [st_ref[j] for j in gs]
            res = [_dot(lhs_ref[j, c], bd(st[j])) for j in gs]
            for j in gs:
                y_ref[j, rows, :] = q1_ref[j, rows, :] + res[j][:chunk]
                st_ref[j] = st[j] * wc_ref[j, c] + res[j][chunk:] + z0_ref[j, c]
            return carry

        lax.fori_loop(0, nch, body, 0)

        inv_n = 1.0 / RWKV_DIM
        for j in gs:
            cols = slice(j * gl, (j + 1) * gl)
            y = y_ref[j]
            mean = headsum(y, pieces=1) * inv_n
            dlt = y - mean
            var = headsum(dlt * dlt, pieces=1) * inv_n
            yn = dlt * lax.rsqrt(var + GN_EPS) * lg_ref[:, cols] + lb_ref[:, cols]
            o_ref[:, cols] = ((yn + bonus_ref[j]) * gate_ref[j]).astype(o_ref.dtype)


def _wkv(r, lw, km, v, kr, a, g, r_k, lnx_g, lnx_b, batch, seq, ts, chunk, gps):
    m, d = r.shape
    gl = GROUP_LANES
    groups = d // gl
    nt = seq // ts
    nch = ts // chunk
    row = pl.BlockSpec((ts, gps * gl), lambda b, t, j: (b * nt + t, j))
    vec = pl.BlockSpec((1, gps * gl), lambda b, t, j: (0, j))
    full = pl.BlockSpec((1, d), lambda b, t, j: (0, 0))
    return pl.pallas_call(
        functools.partial(_wkv_kernel, chunk=chunk, groups=groups, gps=gps),
        grid=(batch, nt, groups // gps),
        in_specs=[row] * 7 + [vec, full, full],
        out_specs=pl.BlockSpec((ts, d), lambda b, t, j: (b * nt + t, 0)),
        out_shape=jax.ShapeDtypeStruct((m, d), BF16),
        scratch_shapes=[pltpu.VMEM((groups, RWKV_DIM, gl), F32),
                        pltpu.VMEM((groups, ts, gl), F32),
                        pltpu.VMEM((groups, ts, gl), F32),
                        pltpu.VMEM((groups, nch, chunk + RWKV_DIM, gl), BF16),
                        pltpu.VMEM((groups, nch, RWKV_DIM, gl), F32),
                        pltpu.VMEM((groups, nch, RWKV_DIM, gl), F32),
                        pltpu.VMEM((groups, ts, gl), F32),
                        pltpu.VMEM((groups, ts, gl), F32)],
        compiler_params=_cparams(("parallel", "arbitrary", "arbitrary"), 56),
        name="wkv7",
    )(r, lw, km, v, kr, a, g, r_k, lnx_g, lnx_b)


def kernel(x, norm_mix_g, norm_ffn_g, ab_w_in, hgrn_lower_bounds, hgrn_norm_g, fox_forget_bias,
           fox_q_norm_g, fox_k_norm_g, ab_w_out, rwkv_mu, rwkv_w_rkv, rwkv_w0, rwkv_w1, rwkv_w2,
           rwkv_a0, rwkv_a1, rwkv_a2, rwkv_g1, rwkv_g2, rwkv_k_k, rwkv_k_a, rwkv_r_k,
           rwkv_lnx_g, rwkv_lnx_b, rwkv_w_o, mlp_w_up, mlp_w_down):
    batch, seq, d = x.shape
    m = batch * seq
    t = _tiles(seq)
    row = lambda a: a.reshape(1, -1).astype(F32)
    bf = lambda a: a.astype(BF16)

    lb_all = jnp.cumsum(jax.nn.softmax(hgrn_lower_bounds.astype(F32), axis=0), axis=0)
    h = x.reshape(m, d)

    n_wide = ab_w_in.shape[-1] - FOX_HEADS
    gate_w = jnp.tile(ab_w_in[0][:, n_wide:], (1, 3))
    w_in = bf(jnp.pad(jnp.concatenate([ab_w_in[0][:, :n_wide], gate_w], axis=1),
                      ((0, 0), (0, LANES - 3 * FOX_HEADS))))
    proj = _inproj(h, row(norm_mix_g[0]), w_in, t.rows)
    ya = _hgrn(proj, row(lb_all[0]), row(hgrn_norm_g[0]), batch, seq, t.time, t.chunk)
    fb = jnp.pad(jnp.tile(row(fox_forget_bias[0]), (1, 3)), ((0, 0), (0, LANES - 3 * FOX_HEADS)))
    qt, ka, vt = _foxprep(proj, fb, row(fox_q_norm_g[0]), row(fox_k_norm_g[0]), batch, seq, t.time)
    yb = _fox(qt, ka, vt, proj, batch, seq, t.time, t.fox_keys, FOX_HEADS)
    h = _mix_mlp([ya, yb], bf(ab_w_out[0]), h, row(norm_ffn_g[0]), bf(mlp_w_up[0]), bf(mlp_w_down[0]),
                 t.mlp_rows, t.ff_chunk, "mix_mlp0")

    outs = _rwkvproj(h, row(norm_mix_g[1]), rwkv_mu[0].astype(F32),
                     bf(rwkv_w_rkv[0, 0]), bf(rwkv_w_rkv[0, 1]), bf(rwkv_w_rkv[0, 2]),
                     bf(rwkv_w1[0]), bf(rwkv_w2[0]), bf(rwkv_a1[0]), bf(rwkv_a2[0]),
                     bf(rwkv_g1[0]), bf(rwkv_g2[0]), row(rwkv_w0[0]), row(rwkv_a0[0]),
                     row(rwkv_k_k[0]), row(rwkv_k_a[0]), seq, t.rows)
    z = _wkv(*outs, row(rwkv_r_k[0]), row(rwkv_lnx_g[0]), row(rwkv_lnx_b[0]),
             batch, seq, t.time, t.chunk, 1)
    h = _mix_mlp([z], bf(rwkv_w_o[0]), h, row(norm_ffn_g[1]), bf(mlp_w_up[1]), bf(mlp_w_down[1]),
                 t.mlp_rows, t.ff_chunk, "mix_mlp1")
    return h.reshape(batch, seq, d)
```

```python
import functools
import math
from typing import NamedTuple

import jax
import jax.numpy as jnp
import numpy as np
from jax import lax
from jax.experimental import pallas as pl
from jax.experimental.pallas import tpu as pltpu

F32 = jnp.float32
BF16 = jnp.bfloat16

RMS_EPS = 1e-6
GN_EPS = 64e-5

HGRN_HEADS = 4
HGRN_DIM = 128
HGRN_SUB = 16
FOX_HEADS = 8
FOX_DIM = 64
FOX_VROWS = FOX_DIM + 16
RWKV_DIM = 64
RWKV_GROUP = 4
GROUP_LANES = RWKV_GROUP * RWKV_DIM
LANES = 128
NEG_BIG = -1e30
LOG2E = 1.4426950408889634

NT_DIMS = (((1,), (1,)), ((), ()))


class _Tiles(NamedTuple):
    rows: int
    mlp_rows: int
    time: int
    fox_keys: int
    chunk: int
    ff_chunk: int


def _tiles(seq):
    return _Tiles(rows=min(512, seq), mlp_rows=min(512, seq), time=min(512, seq),
                  fox_keys=min(256, seq), chunk=64, ff_chunk=1024)


def _cparams(sem, vmem_mb):
    return pltpu.CompilerParams(dimension_semantics=sem, vmem_limit_bytes=vmem_mb * 1024 * 1024)


def _dot(a, b):
    return jnp.dot(a, b, preferred_element_type=F32)


def _dot_nt(a, b):
    return lax.dot_general(a, b, NT_DIMS, preferred_element_type=F32)


def _rms(x, g):
    return x * lax.rsqrt(jnp.mean(x * x, axis=-1, keepdims=True) + RMS_EPS) * g


def _sigmoid(x):
    return 1.0 / (1.0 + jnp.exp(-x))


def _log_sigmoid(x):
    return jnp.minimum(x, 0.0) - jnp.log(1.0 + jnp.exp(-jnp.abs(x)))


def _tril_mask(n, strict=False):
    r = lax.broadcasted_iota(jnp.int32, (n, n), 0)
    c = lax.broadcasted_iota(jnp.int32, (n, n), 1)
    return (c < r) if strict else (c <= r)


def _split3(x):
    hi = x.astype(BF16)
    r1 = x - hi.astype(F32)
    mid = r1.astype(BF16)
    lo = (r1 - mid.astype(F32)).astype(BF16)
    return hi, mid, lo


def _cumsum_rows(x, tril_bf16, pieces=3):
    parts = _split3(x)[:pieces]
    out = _dot(tril_bf16, parts[0])
    for part in parts[1:]:
        out = out + _dot(tril_bf16, part)
    return out


def _inproj_kernel(x_ref, g_ref, w_ref, o_ref):
    hn = _rms(x_ref[...], g_ref[...]).astype(BF16)
    o_ref[...] = _dot(hn, w_ref[...])


def _inproj(x2, g, w, tm):
    m, d = x2.shape
    n = w.shape[1]
    return pl.pallas_call(
        _inproj_kernel,
        grid=(m // tm,),
        in_specs=[
            pl.BlockSpec((tm, d), lambda i: (i, 0)),
            pl.BlockSpec((1, d), lambda i: (0, 0)),
            pl.BlockSpec((d, n), lambda i: (0, 0), pipeline_mode=pl.Buffered(1)),
        ],
        out_specs=pl.BlockSpec((tm, n), lambda i: (i, 0)),
        out_shape=jax.ShapeDtypeStruct((m, n), F32),
        compiler_params=_cparams(("parallel",), 48),
        name="inproj",
    )(x2, g, w)


def _hgrn_kernel(q_ref, f_ref, i_ref, g_ref, lb_ref, ng_ref, o_ref, st_ref, *, chunk):
    @pl.when(pl.program_id(1) == 0)
    def _():
        st_ref[...] = jnp.zeros_like(st_ref)

    ts = q_ref.shape[0]
    tril_b = jnp.where(_tril_mask(chunk), 1.0, 0.0).astype(BF16)

    hs = range(HGRN_HEADS)
    nch = ts // chunk
    tiles = [(c, h) for c in range(nch) for h in hs]
    blk = lambda ref, c, h: ref[c * chunk:(c + 1) * chunk, h * HGRN_DIM:(h + 1) * HGRN_DIM]
    lbs = [lb_ref[:, h * HGRN_DIM:(h + 1) * HGRN_DIM] for h in hs]

    f = [lbs[h] + (1.0 - lbs[h]) * _sigmoid(blk(f_ref, c, h)) for c, h in tiles]
    b = [_cumsum_rows(jnp.log(x), tril_b) for x in f]
    b_last = [x[chunk - 1:chunk, :] for x in b]
    q = [blk(q_ref, c, h) * _sigmoid(blk(q_ref, c, h)) for c, h in tiles]
    k = [1.0 - x for x in f]
    vb = [blk(i_ref, c, h).astype(BF16) for c, h in tiles]
    n = range(len(tiles))

    sub = HGRN_SUB
    tril_sub = _tril_mask(sub)
    score_cols = [[] for _ in n]
    for j in range(chunk // sub):
        lo, hi = j * sub, (j + 1) * sub
        for i in n:
            bj = b[i][lo:hi]
            b_mid = b[i][lo + sub // 2 - 1:lo + sub // 2]
            diag = _dot_nt((q[i][lo:hi] * jnp.exp(bj - b_mid)).astype(BF16),
                           (k[i][lo:hi] * jnp.exp(b_mid - bj)).astype(BF16))
            parts = [jnp.where(tril_sub, diag, 0.0)]
            if lo > 0:
                parts.insert(0, jnp.zeros((lo, sub), F32))
            if hi < chunk:
                b_end = b[i][hi - 1:hi]
                parts.append(_dot_nt((q[i][hi:] * jnp.exp(b[i][hi:] - b_end)).astype(BF16),
                                     (k[i][lo:hi] * jnp.exp(b_end - bj)).astype(BF16)))
            score_cols[i].append(jnp.concatenate(parts, axis=0).astype(BF16))
    o = []
    for i in n:
        acc = _dot(score_cols[i][0], vb[i][:sub])
        for j in range(1, chunk // sub):
            acc = acc + _dot(score_cols[i][j], vb[i][j * sub:(j + 1) * sub])
        o.append(acc)
    inc = [_dot(blk(i_ref, c, h).T.astype(BF16), (k[i] * jnp.exp(b_last[i] - b[i])).astype(BF16))
           for i, (c, h) in enumerate(tiles)]
    dec = [jnp.exp(x) for x in b_last]

    st = [st_ref[h] for h in hs]
    st_in = []
    for i, (c, h) in enumerate(tiles):
        st_in.append(st[h].astype(BF16))
        st[h] = st[h] * dec[i] + inc[i]
    for h in hs:
        st_ref[h] = st[h]

    for i, (c, h) in enumerate(tiles):
        oi = o[i] + _dot_nt((q[i] * jnp.exp(b[i])).astype(BF16), st_in[i])
        ag = blk(g_ref, c, h)
        on = _rms(oi, ng_ref[:, h * HGRN_DIM:(h + 1) * HGRN_DIM])
        o_ref[c * chunk:(c + 1) * chunk, h * HGRN_DIM:(h + 1) * HGRN_DIM] = (
            on * (ag * _sigmoid(ag))).astype(o_ref.dtype)


def _hgrn(proj, lb, ng, batch, seq, ts, chunk):
    m = proj.shape[0]
    w = HGRN_HEADS * HGRN_DIM
    nt = seq // ts
    spec = lambda j: pl.BlockSpec((ts, w), lambda b, t, j=j: (b * nt + t, j))
    vec = pl.BlockSpec((1, w), lambda b, t: (0, 0))
    return pl.pallas_call(
        functools.partial(_hgrn_kernel, chunk=chunk),
        grid=(batch, nt),
        in_specs=[spec(0), spec(1), spec(2), spec(3), vec, vec],
        out_specs=pl.BlockSpec((ts, w), lambda b, t: (b * nt + t, 0)),
        out_shape=jax.ShapeDtypeStruct((m, w), BF16),
        scratch_shapes=[pltpu.VMEM((HGRN_HEADS, HGRN_DIM, HGRN_DIM), F32)],
        compiler_params=_cparams(("parallel", "arbitrary"), 32),
        name="hgrn2",
    )(proj, proj, proj, proj, lb, ng)


def _foxprep_kernel(q_ref, k_ref, v_ref, f_ref, fb_ref, qg_ref, kg_ref, hsum_ref, wq_ref, wk_ref,
                    oq_ref, ok_ref, qt_ref, ka_ref, vt_ref, carry_ref):
    @pl.when(pl.program_id(1) == 0)
    def _():
        carry_ref[...] = jnp.zeros_like(carry_ref)

    ts = q_ref.shape[0]
    pairs = FOX_HEADS // 2
    tril_b = jnp.where(_tril_mask(ts), 1.0, 0.0).astype(BF16)
    lf = _log_sigmoid(f_ref[...] + fb_ref[...])
    c = _cumsum_rows(lf, tril_b) + carry_ref[...]
    carry_ref[...] = c[ts - 1:ts, :]
    c2 = c * LOG2E
    hi = c2.astype(BF16).astype(F32)
    rest = c2 - hi
    mid = rest.astype(BF16).astype(F32)
    lane = lax.broadcasted_iota(jnp.int32, c.shape, 1)
    pieces = jnp.where(lane < FOX_HEADS, hi,
                       jnp.where(lane < 2 * FOX_HEADS, mid, rest - mid)).astype(BF16)

    def headnorm(x_ref, g_ref, r, scale):
        cols = slice(r * 2 * LANES, (r + 1) * 2 * LANES)
        x = x_ref[:, cols]
        xx = x * x
        xh = xx.astype(BF16)
        ss = _dot(xh, hsum_ref[...]) + _dot((xx - xh.astype(F32)).astype(BF16), hsum_ref[...])
        return (x * lax.rsqrt(ss * (1.0 / FOX_DIM) + RMS_EPS) * (g_ref[:, cols] * scale)).astype(BF16)

    qn = [headnorm(q_ref, qg_ref, r, FOX_DIM ** -0.5 * LOG2E) for r in range(pairs // 2)]
    kn = [headnorm(k_ref, kg_ref, r, 1.0) for r in range(pairs // 2)]
    for r in range(pairs):
        src = slice((r % 2) * LANES, (r % 2 + 1) * LANES)
        out = slice(r * 2 * LANES, (r + 1) * 2 * LANES)
        qa = _dot(jnp.concatenate([qn[r // 2][:, src], pieces], axis=1), wq_ref[r]) + oq_ref[:, out]
        ka = _dot(jnp.concatenate([kn[r // 2][:, src], pieces], axis=1), wk_ref[r]) + ok_ref[:, out]
        ka_ref[:, out] = ka.astype(BF16)
        for hh in range(2):
            qt_ref[0, 2 * r + hh] = qa[:, hh * LANES:(hh + 1) * LANES].T.astype(BF16)
        vt = v_ref[:, r * LANES:(r + 1) * LANES].T
        extra = jnp.where(lax.broadcasted_iota(jnp.int32, (FOX_VROWS - FOX_DIM, ts), 0) == 0, 1.0, 0.0)
        for hh in range(2):
            vt_ref[0, 2 * r + hh, 0] = jnp.concatenate(
                [vt[hh * FOX_DIM:(hh + 1) * FOX_DIM], extra], axis=0).astype(BF16)


def _foxprep(proj, fb, qg, kg, batch, seq, ts):
    m = proj.shape[0]
    w = FOX_HEADS * FOX_DIM
    nt = seq // ts
    pairs = FOX_HEADS // 2
    spec = lambda j: pl.BlockSpec((ts, w), lambda b, t, j=j: (b * nt + t, j))
    fcol = (8 * w) // LANES
    const = lambda a: pl.BlockSpec(a.shape, lambda b, t: (0,) * a.ndim)

    wa = FOX_HEADS * LANES
    ch = np.arange(2 * LANES)
    hsum = (ch[:, None] // FOX_DIM == ch[None, :] // FOX_DIM).astype(np.float32)
    wq = np.zeros((pairs, 2 * LANES, 2 * LANES), np.float32)
    wk = np.zeros((pairs, 2 * LANES, 2 * LANES), np.float32)
    oq = np.zeros((1, wa), np.float32)
    ok = np.zeros((1, wa), np.float32)
    for r in range(pairs):
        for hh in range(2):
            h = 2 * r + hh
            d = np.arange(FOX_DIM)
            wq[r, hh * FOX_DIM + d, hh * LANES + d] = 1.0
            wk[r, hh * FOX_DIM + d, hh * LANES + d] = 1.0
            for p in range(3):
                wq[r, LANES + p * FOX_HEADS + h, hh * LANES + FOX_DIM + p] = 1.0
                wk[r, LANES + p * FOX_HEADS + h, hh * LANES + FOX_DIM + 3 + p] = -1.0
                oq[0, h * LANES + FOX_DIM + 3 + p] = 1.0
                ok[0, h * LANES + FOX_DIM + p] = 1.0
    consts = [jnp.asarray(a, BF16) for a in (hsum, wq, wk)] + [jnp.asarray(oq), jnp.asarray(ok)]
    qg = jnp.tile(qg, (1, FOX_HEADS))
    kg = jnp.tile(kg, (1, FOX_HEADS))
    return pl.pallas_call(
        _foxprep_kernel,
        grid=(batch, nt),
        in_specs=[spec(4), spec(5), spec(6),
                  pl.BlockSpec((ts, LANES), lambda b, t: (b * nt + t, fcol)),
                  const(fb), const(qg), const(kg)] + [const(a) for a in consts],
        out_specs=[pl.BlockSpec((1, FOX_HEADS, LANES, ts), lambda b, t: (b, 0, 0, t)),
                   pl.BlockSpec((ts, FOX_HEADS * LANES), lambda b, t: (b * nt + t, 0)),
                   pl.BlockSpec((1, FOX_HEADS, 1, FOX_VROWS, ts), lambda b, t: (b, 0, t, 0, 0))],
        out_shape=[jax.ShapeDtypeStruct((batch, FOX_HEADS, LANES, seq), BF16),
                   jax.ShapeDtypeStruct((m, FOX_HEADS * LANES), BF16),
                   jax.ShapeDtypeStruct((batch, FOX_HEADS, nt, FOX_VROWS, ts), BF16)],
        scratch_shapes=[pltpu.VMEM((1, LANES), F32)],
        compiler_params=_cparams(("parallel", "arbitrary"), 32),
        name="foxprep",
    )(proj, proj, proj, proj, fb, qg, kg, *consts)


def _fox_kernel(qt_ref, k_ref, vt_ref, g_ref, o_ref, *, tq, tk, heads):
    i = pl.program_id(2)
    hs = range(heads)
    ratio = tq // tk

    def step(j, carry, q0):
        ms, accs = carry
        diagonal = q0 is not None
        q0 = q0 or 0
        nq = tq - q0
        nkeys = tk if diagonal else tq
        rows = pl.ds(pl.multiple_of(j * tq, tq) + q0, nkeys)
        keys = slice(q0, q0 + nkeys)
        if diagonal:
            visible = (lax.broadcasted_iota(jnp.int32, (tk, nq), 1)
                       >= lax.broadcasted_iota(jnp.int32, (tk, nq), 0))

        def scores(h):
            s = _dot(k_ref[rows, h * LANES:(h + 1) * LANES], qt_ref[0, h, :, q0:])
            return jnp.where(visible, s, NEG_BIG) if diagonal else s

        def softmax(h, s):
            m_old = ms[h][:, q0:]
            m_new = jnp.maximum(m_old, jnp.max(s, axis=0, keepdims=True))
            return m_new, jnp.exp2(m_old - m_new), jnp.exp2(s - m_new).astype(BF16)

        def values(h, alpha, p):
            return accs[h][:, q0:] * alpha + _dot(vt_ref[0, h, j, :, keys], p)

        s, sm, out = {}, {}, {}
        for t in range(heads + 2):
            if t < heads:
                s[t] = scores(t)
            if 0 <= t - 1 < heads:
                sm[t - 1] = softmax(t - 1, s[t - 1])
            if 0 <= t - 2 < heads:
                out[t - 2] = values(t - 2, sm[t - 2][1], sm[t - 2][2])
        keep = lambda old, new: new if q0 == 0 else jnp.concatenate([old[:, :q0], new], axis=1)
        return (tuple(keep(ms[h], sm[h][0]) for h in hs), tuple(keep(accs[h], out[h]) for h in hs))

    neg = jnp.full((1, tq), NEG_BIG, F32)
    carry = ((neg,) * heads, (jnp.zeros((FOX_VROWS, tq), F32),) * heads)
    carry = lax.fori_loop(0, i, lambda j, c: step(j, c, None), carry)
    for d in range(ratio):
        carry = step(i, carry, d * tk)
    _, accs = carry
    norm = [a[:FOX_DIM] / a[FOX_DIM:FOX_DIM + 1] for a in accs]
    for r in range(heads // 2):
        cols = slice(r * LANES, (r + 1) * LANES)
        out = jnp.concatenate([norm[2 * r], norm[2 * r + 1]], axis=0)
        o_ref[:, cols] = (out.T * _sigmoid(g_ref[:, cols])).astype(o_ref.dtype)


def _fox(qt, ka, vt, proj, batch, seq, tq, tk, heads):
    m = ka.shape[0]
    nq = seq // tq
    groups = FOX_HEADS // heads
    pairs = heads // 2
    wv = heads * FOX_DIM
    gcol = (7 * FOX_HEADS * FOX_DIM) // wv
    return pl.pallas_call(
        functools.partial(_fox_kernel, tq=tq, tk=tk, heads=heads),
        grid=(batch, groups, nq),
        in_specs=[pl.BlockSpec((1, heads, LANES, tq), lambda b, p, i: (b, p, 0, i)),
                  pl.BlockSpec((seq, heads * LANES), lambda b, p, i: (b, p)),
                  pl.BlockSpec((1, heads, nq, FOX_VROWS, tq), lambda b, p, i: (b, p, 0, 0, 0)),
                  pl.BlockSpec((tq, wv), lambda b, p, i: (b * nq + i, gcol + p))],
        out_specs=pl.BlockSpec((tq, wv), lambda b, p, i: (b * nq + i, p)),
        out_shape=jax.ShapeDtypeStruct((m, FOX_HEADS * FOX_DIM), BF16),
        compiler_params=_cparams(("parallel", "parallel", "arbitrary"), 48),
        name="fox_attention",
    )(qt, ka, vt, proj)


def _mix_mlp_kernel(*refs, n_mix, ck):
    ys = refs[:n_mix]
    w_ref, h_ref, g_ref, wu_ref, wd_ref, o_ref = refs[n_mix:]
    y = ys[0][...] if n_mix == 1 else jnp.concatenate([r[...] for r in ys], axis=1)
    x = h_ref[...] + _dot(y, w_ref[...])
    hn = _rms(x, g_ref[...]).astype(BF16)
    acc = x
    for c in range(wu_ref.shape[1] // ck):
        u = jnp.maximum(_dot(hn, wu_ref[:, c * ck:(c + 1) * ck]), 0.0)
        acc = acc + _dot((u * u).astype(BF16), wd_ref[c * ck:(c + 1) * ck, :])
    o_ref[...] = acc


def _mix_mlp(ys, w, h, g, wu, wd, tm, ck, name):
    m, d = h.shape
    const = lambda a: pl.BlockSpec(a.shape, lambda i: (0, 0), pipeline_mode=pl.Buffered(1))
    return pl.pallas_call(
        functools.partial(_mix_mlp_kernel, n_mix=len(ys), ck=ck),
        grid=(m // tm,),
        in_specs=([pl.BlockSpec((tm, y.shape[1]), lambda i: (i, 0)) for y in ys]
                  + [const(w), pl.BlockSpec((tm, d), lambda i: (i, 0)), const(g), const(wu), const(wd)]),
        out_specs=pl.BlockSpec((tm, d), lambda i: (i, 0)),
        out_shape=jax.ShapeDtypeStruct((m, d), F32),
        compiler_params=_cparams(("parallel",), 56),
        name=name,
    )(*ys, w, h, g, wu, wd)


def _rwkvproj_kernel(h_ref, hp_ref, g_ref, mu_ref, wr_ref, wk_ref, wv_ref, w1_ref, w2_ref,
                     a1_ref, a2_ref, g1_ref, g2_ref, w0_ref, a0_ref, kk_ref, ka_ref,
                     r_ref, lw_ref, km_ref, v_ref, kr_ref, a_ref, go_ref, *, tiles_per_seq):
    i = pl.program_id(0)
    tm = h_ref.shape[0]
    gn = g_ref[...]
    hn = _rms(h_ref[...], gn)
    prev = _rms(hp_ref[7:8, :], gn)
    prev = jnp.where(i % tiles_per_seq == 0, jnp.zeros_like(prev), prev)
    row = lax.broadcasted_iota(jnp.int32, hn.shape, 0)
    shifted = jnp.where(row == 0, jnp.broadcast_to(prev, hn.shape), pltpu.roll(hn, 1, 0))
    xx = shifted - hn
    hn_b = hn.astype(BF16)
    xx_b = xx.astype(BF16)
    mix = lambda j: hn_b + xx_b * mu_ref[j:j + 1, :].astype(BF16)
    r = _dot(mix(0), wr_ref[...])
    k = _dot(mix(2), wk_ref[...])
    v = _dot(mix(3), wv_ref[...])
    z = w0_ref[...] + _dot(jnp.tanh(_dot(mix(1), w1_ref[...])).astype(BF16), w2_ref[...])
    a = _sigmoid(a0_ref[...] + _dot(_dot(mix(4), a1_ref[...]).astype(BF16), a2_ref[...]))
    g = _dot(_sigmoid(_dot(mix(5), g1_ref[...])).astype(BF16), g2_ref[...])
    r_ref[...] = r
    lw_ref[...] = _sigmoid(z) * (-math.exp(-0.5))
    km_ref[...] = k * (1.0 + (a - 1.0) * ka_ref[...])
    v_ref[...] = v
    kr_ref[...] = k * kk_ref[...]
    a_ref[...] = a
    go_ref[...] = g


def _rwkvproj(h, g, mu, wr, wk, wv, w1, w2, a1, a2, g1, g2, w0, a0, k_k, k_a, seq, tm):
    m, d = h.shape
    tiles_per_seq = seq // tm
    full = lambda a: pl.BlockSpec(a.shape, lambda i: (0,) * a.ndim, pipeline_mode=pl.Buffered(1))
    row = pl.BlockSpec((tm, d), lambda i: (i, 0))
    prev = pl.BlockSpec((8, d), lambda i: (jnp.maximum(i * (tm // 8) - 1, 0), 0))
    consts = (g, mu, wr, wk, wv, w1, w2, a1, a2, g1, g2, w0, a0, k_k, k_a)
    return pl.pallas_call(
        functools.partial(_rwkvproj_kernel, tiles_per_seq=tiles_per_seq),
        grid=(m // tm,),
        in_specs=[row, prev] + [full(a) for a in consts],
        out_specs=[row] * 7,
        out_shape=[jax.ShapeDtypeStruct((m, d), F32)] * 7,
        compiler_params=_cparams(("parallel",), 56),
        name="rwkv_proj",
    )(h, h, *consts)


def _wkv_kernel(r_ref, lw_ref, km_ref, v_ref, kr_ref, a_ref, g_ref, rk_ref, lg_ref, lb_ref,
                o_ref, st_ref, y_ref, q1_ref, lhs_ref, z0_ref, wc_ref, bonus_ref, gate_ref,
                *, chunk, groups, gps):
    step = pl.program_id(2)

    @pl.when(pl.program_id(1) == 0)
    def _():
        for gi in range(gps):
            st_ref[step * gps + gi] = jnp.zeros(st_ref.shape[1:], F32)

    ts = r_ref.shape[0]
    nch = ts // chunk
    gl = GROUP_LANES
    rb = lax.broadcasted_iota(jnp.int32, (gl, gl), 0) // RWKV_DIM
    cb = lax.broadcasted_iota(jnp.int32, (gl, gl), 1) // RWKV_DIM
    blockmask = rb == cb
    ones_bd = jnp.where(blockmask, 1.0, 0.0).astype(BF16)

    def headsum(x, pieces=2):
        hi = x.astype(BF16)
        out = _dot(hi, ones_bd)
        if pieces == 2:
            out = out + _dot((x - hi.astype(F32)).astype(BF16), ones_bd)
        return out

    def bd(y):
        reps = gl // y.shape[0]
        return jnp.where(blockmask, jnp.concatenate([y] * reps, axis=0), 0.0).astype(BF16)

    def hmm(x, y):
        return _dot(x.astype(BF16), bd(y))

    def tn_blocks(x, y):
        return jnp.where(blockmask, _dot(x.T.astype(BF16), y.astype(BF16)), 0.0)

    t_idx = lax.broadcasted_iota(jnp.int32, (chunk, gl), 0)
    s_idx = lax.broadcasted_iota(jnp.int32, (chunk, gl), 1) % RWKV_DIM
    strict = s_idx < t_idx
    incl = s_idx <= t_idx
    tril_b = jnp.where(_tril_mask(chunk), 1.0, 0.0).astype(BF16)
    zeros_c = jnp.zeros((chunk, gl), F32)

    tiles = [(gi, c) for gi in range(gps) for c in range(nch)]
    lanes_of = lambda gi: slice(gi * gl, (gi + 1) * gl)
    blk = lambda ref, gi, c: ref[c * chunk:(c + 1) * chunk, lanes_of(gi)]
    pad = lambda x: jnp.concatenate([x, zeros_c], axis=0)
    each = lambda fn, *lists: [fn(*args) for args in zip(*lists)]

    kkn_all = []
    for gi in range(gps):
        kr = kr_ref[:, lanes_of(gi)]
        kkn_all.append(kr * lax.rsqrt(jnp.maximum(headsum(kr * kr), 1e-24)))

    lw = [blk(lw_ref, gi, c) for gi, c in tiles]
    cum = each(lambda x: _cumsum_rows(x, tril_b, pieces=2), lw)
    c_last = [x[chunk - 1:chunk, :] for x in cum]
    kkn = [kkn_all[gi][c * chunk:(c + 1) * chunk, :] for gi, c in tiles]
    kka = [kkn[i] * blk(a_ref, gi, c) for i, (gi, c) in enumerate(tiles)]
    km = [blk(km_ref, gi, c) for gi, c in tiles]
    v = [blk(v_ref, gi, c) for gi, c in tiles]
    e_neg = [jnp.exp(-x) for x in cum]
    e_end = each(lambda cl, x: jnp.exp(cl - x), c_last, cum)
    at = each(lambda k, x, l: -k * jnp.exp(x - l), kkn, cum, lw)
    bt = each(jnp.multiply, kka, e_neg)
    kt = each(jnp.multiply, km, e_neg)
    rt = [blk(r_ref, gi, c) * jnp.exp(cum[i]) for i, (gi, c) in enumerate(tiles)]
    bw = each(jnp.multiply, kka, e_end)
    kw = each(jnp.multiply, km, e_end)
    first_head = lax.broadcasted_iota(jnp.int32, (RWKV_DIM, LANES), 1) < RWKV_DIM
    for i, (gi, c) in enumerate(tiles):
        wt = jnp.broadcast_to(jnp.exp(c_last[i]), (LANES, gl)).T
        wc_ref[step * gps + gi, c] = jnp.concatenate(
            [jnp.where(first_head, wt[2 * p * RWKV_DIM:(2 * p + 1) * RWKV_DIM],
                       wt[(2 * p + 1) * RWKV_DIM:(2 * p + 2) * RWKV_DIM])
             for p in range(RWKV_GROUP // 2)], axis=1)

    lhs = each(lambda a, r: jnp.concatenate([a, r], axis=0).astype(BF16), at, rt)
    pb = each(lambda l, b: _dot_nt(l, bd(b)), lhs, bt)
    pk = each(lambda l, k: _dot_nt(l, bd(k)), lhs, kt)
    a_ab = [jnp.where(strict, x[:chunk], 0.0) for x in pb]
    a_rb = [jnp.where(incl, x[chunk:], 0.0) for x in pb]
    a_ak = [jnp.where(strict, x[:chunk], 0.0) for x in pk]
    a_rk = [jnp.where(incl, x[chunk:], 0.0) for x in pk]

    e = [jnp.where((t_idx % 2 == 1) & (s_idx == t_idx - 1), x, 0.0) for x in a_ab]
    size = 2
    while size < chunk:
        off = ((t_idx // size) % 2 == 1) & (s_idx // size == t_idx // size - 1)
        a_off = [jnp.where(off, x, 0.0) for x in a_ab]
        t1 = each(lambda ao, ee: ao + hmm(ao, ee), a_off, e)
        e = each(lambda ee, tt: ee + tt + hmm(ee, tt), e, t1)
        size *= 2

    av = each(lambda ak, rk, vv: hmm(jnp.concatenate([ak, rk], axis=0), vv), a_ak, a_rk, v)
    akv = [x[:chunk] for x in av]
    p1 = each(lambda x, ee: x + hmm(ee, x), akv, e)
    mat = each(lambda x, ee: x + hmm(ee, x), at, e)
    q1 = each(lambda x, arb, pp: x[chunk:] + hmm(arb, pp), av, a_rb, p1)
    r2 = each(lambda r, arb, mm: r + hmm(arb, mm), rt, a_rb, mat)
    pct = each(lambda b, mm: tn_blocks(pad(b), pad(mm)), bw, mat)
    z0 = each(lambda b, k, pp, vv: tn_blocks(jnp.concatenate([b, k], axis=0),
                                             jnp.concatenate([pp, vv], axis=0)), bw, kw, p1, v)
    def fold(x):
        out = x[:RWKV_DIM]
        for h in range(1, RWKV_GROUP):
            out = out + x[h * RWKV_DIM:(h + 1) * RWKV_DIM]
        return out

    for i, (gi, c) in enumerate(tiles):
        g = step * gps + gi
        q1_ref[g, c * chunk:(c + 1) * chunk, :] = q1[i]
        lhs_ref[g, c, :chunk, :] = r2[i].astype(BF16)
        lhs_ref[g, c, chunk:, :] = fold(pct[i]).astype(BF16)
        z0_ref[g, c] = fold(z0[i])
    for gi in range(gps):
        cols = lanes_of(gi)
        bonus_ref[step * gps + gi] = headsum(
            r_ref[:, cols] * km_ref[:, cols] * rk_ref[:, cols], pieces=1) * v_ref[:, cols]
        gate_ref[step * gps + gi] = g_ref[:, cols]

    @pl.when(step == groups // gps - 1)
    def _():
        gs = range(groups)

        def body(c, carry):
            rows = pl.ds(pl.multiple_of(c * chunk, chunk), chunk)
            st = [st_ref[j] for j in gs]
            res = [_dot(lhs_ref[j, c], bd(st[j])) for j in gs]
            for j in gs:
                y_ref[j, rows, :] = q1_ref[j, rows, :] + res[j][:chunk]
                st_ref[j] = st[j] * wc_ref[j, c] + res[j][chunk:] + z0_ref[j, c]
            return carry

        lax.fori_loop(0, nch, body, 0)

        inv_n = 1.0 / RWKV_DIM
        for j in gs:
            cols = slice(j * gl, (j + 1) * gl)
            y = y_ref[j]
            mean = headsum(y, pieces=1) * inv_n
            dlt = y - mean
            var = headsum(dlt * dlt, pieces=1) * inv_n
            yn = dlt * lax.rsqrt(var + GN_EPS) * lg_ref[:, cols] + lb_ref[:, cols]
            o_ref[:, cols] = ((yn + bonus_ref[j]) * gate_ref[j]).astype(o_ref.dtype)


def _wkv(r, lw, km, v, kr, a, g, r_k, lnx_g, lnx_b, batch, seq, ts, chunk, gps):
    m, d = r.shape
    gl = GROUP_LANES
    groups = d // gl
    nt = seq // ts
    nch = ts // chunk
    row = pl.BlockSpec((ts, gps * gl), lambda b, t, j: (b * nt + t, j))
    vec = pl.BlockSpec((1, gps * gl), lambda b, t, j: (0, j))
    full = pl.BlockSpec((1, d), lambda b, t, j: (0, 0))
    return pl.pallas_call(
        functools.partial(_wkv_kernel, chunk=chunk, groups=groups, gps=gps),
        grid=(batch, nt, groups // gps),
        in_specs=[row] * 7 + [vec, full, full],
        out_specs=pl.BlockSpec((ts, d), lambda b, t, j: (b * nt + t, 0)),
        out_shape=jax.ShapeDtypeStruct((m, d), BF16),
        scratch_shapes=[pltpu.VMEM((groups, RWKV_DIM, gl), F32),
                        pltpu.VMEM((groups, ts, gl), F32),
                        pltpu.VMEM((groups, ts, gl), F32),
                        pltpu.VMEM((groups, nch, chunk + RWKV_DIM, gl), BF16),
                        pltpu.VMEM((groups, nch, RWKV_DIM, gl), F32),
                        pltpu.VMEM((groups, nch, RWKV_DIM, gl), F32),
                        pltpu.VMEM((groups, ts, gl), F32),
                        pltpu.VMEM((groups, ts, gl), F32)],
        compiler_params=_cparams(("parallel", "arbitrary", "arbitrary"), 56),
        name="wkv7",
    )(r, lw, km, v, kr, a, g, r_k, lnx_g, lnx_b)


def kernel(x, norm_mix_g, norm_ffn_g, ab_w_in, hgrn_lower_bounds, hgrn_norm_g, fox_forget_bias,
           fox_q_norm_g, fox_k_norm_g, ab_w_out, rwkv_mu, rwkv_w_rkv, rwkv_w0, rwkv_w1, rwkv_w2,
           rwkv_a0, rwkv_a1, rwkv_a2, rwkv_g1, rwkv_g2, rwkv_k_k, rwkv_k_a, rwkv_r_k,
           rwkv_lnx_g, rwkv_lnx_b, rwkv_w_o, mlp_w_up, mlp_w_down):
    batch, seq, d = x.shape
    m = batch * seq
    t = _tiles(seq)
    row = lambda a: a.reshape(1, -1).astype(F32)
    bf = lambda a: a.astype(BF16)

    lb_all = jnp.cumsum(jax.nn.softmax(hgrn_lower_bounds.astype(F32), axis=0), axis=0)
    h = x.reshape(m, d)

    n_wide = ab_w_in.shape[-1] - FOX_HEADS
    gate_w = jnp.tile(ab_w_in[0][:, n_wide:], (1, 3))
    w_in = bf(jnp.pad(jnp.concatenate([ab_w_in[0][:, :n_wide], gate_w], axis=1),
                      ((0, 0), (0, LANES - 3 * FOX_HEADS))))
    proj = _inproj(h, row(norm_mix_g[0]), w_in, t.rows)
    ya = _hgrn(proj, row(lb_all[0]), row(hgrn_norm_g[0]), batch, seq, t.time, t.chunk)
    fb = jnp.pad(jnp.tile(row(fox_forget_bias[0]), (1, 3)), ((0, 0), (0, LANES - 3 * FOX_HEADS)))
    qt, ka, vt = _foxprep(proj, fb, row(fox_q_norm_g[0]), row(fox_k_norm_g[0]), batch, seq, t.time)
    yb = _fox(qt, ka, vt, proj, batch, seq, t.time, t.fox_keys, FOX_HEADS)
    h = _mix_mlp([ya, yb], bf(ab_w_out[0]), h, row(norm_ffn_g[0]), bf(mlp_w_up[0]), bf(mlp_w_down[0]),
                 t.mlp_rows, t.ff_chunk, "mix_mlp0")

    outs = _rwkvproj(h, row(norm_mix_g[1]), rwkv_mu[0].astype(F32),
                     bf(rwkv_w_rkv[0, 0]), bf(rwkv_w_rkv[0, 1]), bf(rwkv_w_rkv[0, 2]),
                     bf(rwkv_w1[0]), bf(rwkv_w2[0]), bf(rwkv_a1[0]), bf(rwkv_a2[0]),
                     bf(rwkv_g1[0]), bf(rwkv_g2[0]), row(rwkv_w0[0]), row(rwkv_a0[0]),
                     row(rwkv_k_k[0]), row(rwkv_k_a[0]), seq, t.rows)
    z = _wkv(*outs, row(rwkv_r_k[0]), row(rwkv_lnx_g[0]), row(rwkv_lnx_b[0]),
             batch, seq, t.time, t.chunk, 1)
    h = _mix_mlp([z], bf(rwkv_w_o[0]), h, row(norm_ffn_g[1]), bf(mlp_w_up[1]), bf(mlp_w_down[1]),
                 t.mlp_rows, t.ff_chunk, "mix_mlp1")
    return h.reshape(batch, seq, d)
```

```python
import functools
import math
from typing import NamedTuple

import jax
import jax.numpy as jnp
import numpy as np
from jax import lax
from jax.experimental import pallas as pl
from jax.experimental.pallas import tpu as pltpu

F32 = jnp.float32
BF16 = jnp.bfloat16

RMS_EPS = 1e-6
GN_EPS = 64e-5

HGRN_HEADS = 4
HGRN_DIM = 128
HGRN_SUB = 16
FOX_HEADS = 8
FOX_DIM = 64
FOX_VROWS = FOX_DIM + 16
RWKV_DIM = 64
RWKV_GROUP = 4
GROUP_LANES = RWKV_GROUP * RWKV_DIM
LANES = 128
VMEM_V7X_MIB = 64
VMEM_SMALL_MIB = VMEM_V7X_MIB // 2
VMEM_MEDIUM_MIB = 3 * VMEM_V7X_MIB // 4
VMEM_LARGE_MIB = 7 * VMEM_V7X_MIB // 8
NEG_BIG = -1e30
LOG2E = 1.4426950408889634

NT_DIMS = (((1,), (1,)), ((), ()))


class _Tiles(NamedTuple):
    rows: int
    mlp_rows: int
    time: int
    fox_keys: int
    chunk: int
    ff_chunk: int


def _tiles(seq):
    return _Tiles(rows=min(512, seq), mlp_rows=min(512, seq), time=min(512, seq),
                  fox_keys=min(256, seq), chunk=64, ff_chunk=1024)


def _cparams(sem, vmem_mb):
    return pltpu.CompilerParams(dimension_semantics=sem, vmem_limit_bytes=vmem_mb * 1024 * 1024)


def _dot(a, b):
    return jnp.dot(a, b, preferred_element_type=F32)


def _dot_nt(a, b):
    return lax.dot_general(a, b, NT_DIMS, preferred_element_type=F32)


def _rms(x, g):
    return x * lax.rsqrt(jnp.mean(x * x, axis=-1, keepdims=True) + RMS_EPS) * g


def _sigmoid(x):
    return 1.0 / (1.0 + jnp.exp(-x))


def _log_sigmoid(x):
    return jnp.minimum(x, 0.0) - jnp.log(1.0 + jnp.exp(-jnp.abs(x)))


def _tril_mask(n, strict=False):
    r = lax.broadcasted_iota(jnp.int32, (n, n), 0)
    c = lax.broadcasted_iota(jnp.int32, (n, n), 1)
    return (c < r) if strict else (c <= r)


def _split3(x):
    hi = x.astype(BF16)
    r1 = x - hi.astype(F32)
    mid = r1.astype(BF16)
    lo = (r1 - mid.astype(F32)).astype(BF16)
    return hi, mid, lo


def _cumsum_rows(x, tril_bf16, pieces=3):
    parts = _split3(x)[:pieces]
    out = _dot(tril_bf16, parts[0])
    for part in parts[1:]:
        out = out + _dot(tril_bf16, part)
    return out


def _inproj_kernel(x_ref, g_ref, w_ref, o_ref):
    hn = _rms(x_ref[...], g_ref[...]).astype(BF16)
    o_ref[...] = _dot(hn, w_ref[...])


def _inproj(x2, g, w, tm):
    m, d = x2.shape
    n = w.shape[1]
    return pl.pallas_call(
        _inproj_kernel,
        grid=(m // tm,),
        in_specs=[
            pl.BlockSpec((tm, d), lambda i: (i, 0)),
            pl.BlockSpec((1, d), lambda i: (0, 0)),
            pl.BlockSpec((d, n), lambda i: (0, 0), pipeline_mode=pl.Buffered(1)),
        ],
        out_specs=pl.BlockSpec((tm, n), lambda i: (i, 0)),
        out_shape=jax.ShapeDtypeStruct((m, n), F32),
        compiler_params=_cparams(("parallel",), VMEM_MEDIUM_MIB),
        name="inproj",
    )(x2, g, w)


def _hgrn_kernel(q_ref, f_ref, i_ref, g_ref, lb_ref, ng_ref, o_ref, st_ref, *, chunk):
    @pl.when(pl.program_id(1) == 0)
    def _():
        st_ref[...] = jnp.zeros_like(st_ref)

    ts = q_ref.shape[0]
    tril_b = jnp.where(_tril_mask(chunk), 1.0, 0.0).astype(BF16)

    hs = range(HGRN_HEADS)
    nch = ts // chunk
    tiles = [(c, h) for c in range(nch) for h in hs]
    blk = lambda ref, c, h: ref[c * chunk:(c + 1) * chunk, h * HGRN_DIM:(h + 1) * HGRN_DIM]
    lbs = [lb_ref[:, h * HGRN_DIM:(h + 1) * HGRN_DIM] for h in hs]

    f = [lbs[h] + (1.0 - lbs[h]) * _sigmoid(blk(f_ref, c, h)) for c, h in tiles]
    b = [_cumsum_rows(jnp.log(x), tril_b) for x in f]
    b_last = [x[chunk - 1:chunk, :] for x in b]
    q = [blk(q_ref, c, h) * _sigmoid(blk(q_ref, c, h)) for c, h in tiles]
    k = [1.0 - x for x in f]
    vb = [blk(i_ref, c, h).astype(BF16) for c, h in tiles]
    n = range(len(tiles))

    sub = HGRN_SUB
    tril_sub = _tril_mask(sub)
    score_cols = [[] for _ in n]
    for j in range(chunk // sub):
        lo, hi = j * sub, (j + 1) * sub
        for i in n:
            bj = b[i][lo:hi]
            b_mid = b[i][lo + sub // 2 - 1:lo + sub // 2]
            diag = _dot_nt((q[i][lo:hi] * jnp.exp(bj - b_mid)).astype(BF16),
                           (k[i][lo:hi] * jnp.exp(b_mid - bj)).astype(BF16))
            parts = [jnp.where(tril_sub, diag, 0.0)]
            if lo > 0:
                parts.insert(0, jnp.zeros((lo, sub), F32))
            if hi < chunk:
                b_end = b[i][hi - 1:hi]
                parts.append(_dot_nt((q[i][hi:] * jnp.exp(b[i][hi:] - b_end)).astype(BF16),
                                     (k[i][lo:hi] * jnp.exp(b_end - bj)).astype(BF16)))
            score_cols[i].append(jnp.concatenate(parts, axis=0).astype(BF16))
    o = []
    for i in n:
        acc = _dot(score_cols[i][0], vb[i][:sub])
        for j in range(1, chunk // sub):
            acc = acc + _dot(score_cols[i][j], vb[i][j * sub:(j + 1) * sub])
        o.append(acc)
    inc = [_dot(blk(i_ref, c, h).T.astype(BF16), (k[i] * jnp.exp(b_last[i] - b[i])).astype(BF16))
           for i, (c, h) in enumerate(tiles)]
    dec = [jnp.exp(x) for x in b_last]

    st = [st_ref[h] for h in hs]
    st_in = []
    for i, (c, h) in enumerate(tiles):
        st_in.append(st[h].astype(BF16))
        st[h] = st[h] * dec[i] + inc[i]
    for h in hs:
        st_ref[h] = st[h]

    for i, (c, h) in enumerate(tiles):
        oi = o[i] + _dot_nt((q[i] * jnp.exp(b[i])).astype(BF16), st_in[i])
        ag = blk(g_ref, c, h)
        on = _rms(oi, ng_ref[:, h * HGRN_DIM:(h + 1) * HGRN_DIM])
        o_ref[c * chunk:(c + 1) * chunk, h * HGRN_DIM:(h + 1) * HGRN_DIM] = (
            on * (ag * _sigmoid(ag))).astype(o_ref.dtype)


def _hgrn(proj, lb, ng, batch, seq, ts, chunk):
    m = proj.shape[0]
    w = HGRN_HEADS * HGRN_DIM
    nt = seq // ts
    spec = lambda j: pl.BlockSpec((ts, w), lambda b, t, j=j: (b * nt + t, j))
    vec = pl.BlockSpec((1, w), lambda b, t: (0, 0))
    return pl.pallas_call(
        functools.partial(_hgrn_kernel, chunk=chunk),
        grid=(batch, nt),
        in_specs=[spec(0), spec(1), spec(2), spec(3), vec, vec],
        out_specs=pl.BlockSpec((ts, w), lambda b, t: (b * nt + t, 0)),
        out_shape=jax.ShapeDtypeStruct((m, w), BF16),
        scratch_shapes=[pltpu.VMEM((HGRN_HEADS, HGRN_DIM, HGRN_DIM), F32)],
        compiler_params=_cparams(("parallel", "arbitrary"), VMEM_SMALL_MIB),
        name="hgrn2",
    )(proj, proj, proj, proj, lb, ng)


def _foxprep_kernel(q_ref, k_ref, v_ref, f_ref, fb_ref, qg_ref, kg_ref, hsum_ref, wq_ref, wk_ref,
                    oq_ref, ok_ref, qt_ref, ka_ref, vt_ref, carry_ref):
    @pl.when(pl.program_id(1) == 0)
    def _():
        carry_ref[...] = jnp.zeros_like(carry_ref)

    ts = q_ref.shape[0]
    pairs = FOX_HEADS // 2
    tril_b = jnp.where(_tril_mask(ts), 1.0, 0.0).astype(BF16)
    lf = _log_sigmoid(f_ref[...] + fb_ref[...])
    c = _cumsum_rows(lf, tril_b) + carry_ref[...]
    carry_ref[...] = c[ts - 1:ts, :]
    c2 = c * LOG2E
    hi = c2.astype(BF16).astype(F32)
    rest = c2 - hi
    mid = rest.astype(BF16).astype(F32)
    lane = lax.broadcasted_iota(jnp.int32, c.shape, 1)
    pieces = jnp.where(lane < FOX_HEADS, hi,
                       jnp.where(lane < 2 * FOX_HEADS, mid, rest - mid)).astype(BF16)

    def headnorm(x_ref, g_ref, r, scale):
        cols = slice(r * 2 * LANES, (r + 1) * 2 * LANES)
        x = x_ref[:, cols]
        xx = x * x
        xh = xx.astype(BF16)
        ss = _dot(xh, hsum_ref[...]) + _dot((xx - xh.astype(F32)).astype(BF16), hsum_ref[...])
        return (x * lax.rsqrt(ss * (1.0 / FOX_DIM) + RMS_EPS) * (g_ref[:, cols] * scale)).astype(BF16)

    qn = [headnorm(q_ref, qg_ref, r, FOX_DIM ** -0.5 * LOG2E) for r in range(pairs // 2)]
    kn = [headnorm(k_ref, kg_ref, r, 1.0) for r in range(pairs // 2)]
    for r in range(pairs):
        src = slice((r % 2) * LANES, (r % 2 + 1) * LANES)
        out = slice(r * 2 * LANES, (r + 1) * 2 * LANES)
        qa = _dot(jnp.concatenate([qn[r // 2][:, src], pieces], axis=1), wq_ref[r]) + oq_ref[:, out]
        ka = _dot(jnp.concatenate([kn[r // 2][:, src], pieces], axis=1), wk_ref[r]) + ok_ref[:, out]
        ka_ref[:, out] = ka.astype(BF16)
        for hh in range(2):
            qt_ref[0, 2 * r + hh] = qa[:, hh * LANES:(hh + 1) * LANES].T.astype(BF16)
        vt = v_ref[:, r * LANES:(r + 1) * LANES].T
        extra = jnp.where(lax.broadcasted_iota(jnp.int32, (FOX_VROWS - FOX_DIM, ts), 0) == 0, 1.0, 0.0)
        for hh in range(2):
            vt_ref[0, 2 * r + hh, 0] = jnp.concatenate(
                [vt[hh * FOX_DIM:(hh + 1) * FOX_DIM], extra], axis=0).astype(BF16)


def _foxprep(proj, fb, qg, kg, batch, seq, ts):
    m = proj.shape[0]
    w = FOX_HEADS * FOX_DIM
    nt = seq // ts
    pairs = FOX_HEADS // 2
    spec = lambda j: pl.BlockSpec((ts, w), lambda b, t, j=j: (b * nt + t, j))
    fcol = (8 * w) // LANES
    const = lambda a: pl.BlockSpec(a.shape, lambda b, t: (0,) * a.ndim)

    wa = FOX_HEADS * LANES
    ch = np.arange(2 * LANES)
    hsum = (ch[:, None] // FOX_DIM == ch[None, :] // FOX_DIM).astype(np.float32)
    wq = np.zeros((pairs, 2 * LANES, 2 * LANES), np.float32)
    wk = np.zeros((pairs, 2 * LANES, 2 * LANES), np.float32)
    oq = np.zeros((1, wa), np.float32)
    ok = np.zeros((1, wa), np.float32)
    for r in range(pairs):
        for hh in range(2):
            h = 2 * r + hh
            d = np.arange(FOX_DIM)
            wq[r, hh * FOX_DIM + d, hh * LANES + d] = 1.0
            wk[r, hh * FOX_DIM + d, hh * LANES + d] = 1.0
            for p in range(3):
                wq[r, LANES + p * FOX_HEADS + h, hh * LANES + FOX_DIM + p] = 1.0
                wk[r, LANES + p * FOX_HEADS + h, hh * LANES + FOX_DIM + 3 + p] = -1.0
                oq[0, h * LANES + FOX_DIM + 3 + p] = 1.0
                ok[0, h * LANES + FOX_DIM + p] = 1.0
    consts = [jnp.asarray(a, BF16) for a in (hsum, wq, wk)] + [jnp.asarray(oq), jnp.asarray(ok)]
    qg = jnp.tile(qg, (1, FOX_HEADS))
    kg = jnp.tile(kg, (1, FOX_HEADS))
    return pl.pallas_call(
        _foxprep_kernel,
        grid=(batch, nt),
        in_specs=[spec(4), spec(5), spec(6),
                  pl.BlockSpec((ts, LANES), lambda b, t: (b * nt + t, fcol)),
                  const(fb), const(qg), const(kg)] + [const(a) for a in consts],
        out_specs=[pl.BlockSpec((1, FOX_HEADS, LANES, ts), lambda b, t: (b, 0, 0, t)),
                   pl.BlockSpec((ts, FOX_HEADS * LANES), lambda b, t: (b * nt + t, 0)),
                   pl.BlockSpec((1, FOX_HEADS, 1, FOX_VROWS, ts), lambda b, t: (b, 0, t, 0, 0))],
        out_shape=[jax.ShapeDtypeStruct((batch, FOX_HEADS, LANES, seq), BF16),
                   jax.ShapeDtypeStruct((m, FOX_HEADS * LANES), BF16),
                   jax.ShapeDtypeStruct((batch, FOX_HEADS, nt, FOX_VROWS, ts), BF16)],
        scratch_shapes=[pltpu.VMEM((1, LANES), F32)],
        compiler_params=_cparams(("parallel", "arbitrary"), VMEM_SMALL_MIB),
        name="foxprep",
    )(proj, proj, proj, proj, fb, qg, kg, *consts)


def _fox_kernel(qt_ref, k_ref, vt_ref, g_ref, o_ref, *, tq, tk, heads):
    i = pl.program_id(2)
    hs = range(heads)
    ratio = tq // tk

    def step(j, carry, q0):
        ms, accs = carry
        diagonal = q0 is not None
        q0 = q0 or 0
        nq = tq - q0
        nkeys = tk if diagonal else tq
        rows = pl.ds(pl.multiple_of(j * tq, tq) + q0, nkeys)
        keys = slice(q0, q0 + nkeys)
        if diagonal:
            visible = (lax.broadcasted_iota(jnp.int32, (tk, nq), 1)
                       >= lax.broadcasted_iota(jnp.int32, (tk, nq), 0))

        def scores(h):
            s = _dot(k_ref[rows, h * LANES:(h + 1) * LANES], qt_ref[0, h, :, q0:])
            return jnp.where(visible, s, NEG_BIG) if diagonal else s

        def softmax(h, s):
            m_old = ms[h][:, q0:]
            m_new = jnp.maximum(m_old, jnp.max(s, axis=0, keepdims=True))
            return m_new, jnp.exp2(m_old - m_new), jnp.exp2(s - m_new).astype(BF16)

        def values(h, alpha, p):
            return accs[h][:, q0:] * alpha + _dot(vt_ref[0, h, j, :, keys], p)

        s = [scores(h) for h in hs]
        sm = [softmax(h, s[h]) for h in hs]
        out = [values(h, sm[h][1], sm[h][2]) for h in hs]
        keep = lambda old, new: new if q0 == 0 else jnp.concatenate([old[:, :q0], new], axis=1)
        return (tuple(keep(ms[h], sm[h][0]) for h in hs), tuple(keep(accs[h], out[h]) for h in hs))

    neg = jnp.full((1, tq), NEG_BIG, F32)
    carry = ((neg,) * heads, (jnp.zeros((FOX_VROWS, tq), F32),) * heads)
    carry = lax.fori_loop(0, i, lambda j, c: step(j, c, None), carry)
    for d in range(ratio):
        carry = step(i, carry, d * tk)
    _, accs = carry
    norm = [a[:FOX_DIM] / a[FOX_DIM:FOX_DIM + 1] for a in accs]
    for r in range(heads // 2):
        cols = slice(r * LANES, (r + 1) * LANES)
        out = jnp.concatenate([norm[2 * r], norm[2 * r + 1]], axis=0)
        o_ref[:, cols] = (out.T * _sigmoid(g_ref[:, cols])).astype(o_ref.dtype)


def _fox(qt, ka, vt, proj, batch, seq, tq, tk, heads):
    m = ka.shape[0]
    nq = seq // tq
    groups = FOX_HEADS // heads
    pairs = heads // 2
    wv = heads * FOX_DIM
    gcol = (7 * FOX_HEADS * FOX_DIM) // wv
    return pl.pallas_call(
        functools.partial(_fox_kernel, tq=tq, tk=tk, heads=heads),
        grid=(batch, groups, nq),
        in_specs=[pl.BlockSpec((1, heads, LANES, tq), lambda b, p, i: (b, p, 0, i)),
                  pl.BlockSpec((seq, heads * LANES), lambda b, p, i: (b, p)),
                  pl.BlockSpec((1, heads, nq, FOX_VROWS, tq), lambda b, p, i: (b, p, 0, 0, 0)),
                  pl.BlockSpec((tq, wv), lambda b, p, i: (b * nq + i, gcol + p))],
        out_specs=pl.BlockSpec((tq, wv), lambda b, p, i: (b * nq + i, p)),
        out_shape=jax.ShapeDtypeStruct((m, FOX_HEADS * FOX_DIM), BF16),
        compiler_params=_cparams(("parallel", "parallel", "arbitrary"), VMEM_MEDIUM_MIB),
        name="fox_attention",
    )(qt, ka, vt, proj)


def _mix_mlp_kernel(*refs, n_mix, ck):
    ys = refs[:n_mix]
    w_ref, h_ref, g_ref, wu_ref, wd_ref, o_ref = refs[n_mix:]
    y = ys[0][...] if n_mix == 1 else jnp.concatenate([r[...] for r in ys], axis=1)
    x = h_ref[...] + _dot(y, w_ref[...])
    hn = _rms(x, g_ref[...]).astype(BF16)
    acc = x
    for c in range(wu_ref.shape[1] // ck):
        u = jnp.maximum(_dot(hn, wu_ref[:, c * ck:(c + 1) * ck]), 0.0)
        acc = acc + _dot((u * u).astype(BF16), wd_ref[c * ck:(c + 1) * ck, :])
    o_ref[...] = acc


def _mix_mlp(ys, w, h, g, wu, wd, tm, ck, name):
    m, d = h.shape
    const = lambda a: pl.BlockSpec(a.shape, lambda i: (0, 0), pipeline_mode=pl.Buffered(1))
    return pl.pallas_call(
        functools.partial(_mix_mlp_kernel, n_mix=len(ys), ck=ck),
        grid=(m // tm,),
        in_specs=([pl.BlockSpec((tm, y.shape[1]), lambda i: (i, 0)) for y in ys]
                  + [const(w), pl.BlockSpec((tm, d), lambda i: (i, 0)), const(g), const(wu), const(wd)]),
        out_specs=pl.BlockSpec((tm, d), lambda i: (i, 0)),
        out_shape=jax.ShapeDtypeStruct((m, d), F32),
        compiler_params=_cparams(("parallel",), VMEM_LARGE_MIB),
        name=name,
    )(*ys, w, h, g, wu, wd)


def _rwkvproj_kernel(h_ref, hp_ref, g_ref, mu_ref, wr_ref, wk_ref, wv_ref, w1_ref, w2_ref,
                     a1_ref, a2_ref, g1_ref, g2_ref, w0_ref, a0_ref, kk_ref, ka_ref,
                     r_ref, lw_ref, km_ref, v_ref, kr_ref, a_ref, go_ref, *, tiles_per_seq):
    i = pl.program_id(0)
    tm = h_ref.shape[0]
    gn = g_ref[...]
    hn = _rms(h_ref[...], gn)
    prev = _rms(hp_ref[7:8, :], gn)
    prev = jnp.where(i % tiles_per_seq == 0, jnp.zeros_like(prev), prev)
    row = lax.broadcasted_iota(jnp.int32, hn.shape, 0)
    shifted = jnp.where(row == 0, jnp.broadcast_to(prev, hn.shape), pltpu.roll(hn, 1, 0))
    xx = shifted - hn
    hn_b = hn.astype(BF16)
    xx_b = xx.astype(BF16)
    mix = lambda j: hn_b + xx_b * mu_ref[j:j + 1, :].astype(BF16)
    r = _dot(mix(0), wr_ref[...])
    k = _dot(mix(2), wk_ref[...])
    v = _dot(mix(3), wv_ref[...])
    z = w0_ref[...] + _dot(jnp.tanh(_dot(mix(1), w1_ref[...])).astype(BF16), w2_ref[...])
    a = _sigmoid(a0_ref[...] + _dot(_dot(mix(4), a1_ref[...]).astype(BF16), a2_ref[...]))
    g = _dot(_sigmoid(_dot(mix(5), g1_ref[...])).astype(BF16), g2_ref[...])
    r_ref[...] = r
    lw_ref[...] = _sigmoid(z) * (-math.exp(-0.5))
    km_ref[...] = k * (1.0 + (a - 1.0) * ka_ref[...])
    v_ref[...] = v
    kr_ref[...] = k * kk_ref[...]
    a_ref[...] = a
    go_ref[...] = g


def _rwkvproj(h, g, mu, wr, wk, wv, w1, w2, a1, a2, g1, g2, w0, a0, k_k, k_a, seq, tm):
    m, d = h.shape
    tiles_per_seq = seq // tm
    full = lambda a: pl.BlockSpec(a.shape, lambda i: (0,) * a.ndim, pipeline_mode=pl.Buffered(1))
    row = pl.BlockSpec((tm, d), lambda i: (i, 0))
    prev = pl.BlockSpec((8, d), lambda i: (jnp.maximum(i * (tm // 8) - 1, 0), 0))
    consts = (g, mu, wr, wk, wv, w1, w2, a1, a2, g1, g2, w0, a0, k_k, k_a)
    return pl.pallas_call(
        functools.partial(_rwkvproj_kernel, tiles_per_seq=tiles_per_seq),
        grid=(m // tm,),
        in_specs=[row, prev] + [full(a) for a in consts],
        out_specs=[row] * 7,
        out_shape=[jax.ShapeDtypeStruct((m, d), F32)] * 7,
        compiler_params=_cparams(("parallel",), VMEM_LARGE_MIB),
        name="rwkv_proj",
    )(h, h, *consts)


def _wkv_kernel(r_ref, lw_ref, km_ref, v_ref, kr_ref, a_ref, g_ref, rk_ref, lg_ref, lb_ref,
                o_ref, st_ref, y_ref, q1_ref, lhs_ref, z0_ref, wc_ref, bonus_ref, gate_ref,
                *, chunk, groups):
    grp = pl.program_id(2)

    @pl.when(pl.program_id(1) == 0)
    def _():
        st_ref[grp] = jnp.zeros(st_ref.shape[1:], F32)

    ts = r_ref.shape[0]
    nch = ts // chunk
    gl = GROUP_LANES
    rb = lax.broadcasted_iota(jnp.int32, (gl, gl), 0) // RWKV_DIM
    cb = lax.broadcasted_iota(jnp.int32, (gl, gl), 1) // RWKV_DIM
    blockmask = rb == cb
    ones_bd = jnp.where(blockmask, 1.0, 0.0).astype(BF16)

    def headsum(x, pieces=2):
        hi = x.astype(BF16)
        out = _dot(hi, ones_bd)
        if pieces == 2:
            out = out + _dot((x - hi.astype(F32)).astype(BF16), ones_bd)
        return out

    def bd(y):
        reps = gl // y.shape[0]
        return jnp.where(blockmask, jnp.concatenate([y] * reps, axis=0), 0.0).astype(BF16)

    def hmm(x, y):
        return _dot(x.astype(BF16), bd(y))

    def tn_blocks(x, y):
        return jnp.where(blockmask, _dot(x.T.astype(BF16), y.astype(BF16)), 0.0)

    t_idx = lax.broadcasted_iota(jnp.int32, (chunk, gl), 0)
    s_idx = lax.broadcasted_iota(jnp.int32, (chunk, gl), 1) % RWKV_DIM
    strict = s_idx < t_idx
    incl = s_idx <= t_idx
    tril_b = jnp.where(_tril_mask(chunk), 1.0, 0.0).astype(BF16)
    zeros_c = jnp.zeros((chunk, gl), F32)

    chunks = range(nch)
    blk = lambda ref, c: ref[c * chunk:(c + 1) * chunk, :]
    pad = lambda x: jnp.concatenate([x, zeros_c], axis=0)
    each = lambda fn, *lists: [fn(*args) for args in zip(*lists)]

    kr = kr_ref[...]
    kkn_all = kr * lax.rsqrt(jnp.maximum(headsum(kr * kr), 1e-24))

    lw = [blk(lw_ref, c) for c in chunks]
    cum = each(lambda x: _cumsum_rows(x, tril_b, pieces=2), lw)
    c_last = [x[chunk - 1:chunk, :] for x in cum]
    kkn = [kkn_all[c * chunk:(c + 1) * chunk, :] for c in chunks]
    kka = [kkn[c] * blk(a_ref, c) for c in chunks]
    km = [blk(km_ref, c) for c in chunks]
    v = [blk(v_ref, c) for c in chunks]
    e_neg = [jnp.exp(-x) for x in cum]
    e_end = each(lambda cl, x: jnp.exp(cl - x), c_last, cum)
    at = each(lambda k, x, l: -k * jnp.exp(x - l), kkn, cum, lw)
    bt = each(jnp.multiply, kka, e_neg)
    kt = each(jnp.multiply, km, e_neg)
    rt = [blk(r_ref, c) * jnp.exp(cum[c]) for c in chunks]
    bw = each(jnp.multiply, kka, e_end)
    kw = each(jnp.multiply, km, e_end)
    first_head = lax.broadcasted_iota(jnp.int32, (RWKV_DIM, LANES), 1) < RWKV_DIM
    for c in chunks:
        wt = jnp.broadcast_to(jnp.exp(c_last[c]), (LANES, gl)).T
        wc_ref[grp, c] = jnp.concatenate(
            [jnp.where(first_head, wt[2 * p * RWKV_DIM:(2 * p + 1) * RWKV_DIM],
                       wt[(2 * p + 1) * RWKV_DIM:(2 * p + 2) * RWKV_DIM])
             for p in range(RWKV_GROUP // 2)], axis=1)

    lhs = each(lambda a, r: jnp.concatenate([a, r], axis=0).astype(BF16), at, rt)
    pb = each(lambda l, b: _dot_nt(l, bd(b)), lhs, bt)
    pk = each(lambda l, k: _dot_nt(l, bd(k)), lhs, kt)
    a_ab = [jnp.where(strict, x[:chunk], 0.0) for x in pb]
    a_rb = [jnp.where(incl, x[chunk:], 0.0) for x in pb]
    a_ak = [jnp.where(strict, x[:chunk], 0.0) for x in pk]
    a_rk = [jnp.where(incl, x[chunk:], 0.0) for x in pk]

    e = [jnp.where((t_idx % 2 == 1) & (s_idx == t_idx - 1), x, 0.0) for x in a_ab]
    size = 2
    while size < chunk:
        off = ((t_idx // size) % 2 == 1) & (s_idx // size == t_idx // size - 1)
        a_off = [jnp.where(off, x, 0.0) for x in a_ab]
        t1 = each(lambda ao, ee: ao + hmm(ao, ee), a_off, e)
        e = each(lambda ee, tt: ee + tt + hmm(ee, tt), e, t1)
        size *= 2

    av = each(lambda ak, rk, vv: hmm(jnp.concatenate([ak, rk], axis=0), vv), a_ak, a_rk, v)
    akv = [x[:chunk] for x in av]
    p1 = each(lambda x, ee: x + hmm(ee, x), akv, e)
    mat = each(lambda x, ee: x + hmm(ee, x), at, e)
    q1 = each(lambda x, arb, pp: x[chunk:] + hmm(arb, pp), av, a_rb, p1)
    r2 = each(lambda r, arb, mm: r + hmm(arb, mm), rt, a_rb, mat)
    pct = each(lambda b, mm: tn_blocks(pad(b), pad(mm)), bw, mat)
    z0 = each(lambda b, k, pp, vv: tn_blocks(jnp.concatenate([b, k], axis=0),
                                             jnp.concatenate([pp, vv], axis=0)), bw, kw, p1, v)
    def fold(x):
        out = x[:RWKV_DIM]
        for h in range(1, RWKV_GROUP):
            out = out + x[h * RWKV_DIM:(h + 1) * RWKV_DIM]
        return out

    for c in chunks:
        q1_ref[grp, c * chunk:(c + 1) * chunk, :] = q1[c]
        lhs_ref[grp, c, :chunk, :] = r2[c].astype(BF16)
        lhs_ref[grp, c, chunk:, :] = fold(pct[c]).astype(BF16)
        z0_ref[grp, c] = fold(z0[c])
    bonus_ref[grp] = headsum(r_ref[...] * km_ref[...] * rk_ref[...], pieces=1) * v_ref[...]
    gate_ref[grp] = g_ref[...]

    @pl.when(grp == groups - 1)
    def _():
        gs = range(groups)

        def body(c, carry):
            rows = pl.ds(pl.multiple_of(c * chunk, chunk), chunk)
            st = [st_ref[j] for j in gs]
            res = [_dot(lhs_ref[j, c], bd(st[j])) for j in gs]
            for j in gs:
                y_ref[j, rows, :] = q1_ref[j, rows, :] + res[j][:chunk]
                st_ref[j] = st[j] * wc_ref[j, c] + res[j][chunk:] + z0_ref[j, c]
            return carry

        lax.fori_loop(0, nch, body, 0)

        inv_n = 1.0 / RWKV_DIM
        for j in gs:
            cols = slice(j * gl, (j + 1) * gl)
            y = y_ref[j]
            mean = headsum(y, pieces=1) * inv_n
            dlt = y - mean
            var = headsum(dlt * dlt, pieces=1) * inv_n
            yn = dlt * lax.rsqrt(var + GN_EPS) * lg_ref[:, cols] + lb_ref[:, cols]
            o_ref[:, cols] = ((yn + bonus_ref[j]) * gate_ref[j]).astype(o_ref.dtype)


def _wkv(r, lw, km, v, kr, a, g, r_k, lnx_g, lnx_b, batch, seq, ts, chunk):
    m, d = r.shape
    gl = GROUP_LANES
    groups = d // gl
    nt = seq // ts
    nch = ts // chunk
    row = pl.BlockSpec((ts, gl), lambda b, t, j: (b * nt + t, j))
    vec = pl.BlockSpec((1, gl), lambda b, t, j: (0, j))
    full = pl.BlockSpec((1, d), lambda b, t, j: (0, 0))
    return pl.pallas_call(
        functools.partial(_wkv_kernel, chunk=chunk, groups=groups),
        grid=(batch, nt, groups),
        in_specs=[row] * 7 + [vec, full, full],
        out_specs=pl.BlockSpec((ts, d), lambda b, t, j: (b * nt + t, 0)),
        out_shape=jax.ShapeDtypeStruct((m, d), BF16),
        scratch_shapes=[pltpu.VMEM((groups, RWKV_DIM, gl), F32),
                        pltpu.VMEM((groups, ts, gl), F32),
                        pltpu.VMEM((groups, ts, gl), F32),
                        pltpu.VMEM((groups, nch, chunk + RWKV_DIM, gl), BF16),
                        pltpu.VMEM((groups, nch, RWKV_DIM, gl), F32),
                        pltpu.VMEM((groups, nch, RWKV_DIM, gl), F32),
                        pltpu.VMEM((groups, ts, gl), F32),
                        pltpu.VMEM((groups, ts, gl), F32)],
        compiler_params=_cparams(("parallel", "arbitrary", "arbitrary"), VMEM_LARGE_MIB),
        name="wkv7",
    )(r, lw, km, v, kr, a, g, r_k, lnx_g, lnx_b)


def kernel(x, norm_mix_g, norm_ffn_g, ab_w_in, hgrn_lower_bounds, hgrn_norm_g, fox_forget_bias,
           fox_q_norm_g, fox_k_norm_g, ab_w_out, rwkv_mu, rwkv_w_rkv, rwkv_w0, rwkv_w1, rwkv_w2,
           rwkv_a0, rwkv_a1, rwkv_a2, rwkv_g1, rwkv_g2, rwkv_k_k, rwkv_k_a, rwkv_r_k,
           rwkv_lnx_g, rwkv_lnx_b, rwkv_w_o, mlp_w_up, mlp_w_down):
    batch, seq, d = x.shape
    m = batch * seq
    t = _tiles(seq)
    row = lambda a: a.reshape(1, -1).astype(F32)
    bf = lambda a: a.astype(BF16)

    lb_all = jnp.cumsum(jax.nn.softmax(hgrn_lower_bounds.astype(F32), axis=0), axis=0)
    h = x.reshape(m, d)

    n_wide = ab_w_in.shape[-1] - FOX_HEADS
    gate_w = jnp.tile(ab_w_in[0][:, n_wide:], (1, 3))
    w_in = bf(jnp.pad(jnp.concatenate([ab_w_in[0][:, :n_wide], gate_w], axis=1),
                      ((0, 0), (0, LANES - 3 * FOX_HEADS))))
    proj = _inproj(h, row(norm_mix_g[0]), w_in, t.rows)
    ya = _hgrn(proj, row(lb_all[0]), row(hgrn_norm_g[0]), batch, seq, t.time, t.chunk)
    fb = jnp.pad(jnp.tile(row(fox_forget_bias[0]), (1, 3)), ((0, 0), (0, LANES - 3 * FOX_HEADS)))
    qt, ka, vt = _foxprep(proj, fb, row(fox_q_norm_g[0]), row(fox_k_norm_g[0]), batch, seq, t.time)
    yb = _fox(qt, ka, vt, proj, batch, seq, t.time, t.fox_keys, FOX_HEADS)
    h = _mix_mlp([ya, yb], bf(ab_w_out[0]), h, row(norm_ffn_g[0]), bf(mlp_w_up[0]), bf(mlp_w_down[0]),
                 t.mlp_rows, t.ff_chunk, "mix_mlp0")

    outs = _rwkvproj(h, row(norm_mix_g[1]), rwkv_mu[0].astype(F32),
                     bf(rwkv_w_rkv[0, 0]), bf(rwkv_w_rkv[0, 1]), bf(rwkv_w_rkv[0, 2]),
                     bf(rwkv_w1[0]), bf(rwkv_w2[0]), bf(rwkv_a1[0]), bf(rwkv_a2[0]),
                     bf(rwkv_g1[0]), bf(rwkv_g2[0]), row(rwkv_w0[0]), row(rwkv_a0[0]),
                     row(rwkv_k_k[0]), row(rwkv_k_a[0]), seq, t.rows)
    z = _wkv(*outs, row(rwkv_r_k[0]), row(rwkv_lnx_g[0]), row(rwkv_lnx_b[0]),
             batch, seq, t.time, t.chunk)
    h = _mix_mlp([z], bf(rwkv_w_o[0]), h, row(norm_ffn_g[1]), bf(mlp_w_up[1]), bf(mlp_w_down[1]),
                 t.mlp_rows, t.ff_chunk, "mix_mlp1")
    return h.reshape(batch, seq, d)
```

```python
import functools
import math
from typing import NamedTuple

import jax
import jax.numpy as jnp
import numpy as np
from jax import lax
from jax.experimental import pallas as pl
from jax.experimental.pallas import tpu as pltpu

F32 = jnp.float32
BF16 = jnp.bfloat16

RMS_EPS = 1e-6
GN_EPS = 64e-5

HGRN_HEADS = 4
HGRN_DIM = 128
HGRN_SUB = 16
FOX_HEADS = 8
FOX_DIM = 64
FOX_VROWS = FOX_DIM + 16
RWKV_DIM = 64
RWKV_GROUP = 4
GROUP_LANES = RWKV_GROUP * RWKV_DIM
WKV_STAGGER = 1
LANES = 128
VMEM_V7X_MIB = 64
VMEM_SMALL_MIB = VMEM_V7X_MIB // 2
VMEM_MEDIUM_MIB = 3 * VMEM_V7X_MIB // 4
VMEM_LARGE_MIB = 7 * VMEM_V7X_MIB // 8
NEG_BIG = -1e30
LOG2E = 1.4426950408889634

NT_DIMS = (((1,), (1,)), ((), ()))


class _Tiles(NamedTuple):
    rows: int
    mlp_rows: int
    time: int
    fox_keys: int
    chunk: int
    ff_chunk: int


def _tiles(seq):
    return _Tiles(rows=min(512, seq), mlp_rows=min(512, seq), time=min(512, seq),
                  fox_keys=min(256, seq), chunk=64, ff_chunk=1024)


def _cparams(sem, vmem_mb):
    return pltpu.CompilerParams(dimension_semantics=sem, vmem_limit_bytes=vmem_mb * 1024 * 1024)


def _dot(a, b):
    return jnp.dot(a, b, preferred_element_type=F32)


def _dot_nt(a, b):
    return lax.dot_general(a, b, NT_DIMS, preferred_element_type=F32)


def _rms(x, g):
    return x * lax.rsqrt(jnp.mean(x * x, axis=-1, keepdims=True) + RMS_EPS) * g


def _sigmoid(x):
    return 1.0 / (1.0 + jnp.exp(-x))


def _log_sigmoid(x):
    return jnp.minimum(x, 0.0) - jnp.log(1.0 + jnp.exp(-jnp.abs(x)))


def _tril_mask(n, strict=False):
    r = lax.broadcasted_iota(jnp.int32, (n, n), 0)
    c = lax.broadcasted_iota(jnp.int32, (n, n), 1)
    return (c < r) if strict else (c <= r)


def _split3(x):
    hi = x.astype(BF16)
    r1 = x - hi.astype(F32)
    mid = r1.astype(BF16)
    lo = (r1 - mid.astype(F32)).astype(BF16)
    return hi, mid, lo


def _cumsum_rows(x, tril_bf16, pieces=3):
    parts = _split3(x)[:pieces]
    out = _dot(tril_bf16, parts[0])
    for part in parts[1:]:
        out = out + _dot(tril_bf16, part)
    return out


def _inproj_kernel(x_ref, g_ref, w_ref, o_ref):
    hn = _rms(x_ref[...], g_ref[...]).astype(BF16)
    o_ref[...] = _dot(hn, w_ref[...])


def _inproj(x2, g, w, tm):
    m, d = x2.shape
    n = w.shape[1]
    return pl.pallas_call(
        _inproj_kernel,
        grid=(m // tm,),
        in_specs=[
            pl.BlockSpec((tm, d), lambda i: (i, 0)),
            pl.BlockSpec((1, d), lambda i: (0, 0)),
            pl.BlockSpec((d, n), lambda i: (0, 0), pipeline_mode=pl.Buffered(1)),
        ],
        out_specs=pl.BlockSpec((tm, n), lambda i: (i, 0)),
        out_shape=jax.ShapeDtypeStruct((m, n), F32),
        compiler_params=_cparams(("parallel",), VMEM_MEDIUM_MIB),
        name="inproj",
    )(x2, g, w)


def _hgrn_kernel(q_ref, f_ref, i_ref, g_ref, lb_ref, ng_ref, o_ref, st_ref, *, chunk):
    @pl.when(pl.program_id(1) == 0)
    def _():
        st_ref[...] = jnp.zeros_like(st_ref)

    ts = q_ref.shape[0]
    tril_b = jnp.where(_tril_mask(chunk), 1.0, 0.0).astype(BF16)

    hs = range(HGRN_HEADS)
    nch = ts // chunk
    tiles = [(c, h) for c in range(nch) for h in hs]
    blk = lambda ref, c, h: ref[c * chunk:(c + 1) * chunk, h * HGRN_DIM:(h + 1) * HGRN_DIM]
    lbs = [lb_ref[:, h * HGRN_DIM:(h + 1) * HGRN_DIM] for h in hs]

    f = [lbs[h] + (1.0 - lbs[h]) * _sigmoid(blk(f_ref, c, h)) for c, h in tiles]
    b = [_cumsum_rows(jnp.log(x), tril_b) for x in f]
    b_last = [x[chunk - 1:chunk, :] for x in b]
    q = [blk(q_ref, c, h) * _sigmoid(blk(q_ref, c, h)) for c, h in tiles]
    k = [1.0 - x for x in f]
    vb = [blk(i_ref, c, h).astype(BF16) for c, h in tiles]
    n = range(len(tiles))

    sub = HGRN_SUB
    tril_sub = _tril_mask(sub)
    score_cols = [[] for _ in n]
    for j in range(chunk // sub):
        lo, hi = j * sub, (j + 1) * sub
        for i in n:
            bj = b[i][lo:hi]
            b_mid = b[i][lo + sub // 2 - 1:lo + sub // 2]
            diag = _dot_nt((q[i][lo:hi] * jnp.exp(bj - b_mid)).astype(BF16),
                           (k[i][lo:hi] * jnp.exp(b_mid - bj)).astype(BF16))
            parts = [jnp.where(tril_sub, diag, 0.0)]
            if lo > 0:
                parts.insert(0, jnp.zeros((lo, sub), F32))
            if hi < chunk:
                b_end = b[i][hi - 1:hi]
                parts.append(_dot_nt((q[i][hi:] * jnp.exp(b[i][hi:] - b_end)).astype(BF16),
                                     (k[i][lo:hi] * jnp.exp(b_end - bj)).astype(BF16)))
            score_cols[i].append(jnp.concatenate(parts, axis=0).astype(BF16))
    o = []
    for i in n:
        acc = _dot(score_cols[i][0], vb[i][:sub])
        for j in range(1, chunk // sub):
            acc = acc + _dot(score_cols[i][j], vb[i][j * sub:(j + 1) * sub])
        o.append(acc)
    inc = [_dot(blk(i_ref, c, h).T.astype(BF16), (k[i] * jnp.exp(b_last[i] - b[i])).astype(BF16))
           for i, (c, h) in enumerate(tiles)]
    dec = [jnp.exp(x) for x in b_last]

    st = [st_ref[h] for h in hs]
    st_in = []
    for i, (c, h) in enumerate(tiles):
        st_in.append(st[h].astype(BF16))
        st[h] = st[h] * dec[i] + inc[i]
    for h in hs:
        st_ref[h] = st[h]

    for i, (c, h) in enumerate(tiles):
        oi = o[i] + _dot_nt((q[i] * jnp.exp(b[i])).astype(BF16), st_in[i])
        ag = blk(g_ref, c, h)
        on = _rms(oi, ng_ref[:, h * HGRN_DIM:(h + 1) * HGRN_DIM])
        o_ref[c * chunk:(c + 1) * chunk, h * HGRN_DIM:(h + 1) * HGRN_DIM] = (
            on * (ag * _sigmoid(ag))).astype(o_ref.dtype)


def _hgrn(proj, lb, ng, batch, seq, ts, chunk):
    m = proj.shape[0]
    w = HGRN_HEADS * HGRN_DIM
    nt = seq // ts
    spec = lambda j: pl.BlockSpec((ts, w), lambda b, t, j=j: (b * nt + t, j))
    vec = pl.BlockSpec((1, w), lambda b, t: (0, 0))
    return pl.pallas_call(
        functools.partial(_hgrn_kernel, chunk=chunk),
        grid=(batch, nt),
        in_specs=[spec(0), spec(1), spec(2), spec(3), vec, vec],
        out_specs=pl.BlockSpec((ts, w), lambda b, t: (b * nt + t, 0)),
        out_shape=jax.ShapeDtypeStruct((m, w), BF16),
        scratch_shapes=[pltpu.VMEM((HGRN_HEADS, HGRN_DIM, HGRN_DIM), F32)],
        compiler_params=_cparams(("parallel", "arbitrary"), VMEM_SMALL_MIB),
        name="hgrn2",
    )(proj, proj, proj, proj, lb, ng)


def _foxprep_kernel(q_ref, k_ref, v_ref, f_ref, fb_ref, qg_ref, kg_ref, hsum_ref, wq_ref, wk_ref,
                    oq_ref, ok_ref, qt_ref, ka_ref, vt_ref, carry_ref):
    @pl.when(pl.program_id(1) == 0)
    def _():
        carry_ref[...] = jnp.zeros_like(carry_ref)

    ts = q_ref.shape[0]
    pairs = FOX_HEADS // 2
    tril_b = jnp.where(_tril_mask(ts), 1.0, 0.0).astype(BF16)
    lf = _log_sigmoid(f_ref[...] + fb_ref[...])
    c = _cumsum_rows(lf, tril_b) + carry_ref[...]
    carry_ref[...] = c[ts - 1:ts, :]
    c2 = c * LOG2E
    hi = c2.astype(BF16).astype(F32)
    rest = c2 - hi
    mid = rest.astype(BF16).astype(F32)
    lane = lax.broadcasted_iota(jnp.int32, c.shape, 1)
    pieces = jnp.where(lane < FOX_HEADS, hi,
                       jnp.where(lane < 2 * FOX_HEADS, mid, rest - mid)).astype(BF16)

    def headnorm(x_ref, g_ref, r, scale):
        cols = slice(r * 2 * LANES, (r + 1) * 2 * LANES)
        x = x_ref[:, cols]
        xx = x * x
        xh = xx.astype(BF16)
        ss = _dot(xh, hsum_ref[...]) + _dot((xx - xh.astype(F32)).astype(BF16), hsum_ref[...])
        return (x * lax.rsqrt(ss * (1.0 / FOX_DIM) + RMS_EPS) * (g_ref[:, cols] * scale)).astype(BF16)

    qn = [headnorm(q_ref, qg_ref, r, FOX_DIM ** -0.5 * LOG2E) for r in range(pairs // 2)]
    kn = [headnorm(k_ref, kg_ref, r, 1.0) for r in range(pairs // 2)]
    for r in range(pairs):
        src = slice((r % 2) * LANES, (r % 2 + 1) * LANES)
        out = slice(r * 2 * LANES, (r + 1) * 2 * LANES)
        qa = _dot(jnp.concatenate([qn[r // 2][:, src], pieces], axis=1), wq_ref[r]) + oq_ref[:, out]
        ka = _dot(jnp.concatenate([kn[r // 2][:, src], pieces], axis=1), wk_ref[r]) + ok_ref[:, out]
        ka_ref[:, out] = ka.astype(BF16)
        for hh in range(2):
            qt_ref[0, 2 * r + hh] = qa[:, hh * LANES:(hh + 1) * LANES].T.astype(BF16)
        vt = v_ref[:, r * LANES:(r + 1) * LANES].T
        extra = jnp.where(lax.broadcasted_iota(jnp.int32, (FOX_VROWS - FOX_DIM, ts), 0) == 0, 1.0, 0.0)
        for hh in range(2):
            vt_ref[0, 2 * r + hh, 0] = jnp.concatenate(
                [vt[hh * FOX_DIM:(hh + 1) * FOX_DIM], extra], axis=0).astype(BF16)


def _foxprep(proj, fb, qg, kg, batch, seq, ts):
    m = proj.shape[0]
    w = FOX_HEADS * FOX_DIM
    nt = seq // ts
    pairs = FOX_HEADS // 2
    spec = lambda j: pl.BlockSpec((ts, w), lambda b, t, j=j: (b * nt + t, j))
    fcol = (8 * w) // LANES
    const = lambda a: pl.BlockSpec(a.shape, lambda b, t: (0,) * a.ndim)

    wa = FOX_HEADS * LANES
    ch = np.arange(2 * LANES)
    hsum = (ch[:, None] // FOX_DIM == ch[None, :] // FOX_DIM).astype(np.float32)
    wq = np.zeros((pairs, 2 * LANES, 2 * LANES), np.float32)
    wk = np.zeros((pairs, 2 * LANES, 2 * LANES), np.float32)
    oq = np.zeros((1, wa), np.float32)
    ok = np.zeros((1, wa), np.float32)
    for r in range(pairs):
        for hh in range(2):
            h = 2 * r + hh
            d = np.arange(FOX_DIM)
            wq[r, hh * FOX_DIM + d, hh * LANES + d] = 1.0
            wk[r, hh * FOX_DIM + d, hh * LANES + d] = 1.0
            for p in range(3):
                wq[r, LANES + p * FOX_HEADS + h, hh * LANES + FOX_DIM + p] = 1.0
                wk[r, LANES + p * FOX_HEADS + h, hh * LANES + FOX_DIM + 3 + p] = -1.0
                oq[0, h * LANES + FOX_DIM + 3 + p] = 1.0
                ok[0, h * LANES + FOX_DIM + p] = 1.0
    consts = [jnp.asarray(a, BF16) for a in (hsum, wq, wk)] + [jnp.asarray(oq), jnp.asarray(ok)]
    qg = jnp.tile(qg, (1, FOX_HEADS))
    kg = jnp.tile(kg, (1, FOX_HEADS))
    return pl.pallas_call(
        _foxprep_kernel,
        grid=(batch, nt),
        in_specs=[spec(4), spec(5), spec(6),
                  pl.BlockSpec((ts, LANES), lambda b, t: (b * nt + t, fcol)),
                  const(fb), const(qg), const(kg)] + [const(a) for a in consts],
        out_specs=[pl.BlockSpec((1, FOX_HEADS, LANES, ts), lambda b, t: (b, 0, 0, t)),
                   pl.BlockSpec((ts, FOX_HEADS * LANES), lambda b, t: (b * nt + t, 0)),
                   pl.BlockSpec((1, FOX_HEADS, 1, FOX_VROWS, ts), lambda b, t: (b, 0, t, 0, 0))],
        out_shape=[jax.ShapeDtypeStruct((batch, FOX_HEADS, LANES, seq), BF16),
                   jax.ShapeDtypeStruct((m, FOX_HEADS * LANES), BF16),
                   jax.ShapeDtypeStruct((batch, FOX_HEADS, nt, FOX_VROWS, ts), BF16)],
        scratch_shapes=[pltpu.VMEM((1, LANES), F32)],
        compiler_params=_cparams(("parallel", "arbitrary"), VMEM_SMALL_MIB),
        name="foxprep",
    )(proj, proj, proj, proj, fb, qg, kg, *consts)


def _fox_kernel(qt_ref, k_ref, vt_ref, g_ref, o_ref, *, tq, tk, heads):
    i = pl.program_id(2)
    hs = range(heads)
    ratio = tq // tk

    def step(j, carry, q0):
        ms, accs = carry
        diagonal = q0 is not None
        q0 = q0 or 0
        nq = tq - q0
        nkeys = tk if diagonal else tq
        rows = pl.ds(pl.multiple_of(j * tq, tq) + q0, nkeys)
        keys = slice(q0, q0 + nkeys)
        if diagonal:
            visible = (lax.broadcasted_iota(jnp.int32, (tk, nq), 1)
                       >= lax.broadcasted_iota(jnp.int32, (tk, nq), 0))

        def scores(h):
            s = _dot(k_ref[rows, h * LANES:(h + 1) * LANES], qt_ref[0, h, :, q0:])
            return jnp.where(visible, s, NEG_BIG) if diagonal else s

        def softmax(h, s):
            m_old = ms[h][:, q0:]
            m_new = jnp.maximum(m_old, jnp.max(s, axis=0, keepdims=True))
            return m_new, jnp.exp2(m_old - m_new), jnp.exp2(s - m_new).astype(BF16)

        def values(h, alpha, p):
            return accs[h][:, q0:] * alpha + _dot(vt_ref[0, h, j, :, keys], p)

        s, sm, out = {}, {}, {}
        for t in range(heads + 2):
            if t < heads:
                s[t] = scores(t)
            if 0 <= t - 1 < heads:
                sm[t - 1] = softmax(t - 1, s[t - 1])
            if 0 <= t - 2 < heads:
                out[t - 2] = values(t - 2, sm[t - 2][1], sm[t - 2][2])
        keep = lambda old, new: new if q0 == 0 else jnp.concatenate([old[:, :q0], new], axis=1)
        return (tuple(keep(ms[h], sm[h][0]) for h in hs), tuple(keep(accs[h], out[h]) for h in hs))

    neg = jnp.full((1, tq), NEG_BIG, F32)
    carry = ((neg,) * heads, (jnp.zeros((FOX_VROWS, tq), F32),) * heads)
    carry = lax.fori_loop(0, i, lambda j, c: step(j, c, None), carry)
    for d in range(ratio):
        carry = step(i, carry, d * tk)
    _, accs = carry
    norm = [a[:FOX_DIM] / a[FOX_DIM:FOX_DIM + 1] for a in accs]
    for r in range(heads // 2):
        cols = slice(r * LANES, (r + 1) * LANES)
        out = jnp.concatenate([norm[2 * r], norm[2 * r + 1]], axis=0)
        o_ref[:, cols] = (out.T * _sigmoid(g_ref[:, cols])).astype(o_ref.dtype)


def _fox(qt, ka, vt, proj, batch, seq, tq, tk, heads):
    m = ka.shape[0]
    nq = seq // tq
    groups = FOX_HEADS // heads
    pairs = heads // 2
    wv = heads * FOX_DIM
    gcol = (7 * FOX_HEADS * FOX_DIM) // wv
    return pl.pallas_call(
        functools.partial(_fox_kernel, tq=tq, tk=tk, heads=heads),
        grid=(batch, groups, nq),
        in_specs=[pl.BlockSpec((1, heads, LANES, tq), lambda b, p, i: (b, p, 0, i)),
                  pl.BlockSpec((seq, heads * LANES), lambda b, p, i: (b, p)),
                  pl.BlockSpec((1, heads, nq, FOX_VROWS, tq), lambda b, p, i: (b, p, 0, 0, 0)),
                  pl.BlockSpec((tq, wv), lambda b, p, i: (b * nq + i, gcol + p))],
        out_specs=pl.BlockSpec((tq, wv), lambda b, p, i: (b * nq + i, p)),
        out_shape=jax.ShapeDtypeStruct((m, FOX_HEADS * FOX_DIM), BF16),
        compiler_params=_cparams(("parallel", "parallel", "arbitrary"), VMEM_MEDIUM_MIB),
        name="fox_attention",
    )(qt, ka, vt, proj)


def _mix_mlp_kernel(*refs, n_mix, ck):
    ys = refs[:n_mix]
    w_ref, h_ref, g_ref, wu_ref, wd_ref, o_ref = refs[n_mix:]
    y = ys[0][...] if n_mix == 1 else jnp.concatenate([r[...] for r in ys], axis=1)
    x = h_ref[...] + _dot(y, w_ref[...])
    hn = _rms(x, g_ref[...]).astype(BF16)
    acc = x
    for c in range(wu_ref.shape[1] // ck):
        u = jnp.maximum(_dot(hn, wu_ref[:, c * ck:(c + 1) * ck]), 0.0)
        acc = acc + _dot((u * u).astype(BF16), wd_ref[c * ck:(c + 1) * ck, :])
    o_ref[...] = acc


def _mix_mlp(ys, w, h, g, wu, wd, tm, ck, name):
    m, d = h.shape
    const = lambda a: pl.BlockSpec(a.shape, lambda i: (0, 0), pipeline_mode=pl.Buffered(1))
    return pl.pallas_call(
        functools.partial(_mix_mlp_kernel, n_mix=len(ys), ck=ck),
        grid=(m // tm,),
        in_specs=([pl.BlockSpec((tm, y.shape[1]), lambda i: (i, 0)) for y in ys]
                  + [const(w), pl.BlockSpec((tm, d), lambda i: (i, 0)), const(g), const(wu), const(wd)]),
        out_specs=pl.BlockSpec((tm, d), lambda i: (i, 0)),
        out_shape=jax.ShapeDtypeStruct((m, d), F32),
        compiler_params=_cparams(("parallel",), VMEM_LARGE_MIB),
        name=name,
    )(*ys, w, h, g, wu, wd)


def _rwkvproj_kernel(h_ref, hp_ref, g_ref, mu_ref, wr_ref, wk_ref, wv_ref, w1_ref, w2_ref,
                     a1_ref, a2_ref, g1_ref, g2_ref, w0_ref, a0_ref, kk_ref, ka_ref,
                     r_ref, lw_ref, km_ref, v_ref, kr_ref, a_ref, go_ref, *, tiles_per_seq):
    i = pl.program_id(0)
    tm = h_ref.shape[0]
    gn = g_ref[...]
    hn = _rms(h_ref[...], gn)
    prev = _rms(hp_ref[7:8, :], gn)
    prev = jnp.where(i % tiles_per_seq == 0, jnp.zeros_like(prev), prev)
    row = lax.broadcasted_iota(jnp.int32, hn.shape, 0)
    shifted = jnp.where(row == 0, jnp.broadcast_to(prev, hn.shape), pltpu.roll(hn, 1, 0))
    xx = shifted - hn
    hn_b = hn.astype(BF16)
    xx_b = xx.astype(BF16)
    mix = lambda j: hn_b + xx_b * mu_ref[j:j + 1, :].astype(BF16)
    r = _dot(mix(0), wr_ref[...])
    k = _dot(mix(2), wk_ref[...])
    v = _dot(mix(3), wv_ref[...])
    z = w0_ref[...] + _dot(jnp.tanh(_dot(mix(1), w1_ref[...])).astype(BF16), w2_ref[...])
    a = _sigmoid(a0_ref[...] + _dot(_dot(mix(4), a1_ref[...]).astype(BF16), a2_ref[...]))
    g = _dot(_sigmoid(_dot(mix(5), g1_ref[...])).astype(BF16), g2_ref[...])
    r_ref[...] = r
    lw_ref[...] = _sigmoid(z) * (-math.exp(-0.5))
    km_ref[...] = k * (1.0 + (a - 1.0) * ka_ref[...])
    v_ref[...] = v
    kr_ref[...] = k * kk_ref[...]
    a_ref[...] = a
    go_ref[...] = g


def _rwkvproj(h, g, mu, wr, wk, wv, w1, w2, a1, a2, g1, g2, w0, a0, k_k, k_a, seq, tm):
    m, d = h.shape
    tiles_per_seq = seq // tm
    full = lambda a: pl.BlockSpec(a.shape, lambda i: (0,) * a.ndim, pipeline_mode=pl.Buffered(1))
    row = pl.BlockSpec((tm, d), lambda i: (i, 0))
    prev = pl.BlockSpec((8, d), lambda i: (jnp.maximum(i * (tm // 8) - 1, 0), 0))
    consts = (g, mu, wr, wk, wv, w1, w2, a1, a2, g1, g2, w0, a0, k_k, k_a)
    return pl.pallas_call(
        functools.partial(_rwkvproj_kernel, tiles_per_seq=tiles_per_seq),
        grid=(m // tm,),
        in_specs=[row, prev] + [full(a) for a in consts],
        out_specs=[row] * 7,
        out_shape=[jax.ShapeDtypeStruct((m, d), F32)] * 7,
        compiler_params=_cparams(("parallel",), VMEM_LARGE_MIB),
        name="rwkv_proj",
    )(h, h, *consts)


def _wkv_kernel(r_ref, lw_ref, km_ref, v_ref, kr_ref, a_ref, g_ref, rk_ref, lg_ref, lb_ref,
                o_ref, st_ref, y_ref, q1_ref, lhs_ref, z0_ref, wc_ref, bonus_ref, gate_ref,
                *, chunk, groups):
    grp = pl.program_id(2)

    @pl.when(pl.program_id(1) == 0)
    def _():
        st_ref[grp] = jnp.zeros(st_ref.shape[1:], F32)

    ts = r_ref.shape[0]
    nch = ts // chunk
    gl = GROUP_LANES
    rb = lax.broadcasted_iota(jnp.int32, (gl, gl), 0) // RWKV_DIM
    cb = lax.broadcasted_iota(jnp.int32, (gl, gl), 1) // RWKV_DIM
    blockmask = rb == cb
    ones_bd = jnp.where(blockmask, 1.0, 0.0).astype(BF16)

    def headsum(x, pieces=2):
        hi = x.astype(BF16)
        out = _dot(hi, ones_bd)
        if pieces == 2:
            out = out + _dot((x - hi.astype(F32)).astype(BF16), ones_bd)
        return out

    def bd(y):
        reps = gl // y.shape[0]
        return jnp.where(blockmask, jnp.concatenate([y] * reps, axis=0), 0.0).astype(BF16)

    def hmm(x, y):
        return _dot(x.astype(BF16), bd(y))

    def tn_blocks(x, y):
        return jnp.where(blockmask, _dot(x.T.astype(BF16), y.astype(BF16)), 0.0)

    t_idx = lax.broadcasted_iota(jnp.int32, (chunk, gl), 0)
    s_idx = lax.broadcasted_iota(jnp.int32, (chunk, gl), 1) % RWKV_DIM
    strict = s_idx < t_idx
    incl = s_idx <= t_idx
    tril_b = jnp.where(_tril_mask(chunk), 1.0, 0.0).astype(BF16)
    zeros_c = jnp.zeros((chunk, gl), F32)

    pad = lambda x: jnp.concatenate([x, zeros_c], axis=0)
    first_head = lax.broadcasted_iota(jnp.int32, (RWKV_DIM, LANES), 1) < RWKV_DIM

    kr = kr_ref[...]
    kkn_all = kr * lax.rsqrt(jnp.maximum(headsum(kr * kr), 1e-24))

    def fold(x):
        out = x[:RWKV_DIM]
        for h in range(1, RWKV_GROUP):
            out = out + x[h * RWKV_DIM:(h + 1) * RWKV_DIM]
        return out

    def prepare(c):
        rows = slice(c * chunk, (c + 1) * chunk)
        lw = lw_ref[rows, :]
        cum = _cumsum_rows(lw, tril_b, pieces=2)
        c_last = cum[chunk - 1:chunk, :]
        kkn = kkn_all[rows, :]
        kka = kkn * a_ref[rows, :]
        km = km_ref[rows, :]
        v = v_ref[rows, :]
        e_neg = jnp.exp(-cum)
        e_end = jnp.exp(c_last - cum)
        at = -kkn * jnp.exp(cum - lw)
        bt = kka * e_neg
        kt = km * e_neg
        rt = r_ref[rows, :] * jnp.exp(cum)
        bw = kka * e_end
        kw = km * e_end
        wt = jnp.broadcast_to(jnp.exp(c_last), (LANES, gl)).T
        wc_ref[grp, c] = jnp.concatenate(
            [jnp.where(first_head, wt[2 * p * RWKV_DIM:(2 * p + 1) * RWKV_DIM],
                       wt[(2 * p + 1) * RWKV_DIM:(2 * p + 2) * RWKV_DIM])
             for p in range(RWKV_GROUP // 2)], axis=1)
        yield

        lhs = jnp.concatenate([at, rt], axis=0).astype(BF16)
        pb = _dot_nt(lhs, bd(bt))
        pk = _dot_nt(lhs, bd(kt))
        a_ab = jnp.where(strict, pb[:chunk], 0.0)
        a_rb = jnp.where(incl, pb[chunk:], 0.0)
        a_ak = jnp.where(strict, pk[:chunk], 0.0)
        a_rk = jnp.where(incl, pk[chunk:], 0.0)
        yield

        e = jnp.where((t_idx % 2 == 1) & (s_idx == t_idx - 1), a_ab, 0.0)
        size = 2
        while size < chunk:
            off = ((t_idx // size) % 2 == 1) & (s_idx // size == t_idx // size - 1)
            a_off = jnp.where(off, a_ab, 0.0)
            t1 = a_off + hmm(a_off, e)
            yield
            e = e + t1 + hmm(e, t1)
            yield
            size *= 2

        av = hmm(jnp.concatenate([a_ak, a_rk], axis=0), v)
        akv = av[:chunk]
        yield
        p1 = akv + hmm(e, akv)
        mat = at + hmm(e, at)
        yield
        q1_ref[grp, rows, :] = av[chunk:] + hmm(a_rb, p1)
        lhs_ref[grp, c, :chunk, :] = (rt + hmm(a_rb, mat)).astype(BF16)
        lhs_ref[grp, c, chunk:, :] = fold(tn_blocks(pad(bw), pad(mat))).astype(BF16)
        z0_ref[grp, c] = fold(tn_blocks(jnp.concatenate([bw, kw], axis=0),
                                        jnp.concatenate([p1, v], axis=0)))

    stages = [prepare(c) for c in range(nch)]
    live = set(range(nch))
    tick = 0
    while live:
        for c in sorted(live):
            if tick >= (c % 2) * WKV_STAGGER:
                try:
                    next(stages[c])
                except StopIteration:
                    live.discard(c)
        tick += 1

    bonus_ref[grp] = headsum(r_ref[...] * km_ref[...] * rk_ref[...], pieces=1) * v_ref[...]
    gate_ref[grp] = g_ref[...]

    @pl.when(grp == groups - 1)
    def _():
        gs = range(groups)

        def body(c, carry):
            rows = pl.ds(pl.multiple_of(c * chunk, chunk), chunk)
            st = [st_ref[j] for j in gs]
            res = [_dot(lhs_ref[j, c], bd(st[j])) for j in gs]
            for j in gs:
                y_ref[j, rows, :] = q1_ref[j, rows, :] + res[j][:chunk]
                st_ref[j] = st[j] * wc_ref[j, c] + res[j][chunk:] + z0_ref[j, c]
            return carry

        lax.fori_loop(0, nch, body, 0)

        inv_n = 1.0 / RWKV_DIM
        for j in gs:
            cols = slice(j * gl, (j + 1) * gl)
            y = y_ref[j]
            mean = headsum(y, pieces=1) * inv_n
            dlt = y - mean
            var = headsum(dlt * dlt, pieces=1) * inv_n
            yn = dlt * lax.rsqrt(var + GN_EPS) * lg_ref[:, cols] + lb_ref[:, cols]
            o_ref[:, cols] = ((yn + bonus_ref[j]) * gate_ref[j]).astype(o_ref.dtype)


def _wkv(r, lw, km, v, kr, a, g, r_k, lnx_g, lnx_b, batch, seq, ts, chunk):
    m, d = r.shape
    gl = GROUP_LANES
    groups = d // gl
    nt = seq // ts
    nch = ts // chunk
    row = pl.BlockSpec((ts, gl), lambda b, t, j: (b * nt + t, j))
    vec = pl.BlockSpec((1, gl), lambda b, t, j: (0, j))
    full = pl.BlockSpec((1, d), lambda b, t, j: (0, 0))
    return pl.pallas_call(
        functools.partial(_wkv_kernel, chunk=chunk, groups=groups),
        grid=(batch, nt, groups),
        in_specs=[row] * 7 + [vec, full, full],
        out_specs=pl.BlockSpec((ts, d), lambda b, t, j: (b * nt + t, 0)),
        out_shape=jax.ShapeDtypeStruct((m, d), BF16),
        scratch_shapes=[pltpu.VMEM((groups, RWKV_DIM, gl), F32),
                        pltpu.VMEM((groups, ts, gl), F32),
                        pltpu.VMEM((groups, ts, gl), F32),
                        pltpu.VMEM((groups, nch, chunk + RWKV_DIM, gl), BF16),
                        pltpu.VMEM((groups, nch, RWKV_DIM, gl), F32),
                        pltpu.VMEM((groups, nch, RWKV_DIM, gl), F32),
                        pltpu.VMEM((groups, ts, gl), F32),
                        pltpu.VMEM((groups, ts, gl), F32)],
        compiler_params=_cparams(("parallel", "arbitrary", "arbitrary"), VMEM_LARGE_MIB),
        name="wkv7",
    )(r, lw, km, v, kr, a, g, r_k, lnx_g, lnx_b)


def kernel(x, norm_mix_g, norm_ffn_g, ab_w_in, hgrn_lower_bounds, hgrn_norm_g, fox_forget_bias,
           fox_q_norm_g, fox_k_norm_g, ab_w_out, rwkv_mu, rwkv_w_rkv, rwkv_w0, rwkv_w1, rwkv_w2,
           rwkv_a0, rwkv_a1, rwkv_a2, rwkv_g1, rwkv_g2, rwkv_k_k, rwkv_k_a, rwkv_r_k,
           rwkv_lnx_g, rwkv_lnx_b, rwkv_w_o, mlp_w_up, mlp_w_down):
    batch, seq, d = x.shape
    m = batch * seq
    t = _tiles(seq)
    row = lambda a: a.reshape(1, -1).astype(F32)
    bf = lambda a: a.astype(BF16)

    lb_all = jnp.cumsum(jax.nn.softmax(hgrn_lower_bounds.astype(F32), axis=0), axis=0)
    h = x.reshape(m, d)

    n_wide = ab_w_in.shape[-1] - FOX_HEADS
    gate_w = jnp.tile(ab_w_in[0][:, n_wide:], (1, 3))
    w_in = bf(jnp.pad(jnp.concatenate([ab_w_in[0][:, :n_wide], gate_w], axis=1),
                      ((0, 0), (0, LANES - 3 * FOX_HEADS))))
    proj = _inproj(h, row(norm_mix_g[0]), w_in, t.rows)
    ya = _hgrn(proj, row(lb_all[0]), row(hgrn_norm_g[0]), batch, seq, t.time, t.chunk)
    fb = jnp.pad(jnp.tile(row(fox_forget_bias[0]), (1, 3)), ((0, 0), (0, LANES - 3 * FOX_HEADS)))
    qt, ka, vt = _foxprep(proj, fb, row(fox_q_norm_g[0]), row(fox_k_norm_g[0]), batch, seq, t.time)
    yb = _fox(qt, ka, vt, proj, batch, seq, t.time, t.fox_keys, FOX_HEADS)
    h = _mix_mlp([ya, yb], bf(ab_w_out[0]), h, row(norm_ffn_g[0]), bf(mlp_w_up[0]), bf(mlp_w_down[0]),
                 t.mlp_rows, t.ff_chunk, "mix_mlp0")

    outs = _rwkvproj(h, row(norm_mix_g[1]), rwkv_mu[0].astype(F32),
                     bf(rwkv_w_rkv[0, 0]), bf(rwkv_w_rkv[0, 1]), bf(rwkv_w_rkv[0, 2]),
                     bf(rwkv_w1[0]), bf(rwkv_w2[0]), bf(rwkv_a1[0]), bf(rwkv_a2[0]),
                     bf(rwkv_g1[0]), bf(rwkv_g2[0]), row(rwkv_w0[0]), row(rwkv_a0[0]),
                     row(rwkv_k_k[0]), row(rwkv_k_a[0]), seq, t.rows)
    z = _wkv(*outs, row(rwkv_r_k[0]), row(rwkv_lnx_g[0]), row(rwkv_lnx_b[0]),
             batch, seq, t.time, t.chunk)
    h = _mix_mlp([z], bf(rwkv_w_o[0]), h, row(norm_ffn_g[1]), bf(mlp_w_up[1]), bf(mlp_w_down[1]),
                 t.mlp_rows, t.ff_chunk, "mix_mlp1")
    return h.reshape(batch, seq, d)
```

```python
import functools
import math
from typing import NamedTuple

import jax
import jax.numpy as jnp
import numpy as np
from jax import lax
from jax.experimental import pallas as pl
from jax.experimental.pallas import tpu as pltpu

F32 = jnp.float32
BF16 = jnp.bfloat16

RMS_EPS = 1e-6
GN_EPS = 64e-5

HGRN_HEADS = 4
HGRN_DIM = 128
HGRN_SUB = 16
FOX_HEADS = 8
FOX_DIM = 64
FOX_VROWS = FOX_DIM + 16
RWKV_DIM = 64
RWKV_GROUP = 4
GROUP_LANES = RWKV_GROUP * RWKV_DIM
WKV_STAGGER = 0
LANES = 128
VMEM_V7X_MIB = 64
VMEM_SMALL_MIB = VMEM_V7X_MIB // 2
VMEM_MEDIUM_MIB = 3 * VMEM_V7X_MIB // 4
VMEM_LARGE_MIB = 7 * VMEM_V7X_MIB // 8
NEG_BIG = -1e30
LOG2E = 1.4426950408889634

NT_DIMS = (((1,), (1,)), ((), ()))


class _Tiles(NamedTuple):
    rows: int
    mlp_rows: int
    time: int
    fox_keys: int
    chunk: int
    ff_chunk: int


def _tiles(seq):
    return _Tiles(rows=min(512, seq), mlp_rows=min(512, seq), time=min(512, seq),
                  fox_keys=min(256, seq), chunk=64, ff_chunk=1024)


def _cparams(sem, vmem_mb):
    return pltpu.CompilerParams(dimension_semantics=sem, vmem_limit_bytes=vmem_mb * 1024 * 1024)


def _dot(a, b):
    return jnp.dot(a, b, preferred_element_type=F32)


def _dot_nt(a, b):
    return lax.dot_general(a, b, NT_DIMS, preferred_element_type=F32)


def _rms(x, g):
    return x * lax.rsqrt(jnp.mean(x * x, axis=-1, keepdims=True) + RMS_EPS) * g


def _sigmoid(x):
    return 1.0 / (1.0 + jnp.exp(-x))


def _log_sigmoid(x):
    return jnp.minimum(x, 0.0) - jnp.log(1.0 + jnp.exp(-jnp.abs(x)))


def _tril_mask(n, strict=False):
    r = lax.broadcasted_iota(jnp.int32, (n, n), 0)
    c = lax.broadcasted_iota(jnp.int32, (n, n), 1)
    return (c < r) if strict else (c <= r)


def _split3(x):
    hi = x.astype(BF16)
    r1 = x - hi.astype(F32)
    mid = r1.astype(BF16)
    lo = (r1 - mid.astype(F32)).astype(BF16)
    return hi, mid, lo


def _cumsum_rows(x, tril_bf16, pieces=3):
    parts = _split3(x)[:pieces]
    out = _dot(tril_bf16, parts[0])
    for part in parts[1:]:
        out = out + _dot(tril_bf16, part)
    return out


def _inproj_kernel(x_ref, g_ref, w_ref, o_ref):
    hn = _rms(x_ref[...], g_ref[...]).astype(BF16)
    o_ref[...] = _dot(hn, w_ref[...])


def _inproj(x2, g, w, tm):
    m, d = x2.shape
    n = w.shape[1]
    return pl.pallas_call(
        _inproj_kernel,
        grid=(m // tm,),
        in_specs=[
            pl.BlockSpec((tm, d), lambda i: (i, 0)),
            pl.BlockSpec((1, d), lambda i: (0, 0)),
            pl.BlockSpec((d, n), lambda i: (0, 0), pipeline_mode=pl.Buffered(1)),
        ],
        out_specs=pl.BlockSpec((tm, n), lambda i: (i, 0)),
        out_shape=jax.ShapeDtypeStruct((m, n), F32),
        compiler_params=_cparams(("parallel",), VMEM_MEDIUM_MIB),
        name="inproj",
    )(x2, g, w)


def _hgrn_kernel(q_ref, f_ref, i_ref, g_ref, lb_ref, ng_ref, o_ref, st_ref, *, chunk):
    @pl.when(pl.program_id(1) == 0)
    def _():
        st_ref[...] = jnp.zeros_like(st_ref)

    ts = q_ref.shape[0]
    tril_b = jnp.where(_tril_mask(chunk), 1.0, 0.0).astype(BF16)

    hs = range(HGRN_HEADS)
    nch = ts // chunk
    tiles = [(c, h) for c in range(nch) for h in hs]
    blk = lambda ref, c, h: ref[c * chunk:(c + 1) * chunk, h * HGRN_DIM:(h + 1) * HGRN_DIM]
    lbs = [lb_ref[:, h * HGRN_DIM:(h + 1) * HGRN_DIM] for h in hs]

    f = [lbs[h] + (1.0 - lbs[h]) * _sigmoid(blk(f_ref, c, h)) for c, h in tiles]
    b = [_cumsum_rows(jnp.log(x), tril_b) for x in f]
    b_last = [x[chunk - 1:chunk, :] for x in b]
    q = [blk(q_ref, c, h) * _sigmoid(blk(q_ref, c, h)) for c, h in tiles]
    k = [1.0 - x for x in f]
    vb = [blk(i_ref, c, h).astype(BF16) for c, h in tiles]
    n = range(len(tiles))

    sub = HGRN_SUB
    tril_sub = _tril_mask(sub)
    score_cols = [[] for _ in n]
    for j in range(chunk // sub):
        lo, hi = j * sub, (j + 1) * sub
        for i in n:
            bj = b[i][lo:hi]
            b_mid = b[i][lo + sub // 2 - 1:lo + sub // 2]
            diag = _dot_nt((q[i][lo:hi] * jnp.exp(bj - b_mid)).astype(BF16),
                           (k[i][lo:hi] * jnp.exp(b_mid - bj)).astype(BF16))
            parts = [jnp.where(tril_sub, diag, 0.0)]
            if lo > 0:
                parts.insert(0, jnp.zeros((lo, sub), F32))
            if hi < chunk:
                b_end = b[i][hi - 1:hi]
                parts.append(_dot_nt((q[i][hi:] * jnp.exp(b[i][hi:] - b_end)).astype(BF16),
                                     (k[i][lo:hi] * jnp.exp(b_end - bj)).astype(BF16)))
            score_cols[i].append(jnp.concatenate(parts, axis=0).astype(BF16))
    o = []
    for i in n:
        acc = _dot(score_cols[i][0], vb[i][:sub])
        for j in range(1, chunk // sub):
            acc = acc + _dot(score_cols[i][j], vb[i][j * sub:(j + 1) * sub])
        o.append(acc)
    inc = [_dot(blk(i_ref, c, h).T.astype(BF16), (k[i] * jnp.exp(b_last[i] - b[i])).astype(BF16))
           for i, (c, h) in enumerate(tiles)]
    dec = [jnp.exp(x) for x in b_last]

    st = [st_ref[h] for h in hs]
    st_in = []
    for i, (c, h) in enumerate(tiles):
        st_in.append(st[h].astype(BF16))
        st[h] = st[h] * dec[i] + inc[i]
    for h in hs:
        st_ref[h] = st[h]

    for i, (c, h) in enumerate(tiles):
        oi = o[i] + _dot_nt((q[i] * jnp.exp(b[i])).astype(BF16), st_in[i])
        ag = blk(g_ref, c, h)
        on = _rms(oi, ng_ref[:, h * HGRN_DIM:(h + 1) * HGRN_DIM])
        o_ref[c * chunk:(c + 1) * chunk, h * HGRN_DIM:(h + 1) * HGRN_DIM] = (
            on * (ag * _sigmoid(ag))).astype(o_ref.dtype)


def _hgrn(proj, lb, ng, batch, seq, ts, chunk):
    m = proj.shape[0]
    w = HGRN_HEADS * HGRN_DIM
    nt = seq // ts
    spec = lambda j: pl.BlockSpec((ts, w), lambda b, t, j=j: (b * nt + t, j))
    vec = pl.BlockSpec((1, w), lambda b, t: (0, 0))
    return pl.pallas_call(
        functools.partial(_hgrn_kernel, chunk=chunk),
        grid=(batch, nt),
        in_specs=[spec(0), spec(1), spec(2), spec(3), vec, vec],
        out_specs=pl.BlockSpec((ts, w), lambda b, t: (b * nt + t, 0)),
        out_shape=jax.ShapeDtypeStruct((m, w), BF16),
        scratch_shapes=[pltpu.VMEM((HGRN_HEADS, HGRN_DIM, HGRN_DIM), F32)],
        compiler_params=_cparams(("parallel", "arbitrary"), VMEM_SMALL_MIB),
        name="hgrn2",
    )(proj, proj, proj, proj, lb, ng)


def _foxprep_kernel(q_ref, k_ref, v_ref, f_ref, fb_ref, qg_ref, kg_ref, hsum_ref, wq_ref, wk_ref,
                    oq_ref, ok_ref, qt_ref, ka_ref, vt_ref, carry_ref):
    @pl.when(pl.program_id(1) == 0)
    def _():
        carry_ref[...] = jnp.zeros_like(carry_ref)

    ts = q_ref.shape[0]
    pairs = FOX_HEADS // 2
    tril_b = jnp.where(_tril_mask(ts), 1.0, 0.0).astype(BF16)
    lf = _log_sigmoid(f_ref[...] + fb_ref[...])
    c = _cumsum_rows(lf, tril_b) + carry_ref[...]
    carry_ref[...] = c[ts - 1:ts, :]
    c2 = c * LOG2E
    hi = c2.astype(BF16).astype(F32)
    rest = c2 - hi
    mid = rest.astype(BF16).astype(F32)
    lane = lax.broadcasted_iota(jnp.int32, c.shape, 1)
    pieces = jnp.where(lane < FOX_HEADS, hi,
                       jnp.where(lane < 2 * FOX_HEADS, mid, rest - mid)).astype(BF16)

    def headnorm(x_ref, g_ref, r, scale):
        cols = slice(r * 2 * LANES, (r + 1) * 2 * LANES)
        x = x_ref[:, cols]
        xx = x * x
        xh = xx.astype(BF16)
        ss = _dot(xh, hsum_ref[...]) + _dot((xx - xh.astype(F32)).astype(BF16), hsum_ref[...])
        return (x * lax.rsqrt(ss * (1.0 / FOX_DIM) + RMS_EPS) * (g_ref[:, cols] * scale)).astype(BF16)

    qn = [headnorm(q_ref, qg_ref, r, FOX_DIM ** -0.5 * LOG2E) for r in range(pairs // 2)]
    kn = [headnorm(k_ref, kg_ref, r, 1.0) for r in range(pairs // 2)]
    for r in range(pairs):
        src = slice((r % 2) * LANES, (r % 2 + 1) * LANES)
        out = slice(r * 2 * LANES, (r + 1) * 2 * LANES)
        qa = _dot(jnp.concatenate([qn[r // 2][:, src], pieces], axis=1), wq_ref[r]) + oq_ref[:, out]
        ka = _dot(jnp.concatenate([kn[r // 2][:, src], pieces], axis=1), wk_ref[r]) + ok_ref[:, out]
        ka_ref[:, out] = ka.astype(BF16)
        for hh in range(2):
            qt_ref[0, 2 * r + hh] = qa[:, hh * LANES:(hh + 1) * LANES].T.astype(BF16)
        vt = v_ref[:, r * LANES:(r + 1) * LANES].T
        extra = jnp.where(lax.broadcasted_iota(jnp.int32, (FOX_VROWS - FOX_DIM, ts), 0) == 0, 1.0, 0.0)
        for hh in range(2):
            vt_ref[0, 2 * r + hh, 0] = jnp.concatenate(
                [vt[hh * FOX_DIM:(hh + 1) * FOX_DIM], extra], axis=0).astype(BF16)


def _foxprep(proj, fb, qg, kg, batch, seq, ts):
    m = proj.shape[0]
    w = FOX_HEADS * FOX_DIM
    nt = seq // ts
    pairs = FOX_HEADS // 2
    spec = lambda j: pl.BlockSpec((ts, w), lambda b, t, j=j: (b * nt + t, j))
    fcol = (8 * w) // LANES
    const = lambda a: pl.BlockSpec(a.shape, lambda b, t: (0,) * a.ndim)

    wa = FOX_HEADS * LANES
    ch = np.arange(2 * LANES)
    hsum = (ch[:, None] // FOX_DIM == ch[None, :] // FOX_DIM).astype(np.float32)
    wq = np.zeros((pairs, 2 * LANES, 2 * LANES), np.float32)
    wk = np.zeros((pairs, 2 * LANES, 2 * LANES), np.float32)
    oq = np.zeros((1, wa), np.float32)
    ok = np.zeros((1, wa), np.float32)
    for r in range(pairs):
        for hh in range(2):
            h = 2 * r + hh
            d = np.arange(FOX_DIM)
            wq[r, hh * FOX_DIM + d, hh * LANES + d] = 1.0
            wk[r, hh * FOX_DIM + d, hh * LANES + d] = 1.0
            for p in range(3):
                wq[r, LANES + p * FOX_HEADS + h, hh * LANES + FOX_DIM + p] = 1.0
                wk[r, LANES + p * FOX_HEADS + h, hh * LANES + FOX_DIM + 3 + p] = -1.0
                oq[0, h * LANES + FOX_DIM + 3 + p] = 1.0
                ok[0, h * LANES + FOX_DIM + p] = 1.0
    consts = [jnp.asarray(a, BF16) for a in (hsum, wq, wk)] + [jnp.asarray(oq), jnp.asarray(ok)]
    qg = jnp.tile(qg, (1, FOX_HEADS))
    kg = jnp.tile(kg, (1, FOX_HEADS))
    return pl.pallas_call(
        _foxprep_kernel,
        grid=(batch, nt),
        in_specs=[spec(4), spec(5), spec(6),
                  pl.BlockSpec((ts, LANES), lambda b, t: (b * nt + t, fcol)),
                  const(fb), const(qg), const(kg)] + [const(a) for a in consts],
        out_specs=[pl.BlockSpec((1, FOX_HEADS, LANES, ts), lambda b, t: (b, 0, 0, t)),
                   pl.BlockSpec((ts, FOX_HEADS * LANES), lambda b, t: (b * nt + t, 0)),
                   pl.BlockSpec((1, FOX_HEADS, 1, FOX_VROWS, ts), lambda b, t: (b, 0, t, 0, 0))],
        out_shape=[jax.ShapeDtypeStruct((batch, FOX_HEADS, LANES, seq), BF16),
                   jax.ShapeDtypeStruct((m, FOX_HEADS * LANES), BF16),
                   jax.ShapeDtypeStruct((batch, FOX_HEADS, nt, FOX_VROWS, ts), BF16)],
        scratch_shapes=[pltpu.VMEM((1, LANES), F32)],
        compiler_params=_cparams(("parallel", "arbitrary"), VMEM_SMALL_MIB),
        name="foxprep",
    )(proj, proj, proj, proj, fb, qg, kg, *consts)


def _fox_kernel(qt_ref, k_ref, vt_ref, g_ref, o_ref, *, tq, tk, heads):
    i = pl.program_id(2)
    hs = range(heads)
    ratio = tq // tk

    def step(j, carry, q0):
        ms, accs = carry
        diagonal = q0 is not None
        q0 = q0 or 0
        nq = tq - q0
        nkeys = tk if diagonal else tq
        rows = pl.ds(pl.multiple_of(j * tq, tq) + q0, nkeys)
        keys = slice(q0, q0 + nkeys)
        if diagonal:
            visible = (lax.broadcasted_iota(jnp.int32, (tk, nq), 1)
                       >= lax.broadcasted_iota(jnp.int32, (tk, nq), 0))

        def scores(h):
            s = _dot(k_ref[rows, h * LANES:(h + 1) * LANES], qt_ref[0, h, :, q0:])
            return jnp.where(visible, s, NEG_BIG) if diagonal else s

        def softmax(h, s):
            m_old = ms[h][:, q0:]
            m_new = jnp.maximum(m_old, jnp.max(s, axis=0, keepdims=True))
            return m_new, jnp.exp2(m_old - m_new), jnp.exp2(s - m_new).astype(BF16)

        def values(h, alpha, p):
            return accs[h][:, q0:] * alpha + _dot(vt_ref[0, h, j, :, keys], p)

        s, sm, out = {}, {}, {}
        for t in range(heads + 2):
            if t < heads:
                s[t] = scores(t)
            if 0 <= t - 1 < heads:
                sm[t - 1] = softmax(t - 1, s[t - 1])
            if 0 <= t - 2 < heads:
                out[t - 2] = values(t - 2, sm[t - 2][1], sm[t - 2][2])
        keep = lambda old, new: new if q0 == 0 else jnp.concatenate([old[:, :q0], new], axis=1)
        return (tuple(keep(ms[h], sm[h][0]) for h in hs), tuple(keep(accs[h], out[h]) for h in hs))

    neg = jnp.full((1, tq), NEG_BIG, F32)
    carry = ((neg,) * heads, (jnp.zeros((FOX_VROWS, tq), F32),) * heads)
    carry = lax.fori_loop(0, i, lambda j, c: step(j, c, None), carry)
    for d in range(ratio):
        carry = step(i, carry, d * tk)
    _, accs = carry
    norm = [a[:FOX_DIM] / a[FOX_DIM:FOX_DIM + 1] for a in accs]
    for r in range(heads // 2):
        cols = slice(r * LANES, (r + 1) * LANES)
        out = jnp.concatenate([norm[2 * r], norm[2 * r + 1]], axis=0)
        o_ref[:, cols] = (out.T * _sigmoid(g_ref[:, cols])).astype(o_ref.dtype)


def _fox(qt, ka, vt, proj, batch, seq, tq, tk, heads):
    m = ka.shape[0]
    nq = seq // tq
    groups = FOX_HEADS // heads
    pairs = heads // 2
    wv = heads * FOX_DIM
    gcol = (7 * FOX_HEADS * FOX_DIM) // wv
    return pl.pallas_call(
        functools.partial(_fox_kernel, tq=tq, tk=tk, heads=heads),
        grid=(batch, groups, nq),
        in_specs=[pl.BlockSpec((1, heads, LANES, tq), lambda b, p, i: (b, p, 0, i)),
                  pl.BlockSpec((seq, heads * LANES), lambda b, p, i: (b, p)),
                  pl.BlockSpec((1, heads, nq, FOX_VROWS, tq), lambda b, p, i: (b, p, 0, 0, 0)),
                  pl.BlockSpec((tq, wv), lambda b, p, i: (b * nq + i, gcol + p))],
        out_specs=pl.BlockSpec((tq, wv), lambda b, p, i: (b * nq + i, p)),
        out_shape=jax.ShapeDtypeStruct((m, FOX_HEADS * FOX_DIM), BF16),
        compiler_params=_cparams(("parallel", "parallel", "arbitrary"), VMEM_MEDIUM_MIB),
        name="fox_attention",
    )(qt, ka, vt, proj)


def _mix_mlp_kernel(*refs, n_mix, ck):
    ys = refs[:n_mix]
    w_ref, h_ref, g_ref, wu_ref, wd_ref, o_ref = refs[n_mix:]
    y = ys[0][...] if n_mix == 1 else jnp.concatenate([r[...] for r in ys], axis=1)
    x = h_ref[...] + _dot(y, w_ref[...])
    hn = _rms(x, g_ref[...]).astype(BF16)
    acc = x
    for c in range(wu_ref.shape[1] // ck):
        u = jnp.maximum(_dot(hn, wu_ref[:, c * ck:(c + 1) * ck]), 0.0)
        acc = acc + _dot((u * u).astype(BF16), wd_ref[c * ck:(c + 1) * ck, :])
    o_ref[...] = acc


def _mix_mlp(ys, w, h, g, wu, wd, tm, ck, name):
    m, d = h.shape
    const = lambda a: pl.BlockSpec(a.shape, lambda i: (0, 0), pipeline_mode=pl.Buffered(1))
    return pl.pallas_call(
        functools.partial(_mix_mlp_kernel, n_mix=len(ys), ck=ck),
        grid=(m // tm,),
        in_specs=([pl.BlockSpec((tm, y.shape[1]), lambda i: (i, 0)) for y in ys]
                  + [const(w), pl.BlockSpec((tm, d), lambda i: (i, 0)), const(g), const(wu), const(wd)]),
        out_specs=pl.BlockSpec((tm, d), lambda i: (i, 0)),
        out_shape=jax.ShapeDtypeStruct((m, d), F32),
        compiler_params=_cparams(("parallel",), VMEM_LARGE_MIB),
        name=name,
    )(*ys, w, h, g, wu, wd)


def _rwkvproj_kernel(h_ref, hp_ref, g_ref, mu_ref, wr_ref, wk_ref, wv_ref, w1_ref, w2_ref,
                     a1_ref, a2_ref, g1_ref, g2_ref, w0_ref, a0_ref, kk_ref, ka_ref,
                     r_ref, lw_ref, km_ref, v_ref, kr_ref, a_ref, go_ref, *, tiles_per_seq):
    i = pl.program_id(0)
    tm = h_ref.shape[0]
    gn = g_ref[...]
    hn = _rms(h_ref[...], gn)
    prev = _rms(hp_ref[7:8, :], gn)
    prev = jnp.where(i % tiles_per_seq == 0, jnp.zeros_like(prev), prev)
    row = lax.broadcasted_iota(jnp.int32, hn.shape, 0)
    shifted = jnp.where(row == 0, jnp.broadcast_to(prev, hn.shape), pltpu.roll(hn, 1, 0))
    xx = shifted - hn
    hn_b = hn.astype(BF16)
    xx_b = xx.astype(BF16)
    mix = lambda j: hn_b + xx_b * mu_ref[j:j + 1, :].astype(BF16)
    r = _dot(mix(0), wr_ref[...])
    k = _dot(mix(2), wk_ref[...])
    v = _dot(mix(3), wv_ref[...])
    z = w0_ref[...] + _dot(jnp.tanh(_dot(mix(1), w1_ref[...])).astype(BF16), w2_ref[...])
    a = _sigmoid(a0_ref[...] + _dot(_dot(mix(4), a1_ref[...]).astype(BF16), a2_ref[...]))
    g = _dot(_sigmoid(_dot(mix(5), g1_ref[...])).astype(BF16), g2_ref[...])
    r_ref[...] = r
    lw_ref[...] = _sigmoid(z) * (-math.exp(-0.5))
    km_ref[...] = k * (1.0 + (a - 1.0) * ka_ref[...])
    v_ref[...] = v
    kr_ref[...] = k * kk_ref[...]
    a_ref[...] = a
    go_ref[...] = g


def _rwkvproj(h, g, mu, wr, wk, wv, w1, w2, a1, a2, g1, g2, w0, a0, k_k, k_a, seq, tm):
    m, d = h.shape
    tiles_per_seq = seq // tm
    full = lambda a: pl.BlockSpec(a.shape, lambda i: (0,) * a.ndim, pipeline_mode=pl.Buffered(1))
    row = pl.BlockSpec((tm, d), lambda i: (i, 0))
    prev = pl.BlockSpec((8, d), lambda i: (jnp.maximum(i * (tm // 8) - 1, 0), 0))
    consts = (g, mu, wr, wk, wv, w1, w2, a1, a2, g1, g2, w0, a0, k_k, k_a)
    return pl.pallas_call(
        functools.partial(_rwkvproj_kernel, tiles_per_seq=tiles_per_seq),
        grid=(m // tm,),
        in_specs=[row, prev] + [full(a) for a in consts],
        out_specs=[row] * 7,
        out_shape=[jax.ShapeDtypeStruct((m, d), F32)] * 7,
        compiler_params=_cparams(("parallel",), VMEM_LARGE_MIB),
        name="rwkv_proj",
    )(h, h, *consts)


def _wkv_kernel(r_ref, lw_ref, km_ref, v_ref, kr_ref, a_ref, g_ref, rk_ref, lg_ref, lb_ref,
                o_ref, st_ref, y_ref, q1_ref, lhs_ref, z0_ref, wc_ref, bonus_ref, gate_ref,
                *, chunk, groups):
    grp = pl.program_id(2)

    @pl.when(pl.program_id(1) == 0)
    def _():
        st_ref[grp] = jnp.zeros(st_ref.shape[1:], F32)

    ts = r_ref.shape[0]
    nch = ts // chunk
    gl = GROUP_LANES
    rb = lax.broadcasted_iota(jnp.int32, (gl, gl), 0) // RWKV_DIM
    cb = lax.broadcasted_iota(jnp.int32, (gl, gl), 1) // RWKV_DIM
    blockmask = rb == cb
    ones_bd = jnp.where(blockmask, 1.0, 0.0).astype(BF16)

    def headsum(x, pieces=2):
        hi = x.astype(BF16)
        out = _dot(hi, ones_bd)
        if pieces == 2:
            out = out + _dot((x - hi.astype(F32)).astype(BF16), ones_bd)
        return out

    def bd(y):
        reps = gl // y.shape[0]
        return jnp.where(blockmask, jnp.concatenate([y] * reps, axis=0), 0.0).astype(BF16)

    def hmm(x, y):
        return _dot(x.astype(BF16), bd(y))

    def tn_blocks(x, y):
        return jnp.where(blockmask, _dot(x.T.astype(BF16), y.astype(BF16)), 0.0)

    t_idx = lax.broadcasted_iota(jnp.int32, (chunk, gl), 0)
    s_idx = lax.broadcasted_iota(jnp.int32, (chunk, gl), 1) % RWKV_DIM
    strict = s_idx < t_idx
    incl = s_idx <= t_idx
    tril_b = jnp.where(_tril_mask(chunk), 1.0, 0.0).astype(BF16)
    zeros_c = jnp.zeros((chunk, gl), F32)

    pad = lambda x: jnp.concatenate([x, zeros_c], axis=0)
    first_head = lax.broadcasted_iota(jnp.int32, (RWKV_DIM, LANES), 1) < RWKV_DIM

    kr = kr_ref[...]
    kkn_all = kr * lax.rsqrt(jnp.maximum(headsum(kr * kr), 1e-24))

    def fold(x):
        out = x[:RWKV_DIM]
        for h in range(1, RWKV_GROUP):
            out = out + x[h * RWKV_DIM:(h + 1) * RWKV_DIM]
        return out

    def prepare(c):
        rows = slice(c * chunk, (c + 1) * chunk)
        lw = lw_ref[rows, :]
        cum = _cumsum_rows(lw, tril_b, pieces=2)
        c_last = cum[chunk - 1:chunk, :]
        kkn = kkn_all[rows, :]
        kka = kkn * a_ref[rows, :]
        km = km_ref[rows, :]
        v = v_ref[rows, :]
        e_neg = jnp.exp(-cum)
        e_end = jnp.exp(c_last - cum)
        at = -kkn * jnp.exp(cum - lw)
        bt = kka * e_neg
        kt = km * e_neg
        rt = r_ref[rows, :] * jnp.exp(cum)
        bw = kka * e_end
        kw = km * e_end
        wt = jnp.broadcast_to(jnp.exp(c_last), (LANES, gl)).T
        wc_ref[grp, c] = jnp.concatenate(
            [jnp.where(first_head, wt[2 * p * RWKV_DIM:(2 * p + 1) * RWKV_DIM],
                       wt[(2 * p + 1) * RWKV_DIM:(2 * p + 2) * RWKV_DIM])
             for p in range(RWKV_GROUP // 2)], axis=1)
        yield

        lhs = jnp.concatenate([at, rt], axis=0).astype(BF16)
        pb = _dot_nt(lhs, bd(bt))
        pk = _dot_nt(lhs, bd(kt))
        a_ab = jnp.where(strict, pb[:chunk], 0.0)
        a_rb = jnp.where(incl, pb[chunk:], 0.0)
        a_ak = jnp.where(strict, pk[:chunk], 0.0)
        a_rk = jnp.where(incl, pk[chunk:], 0.0)
        yield

        e = jnp.where((t_idx % 2 == 1) & (s_idx == t_idx - 1), a_ab, 0.0)
        size = 2
        while size < chunk:
            off = ((t_idx // size) % 2 == 1) & (s_idx // size == t_idx // size - 1)
            a_off = jnp.where(off, a_ab, 0.0)
            t1 = a_off + hmm(a_off, e)
            yield
            e = e + t1 + hmm(e, t1)
            yield
            size *= 2

        av = hmm(jnp.concatenate([a_ak, a_rk], axis=0), v)
        akv = av[:chunk]
        yield
        p1 = akv + hmm(e, akv)
        mat = at + hmm(e, at)
        yield
        q1_ref[grp, rows, :] = av[chunk:] + hmm(a_rb, p1)
        lhs_ref[grp, c, :chunk, :] = (rt + hmm(a_rb, mat)).astype(BF16)
        lhs_ref[grp, c, chunk:, :] = fold(tn_blocks(pad(bw), pad(mat))).astype(BF16)
        z0_ref[grp, c] = fold(tn_blocks(jnp.concatenate([bw, kw], axis=0),
                                        jnp.concatenate([p1, v], axis=0)))

    stages = [prepare(c) for c in range(nch)]
    live = set(range(nch))
    tick = 0
    while live:
        for c in sorted(live):
            if tick >= (c % 2) * WKV_STAGGER:
                try:
                    next(stages[c])
                except StopIteration:
                    live.discard(c)
        tick += 1

    bonus_ref[grp] = headsum(r_ref[...] * km_ref[...] * rk_ref[...], pieces=1) * v_ref[...]
    gate_ref[grp] = g_ref[...]

    @pl.when(grp == groups - 1)
    def _():
        gs = range(groups)

        def body(c, carry):
            rows = pl.ds(pl.multiple_of(c * chunk, chunk), chunk)
            st = [st_ref[j] for j in gs]
            res = [_dot(lhs_ref[j, c], bd(st[j])) for j in gs]
            for j in gs:
                y_ref[j, rows, :] = q1_ref[j, rows, :] + res[j][:chunk]
                st_ref[j] = st[j] * wc_ref[j, c] + res[j][chunk:] + z0_ref[j, c]
            return carry

        lax.fori_loop(0, nch, body, 0)

        inv_n = 1.0 / RWKV_DIM
        for j in gs:
            cols = slice(j * gl, (j + 1) * gl)
            y = y_ref[j]
            mean = headsum(y, pieces=1) * inv_n
            dlt = y - mean
            var = headsum(dlt * dlt, pieces=1) * inv_n
            yn = dlt * lax.rsqrt(var + GN_EPS) * lg_ref[:, cols] + lb_ref[:, cols]
            o_ref[:, cols] = ((yn + bonus_ref[j]) * gate_ref[j]).astype(o_ref.dtype)


def _wkv(r, lw, km, v, kr, a, g, r_k, lnx_g, lnx_b, batch, seq, ts, chunk):
    m, d = r.shape
    gl = GROUP_LANES
    groups = d // gl
    nt = seq // ts
    nch = ts // chunk
    row = pl.BlockSpec((ts, gl), lambda b, t, j: (b * nt + t, j))
    vec = pl.BlockSpec((1, gl), lambda b, t, j: (0, j))
    full = pl.BlockSpec((1, d), lambda b, t, j: (0, 0))
    return pl.pallas_call(
        functools.partial(_wkv_kernel, chunk=chunk, groups=groups),
        grid=(batch, nt, groups),
        in_specs=[row] * 7 + [vec, full, full],
        out_specs=pl.BlockSpec((ts, d), lambda b, t, j: (b * nt + t, 0)),
        out_shape=jax.ShapeDtypeStruct((m, d), BF16),
        scratch_shapes=[pltpu.VMEM((groups, RWKV_DIM, gl), F32),
                        pltpu.VMEM((groups, ts, gl), F32),
                        pltpu.VMEM((groups, ts, gl), F32),
                        pltpu.VMEM((groups, nch, chunk + RWKV_DIM, gl), BF16),
                        pltpu.VMEM((groups, nch, RWKV_DIM, gl), F32),
                        pltpu.VMEM((groups, nch, RWKV_DIM, gl), F32),
                        pltpu.VMEM((groups, ts, gl), F32),
                        pltpu.VMEM((groups, ts, gl), F32)],
        compiler_params=_cparams(("parallel", "arbitrary", "arbitrary"), VMEM_LARGE_MIB),
        name="wkv7",
    )(r, lw, km, v, kr, a, g, r_k, lnx_g, lnx_b)


def kernel(x, norm_mix_g, norm_ffn_g, ab_w_in, hgrn_lower_bounds, hgrn_norm_g, fox_forget_bias,
           fox_q_norm_g, fox_k_norm_g, ab_w_out, rwkv_mu, rwkv_w_rkv, rwkv_w0, rwkv_w1, rwkv_w2,
           rwkv_a0, rwkv_a1, rwkv_a2, rwkv_g1, rwkv_g2, rwkv_k_k, rwkv_k_a, rwkv_r_k,
           rwkv_lnx_g, rwkv_lnx_b, rwkv_w_o, mlp_w_up, mlp_w_down):
    batch, seq, d = x.shape
    m = batch * seq
    t = _tiles(seq)
    row = lambda a: a.reshape(1, -1).astype(F32)
    bf = lambda a: a.astype(BF16)

    lb_all = jnp.cumsum(jax.nn.softmax(hgrn_lower_bounds.astype(F32), axis=0), axis=0)
    h = x.reshape(m, d)

    n_wide = ab_w_in.shape[-1] - FOX_HEADS
    gate_w = jnp.tile(ab_w_in[0][:, n_wide:], (1, 3))
    w_in = bf(jnp.pad(jnp.concatenate([ab_w_in[0][:, :n_wide], gate_w], axis=1),
                      ((0, 0), (0, LANES - 3 * FOX_HEADS))))
    proj = _inproj(h, row(norm_mix_g[0]), w_in, t.rows)
    ya = _hgrn(proj, row(lb_all[0]), row(hgrn_norm_g[0]), batch, seq, t.time, t.chunk)
    fb = jnp.pad(jnp.tile(row(fox_forget_bias[0]), (1, 3)), ((0, 0), (0, LANES - 3 * FOX_HEADS)))
    qt, ka, vt = _foxprep(proj, fb, row(fox_q_norm_g[0]), row(fox_k_norm_g[0]), batch, seq, t.time)
    yb = _fox(qt, ka, vt, proj, batch, seq, t.time, t.fox_keys, FOX_HEADS)
    h = _mix_mlp([ya, yb], bf(ab_w_out[0]), h, row(norm_ffn_g[0]), bf(mlp_w_up[0]), bf(mlp_w_down[0]),
                 t.mlp_rows, t.ff_chunk, "mix_mlp0")

    outs = _rwkvproj(h, row(norm_mix_g[1]), rwkv_mu[0].astype(F32),
                     bf(rwkv_w_rkv[0, 0]), bf(rwkv_w_rkv[0, 1]), bf(rwkv_w_rkv[0, 2]),
                     bf(rwkv_w1[0]), bf(rwkv_w2[0]), bf(rwkv_a1[0]), bf(rwkv_a2[0]),
                     bf(rwkv_g1[0]), bf(rwkv_g2[0]), row(rwkv_w0[0]), row(rwkv_a0[0]),
                     row(rwkv_k_k[0]), row(rwkv_k_a[0]), seq, t.rows)
    z = _wkv(*outs, row(rwkv_r_k[0]), row(rwkv_lnx_g[0]), row(rwkv_lnx_b[0]),
             batch, seq, t.time, t.chunk)
    h = _mix_mlp([z], bf(rwkv_w_o[0]), h, row(norm_ffn_g[1]), bf(mlp_w_up[1]), bf(mlp_w_down[1]),
                 t.mlp_rows, t.ff_chunk, "mix_mlp1")
    return h.reshape(batch, seq, d)
```

```python
import functools
import math
from typing import NamedTuple

import jax
import jax.numpy as jnp
import numpy as np
from jax import lax
from jax.experimental import pallas as pl
from jax.experimental.pallas import tpu as pltpu

F32 = jnp.float32
BF16 = jnp.bfloat16

RMS_EPS = 1e-6
GN_EPS = 64e-5

HGRN_HEADS = 4
HGRN_DIM = 128
HGRN_SUB = 16
FOX_HEADS = 8
FOX_DIM = 64
FOX_VROWS = FOX_DIM + 16
RWKV_DIM = 64
RWKV_GROUP = 4
GROUP_LANES = RWKV_GROUP * RWKV_DIM
LANES = 128
VMEM_V7X_MIB = 64
VMEM_SMALL_MIB = VMEM_V7X_MIB // 2
VMEM_MEDIUM_MIB = 3 * VMEM_V7X_MIB // 4
VMEM_LARGE_MIB = 7 * VMEM_V7X_MIB // 8
NEG_BIG = -1e30
LOG2E = 1.4426950408889634

NT_DIMS = (((1,), (1,)), ((), ()))


class _Tiles(NamedTuple):
    rows: int
    mlp_rows: int
    time: int
    fox_keys: int
    chunk: int
    ff_chunk: int


def _tiles(seq):
    return _Tiles(rows=min(512, seq), mlp_rows=min(512, seq), time=min(512, seq),
                  fox_keys=min(256, seq), chunk=64, ff_chunk=1024)


def _cparams(sem, vmem_mb):
    return pltpu.CompilerParams(dimension_semantics=sem, vmem_limit_bytes=vmem_mb * 1024 * 1024)


def _dot(a, b):
    return jnp.dot(a, b, preferred_element_type=F32)


def _dot_nt(a, b):
    return lax.dot_general(a, b, NT_DIMS, preferred_element_type=F32)


def _rms(x, g):
    return x * lax.rsqrt(jnp.mean(x * x, axis=-1, keepdims=True) + RMS_EPS) * g


def _sigmoid(x):
    return 1.0 / (1.0 + jnp.exp(-x))


def _log_sigmoid(x):
    return jnp.minimum(x, 0.0) - jnp.log(1.0 + jnp.exp(-jnp.abs(x)))


def _tril_mask(n, strict=False):
    r = lax.broadcasted_iota(jnp.int32, (n, n), 0)
    c = lax.broadcasted_iota(jnp.int32, (n, n), 1)
    return (c < r) if strict else (c <= r)


def _split3(x):
    hi = x.astype(BF16)
    r1 = x - hi.astype(F32)
    mid = r1.astype(BF16)
    lo = (r1 - mid.astype(F32)).astype(BF16)
    return hi, mid, lo


def _cumsum_rows(x, tril_bf16, pieces=3):
    parts = _split3(x)[:pieces]
    out = _dot(tril_bf16, parts[0])
    for part in parts[1:]:
        out = out + _dot(tril_bf16, part)
    return out


def _inproj_kernel(x_ref, g_ref, w_ref, o_ref):
    hn = _rms(x_ref[...], g_ref[...]).astype(BF16)
    o_ref[...] = _dot(hn, w_ref[...])


def _inproj(x2, g, w, tm):
    m, d = x2.shape
    n = w.shape[1]
    return pl.pallas_call(
        _inproj_kernel,
        grid=(m // tm,),
        in_specs=[
            pl.BlockSpec((tm, d), lambda i: (i, 0)),
            pl.BlockSpec((1, d), lambda i: (0, 0)),
            pl.BlockSpec((d, n), lambda i: (0, 0), pipeline_mode=pl.Buffered(1)),
        ],
        out_specs=pl.BlockSpec((tm, n), lambda i: (i, 0)),
        out_shape=jax.ShapeDtypeStruct((m, n), F32),
        compiler_params=_cparams(("parallel",), VMEM_MEDIUM_MIB),
        name="inproj",
    )(x2, g, w)


def _hgrn_kernel(q_ref, f_ref, i_ref, g_ref, lb_ref, ng_ref, o_ref, st_ref, *, chunk):
    @pl.when(pl.program_id(1) == 0)
    def _():
        st_ref[...] = jnp.zeros_like(st_ref)

    ts = q_ref.shape[0]
    tril_b = jnp.where(_tril_mask(chunk), 1.0, 0.0).astype(BF16)

    hs = range(HGRN_HEADS)
    nch = ts // chunk
    tiles = [(c, h) for c in range(nch) for h in hs]
    blk = lambda ref, c, h: ref[c * chunk:(c + 1) * chunk, h * HGRN_DIM:(h + 1) * HGRN_DIM]
    lbs = [lb_ref[:, h * HGRN_DIM:(h + 1) * HGRN_DIM] for h in hs]

    f = [lbs[h] + (1.0 - lbs[h]) * _sigmoid(blk(f_ref, c, h)) for c, h in tiles]
    b = [_cumsum_rows(jnp.log(x), tril_b) for x in f]
    b_last = [x[chunk - 1:chunk, :] for x in b]
    q = [blk(q_ref, c, h) * _sigmoid(blk(q_ref, c, h)) for c, h in tiles]
    k = [1.0 - x for x in f]
    vb = [blk(i_ref, c, h).astype(BF16) for c, h in tiles]
    n = range(len(tiles))

    sub = HGRN_SUB
    tril_sub = _tril_mask(sub)
    score_cols = [[] for _ in n]
    for j in range(chunk // sub):
        lo, hi = j * sub, (j + 1) * sub
        for i in n:
            bj = b[i][lo:hi]
            b_mid = b[i][lo + sub // 2 - 1:lo + sub // 2]
            diag = _dot_nt((q[i][lo:hi] * jnp.exp(bj - b_mid)).astype(BF16),
                           (k[i][lo:hi] * jnp.exp(b_mid - bj)).astype(BF16))
            parts = [jnp.where(tril_sub, diag, 0.0)]
            if lo > 0:
                parts.insert(0, jnp.zeros((lo, sub), F32))
            if hi < chunk:
                b_end = b[i][hi - 1:hi]
                parts.append(_dot_nt((q[i][hi:] * jnp.exp(b[i][hi:] - b_end)).astype(BF16),
                                     (k[i][lo:hi] * jnp.exp(b_end - bj)).astype(BF16)))
            score_cols[i].append(jnp.concatenate(parts, axis=0).astype(BF16))
    o = []
    for i in n:
        acc = _dot(score_cols[i][0], vb[i][:sub])
        for j in range(1, chunk // sub):
            acc = acc + _dot(score_cols[i][j], vb[i][j * sub:(j + 1) * sub])
        o.append(acc)
    inc = [_dot(blk(i_ref, c, h).T.astype(BF16), (k[i] * jnp.exp(b_last[i] - b[i])).astype(BF16))
           for i, (c, h) in enumerate(tiles)]
    dec = [jnp.exp(x) for x in b_last]

    st = [st_ref[h] for h in hs]
    st_in = []
    for i, (c, h) in enumerate(tiles):
        st_in.append(st[h].astype(BF16))
        st[h] = st[h] * dec[i] + inc[i]
    for h in hs:
        st_ref[h] = st[h]

    for i, (c, h) in enumerate(tiles):
        oi = o[i] + _dot_nt((q[i] * jnp.exp(b[i])).astype(BF16), st_in[i])
        ag = blk(g_ref, c, h)
        on = _rms(oi, ng_ref[:, h * HGRN_DIM:(h + 1) * HGRN_DIM])
        o_ref[c * chunk:(c + 1) * chunk, h * HGRN_DIM:(h + 1) * HGRN_DIM] = (
            on * (ag * _sigmoid(ag))).astype(o_ref.dtype)


def _hgrn(proj, lb, ng, batch, seq, ts, chunk):
    m = proj.shape[0]
    w = HGRN_HEADS * HGRN_DIM
    nt = seq // ts
    spec = lambda j: pl.BlockSpec((ts, w), lambda b, t, j=j: (b * nt + t, j))
    vec = pl.BlockSpec((1, w), lambda b, t: (0, 0))
    return pl.pallas_call(
        functools.partial(_hgrn_kernel, chunk=chunk),
        grid=(batch, nt),
        in_specs=[spec(0), spec(1), spec(2), spec(3), vec, vec],
        out_specs=pl.BlockSpec((ts, w), lambda b, t: (b * nt + t, 0)),
        out_shape=jax.ShapeDtypeStruct((m, w), BF16),
        scratch_shapes=[pltpu.VMEM((HGRN_HEADS, HGRN_DIM, HGRN_DIM), F32)],
        compiler_params=_cparams(("parallel", "arbitrary"), VMEM_SMALL_MIB),
        name="hgrn2",
    )(proj, proj, proj, proj, lb, ng)


def _foxprep_kernel(q_ref, k_ref, v_ref, f_ref, fb_ref, qg_ref, kg_ref, hsum_ref, wq_ref, wk_ref,
                    oq_ref, ok_ref, qt_ref, ka_ref, vt_ref, carry_ref):
    @pl.when(pl.program_id(1) == 0)
    def _():
        carry_ref[...] = jnp.zeros_like(carry_ref)

    ts = q_ref.shape[0]
    pairs = FOX_HEADS // 2
    tril_b = jnp.where(_tril_mask(ts), 1.0, 0.0).astype(BF16)
    lf = _log_sigmoid(f_ref[...] + fb_ref[...])
    c = _cumsum_rows(lf, tril_b) + carry_ref[...]
    carry_ref[...] = c[ts - 1:ts, :]
    c2 = c * LOG2E
    hi = c2.astype(BF16).astype(F32)
    rest = c2 - hi
    mid = rest.astype(BF16).astype(F32)
    lane = lax.broadcasted_iota(jnp.int32, c.shape, 1)
    pieces = jnp.where(lane < FOX_HEADS, hi,
                       jnp.where(lane < 2 * FOX_HEADS, mid, rest - mid)).astype(BF16)

    def headnorm(x_ref, g_ref, r, scale):
        cols = slice(r * 2 * LANES, (r + 1) * 2 * LANES)
        x = x_ref[:, cols]
        xx = x * x
        xh = xx.astype(BF16)
        ss = _dot(xh, hsum_ref[...]) + _dot((xx - xh.astype(F32)).astype(BF16), hsum_ref[...])
        return (x * lax.rsqrt(ss * (1.0 / FOX_DIM) + RMS_EPS) * (g_ref[:, cols] * scale)).astype(BF16)

    qn = [headnorm(q_ref, qg_ref, r, FOX_DIM ** -0.5 * LOG2E) for r in range(pairs // 2)]
    kn = [headnorm(k_ref, kg_ref, r, 1.0) for r in range(pairs // 2)]
    for r in range(pairs):
        src = slice((r % 2) * LANES, (r % 2 + 1) * LANES)
        out = slice(r * 2 * LANES, (r + 1) * 2 * LANES)
        qa = _dot(jnp.concatenate([qn[r // 2][:, src], pieces], axis=1), wq_ref[r]) + oq_ref[:, out]
        ka = _dot(jnp.concatenate([kn[r // 2][:, src], pieces], axis=1), wk_ref[r]) + ok_ref[:, out]
        ka_ref[:, out] = ka.astype(BF16)
        for hh in range(2):
            qt_ref[0, 2 * r + hh] = qa[:, hh * LANES:(hh + 1) * LANES].T.astype(BF16)
        vt = v_ref[:, r * LANES:(r + 1) * LANES].T
        extra = jnp.where(lax.broadcasted_iota(jnp.int32, (FOX_VROWS - FOX_DIM, ts), 0) == 0, 1.0, 0.0)
        for hh in range(2):
            vt_ref[0, 2 * r + hh, 0] = jnp.concatenate(
                [vt[hh * FOX_DIM:(hh + 1) * FOX_DIM], extra], axis=0).astype(BF16)


def _foxprep(proj, fb, qg, kg, batch, seq, ts):
    m = proj.shape[0]
    w = FOX_HEADS * FOX_DIM
    nt = seq // ts
    pairs = FOX_HEADS // 2
    spec = lambda j: pl.BlockSpec((ts, w), lambda b, t, j=j: (b * nt + t, j))
    fcol = (8 * w) // LANES
    const = lambda a: pl.BlockSpec(a.shape, lambda b, t: (0,) * a.ndim)

    wa = FOX_HEADS * LANES
    ch = np.arange(2 * LANES)
    hsum = (ch[:, None] // FOX_DIM == ch[None, :] // FOX_DIM).astype(np.float32)
    wq = np.zeros((pairs, 2 * LANES, 2 * LANES), np.float32)
    wk = np.zeros((pairs, 2 * LANES, 2 * LANES), np.float32)
    oq = np.zeros((1, wa), np.float32)
    ok = np.zeros((1, wa), np.float32)
    for r in range(pairs):
        for hh in range(2):
            h = 2 * r + hh
            d = np.arange(FOX_DIM)
            wq[r, hh * FOX_DIM + d, hh * LANES + d] = 1.0
            wk[r, hh * FOX_DIM + d, hh * LANES + d] = 1.0
            for p in range(3):
                wq[r, LANES + p * FOX_HEADS + h, hh * LANES + FOX_DIM + p] = 1.0
                wk[r, LANES + p * FOX_HEADS + h, hh * LANES + FOX_DIM + 3 + p] = -1.0
                oq[0, h * LANES + FOX_DIM + 3 + p] = 1.0
                ok[0, h * LANES + FOX_DIM + p] = 1.0
    consts = [jnp.asarray(a, BF16) for a in (hsum, wq, wk)] + [jnp.asarray(oq), jnp.asarray(ok)]
    qg = jnp.tile(qg, (1, FOX_HEADS))
    kg = jnp.tile(kg, (1, FOX_HEADS))
    return pl.pallas_call(
        _foxprep_kernel,
        grid=(batch, nt),
        in_specs=[spec(4), spec(5), spec(6),
                  pl.BlockSpec((ts, LANES), lambda b, t: (b * nt + t, fcol)),
                  const(fb), const(qg), const(kg)] + [const(a) for a in consts],
        out_specs=[pl.BlockSpec((1, FOX_HEADS, LANES, ts), lambda b, t: (b, 0, 0, t)),
                   pl.BlockSpec((ts, FOX_HEADS * LANES), lambda b, t: (b * nt + t, 0)),
                   pl.BlockSpec((1, FOX_HEADS, 1, FOX_VROWS, ts), lambda b, t: (b, 0, t, 0, 0))],
        out_shape=[jax.ShapeDtypeStruct((batch, FOX_HEADS, LANES, seq), BF16),
                   jax.ShapeDtypeStruct((m, FOX_HEADS * LANES), BF16),
                   jax.ShapeDtypeStruct((batch, FOX_HEADS, nt, FOX_VROWS, ts), BF16)],
        scratch_shapes=[pltpu.VMEM((1, LANES), F32)],
        compiler_params=_cparams(("parallel", "arbitrary"), VMEM_SMALL_MIB),
        name="foxprep",
    )(proj, proj, proj, proj, fb, qg, kg, *consts)


def _fox_kernel(qt_ref, k_ref, vt_ref, g_ref, o_ref, *, tq, tk, heads):
    i = pl.program_id(2)
    hs = range(heads)
    ratio = tq // tk

    def step(j, carry, q0):
        ms, accs = carry
        diagonal = q0 is not None
        q0 = q0 or 0
        nq = tq - q0
        nkeys = tk if diagonal else tq
        rows = pl.ds(pl.multiple_of(j * tq, tq) + q0, nkeys)
        keys = slice(q0, q0 + nkeys)
        if diagonal:
            visible = (lax.broadcasted_iota(jnp.int32, (tk, nq), 1)
                       >= lax.broadcasted_iota(jnp.int32, (tk, nq), 0))

        def scores(h):
            s = _dot(k_ref[rows, h * LANES:(h + 1) * LANES], qt_ref[0, h, :, q0:])
            return jnp.where(visible, s, NEG_BIG) if diagonal else s

        def softmax(h, s):
            m_old = ms[h][:, q0:]
            m_new = jnp.maximum(m_old, jnp.max(s, axis=0, keepdims=True))
            return m_new, jnp.exp2(m_old - m_new), jnp.exp2(s - m_new).astype(BF16)

        def values(h, alpha, p):
            return accs[h][:, q0:] * alpha + _dot(vt_ref[0, h, j, :, keys], p)

        s, sm, out = {}, {}, {}
        for t in range(heads + 2):
            if t < heads:
                s[t] = scores(t)
            if 0 <= t - 1 < heads:
                sm[t - 1] = softmax(t - 1, s[t - 1])
            if 0 <= t - 2 < heads:
                out[t - 2] = values(t - 2, sm[t - 2][1], sm[t - 2][2])
        keep = lambda old, new: new if q0 == 0 else jnp.concatenate([old[:, :q0], new], axis=1)
        return (tuple(keep(ms[h], sm[h][0]) for h in hs), tuple(keep(accs[h], out[h]) for h in hs))

    neg = jnp.full((1, tq), NEG_BIG, F32)
    carry = ((neg,) * heads, (jnp.zeros((FOX_VROWS, tq), F32),) * heads)
    carry = lax.fori_loop(0, i, lambda j, c: step(j, c, None), carry)
    for d in range(ratio):
        carry = step(i, carry, d * tk)
    _, accs = carry
    norm = [a[:FOX_DIM] / a[FOX_DIM:FOX_DIM + 1] for a in accs]
    for r in range(heads // 2):
        cols = slice(r * LANES, (r + 1) * LANES)
        out = jnp.concatenate([norm[2 * r], norm[2 * r + 1]], axis=0)
        o_ref[:, cols] = (out.T * _sigmoid(g_ref[:, cols])).astype(o_ref.dtype)


def _fox(qt, ka, vt, proj, batch, seq, tq, tk, heads):
    m = ka.shape[0]
    nq = seq // tq
    groups = FOX_HEADS // heads
    pairs = heads // 2
    wv = heads * FOX_DIM
    gcol = (7 * FOX_HEADS * FOX_DIM) // wv
    return pl.pallas_call(
        functools.partial(_fox_kernel, tq=tq, tk=tk, heads=heads),
        grid=(batch, groups, nq),
        in_specs=[pl.BlockSpec((1, heads, LANES, tq), lambda b, p, i: (b, p, 0, i)),
                  pl.BlockSpec((seq, heads * LANES), lambda b, p, i: (b, p)),
                  pl.BlockSpec((1, heads, nq, FOX_VROWS, tq), lambda b, p, i: (b, p, 0, 0, 0)),
                  pl.BlockSpec((tq, wv), lambda b, p, i: (b * nq + i, gcol + p))],
        out_specs=pl.BlockSpec((tq, wv), lambda b, p, i: (b * nq + i, p)),
        out_shape=jax.ShapeDtypeStruct((m, FOX_HEADS * FOX_DIM), BF16),
        compiler_params=_cparams(("parallel", "parallel", "arbitrary"), VMEM_MEDIUM_MIB),
        name="fox_attention",
    )(qt, ka, vt, proj)


def _mix_mlp_kernel(*refs, n_mix, ck):
    ys = refs[:n_mix]
    w_ref, h_ref, g_ref, wu_ref, wd_ref, o_ref = refs[n_mix:]
    y = ys[0][...] if n_mix == 1 else jnp.concatenate([r[...] for r in ys], axis=1)
    x = h_ref[...] + _dot(y, w_ref[...])
    hn = _rms(x, g_ref[...]).astype(BF16)
    acc = x
    for c in range(wu_ref.shape[1] // ck):
        u = jnp.maximum(_dot(hn, wu_ref[:, c * ck:(c + 1) * ck]), 0.0)
        acc = acc + _dot((u * u).astype(BF16), wd_ref[c * ck:(c + 1) * ck, :])
    o_ref[...] = acc


def _mix_mlp(ys, w, h, g, wu, wd, tm, ck, name):
    m, d = h.shape
    const = lambda a: pl.BlockSpec(a.shape, lambda i: (0, 0), pipeline_mode=pl.Buffered(1))
    return pl.pallas_call(
        functools.partial(_mix_mlp_kernel, n_mix=len(ys), ck=ck),
        grid=(m // tm,),
        in_specs=([pl.BlockSpec((tm, y.shape[1]), lambda i: (i, 0)) for y in ys]
                  + [const(w), pl.BlockSpec((tm, d), lambda i: (i, 0)), const(g), const(wu), const(wd)]),
        out_specs=pl.BlockSpec((tm, d), lambda i: (i, 0)),
        out_shape=jax.ShapeDtypeStruct((m, d), F32),
        compiler_params=_cparams(("parallel",), VMEM_LARGE_MIB),
        name=name,
    )(*ys, w, h, g, wu, wd)


def _rwkvproj_kernel(h_ref, hp_ref, g_ref, mu_ref, wr_ref, wk_ref, wv_ref, w1_ref, w2_ref,
                     a1_ref, a2_ref, g1_ref, g2_ref, w0_ref, a0_ref, kk_ref, ka_ref,
                     r_ref, lw_ref, km_ref, v_ref, kr_ref, a_ref, go_ref, *, tiles_per_seq):
    i = pl.program_id(0)
    tm = h_ref.shape[0]
    gn = g_ref[...]
    hn = _rms(h_ref[...], gn)
    prev = _rms(hp_ref[7:8, :], gn)
    prev = jnp.where(i % tiles_per_seq == 0, jnp.zeros_like(prev), prev)
    row = lax.broadcasted_iota(jnp.int32, hn.shape, 0)
    shifted = jnp.where(row == 0, jnp.broadcast_to(prev, hn.shape), pltpu.roll(hn, 1, 0))
    xx = shifted - hn
    hn_b = hn.astype(BF16)
    xx_b = xx.astype(BF16)
    mix = lambda j: hn_b + xx_b * mu_ref[j:j + 1, :].astype(BF16)
    r = _dot(mix(0), wr_ref[...])
    k = _dot(mix(2), wk_ref[...])
    v = _dot(mix(3), wv_ref[...])
    z = w0_ref[...] + _dot(jnp.tanh(_dot(mix(1), w1_ref[...])).astype(BF16), w2_ref[...])
    a = _sigmoid(a0_ref[...] + _dot(_dot(mix(4), a1_ref[...]).astype(BF16), a2_ref[...]))
    g = _dot(_sigmoid(_dot(mix(5), g1_ref[...])).astype(BF16), g2_ref[...])
    r_ref[...] = r.astype(r_ref.dtype)
    lw_ref[...] = _sigmoid(z) * (-math.exp(-0.5))
    km_ref[...] = (k * (1.0 + (a - 1.0) * ka_ref[...])).astype(km_ref.dtype)
    v_ref[...] = v.astype(v_ref.dtype)
    kr_ref[...] = (k * kk_ref[...]).astype(kr_ref.dtype)
    a_ref[...] = a.astype(a_ref.dtype)
    go_ref[...] = g.astype(go_ref.dtype)


def _rwkvproj(h, g, mu, wr, wk, wv, w1, w2, a1, a2, g1, g2, w0, a0, k_k, k_a, seq, tm):
    m, d = h.shape
    tiles_per_seq = seq // tm
    full = lambda a: pl.BlockSpec(a.shape, lambda i: (0,) * a.ndim, pipeline_mode=pl.Buffered(1))
    row = pl.BlockSpec((tm, d), lambda i: (i, 0))
    prev = pl.BlockSpec((8, d), lambda i: (jnp.maximum(i * (tm // 8) - 1, 0), 0))
    consts = (g, mu, wr, wk, wv, w1, w2, a1, a2, g1, g2, w0, a0, k_k, k_a)
    return pl.pallas_call(
        functools.partial(_rwkvproj_kernel, tiles_per_seq=tiles_per_seq),
        grid=(m // tm,),
        in_specs=[row, prev] + [full(a) for a in consts],
        out_specs=[row] * 7,
        out_shape=[jax.ShapeDtypeStruct((m, d), F32 if i == 1 else BF16) for i in range(7)],
        compiler_params=_cparams(("parallel",), VMEM_LARGE_MIB),
        name="rwkv_proj",
    )(h, h, *consts)


def _wkv_kernel(r_ref, lw_ref, km_ref, v_ref, kr_ref, a_ref, g_ref, rk_ref, lg_ref, lb_ref,
                o_ref, st_ref, y_ref, q1_ref, lhs_ref, z0_ref, wc_ref, bonus_ref, gate_ref,
                *, chunk, groups):
    grp = pl.program_id(2)

    @pl.when(pl.program_id(1) == 0)
    def _():
        st_ref[grp] = jnp.zeros(st_ref.shape[1:], F32)

    ts = r_ref.shape[0]
    nch = ts // chunk
    gl = GROUP_LANES
    rb = lax.broadcasted_iota(jnp.int32, (gl, gl), 0) // RWKV_DIM
    cb = lax.broadcasted_iota(jnp.int32, (gl, gl), 1) // RWKV_DIM
    blockmask = rb == cb
    ones_bd = jnp.where(blockmask, 1.0, 0.0).astype(BF16)

    def headsum(x, pieces=2):
        hi = x.astype(BF16)
        out = _dot(hi, ones_bd)
        if pieces == 2:
            out = out + _dot((x - hi.astype(F32)).astype(BF16), ones_bd)
        return out

    def bd(y):
        reps = gl // y.shape[0]
        return jnp.where(blockmask, jnp.concatenate([y] * reps, axis=0), 0.0).astype(BF16)

    def hmm(x, y):
        return _dot(x.astype(BF16), bd(y))

    def tn_blocks(x, y):
        return jnp.where(blockmask, _dot(x.T.astype(BF16), y.astype(BF16)), 0.0)

    t_idx = lax.broadcasted_iota(jnp.int32, (chunk, gl), 0)
    s_idx = lax.broadcasted_iota(jnp.int32, (chunk, gl), 1) % RWKV_DIM
    strict = s_idx < t_idx
    incl = s_idx <= t_idx
    tril_b = jnp.where(_tril_mask(chunk), 1.0, 0.0).astype(BF16)
    zeros_c = jnp.zeros((chunk, gl), F32)

    chunks = range(nch)
    blk = lambda ref, c: ref[c * chunk:(c + 1) * chunk, :].astype(F32)
    pad = lambda x: jnp.concatenate([x, zeros_c], axis=0)
    each = lambda fn, *lists: [fn(*args) for args in zip(*lists)]

    kr = kr_ref[...].astype(F32)
    kkn_all = kr * lax.rsqrt(jnp.maximum(headsum(kr * kr), 1e-24))

    lw = [blk(lw_ref, c) for c in chunks]
    cum = each(lambda x: _cumsum_rows(x, tril_b, pieces=2), lw)
    c_last = [x[chunk - 1:chunk, :] for x in cum]
    kkn = [kkn_all[c * chunk:(c + 1) * chunk, :] for c in chunks]
    kka = [kkn[c] * blk(a_ref, c) for c in chunks]
    km = [blk(km_ref, c) for c in chunks]
    v = [blk(v_ref, c) for c in chunks]
    e_neg = [jnp.exp(-x) for x in cum]
    e_end = each(lambda cl, x: jnp.exp(cl - x), c_last, cum)
    at = each(lambda k, x, l: -k * jnp.exp(x - l), kkn, cum, lw)
    bt = each(jnp.multiply, kka, e_neg)
    kt = each(jnp.multiply, km, e_neg)
    rt = [blk(r_ref, c) * jnp.exp(cum[c]) for c in chunks]
    bw = each(jnp.multiply, kka, e_end)
    kw = each(jnp.multiply, km, e_end)
    first_head = lax.broadcasted_iota(jnp.int32, (RWKV_DIM, LANES), 1) < RWKV_DIM
    for c in chunks:
        wt = jnp.broadcast_to(jnp.exp(c_last[c]), (LANES, gl)).T
        wc_ref[grp, c] = jnp.concatenate(
            [jnp.where(first_head, wt[2 * p * RWKV_DIM:(2 * p + 1) * RWKV_DIM],
                       wt[(2 * p + 1) * RWKV_DIM:(2 * p + 2) * RWKV_DIM])
             for p in range(RWKV_GROUP // 2)], axis=1)

    lhs = each(lambda a, r: jnp.concatenate([a, r], axis=0).astype(BF16), at, rt)
    pb = each(lambda l, b: _dot_nt(l, bd(b)), lhs, bt)
    pk = each(lambda l, k: _dot_nt(l, bd(k)), lhs, kt)
    a_ab = [jnp.where(strict, x[:chunk], 0.0) for x in pb]
    a_rb = [jnp.where(incl, x[chunk:], 0.0) for x in pb]
    a_ak = [jnp.where(strict, x[:chunk], 0.0) for x in pk]
    a_rk = [jnp.where(incl, x[chunk:], 0.0) for x in pk]

    e = [jnp.where((t_idx % 2 == 1) & (s_idx == t_idx - 1), x, 0.0) for x in a_ab]
    size = 2
    while size < chunk:
        off = ((t_idx // size) % 2 == 1) & (s_idx // size == t_idx // size - 1)
        a_off = [jnp.where(off, x, 0.0) for x in a_ab]
        t1 = each(lambda ao, ee: ao + hmm(ao, ee), a_off, e)
        e = each(lambda ee, tt: ee + tt + hmm(ee, tt), e, t1)
        size *= 2

    av = each(lambda ak, rk, vv: hmm(jnp.concatenate([ak, rk], axis=0), vv), a_ak, a_rk, v)
    akv = [x[:chunk] for x in av]
    p1 = each(lambda x, ee: x + hmm(ee, x), akv, e)
    mat = each(lambda x, ee: x + hmm(ee, x), at, e)
    q1 = each(lambda x, arb, pp: x[chunk:] + hmm(arb, pp), av, a_rb, p1)
    r2 = each(lambda r, arb, mm: r + hmm(arb, mm), rt, a_rb, mat)
    pct = each(lambda b, mm: tn_blocks(pad(b), pad(mm)), bw, mat)
    z0 = each(lambda b, k, pp, vv: tn_blocks(jnp.concatenate([b, k], axis=0),
                                             jnp.concatenate([pp, vv], axis=0)), bw, kw, p1, v)
    def fold(x):
        out = x[:RWKV_DIM]
        for h in range(1, RWKV_GROUP):
            out = out + x[h * RWKV_DIM:(h + 1) * RWKV_DIM]
        return out

    for c in chunks:
        q1_ref[grp, c * chunk:(c + 1) * chunk, :] = q1[c]
        lhs_ref[grp, c, :chunk, :] = r2[c].astype(BF16)
        lhs_ref[grp, c, chunk:, :] = fold(pct[c]).astype(BF16)
        z0_ref[grp, c] = fold(z0[c])
    bonus_ref[grp] = headsum(r_ref[...].astype(F32) * km_ref[...].astype(F32) * rk_ref[...],
                             pieces=1) * v_ref[...].astype(F32)
    gate_ref[grp] = g_ref[...].astype(F32)

    @pl.when(grp == groups - 1)
    def _():
        gs = range(groups)

        def body(c, carry):
            rows = pl.ds(pl.multiple_of(c * chunk, chunk), chunk)
            st = [st_ref[j] for j in gs]
            res = [_dot(lhs_ref[j, c], bd(st[j])) for j in gs]
            for j in gs:
                y_ref[j, rows, :] = q1_ref[j, rows, :] + res[j][:chunk]
                st_ref[j] = st[j] * wc_ref[j, c] + res[j][chunk:] + z0_ref[j, c]
            return carry

        lax.fori_loop(0, nch, body, 0)

        inv_n = 1.0 / RWKV_DIM
        for j in gs:
            cols = slice(j * gl, (j + 1) * gl)
            y = y_ref[j]
            mean = headsum(y, pieces=1) * inv_n
            dlt = y - mean
            var = headsum(dlt * dlt, pieces=1) * inv_n
            yn = dlt * lax.rsqrt(var + GN_EPS) * lg_ref[:, cols] + lb_ref[:, cols]
            o_ref[:, cols] = ((yn + bonus_ref[j]) * gate_ref[j]).astype(o_ref.dtype)


def _wkv(r, lw, km, v, kr, a, g, r_k, lnx_g, lnx_b, batch, seq, ts, chunk):
    m, d = r.shape
    gl = GROUP_LANES
    groups = d // gl
    nt = seq // ts
    nch = ts // chunk
    row = pl.BlockSpec((ts, gl), lambda b, t, j: (b * nt + t, j))
    vec = pl.BlockSpec((1, gl), lambda b, t, j: (0, j))
    full = pl.BlockSpec((1, d), lambda b, t, j: (0, 0))
    return pl.pallas_call(
        functools.partial(_wkv_kernel, chunk=chunk, groups=groups),
        grid=(batch, nt, groups),
        in_specs=[row] * 7 + [vec, full, full],
        out_specs=pl.BlockSpec((ts, d), lambda b, t, j: (b * nt + t, 0)),
        out_shape=jax.ShapeDtypeStruct((m, d), BF16),
        scratch_shapes=[pltpu.VMEM((groups, RWKV_DIM, gl), F32),
                        pltpu.VMEM((groups, ts, gl), F32),
                        pltpu.VMEM((groups, ts, gl), F32),
                        pltpu.VMEM((groups, nch, chunk + RWKV_DIM, gl), BF16),
                        pltpu.VMEM((groups, nch, RWKV_DIM, gl), F32),
                        pltpu.VMEM((groups, nch, RWKV_DIM, gl), F32),
                        pltpu.VMEM((groups, ts, gl), F32),
                        pltpu.VMEM((groups, ts, gl), F32)],
        compiler_params=_cparams(("parallel", "arbitrary", "arbitrary"), VMEM_LARGE_MIB),
        name="wkv7",
    )(r, lw, km, v, kr, a, g, r_k, lnx_g, lnx_b)


def kernel(x, norm_mix_g, norm_ffn_g, ab_w_in, hgrn_lower_bounds, hgrn_norm_g, fox_forget_bias,
           fox_q_norm_g, fox_k_norm_g, ab_w_out, rwkv_mu, rwkv_w_rkv, rwkv_w0, rwkv_w1, rwkv_w2,
           rwkv_a0, rwkv_a1, rwkv_a2, rwkv_g1, rwkv_g2, rwkv_k_k, rwkv_k_a, rwkv_r_k,
           rwkv_lnx_g, rwkv_lnx_b, rwkv_w_o, mlp_w_up, mlp_w_down):
    batch, seq, d = x.shape
    m = batch * seq
    t = _tiles(seq)
    row = lambda a: a.reshape(1, -1).astype(F32)
    bf = lambda a: a.astype(BF16)

    lb_all = jnp.cumsum(jax.nn.softmax(hgrn_lower_bounds.astype(F32), axis=0), axis=0)
    h = x.reshape(m, d)

    n_wide = ab_w_in.shape[-1] - FOX_HEADS
    gate_w = jnp.tile(ab_w_in[0][:, n_wide:], (1, 3))
    w_in = bf(jnp.pad(jnp.concatenate([ab_w_in[0][:, :n_wide], gate_w], axis=1),
                      ((0, 0), (0, LANES - 3 * FOX_HEADS))))
    proj = _inproj(h, row(norm_mix_g[0]), w_in, t.rows)
    ya = _hgrn(proj, row(lb_all[0]), row(hgrn_norm_g[0]), batch, seq, t.time, t.chunk)
    fb = jnp.pad(jnp.tile(row(fox_forget_bias[0]), (1, 3)), ((0, 0), (0, LANES - 3 * FOX_HEADS)))
    qt, ka, vt = _foxprep(proj, fb, row(fox_q_norm_g[0]), row(fox_k_norm_g[0]), batch, seq, t.time)
    yb = _fox(qt, ka, vt, proj, batch, seq, t.time, t.fox_keys, FOX_HEADS)
    h = _mix_mlp([ya, yb], bf(ab_w_out[0]), h, row(norm_ffn_g[0]), bf(mlp_w_up[0]), bf(mlp_w_down[0]),
                 t.mlp_rows, t.ff_chunk, "mix_mlp0")

    outs = _rwkvproj(h, row(norm_mix_g[1]), rwkv_mu[0].astype(F32),
                     bf(rwkv_w_rkv[0, 0]), bf(rwkv_w_rkv[0, 1]), bf(rwkv_w_rkv[0, 2]),
                     bf(rwkv_w1[0]), bf(rwkv_w2[0]), bf(rwkv_a1[0]), bf(rwkv_a2[0]),
                     bf(rwkv_g1[0]), bf(rwkv_g2[0]), row(rwkv_w0[0]), row(rwkv_a0[0]),
                     row(rwkv_k_k[0]), row(rwkv_k_a[0]), seq, t.rows)
    z = _wkv(*outs, row(rwkv_r_k[0]), row(rwkv_lnx_g[0]), row(rwkv_lnx_b[0]),
             batch, seq, t.time, t.chunk)
    h = _mix_mlp([z], bf(rwkv_w_o[0]), h, row(norm_ffn_g[1]), bf(mlp_w_up[1]), bf(mlp_w_down[1]),
                 t.mlp_rows, t.ff_chunk, "mix_mlp1")
    return h.reshape(batch, seq, d)
```

```python
import functools
import math
from typing import NamedTuple

import jax
import jax.numpy as jnp
import numpy as np
from jax import lax
from jax.experimental import pallas as pl
from jax.experimental.pallas import tpu as pltpu

F32 = jnp.float32
BF16 = jnp.bfloat16

RMS_EPS = 1e-6
GN_EPS = 64e-5

HGRN_HEADS = 4
HGRN_DIM = 128
HGRN_SUB = 16
FOX_HEADS = 8
FOX_DIM = 64
FOX_VROWS = FOX_DIM + 16
RWKV_DIM = 64
RWKV_GROUP = 4
GROUP_LANES = RWKV_GROUP * RWKV_DIM
LANES = 128
VMEM_V7X_MIB = 64
VMEM_SMALL_MIB = VMEM_V7X_MIB // 2
VMEM_MEDIUM_MIB = 3 * VMEM_V7X_MIB // 4
VMEM_LARGE_MIB = 7 * VMEM_V7X_MIB // 8
NEG_BIG = -1e30
LOG2E = 1.4426950408889634

NT_DIMS = (((1,), (1,)), ((), ()))


class _Tiles(NamedTuple):
    rows: int
    mlp_rows: int
    time: int
    fox_keys: int
    chunk: int
    ff_chunk: int


def _tiles(seq):
    return _Tiles(rows=min(512, seq), mlp_rows=min(512, seq), time=min(512, seq),
                  fox_keys=min(256, seq), chunk=64, ff_chunk=1024)


def _cparams(sem, vmem_mb):
    return pltpu.CompilerParams(dimension_semantics=sem, vmem_limit_bytes=vmem_mb * 1024 * 1024)


def _dot(a, b):
    return jnp.dot(a, b, preferred_element_type=F32)


def _dot_nt(a, b):
    return lax.dot_general(a, b, NT_DIMS, preferred_element_type=F32)


def _rms(x, g):
    return x * lax.rsqrt(jnp.mean(x * x, axis=-1, keepdims=True) + RMS_EPS) * g


def _sigmoid(x):
    return 1.0 / (1.0 + jnp.exp(-x))


def _log_sigmoid(x):
    return jnp.minimum(x, 0.0) - jnp.log(1.0 + jnp.exp(-jnp.abs(x)))


def _tril_mask(n, strict=False):
    r = lax.broadcasted_iota(jnp.int32, (n, n), 0)
    c = lax.broadcasted_iota(jnp.int32, (n, n), 1)
    return (c < r) if strict else (c <= r)


def _split3(x):
    hi = x.astype(BF16)
    r1 = x - hi.astype(F32)
    mid = r1.astype(BF16)
    lo = (r1 - mid.astype(F32)).astype(BF16)
    return hi, mid, lo


def _cumsum_rows(x, tril_bf16, pieces=3):
    parts = _split3(x)[:pieces]
    out = _dot(tril_bf16, parts[0])
    for part in parts[1:]:
        out = out + _dot(tril_bf16, part)
    return out


def _inproj_kernel(x_ref, g_ref, w_ref, o_ref):
    hn = _rms(x_ref[...], g_ref[...]).astype(BF16)
    o_ref[...] = _dot(hn, w_ref[...])


def _inproj(x2, g, w, tm):
    m, d = x2.shape
    n = w.shape[1]
    return pl.pallas_call(
        _inproj_kernel,
        grid=(m // tm,),
        in_specs=[
            pl.BlockSpec((tm, d), lambda i: (i, 0)),
            pl.BlockSpec((1, d), lambda i: (0, 0)),
            pl.BlockSpec((d, n), lambda i: (0, 0), pipeline_mode=pl.Buffered(1)),
        ],
        out_specs=pl.BlockSpec((tm, n), lambda i: (i, 0)),
        out_shape=jax.ShapeDtypeStruct((m, n), F32),
        compiler_params=_cparams(("parallel",), VMEM_MEDIUM_MIB),
        name="inproj",
    )(x2, g, w)


def _hgrn_kernel(q_ref, f_ref, i_ref, g_ref, lb_ref, ng_ref, o_ref, st_ref, *, chunk):
    @pl.when(pl.program_id(1) == 0)
    def _():
        st_ref[...] = jnp.zeros_like(st_ref)

    ts = q_ref.shape[0]
    tril_b = jnp.where(_tril_mask(chunk), 1.0, 0.0).astype(BF16)

    hs = range(HGRN_HEADS)
    nch = ts // chunk
    tiles = [(c, h) for c in range(nch) for h in hs]
    blk = lambda ref, c, h: ref[c * chunk:(c + 1) * chunk, h * HGRN_DIM:(h + 1) * HGRN_DIM]
    lbs = [lb_ref[:, h * HGRN_DIM:(h + 1) * HGRN_DIM] for h in hs]

    f = [lbs[h] + (1.0 - lbs[h]) * _sigmoid(blk(f_ref, c, h)) for c, h in tiles]
    b = [_cumsum_rows(jnp.log(x), tril_b) for x in f]
    b_last = [x[chunk - 1:chunk, :] for x in b]
    q = [blk(q_ref, c, h) * _sigmoid(blk(q_ref, c, h)) for c, h in tiles]
    k = [1.0 - x for x in f]
    vb = [blk(i_ref, c, h).astype(BF16) for c, h in tiles]
    n = range(len(tiles))

    sub = HGRN_SUB
    tril_sub = _tril_mask(sub)
    score_cols = [[] for _ in n]
    for j in range(chunk // sub):
        lo, hi = j * sub, (j + 1) * sub
        for i in n:
            bj = b[i][lo:hi]
            b_mid = b[i][lo + sub // 2 - 1:lo + sub // 2]
            diag = _dot_nt((q[i][lo:hi] * jnp.exp(bj - b_mid)).astype(BF16),
                           (k[i][lo:hi] * jnp.exp(b_mid - bj)).astype(BF16))
            parts = [jnp.where(tril_sub, diag, 0.0)]
            if lo > 0:
                parts.insert(0, jnp.zeros((lo, sub), F32))
            if hi < chunk:
                b_end = b[i][hi - 1:hi]
                parts.append(_dot_nt((q[i][hi:] * jnp.exp(b[i][hi:] - b_end)).astype(BF16),
                                     (k[i][lo:hi] * jnp.exp(b_end - bj)).astype(BF16)))
            score_cols[i].append(jnp.concatenate(parts, axis=0).astype(BF16))
    o = []
    for i in n:
        acc = _dot(score_cols[i][0], vb[i][:sub])
        for j in range(1, chunk // sub):
            acc = acc + _dot(score_cols[i][j], vb[i][j * sub:(j + 1) * sub])
        o.append(acc)
    inc = [_dot(blk(i_ref, c, h).T.astype(BF16), (k[i] * jnp.exp(b_last[i] - b[i])).astype(BF16))
           for i, (c, h) in enumerate(tiles)]
    dec = [jnp.exp(x) for x in b_last]

    st = [st_ref[h] for h in hs]
    st_in = []
    for i, (c, h) in enumerate(tiles):
        st_in.append(st[h].astype(BF16))
        st[h] = st[h] * dec[i] + inc[i]
    for h in hs:
        st_ref[h] = st[h]

    for i, (c, h) in enumerate(tiles):
        oi = o[i] + _dot_nt((q[i] * jnp.exp(b[i])).astype(BF16), st_in[i])
        ag = blk(g_ref, c, h)
        on = _rms(oi, ng_ref[:, h * HGRN_DIM:(h + 1) * HGRN_DIM])
        o_ref[c * chunk:(c + 1) * chunk, h * HGRN_DIM:(h + 1) * HGRN_DIM] = (
            on * (ag * _sigmoid(ag))).astype(o_ref.dtype)


def _hgrn(proj, lb, ng, batch, seq, ts, chunk):
    m = proj.shape[0]
    w = HGRN_HEADS * HGRN_DIM
    nt = seq // ts
    spec = lambda j: pl.BlockSpec((ts, w), lambda b, t, j=j: (b * nt + t, j))
    vec = pl.BlockSpec((1, w), lambda b, t: (0, 0))
    return pl.pallas_call(
        functools.partial(_hgrn_kernel, chunk=chunk),
        grid=(batch, nt),
        in_specs=[spec(0), spec(1), spec(2), spec(3), vec, vec],
        out_specs=pl.BlockSpec((ts, w), lambda b, t: (b * nt + t, 0)),
        out_shape=jax.ShapeDtypeStruct((m, w), BF16),
        scratch_shapes=[pltpu.VMEM((HGRN_HEADS, HGRN_DIM, HGRN_DIM), F32)],
        compiler_params=_cparams(("parallel", "arbitrary"), VMEM_SMALL_MIB),
        name="hgrn2",
    )(proj, proj, proj, proj, lb, ng)


def _foxprep_kernel(q_ref, k_ref, v_ref, f_ref, fb_ref, qg_ref, kg_ref, hsum_ref, wq_ref, wk_ref,
                    oq_ref, ok_ref, qt_ref, ka_ref, vt_ref, carry_ref):
    @pl.when(pl.program_id(1) == 0)
    def _():
        carry_ref[...] = jnp.zeros_like(carry_ref)

    ts = q_ref.shape[0]
    pairs = FOX_HEADS // 2
    tril_b = jnp.where(_tril_mask(ts), 1.0, 0.0).astype(BF16)
    lf = _log_sigmoid(f_ref[...] + fb_ref[...])
    c = _cumsum_rows(lf, tril_b) + carry_ref[...]
    carry_ref[...] = c[ts - 1:ts, :]
    c2 = c * LOG2E
    hi = c2.astype(BF16).astype(F32)
    rest = c2 - hi
    mid = rest.astype(BF16).astype(F32)
    lane = lax.broadcasted_iota(jnp.int32, c.shape, 1)
    pieces = jnp.where(lane < FOX_HEADS, hi,
                       jnp.where(lane < 2 * FOX_HEADS, mid, rest - mid)).astype(BF16)

    def headnorm(x_ref, g_ref, r, scale):
        cols = slice(r * 2 * LANES, (r + 1) * 2 * LANES)
        x = x_ref[:, cols]
        xx = x * x
        xh = xx.astype(BF16)
        ss = _dot(xh, hsum_ref[...]) + _dot((xx - xh.astype(F32)).astype(BF16), hsum_ref[...])
        return (x * lax.rsqrt(ss * (1.0 / FOX_DIM) + RMS_EPS) * (g_ref[:, cols] * scale)).astype(BF16)

    qn = [headnorm(q_ref, qg_ref, r, FOX_DIM ** -0.5 * LOG2E) for r in range(pairs // 2)]
    kn = [headnorm(k_ref, kg_ref, r, 1.0) for r in range(pairs // 2)]
    for r in range(pairs):
        src = slice((r % 2) * LANES, (r % 2 + 1) * LANES)
        out = slice(r * 2 * LANES, (r + 1) * 2 * LANES)
        qa = _dot(jnp.concatenate([qn[r // 2][:, src], pieces], axis=1), wq_ref[r]) + oq_ref[:, out]
        ka = _dot(jnp.concatenate([kn[r // 2][:, src], pieces], axis=1), wk_ref[r]) + ok_ref[:, out]
        ka_ref[:, out] = ka.astype(BF16)
        for hh in range(2):
            qt_ref[0, 2 * r + hh] = qa[:, hh * LANES:(hh + 1) * LANES].T.astype(BF16)
        vt = v_ref[:, r * LANES:(r + 1) * LANES].T
        extra = jnp.where(lax.broadcasted_iota(jnp.int32, (FOX_VROWS - FOX_DIM, ts), 0) == 0, 1.0, 0.0)
        for hh in range(2):
            vt_ref[0, 2 * r + hh, 0] = jnp.concatenate(
                [vt[hh * FOX_DIM:(hh + 1) * FOX_DIM], extra], axis=0).astype(BF16)


def _foxprep(proj, fb, qg, kg, batch, seq, ts):
    m = proj.shape[0]
    w = FOX_HEADS * FOX_DIM
    nt = seq // ts
    pairs = FOX_HEADS // 2
    spec = lambda j: pl.BlockSpec((ts, w), lambda b, t, j=j: (b * nt + t, j))
    fcol = (8 * w) // LANES
    const = lambda a: pl.BlockSpec(a.shape, lambda b, t: (0,) * a.ndim)

    wa = FOX_HEADS * LANES
    ch = np.arange(2 * LANES)
    hsum = (ch[:, None] // FOX_DIM == ch[None, :] // FOX_DIM).astype(np.float32)
    wq = np.zeros((pairs, 2 * LANES, 2 * LANES), np.float32)
    wk = np.zeros((pairs, 2 * LANES, 2 * LANES), np.float32)
    oq = np.zeros((1, wa), np.float32)
    ok = np.zeros((1, wa), np.float32)
    for r in range(pairs):
        for hh in range(2):
            h = 2 * r + hh
            d = np.arange(FOX_DIM)
            wq[r, hh * FOX_DIM + d, hh * LANES + d] = 1.0
            wk[r, hh * FOX_DIM + d, hh * LANES + d] = 1.0
            for p in range(3):
                wq[r, LANES + p * FOX_HEADS + h, hh * LANES + FOX_DIM + p] = 1.0
                wk[r, LANES + p * FOX_HEADS + h, hh * LANES + FOX_DIM + 3 + p] = -1.0
                oq[0, h * LANES + FOX_DIM + 3 + p] = 1.0
                ok[0, h * LANES + FOX_DIM + p] = 1.0
    consts = [jnp.asarray(a, BF16) for a in (hsum, wq, wk)] + [jnp.asarray(oq), jnp.asarray(ok)]
    qg = jnp.tile(qg, (1, FOX_HEADS))
    kg = jnp.tile(kg, (1, FOX_HEADS))
    return pl.pallas_call(
        _foxprep_kernel,
        grid=(batch, nt),
        in_specs=[spec(4), spec(5), spec(6),
                  pl.BlockSpec((ts, LANES), lambda b, t: (b * nt + t, fcol)),
                  const(fb), const(qg), const(kg)] + [const(a) for a in consts],
        out_specs=[pl.BlockSpec((1, FOX_HEADS, LANES, ts), lambda b, t: (b, 0, 0, t)),
                   pl.BlockSpec((ts, FOX_HEADS * LANES), lambda b, t: (b * nt + t, 0)),
                   pl.BlockSpec((1, FOX_HEADS, 1, FOX_VROWS, ts), lambda b, t: (b, 0, t, 0, 0))],
        out_shape=[jax.ShapeDtypeStruct((batch, FOX_HEADS, LANES, seq), BF16),
                   jax.ShapeDtypeStruct((m, FOX_HEADS * LANES), BF16),
                   jax.ShapeDtypeStruct((batch, FOX_HEADS, nt, FOX_VROWS, ts), BF16)],
        scratch_shapes=[pltpu.VMEM((1, LANES), F32)],
        compiler_params=_cparams(("parallel", "arbitrary"), VMEM_SMALL_MIB),
        name="foxprep",
    )(proj, proj, proj, proj, fb, qg, kg, *consts)


def _fox_kernel(qt_ref, k_ref, vt_ref, g_ref, o_ref, *, tq, tk, heads):
    i = pl.program_id(2)
    hs = range(heads)
    ratio = tq // tk

    def step(j, carry, q0):
        ms, accs = carry
        diagonal = q0 is not None
        q0 = q0 or 0
        nq = tq - q0
        nkeys = tk if diagonal else tq
        rows = pl.ds(pl.multiple_of(j * tq, tq) + q0, nkeys)
        keys = slice(q0, q0 + nkeys)
        if diagonal:
            visible = (lax.broadcasted_iota(jnp.int32, (tk, nq), 1)
                       >= lax.broadcasted_iota(jnp.int32, (tk, nq), 0))

        def scores(h):
            s = _dot(k_ref[rows, h * LANES:(h + 1) * LANES], qt_ref[0, h, :, q0:])
            return jnp.where(visible, s, NEG_BIG) if diagonal else s

        def softmax(h, s):
            m_old = ms[h][:, q0:]
            m_new = jnp.maximum(m_old, jnp.max(s, axis=0, keepdims=True))
            return m_new, jnp.exp2(m_old - m_new), jnp.exp2(s - m_new).astype(BF16)

        def values(h, alpha, p):
            return accs[h][:, q0:] * alpha + _dot(vt_ref[0, h, j, :, keys], p)

        s, sm, out = {}, {}, {}
        for t in range(heads + 3):
            if t < heads:
                s[t] = scores(t)
            if 0 <= t - 1 < heads:
                sm[t - 1] = softmax(t - 1, s[t - 1])
            if 0 <= t - 3 < heads:
                out[t - 3] = values(t - 3, sm[t - 3][1], sm[t - 3][2])
        keep = lambda old, new: new if q0 == 0 else jnp.concatenate([old[:, :q0], new], axis=1)
        return (tuple(keep(ms[h], sm[h][0]) for h in hs), tuple(keep(accs[h], out[h]) for h in hs))

    neg = jnp.full((1, tq), NEG_BIG, F32)
    carry = ((neg,) * heads, (jnp.zeros((FOX_VROWS, tq), F32),) * heads)
    carry = lax.fori_loop(0, i, lambda j, c: step(j, c, None), carry)
    for d in range(ratio):
        carry = step(i, carry, d * tk)
    _, accs = carry
    norm = [a[:FOX_DIM] / a[FOX_DIM:FOX_DIM + 1] for a in accs]
    for r in range(heads // 2):
        cols = slice(r * LANES, (r + 1) * LANES)
        out = jnp.concatenate([norm[2 * r], norm[2 * r + 1]], axis=0)
        o_ref[:, cols] = (out.T * _sigmoid(g_ref[:, cols])).astype(o_ref.dtype)


def _fox(qt, ka, vt, proj, batch, seq, tq, tk, heads):
    m = ka.shape[0]
    nq = seq // tq
    groups = FOX_HEADS // heads
    pairs = heads // 2
    wv = heads * FOX_DIM
    gcol = (7 * FOX_HEADS * FOX_DIM) // wv
    return pl.pallas_call(
        functools.partial(_fox_kernel, tq=tq, tk=tk, heads=heads),
        grid=(batch, groups, nq),
        in_specs=[pl.BlockSpec((1, heads, LANES, tq), lambda b, p, i: (b, p, 0, i)),
                  pl.BlockSpec((seq, heads * LANES), lambda b, p, i: (b, p)),
                  pl.BlockSpec((1, heads, nq, FOX_VROWS, tq), lambda b, p, i: (b, p, 0, 0, 0)),
                  pl.BlockSpec((tq, wv), lambda b, p, i: (b * nq + i, gcol + p))],
        out_specs=pl.BlockSpec((tq, wv), lambda b, p, i: (b * nq + i, p)),
        out_shape=jax.ShapeDtypeStruct((m, FOX_HEADS * FOX_DIM), BF16),
        compiler_params=_cparams(("parallel", "parallel", "arbitrary"), VMEM_MEDIUM_MIB),
        name="fox_attention",
    )(qt, ka, vt, proj)


def _mix_mlp_kernel(*refs, n_mix, ck):
    ys = refs[:n_mix]
    w_ref, h_ref, g_ref, wu_ref, wd_ref, o_ref = refs[n_mix:]
    y = ys[0][...] if n_mix == 1 else jnp.concatenate([r[...] for r in ys], axis=1)
    x = h_ref[...] + _dot(y, w_ref[...])
    hn = _rms(x, g_ref[...]).astype(BF16)
    acc = x
    for c in range(wu_ref.shape[1] // ck):
        u = jnp.maximum(_dot(hn, wu_ref[:, c * ck:(c + 1) * ck]), 0.0)
        acc = acc + _dot((u * u).astype(BF16), wd_ref[c * ck:(c + 1) * ck, :])
    o_ref[...] = acc


def _mix_mlp(ys, w, h, g, wu, wd, tm, ck, name):
    m, d = h.shape
    const = lambda a: pl.BlockSpec(a.shape, lambda i: (0, 0), pipeline_mode=pl.Buffered(1))
    return pl.pallas_call(
        functools.partial(_mix_mlp_kernel, n_mix=len(ys), ck=ck),
        grid=(m // tm,),
        in_specs=([pl.BlockSpec((tm, y.shape[1]), lambda i: (i, 0)) for y in ys]
                  + [const(w), pl.BlockSpec((tm, d), lambda i: (i, 0)), const(g), const(wu), const(wd)]),
        out_specs=pl.BlockSpec((tm, d), lambda i: (i, 0)),
        out_shape=jax.ShapeDtypeStruct((m, d), F32),
        compiler_params=_cparams(("parallel",), VMEM_LARGE_MIB),
        name=name,
    )(*ys, w, h, g, wu, wd)


def _rwkvproj_kernel(h_ref, hp_ref, g_ref, mu_ref, wr_ref, wk_ref, wv_ref, w1_ref, w2_ref,
                     a1_ref, a2_ref, g1_ref, g2_ref, w0_ref, a0_ref, kk_ref, ka_ref,
                     r_ref, lw_ref, km_ref, v_ref, kr_ref, a_ref, go_ref, *, tiles_per_seq):
    i = pl.program_id(0)
    tm = h_ref.shape[0]
    gn = g_ref[...]
    hn = _rms(h_ref[...], gn)
    prev = _rms(hp_ref[7:8, :], gn)
    prev = jnp.where(i % tiles_per_seq == 0, jnp.zeros_like(prev), prev)
    row = lax.broadcasted_iota(jnp.int32, hn.shape, 0)
    shifted = jnp.where(row == 0, jnp.broadcast_to(prev, hn.shape), pltpu.roll(hn, 1, 0))
    xx = shifted - hn
    hn_b = hn.astype(BF16)
    xx_b = xx.astype(BF16)
    mix = lambda j: hn_b + xx_b * mu_ref[j:j + 1, :].astype(BF16)
    r = _dot(mix(0), wr_ref[...])
    k = _dot(mix(2), wk_ref[...])
    v = _dot(mix(3), wv_ref[...])
    z = w0_ref[...] + _dot(jnp.tanh(_dot(mix(1), w1_ref[...])).astype(BF16), w2_ref[...])
    a = _sigmoid(a0_ref[...] + _dot(_dot(mix(4), a1_ref[...]).astype(BF16), a2_ref[...]))
    g = _dot(_sigmoid(_dot(mix(5), g1_ref[...])).astype(BF16), g2_ref[...])
    r_ref[...] = r
    lw_ref[...] = _sigmoid(z) * (-math.exp(-0.5))
    km_ref[...] = k * (1.0 + (a - 1.0) * ka_ref[...])
    v_ref[...] = v
    kr_ref[...] = k * kk_ref[...]
    a_ref[...] = a
    go_ref[...] = g


def _rwkvproj(h, g, mu, wr, wk, wv, w1, w2, a1, a2, g1, g2, w0, a0, k_k, k_a, seq, tm):
    m, d = h.shape
    tiles_per_seq = seq // tm
    full = lambda a: pl.BlockSpec(a.shape, lambda i: (0,) * a.ndim, pipeline_mode=pl.Buffered(1))
    row = pl.BlockSpec((tm, d), lambda i: (i, 0))
    prev = pl.BlockSpec((8, d), lambda i: (jnp.maximum(i * (tm // 8) - 1, 0), 0))
    consts = (g, mu, wr, wk, wv, w1, w2, a1, a2, g1, g2, w0, a0, k_k, k_a)
    return pl.pallas_call(
        functools.partial(_rwkvproj_kernel, tiles_per_seq=tiles_per_seq),
        grid=(m // tm,),
        in_specs=[row, prev] + [full(a) for a in consts],
        out_specs=[row] * 7,
        out_shape=[jax.ShapeDtypeStruct((m, d), F32)] * 7,
        compiler_params=_cparams(("parallel",), VMEM_LARGE_MIB),
        name="rwkv_proj",
    )(h, h, *consts)


def _wkv_kernel(r_ref, lw_ref, km_ref, v_ref, kr_ref, a_ref, g_ref, rk_ref, lg_ref, lb_ref,
                o_ref, st_ref, y_ref, q1_ref, lhs_ref, z0_ref, wc_ref, bonus_ref, gate_ref,
                *, chunk, groups):
    grp = pl.program_id(2)

    @pl.when(pl.program_id(1) == 0)
    def _():
        st_ref[grp] = jnp.zeros(st_ref.shape[1:], F32)

    ts = r_ref.shape[0]
    nch = ts // chunk
    gl = GROUP_LANES
    rb = lax.broadcasted_iota(jnp.int32, (gl, gl), 0) // RWKV_DIM
    cb = lax.broadcasted_iota(jnp.int32, (gl, gl), 1) // RWKV_DIM
    blockmask = rb == cb
    ones_bd = jnp.where(blockmask, 1.0, 0.0).astype(BF16)

    def headsum(x, pieces=2):
        hi = x.astype(BF16)
        out = _dot(hi, ones_bd)
        if pieces == 2:
            out = out + _dot((x - hi.astype(F32)).astype(BF16), ones_bd)
        return out

    def bd(y):
        reps = gl // y.shape[0]
        return jnp.where(blockmask, jnp.concatenate([y] * reps, axis=0), 0.0).astype(BF16)

    def hmm(x, y):
        return _dot(x.astype(BF16), bd(y))

    def tn_blocks(x, y):
        return jnp.where(blockmask, _dot(x.T.astype(BF16), y.astype(BF16)), 0.0)

    t_idx = lax.broadcasted_iota(jnp.int32, (chunk, gl), 0)
    s_idx = lax.broadcasted_iota(jnp.int32, (chunk, gl), 1) % RWKV_DIM
    strict = s_idx < t_idx
    incl = s_idx <= t_idx
    tril_b = jnp.where(_tril_mask(chunk), 1.0, 0.0).astype(BF16)
    zeros_c = jnp.zeros((chunk, gl), F32)

    chunks = range(nch)
    blk = lambda ref, c: ref[c * chunk:(c + 1) * chunk, :]
    pad = lambda x: jnp.concatenate([x, zeros_c], axis=0)
    each = lambda fn, *lists: [fn(*args) for args in zip(*lists)]

    kr = kr_ref[...]
    kkn_all = kr * lax.rsqrt(jnp.maximum(headsum(kr * kr), 1e-24))

    lw = [blk(lw_ref, c) for c in chunks]
    cum = each(lambda x: _cumsum_rows(x, tril_b, pieces=2), lw)
    c_last = [x[chunk - 1:chunk, :] for x in cum]
    kkn = [kkn_all[c * chunk:(c + 1) * chunk, :] for c in chunks]
    kka = [kkn[c] * blk(a_ref, c) for c in chunks]
    km = [blk(km_ref, c) for c in chunks]
    v = [blk(v_ref, c) for c in chunks]
    e_neg = [jnp.exp(-x) for x in cum]
    e_end = each(lambda cl, x: jnp.exp(cl - x), c_last, cum)
    at = each(lambda k, x, l: -k * jnp.exp(x - l), kkn, cum, lw)
    bt = each(jnp.multiply, kka, e_neg)
    kt = each(jnp.multiply, km, e_neg)
    rt = [blk(r_ref, c) * jnp.exp(cum[c]) for c in chunks]
    bw = each(jnp.multiply, kka, e_end)
    kw = each(jnp.multiply, km, e_end)
    first_head = lax.broadcasted_iota(jnp.int32, (RWKV_DIM, LANES), 1) < RWKV_DIM
    for c in chunks:
        wt = jnp.broadcast_to(jnp.exp(c_last[c]), (LANES, gl)).T
        wc_ref[grp, c] = jnp.concatenate(
            [jnp.where(first_head, wt[2 * p * RWKV_DIM:(2 * p + 1) * RWKV_DIM],
                       wt[(2 * p + 1) * RWKV_DIM:(2 * p + 2) * RWKV_DIM])
             for p in range(RWKV_GROUP // 2)], axis=1)

    lhs = each(lambda a, r: jnp.concatenate([a, r], axis=0).astype(BF16), at, rt)
    pb = each(lambda l, b: _dot_nt(l, bd(b)), lhs, bt)
    pk = each(lambda l, k: _dot_nt(l, bd(k)), lhs, kt)
    a_ab = [jnp.where(strict, x[:chunk], 0.0) for x in pb]
    a_rb = [jnp.where(incl, x[chunk:], 0.0) for x in pb]
    a_ak = [jnp.where(strict, x[:chunk], 0.0) for x in pk]
    a_rk = [jnp.where(incl, x[chunk:], 0.0) for x in pk]

    e = [jnp.where((t_idx % 2 == 1) & (s_idx == t_idx - 1), x, 0.0) for x in a_ab]
    size = 2
    while size < chunk:
        off = ((t_idx // size) % 2 == 1) & (s_idx // size == t_idx // size - 1)
        a_off = [jnp.where(off, x, 0.0) for x in a_ab]
        t1 = each(lambda ao, ee: ao + hmm(ao, ee), a_off, e)
        e = each(lambda ee, tt: ee + tt + hmm(ee, tt), e, t1)
        size *= 2

    av = each(lambda ak, rk, vv: hmm(jnp.concatenate([ak, rk], axis=0), vv), a_ak, a_rk, v)
    akv = [x[:chunk] for x in av]
    p1 = each(lambda x, ee: x + hmm(ee, x), akv, e)
    mat = each(lambda x, ee: x + hmm(ee, x), at, e)
    q1 = each(lambda x, arb, pp: x[chunk:] + hmm(arb, pp), av, a_rb, p1)
    r2 = each(lambda r, arb, mm: r + hmm(arb, mm), rt, a_rb, mat)
    pct = each(lambda b, mm: tn_blocks(pad(b), pad(mm)), bw, mat)
    z0 = each(lambda b, k, pp, vv: tn_blocks(jnp.concatenate([b, k], axis=0),
                                             jnp.concatenate([pp, vv], axis=0)), bw, kw, p1, v)
    def fold(x):
        out = x[:RWKV_DIM]
        for h in range(1, RWKV_GROUP):
            out = out + x[h * RWKV_DIM:(h + 1) * RWKV_DIM]
        return out

    for c in chunks:
        q1_ref[grp, c * chunk:(c + 1) * chunk, :] = q1[c]
        lhs_ref[grp, c, :chunk, :] = r2[c].astype(BF16)
        lhs_ref[grp, c, chunk:, :] = fold(pct[c]).astype(BF16)
        z0_ref[grp, c] = fold(z0[c])
    bonus_ref[grp] = headsum(r_ref[...] * km_ref[...] * rk_ref[...], pieces=1) * v_ref[...]
    gate_ref[grp] = g_ref[...]

    @pl.when(grp == groups - 1)
    def _():
        gs = range(groups)

        def body(c, carry):
            rows = pl.ds(pl.multiple_of(c * chunk, chunk), chunk)
            st = [st_ref[j] for j in gs]
            res = [_dot(lhs_ref[j, c], bd(st[j])) for j in gs]
            for j in gs:
                y_ref[j, rows, :] = q1_ref[j, rows, :] + res[j][:chunk]
                st_ref[j] = st[j] * wc_ref[j, c] + res[j][chunk:] + z0_ref[j, c]
            return carry

        lax.fori_loop(0, nch, body, 0)

        inv_n = 1.0 / RWKV_DIM
        for j in gs:
            cols = slice(j * gl, (j + 1) * gl)
            y = y_ref[j]
            mean = headsum(y, pieces=1) * inv_n
            dlt = y - mean
            var = headsum(dlt * dlt, pieces=1) * inv_n
            yn = dlt * lax.rsqrt(var + GN_EPS) * lg_ref[:, cols] + lb_ref[:, cols]
            o_ref[:, cols] = ((yn + bonus_ref[j]) * gate_ref[j]).astype(o_ref.dtype)


def _wkv(r, lw, km, v, kr, a, g, r_k, lnx_g, lnx_b, batch, seq, ts, chunk):
    m, d = r.shape
    gl = GROUP_LANES
    groups = d // gl
    nt = seq // ts
    nch = ts // chunk
    row = pl.BlockSpec((ts, gl), lambda b, t, j: (b * nt + t, j))
    vec = pl.BlockSpec((1, gl), lambda b, t, j: (0, j))
    full = pl.BlockSpec((1, d), lambda b, t, j: (0, 0))
    return pl.pallas_call(
        functools.partial(_wkv_kernel, chunk=chunk, groups=groups),
        grid=(batch, nt, groups),
        in_specs=[row] * 7 + [vec, full, full],
        out_specs=pl.BlockSpec((ts, d), lambda b, t, j: (b * nt + t, 0)),
        out_shape=jax.ShapeDtypeStruct((m, d), BF16),
        scratch_shapes=[pltpu.VMEM((groups, RWKV_DIM, gl), F32),
                        pltpu.VMEM((groups, ts, gl), F32),
                        pltpu.VMEM((groups, ts, gl), F32),
                        pltpu.VMEM((groups, nch, chunk + RWKV_DIM, gl), BF16),
                        pltpu.VMEM((groups, nch, RWKV_DIM, gl), F32),
                        pltpu.VMEM((groups, nch, RWKV_DIM, gl), F32),
                        pltpu.VMEM((groups, ts, gl), F32),
                        pltpu.VMEM((groups, ts, gl), F32)],
        compiler_params=_cparams(("parallel", "arbitrary", "arbitrary"), VMEM_LARGE_MIB),
        name="wkv7",
    )(r, lw, km, v, kr, a, g, r_k, lnx_g, lnx_b)


def kernel(x, norm_mix_g, norm_ffn_g, ab_w_in, hgrn_lower_bounds, hgrn_norm_g, fox_forget_bias,
           fox_q_norm_g, fox_k_norm_g, ab_w_out, rwkv_mu, rwkv_w_rkv, rwkv_w0, rwkv_w1, rwkv_w2,
           rwkv_a0, rwkv_a1, rwkv_a2, rwkv_g1, rwkv_g2, rwkv_k_k, rwkv_k_a, rwkv_r_k,
           rwkv_lnx_g, rwkv_lnx_b, rwkv_w_o, mlp_w_up, mlp_w_down):
    batch, seq, d = x.shape
    m = batch * seq
    t = _tiles(seq)
    row = lambda a: a.reshape(1, -1).astype(F32)
    bf = lambda a: a.astype(BF16)

    lb_all = jnp.cumsum(jax.nn.softmax(hgrn_lower_bounds.astype(F32), axis=0), axis=0)
    h = x.reshape(m, d)

    n_wide = ab_w_in.shape[-1] - FOX_HEADS
    gate_w = jnp.tile(ab_w_in[0][:, n_wide:], (1, 3))
    w_in = bf(jnp.pad(jnp.concatenate([ab_w_in[0][:, :n_wide], gate_w], axis=1),
                      ((0, 0), (0, LANES - 3 * FOX_HEADS))))
    proj = _inproj(h, row(norm_mix_g[0]), w_in, t.rows)
    ya = _hgrn(proj, row(lb_all[0]), row(hgrn_norm_g[0]), batch, seq, t.time, t.chunk)
    fb = jnp.pad(jnp.tile(row(fox_forget_bias[0]), (1, 3)), ((0, 0), (0, LANES - 3 * FOX_HEADS)))
    qt, ka, vt = _foxprep(proj, fb, row(fox_q_norm_g[0]), row(fox_k_norm_g[0]), batch, seq, t.time)
    yb = _fox(qt, ka, vt, proj, batch, seq, t.time, t.fox_keys, FOX_HEADS)
    h = _mix_mlp([ya, yb], bf(ab_w_out[0]), h, row(norm_ffn_g[0]), bf(mlp_w_up[0]), bf(mlp_w_down[0]),
                 t.mlp_rows, t.ff_chunk, "mix_mlp0")

    outs = _rwkvproj(h, row(norm_mix_g[1]), rwkv_mu[0].astype(F32),
                     bf(rwkv_w_rkv[0, 0]), bf(rwkv_w_rkv[0, 1]), bf(rwkv_w_rkv[0, 2]),
                     bf(rwkv_w1[0]), bf(rwkv_w2[0]), bf(rwkv_a1[0]), bf(rwkv_a2[0]),
                     bf(rwkv_g1[0]), bf(rwkv_g2[0]), row(rwkv_w0[0]), row(rwkv_a0[0]),
                     row(rwkv_k_k[0]), row(rwkv_k_a[0]), seq, t.rows)
    z = _wkv(*outs, row(rwkv_r_k[0]), row(rwkv_lnx_g[0]), row(rwkv_lnx_b[0]),
             batch, seq, t.time, t.chunk)
    h = _mix_mlp([z], bf(rwkv_w_o[0]), h, row(norm_ffn_g[1]), bf(mlp_w_up[1]), bf(mlp_w_down[1]),
                 t.mlp_rows, t.ff_chunk, "mix_mlp1")
    return h.reshape(batch, seq, d)
```

```python
import functools
import math
from typing import NamedTuple

import jax
import jax.numpy as jnp
import numpy as np
from jax import lax
from jax.experimental import pallas as pl
from jax.experimental.pallas import tpu as pltpu

F32 = jnp.float32
BF16 = jnp.bfloat16

RMS_EPS = 1e-6
GN_EPS = 64e-5

HGRN_HEADS = 4
HGRN_DIM = 128
HGRN_SUB = 16
FOX_HEADS = 8
FOX_DIM = 64
FOX_VROWS = FOX_DIM + 16
RWKV_DIM = 64
RWKV_GROUP = 4
GROUP_LANES = RWKV_GROUP * RWKV_DIM
LANES = 128
VMEM_V7X_MIB = 64
VMEM_SMALL_MIB = VMEM_V7X_MIB // 2
VMEM_MEDIUM_MIB = 3 * VMEM_V7X_MIB // 4
VMEM_LARGE_MIB = 7 * VMEM_V7X_MIB // 8
NEG_BIG = -1e30
LOG2E = 1.4426950408889634

NT_DIMS = (((1,), (1,)), ((), ()))


class _Tiles(NamedTuple):
    rows: int
    mlp_rows: int
    time: int
    fox_keys: int
    chunk: int
    ff_chunk: int


def _tiles(seq):
    return _Tiles(rows=min(512, seq), mlp_rows=min(512, seq), time=min(512, seq),
                  fox_keys=min(256, seq), chunk=64, ff_chunk=1024)


def _cparams(sem, vmem_mb):
    return pltpu.CompilerParams(dimension_semantics=sem, vmem_limit_bytes=vmem_mb * 1024 * 1024)


def _dot(a, b):
    return jnp.dot(a, b, preferred_element_type=F32)


def _dot_nt(a, b):
    return lax.dot_general(a, b, NT_DIMS, preferred_element_type=F32)


def _rms(x, g):
    return x * lax.rsqrt(jnp.mean(x * x, axis=-1, keepdims=True) + RMS_EPS) * g


def _sigmoid(x):
    return 1.0 / (1.0 + jnp.exp(-x))


def _log_sigmoid(x):
    return jnp.minimum(x, 0.0) - jnp.log(1.0 + jnp.exp(-jnp.abs(x)))


def _tril_mask(n, strict=False):
    r = lax.broadcasted_iota(jnp.int32, (n, n), 0)
    c = lax.broadcasted_iota(jnp.int32, (n, n), 1)
    return (c < r) if strict else (c <= r)


def _split3(x):
    hi = x.astype(BF16)
    r1 = x - hi.astype(F32)
    mid = r1.astype(BF16)
    lo = (r1 - mid.astype(F32)).astype(BF16)
    return hi, mid, lo


def _cumsum_rows(x, tril_bf16, pieces=3):
    parts = _split3(x)[:pieces]
    out = _dot(tril_bf16, parts[0])
    for part in parts[1:]:
        out = out + _dot(tril_bf16, part)
    return out


def _hgrn_kernel(q_ref, f_ref, i_ref, g_ref, lb_ref, ng_ref, o_ref, st_ref, *, chunk):
    @pl.when(pl.program_id(1) == 0)
    def _():
        st_ref[...] = jnp.zeros_like(st_ref)

    ts = q_ref.shape[0]
    tril_b = jnp.where(_tril_mask(chunk), 1.0, 0.0).astype(BF16)

    hs = range(HGRN_HEADS)
    nch = ts // chunk
    tiles = [(c, h) for c in range(nch) for h in hs]
    blk = lambda ref, c, h: ref[c * chunk:(c + 1) * chunk, h * HGRN_DIM:(h + 1) * HGRN_DIM]
    lbs = [lb_ref[:, h * HGRN_DIM:(h + 1) * HGRN_DIM] for h in hs]

    f = [lbs[h] + (1.0 - lbs[h]) * _sigmoid(blk(f_ref, c, h)) for c, h in tiles]
    b = [_cumsum_rows(jnp.log(x), tril_b) for x in f]
    b_last = [x[chunk - 1:chunk, :] for x in b]
    q = [blk(q_ref, c, h) * _sigmoid(blk(q_ref, c, h)) for c, h in tiles]
    k = [1.0 - x for x in f]
    vb = [blk(i_ref, c, h).astype(BF16) for c, h in tiles]
    n = range(len(tiles))

    sub = HGRN_SUB
    tril_sub = _tril_mask(sub)
    score_cols = [[] for _ in n]
    for j in range(chunk // sub):
        lo, hi = j * sub, (j + 1) * sub
        for i in n:
            bj = b[i][lo:hi]
            b_mid = b[i][lo + sub // 2 - 1:lo + sub // 2]
            diag = _dot_nt((q[i][lo:hi] * jnp.exp(bj - b_mid)).astype(BF16),
                           (k[i][lo:hi] * jnp.exp(b_mid - bj)).astype(BF16))
            parts = [jnp.where(tril_sub, diag, 0.0)]
            if lo > 0:
                parts.insert(0, jnp.zeros((lo, sub), F32))
            if hi < chunk:
                b_end = b[i][hi - 1:hi]
                parts.append(_dot_nt((q[i][hi:] * jnp.exp(b[i][hi:] - b_end)).astype(BF16),
                                     (k[i][lo:hi] * jnp.exp(b_end - bj)).astype(BF16)))
            score_cols[i].append(jnp.concatenate(parts, axis=0).astype(BF16))
    o = []
    for i in n:
        acc = _dot(score_cols[i][0], vb[i][:sub])
        for j in range(1, chunk // sub):
            acc = acc + _dot(score_cols[i][j], vb[i][j * sub:(j + 1) * sub])
        o.append(acc)
    inc = [_dot(blk(i_ref, c, h).T.astype(BF16), (k[i] * jnp.exp(b_last[i] - b[i])).astype(BF16))
           for i, (c, h) in enumerate(tiles)]
    dec = [jnp.exp(x) for x in b_last]

    st = [st_ref[h] for h in hs]
    st_in = []
    for i, (c, h) in enumerate(tiles):
        st_in.append(st[h].astype(BF16))
        st[h] = st[h] * dec[i] + inc[i]
    for h in hs:
        st_ref[h] = st[h]

    for i, (c, h) in enumerate(tiles):
        oi = o[i] + _dot_nt((q[i] * jnp.exp(b[i])).astype(BF16), st_in[i])
        ag = blk(g_ref, c, h)
        on = _rms(oi, ng_ref[:, h * HGRN_DIM:(h + 1) * HGRN_DIM])
        o_ref[c * chunk:(c + 1) * chunk, h * HGRN_DIM:(h + 1) * HGRN_DIM] = (
            on * (ag * _sigmoid(ag))).astype(o_ref.dtype)


def _hgrn(proj, lb, ng, batch, seq, ts, chunk):
    m = proj.shape[0]
    w = HGRN_HEADS * HGRN_DIM
    nt = seq // ts
    spec = lambda j: pl.BlockSpec((ts, w), lambda b, t, j=j: (b * nt + t, j))
    vec = pl.BlockSpec((1, w), lambda b, t: (0, 0))
    return pl.pallas_call(
        functools.partial(_hgrn_kernel, chunk=chunk),
        grid=(batch, nt),
        in_specs=[spec(0), spec(1), spec(2), spec(3), vec, vec],
        out_specs=pl.BlockSpec((ts, w), lambda b, t: (b * nt + t, 0)),
        out_shape=jax.ShapeDtypeStruct((m, w), BF16),
        scratch_shapes=[pltpu.VMEM((HGRN_HEADS, HGRN_DIM, HGRN_DIM), F32)],
        compiler_params=_cparams(("parallel", "arbitrary"), VMEM_SMALL_MIB),
        name="hgrn2",
    )(proj, proj, proj, proj, lb, ng)


def _inproj_fox_kernel(x_ref, g_ref, w_ref, fb_ref, qg_ref, kg_ref, hsum_ref, wq_ref, wk_ref,
                       oq_ref, ok_ref, ha_ref, go_ref, qt_ref, ka_ref, vt_ref, carry_ref):
    @pl.when(pl.program_id(1) == 0)
    def _():
        carry_ref[...] = jnp.zeros_like(carry_ref)

    ts = x_ref.shape[0]
    pairs = FOX_HEADS // 2
    hn = _rms(x_ref[...], g_ref[...]).astype(BF16)
    n_hgrn = ha_ref.shape[1]
    w = FOX_HEADS * FOX_DIM
    part = lambda j: _dot(hn, w_ref[:, n_hgrn + j * w:n_hgrn + (j + 1) * w])
    ha_ref[...] = _dot(hn, w_ref[:, :n_hgrn])
    q, k, v = part(0), part(1), part(2)
    go_ref[...] = part(3)
    f = _dot(hn, w_ref[:, n_hgrn + 4 * w:])

    tril_b = jnp.where(_tril_mask(ts), 1.0, 0.0).astype(BF16)
    lf = _log_sigmoid(f + fb_ref[...])
    c = _cumsum_rows(lf, tril_b) + carry_ref[...]
    carry_ref[...] = c[ts - 1:ts, :]
    c2 = c * LOG2E
    hi = c2.astype(BF16).astype(F32)
    rest = c2 - hi
    mid = rest.astype(BF16).astype(F32)
    lane = lax.broadcasted_iota(jnp.int32, c.shape, 1)
    pieces = jnp.where(lane < FOX_HEADS, hi,
                       jnp.where(lane < 2 * FOX_HEADS, mid, rest - mid)).astype(BF16)

    def headnorm(x_all, g_ref, r, scale):
        cols = slice(r * 2 * LANES, (r + 1) * 2 * LANES)
        x = x_all[:, cols]
        xx = x * x
        xh = xx.astype(BF16)
        ss = _dot(xh, hsum_ref[...]) + _dot((xx - xh.astype(F32)).astype(BF16), hsum_ref[...])
        return (x * lax.rsqrt(ss * (1.0 / FOX_DIM) + RMS_EPS) * (g_ref[:, cols] * scale)).astype(BF16)

    qn = [headnorm(q, qg_ref, r, FOX_DIM ** -0.5 * LOG2E) for r in range(pairs // 2)]
    kn = [headnorm(k, kg_ref, r, 1.0) for r in range(pairs // 2)]
    for r in range(pairs):
        src = slice((r % 2) * LANES, (r % 2 + 1) * LANES)
        out = slice(r * 2 * LANES, (r + 1) * 2 * LANES)
        qa = _dot(jnp.concatenate([qn[r // 2][:, src], pieces], axis=1), wq_ref[r]) + oq_ref[:, out]
        ka = _dot(jnp.concatenate([kn[r // 2][:, src], pieces], axis=1), wk_ref[r]) + ok_ref[:, out]
        ka_ref[:, out] = ka.astype(BF16)
        for hh in range(2):
            qt_ref[0, 2 * r + hh] = qa[:, hh * LANES:(hh + 1) * LANES].T.astype(BF16)
        vt = v[:, r * LANES:(r + 1) * LANES].T
        extra = jnp.where(lax.broadcasted_iota(jnp.int32, (FOX_VROWS - FOX_DIM, ts), 0) == 0, 1.0, 0.0)
        for hh in range(2):
            vt_ref[0, 2 * r + hh, 0] = jnp.concatenate(
                [vt[hh * FOX_DIM:(hh + 1) * FOX_DIM], extra], axis=0).astype(BF16)


def _inproj_fox(x2, g, w_in, fb, qg, kg, batch, seq, ts):
    m, d = x2.shape
    w = FOX_HEADS * FOX_DIM
    n_hgrn = 4 * HGRN_HEADS * HGRN_DIM
    nt = seq // ts
    pairs = FOX_HEADS // 2
    const = lambda a: pl.BlockSpec(a.shape, lambda b, t: (0,) * a.ndim)
    rows = lambda width: pl.BlockSpec((ts, width), lambda b, t: (b * nt + t, 0))

    wa = FOX_HEADS * LANES
    ch = np.arange(2 * LANES)
    hsum = (ch[:, None] // FOX_DIM == ch[None, :] // FOX_DIM).astype(np.float32)
    wq = np.zeros((pairs, 2 * LANES, 2 * LANES), np.float32)
    wk = np.zeros((pairs, 2 * LANES, 2 * LANES), np.float32)
    oq = np.zeros((1, wa), np.float32)
    ok = np.zeros((1, wa), np.float32)
    for r in range(pairs):
        for hh in range(2):
            h = 2 * r + hh
            ch = np.arange(FOX_DIM)
            wq[r, hh * FOX_DIM + ch, hh * LANES + ch] = 1.0
            wk[r, hh * FOX_DIM + ch, hh * LANES + ch] = 1.0
            for p in range(3):
                wq[r, LANES + p * FOX_HEADS + h, hh * LANES + FOX_DIM + p] = 1.0
                wk[r, LANES + p * FOX_HEADS + h, hh * LANES + FOX_DIM + 3 + p] = -1.0
                oq[0, h * LANES + FOX_DIM + 3 + p] = 1.0
                ok[0, h * LANES + FOX_DIM + p] = 1.0
    consts = [jnp.asarray(a, BF16) for a in (hsum, wq, wk)] + [jnp.asarray(oq), jnp.asarray(ok)]
    qg = jnp.tile(qg, (1, FOX_HEADS))
    kg = jnp.tile(kg, (1, FOX_HEADS))
    return pl.pallas_call(
        _inproj_fox_kernel,
        grid=(batch, nt),
        in_specs=[rows(d), const(g),
                  pl.BlockSpec(w_in.shape, lambda b, t: (0, 0), pipeline_mode=pl.Buffered(1)),
                  const(fb), const(qg), const(kg)] + [const(a) for a in consts],
        out_specs=[rows(n_hgrn), rows(w),
                   pl.BlockSpec((1, FOX_HEADS, LANES, ts), lambda b, t: (b, 0, 0, t)),
                   rows(FOX_HEADS * LANES),
                   pl.BlockSpec((1, FOX_HEADS, 1, FOX_VROWS, ts), lambda b, t: (b, 0, t, 0, 0))],
        out_shape=[jax.ShapeDtypeStruct((m, n_hgrn), F32),
                   jax.ShapeDtypeStruct((m, w), F32),
                   jax.ShapeDtypeStruct((batch, FOX_HEADS, LANES, seq), BF16),
                   jax.ShapeDtypeStruct((m, FOX_HEADS * LANES), BF16),
                   jax.ShapeDtypeStruct((batch, FOX_HEADS, nt, FOX_VROWS, ts), BF16)],
        scratch_shapes=[pltpu.VMEM((1, LANES), F32)],
        compiler_params=_cparams(("parallel", "arbitrary"), VMEM_MEDIUM_MIB),
        name="inproj_fox",
    )(x2, g, w_in, fb, qg, kg, *consts)


def _fox_kernel(qt_ref, k_ref, vt_ref, g_ref, o_ref, *, tq, tk, heads):
    i = pl.program_id(2)
    hs = range(heads)
    ratio = tq // tk

    def step(j, carry, q0):
        ms, accs = carry
        diagonal = q0 is not None
        q0 = q0 or 0
        nq = tq - q0
        nkeys = tk if diagonal else tq
        rows = pl.ds(pl.multiple_of(j * tq, tq) + q0, nkeys)
        keys = slice(q0, q0 + nkeys)
        if diagonal:
            visible = (lax.broadcasted_iota(jnp.int32, (tk, nq), 1)
                       >= lax.broadcasted_iota(jnp.int32, (tk, nq), 0))

        def scores(h):
            s = _dot(k_ref[rows, h * LANES:(h + 1) * LANES], qt_ref[0, h, :, q0:])
            return jnp.where(visible, s, NEG_BIG) if diagonal else s

        def softmax(h, s):
            m_old = ms[h][:, q0:]
            m_new = jnp.maximum(m_old, jnp.max(s, axis=0, keepdims=True))
            return m_new, jnp.exp2(m_old - m_new), jnp.exp2(s - m_new).astype(BF16)

        def values(h, alpha, p):
            return accs[h][:, q0:] * alpha + _dot(vt_ref[0, h, j, :, keys], p)

        s, sm, out = {}, {}, {}
        for t in range(heads + 2):
            if t < heads:
                s[t] = scores(t)
            if 0 <= t - 1 < heads:
                sm[t - 1] = softmax(t - 1, s[t - 1])
            if 0 <= t - 2 < heads:
                out[t - 2] = values(t - 2, sm[t - 2][1], sm[t - 2][2])
        keep = lambda old, new: new if q0 == 0 else jnp.concatenate([old[:, :q0], new], axis=1)
        return (tuple(keep(ms[h], sm[h][0]) for h in hs), tuple(keep(accs[h], out[h]) for h in hs))

    neg = jnp.full((1, tq), NEG_BIG, F32)
    carry = ((neg,) * heads, (jnp.zeros((FOX_VROWS, tq), F32),) * heads)
    carry = lax.fori_loop(0, i, lambda j, c: step(j, c, None), carry)
    for d in range(ratio):
        carry = step(i, carry, d * tk)
    _, accs = carry
    norm = [a[:FOX_DIM] / a[FOX_DIM:FOX_DIM + 1] for a in accs]
    for r in range(heads // 2):
        cols = slice(r * LANES, (r + 1) * LANES)
        out = jnp.concatenate([norm[2 * r], norm[2 * r + 1]], axis=0)
        o_ref[:, cols] = (out.T * _sigmoid(g_ref[:, cols])).astype(o_ref.dtype)


def _fox(qt, ka, vt, gate, batch, seq, tq, tk, heads):
    m = ka.shape[0]
    nq = seq // tq
    groups = FOX_HEADS // heads
    wv = heads * FOX_DIM
    return pl.pallas_call(
        functools.partial(_fox_kernel, tq=tq, tk=tk, heads=heads),
        grid=(batch, groups, nq),
        in_specs=[pl.BlockSpec((1, heads, LANES, tq), lambda b, p, i: (b, p, 0, i)),
                  pl.BlockSpec((seq, heads * LANES), lambda b, p, i: (b, p)),
                  pl.BlockSpec((1, heads, nq, FOX_VROWS, tq), lambda b, p, i: (b, p, 0, 0, 0)),
                  pl.BlockSpec((tq, wv), lambda b, p, i: (b * nq + i, p))],
        out_specs=pl.BlockSpec((tq, wv), lambda b, p, i: (b * nq + i, p)),
        out_shape=jax.ShapeDtypeStruct((m, FOX_HEADS * FOX_DIM), BF16),
        compiler_params=_cparams(("parallel", "parallel", "arbitrary"), VMEM_MEDIUM_MIB),
        name="fox_attention",
    )(qt, ka, vt, gate)


def _mix_mlp_kernel(*refs, n_mix, ck):
    ys = refs[:n_mix]
    w_ref, h_ref, g_ref, wu_ref, wd_ref, o_ref = refs[n_mix:]
    y = ys[0][...] if n_mix == 1 else jnp.concatenate([r[...] for r in ys], axis=1)
    x = h_ref[...] + _dot(y, w_ref[...])
    hn = _rms(x, g_ref[...]).astype(BF16)
    acc = x
    for c in range(wu_ref.shape[1] // ck):
        u = jnp.maximum(_dot(hn, wu_ref[:, c * ck:(c + 1) * ck]), 0.0)
        acc = acc + _dot((u * u).astype(BF16), wd_ref[c * ck:(c + 1) * ck, :])
    o_ref[...] = acc


def _mix_mlp(ys, w, h, g, wu, wd, tm, ck, name):
    m, d = h.shape
    const = lambda a: pl.BlockSpec(a.shape, lambda i: (0, 0), pipeline_mode=pl.Buffered(1))
    return pl.pallas_call(
        functools.partial(_mix_mlp_kernel, n_mix=len(ys), ck=ck),
        grid=(m // tm,),
        in_specs=([pl.BlockSpec((tm, y.shape[1]), lambda i: (i, 0)) for y in ys]
                  + [const(w), pl.BlockSpec((tm, d), lambda i: (i, 0)), const(g), const(wu), const(wd)]),
        out_specs=pl.BlockSpec((tm, d), lambda i: (i, 0)),
        out_shape=jax.ShapeDtypeStruct((m, d), F32),
        compiler_params=_cparams(("parallel",), VMEM_LARGE_MIB),
        name=name,
    )(*ys, w, h, g, wu, wd)


def _rwkvproj_kernel(h_ref, hp_ref, g_ref, mu_ref, wr_ref, wk_ref, wv_ref, w1_ref, w2_ref,
                     a1_ref, a2_ref, g1_ref, g2_ref, w0_ref, a0_ref, kk_ref, ka_ref,
                     r_ref, lw_ref, km_ref, v_ref, kr_ref, a_ref, go_ref, *, tiles_per_seq):
    i = pl.program_id(0)
    tm = h_ref.shape[0]
    gn = g_ref[...]
    hn = _rms(h_ref[...], gn)
    prev = _rms(hp_ref[7:8, :], gn)
    prev = jnp.where(i % tiles_per_seq == 0, jnp.zeros_like(prev), prev)
    row = lax.broadcasted_iota(jnp.int32, hn.shape, 0)
    shifted = jnp.where(row == 0, jnp.broadcast_to(prev, hn.shape), pltpu.roll(hn, 1, 0))
    xx = shifted - hn
    hn_b = hn.astype(BF16)
    xx_b = xx.astype(BF16)
    mix = lambda j: hn_b + xx_b * mu_ref[j:j + 1, :].astype(BF16)
    r = _dot(mix(0), wr_ref[...])
    k = _dot(mix(2), wk_ref[...])
    v = _dot(mix(3), wv_ref[...])
    z = w0_ref[...] + _dot(jnp.tanh(_dot(mix(1), w1_ref[...])).astype(BF16), w2_ref[...])
    a = _sigmoid(a0_ref[...] + _dot(_dot(mix(4), a1_ref[...]).astype(BF16), a2_ref[...]))
    g = _dot(_sigmoid(_dot(mix(5), g1_ref[...])).astype(BF16), g2_ref[...])
    r_ref[...] = r
    lw_ref[...] = _sigmoid(z) * (-math.exp(-0.5))
    km_ref[...] = k * (1.0 + (a - 1.0) * ka_ref[...])
    v_ref[...] = v
    kr_ref[...] = k * kk_ref[...]
    a_ref[...] = a
    go_ref[...] = g


def _rwkvproj(h, g, mu, wr, wk, wv, w1, w2, a1, a2, g1, g2, w0, a0, k_k, k_a, seq, tm):
    m, d = h.shape
    tiles_per_seq = seq // tm
    full = lambda a: pl.BlockSpec(a.shape, lambda i: (0,) * a.ndim, pipeline_mode=pl.Buffered(1))
    row = pl.BlockSpec((tm, d), lambda i: (i, 0))
    prev = pl.BlockSpec((8, d), lambda i: (jnp.maximum(i * (tm // 8) - 1, 0), 0))
    consts = (g, mu, wr, wk, wv, w1, w2, a1, a2, g1, g2, w0, a0, k_k, k_a)
    return pl.pallas_call(
        functools.partial(_rwkvproj_kernel, tiles_per_seq=tiles_per_seq),
        grid=(m // tm,),
        in_specs=[row, prev] + [full(a) for a in consts],
        out_specs=[row] * 7,
        out_shape=[jax.ShapeDtypeStruct((m, d), F32)] * 7,
        compiler_params=_cparams(("parallel",), VMEM_LARGE_MIB),
        name="rwkv_proj",
    )(h, h, *consts)


def _wkv_kernel(r_ref, lw_ref, km_ref, v_ref, kr_ref, a_ref, g_ref, rk_ref, lg_ref, lb_ref,
                o_ref, st_ref, y_ref, q1_ref, lhs_ref, z0_ref, wc_ref, bonus_ref, gate_ref,
                *, chunk, groups):
    grp = pl.program_id(2)

    @pl.when(pl.program_id(1) == 0)
    def _():
        st_ref[grp] = jnp.zeros(st_ref.shape[1:], F32)

    ts = r_ref.shape[0]
    nch = ts // chunk
    gl = GROUP_LANES
    rb = lax.broadcasted_iota(jnp.int32, (gl, gl), 0) // RWKV_DIM
    cb = lax.broadcasted_iota(jnp.int32, (gl, gl), 1) // RWKV_DIM
    blockmask = rb == cb
    ones_bd = jnp.where(blockmask, 1.0, 0.0).astype(BF16)

    def headsum(x, pieces=2):
        hi = x.astype(BF16)
        out = _dot(hi, ones_bd)
        if pieces == 2:
            out = out + _dot((x - hi.astype(F32)).astype(BF16), ones_bd)
        return out

    def bd(y):
        reps = gl // y.shape[0]
        return jnp.where(blockmask, jnp.concatenate([y] * reps, axis=0), 0.0).astype(BF16)

    def hmm(x, y):
        return _dot(x.astype(BF16), bd(y))

    def tn_blocks(x, y):
        return jnp.where(blockmask, _dot(x.T.astype(BF16), y.astype(BF16)), 0.0)

    t_idx = lax.broadcasted_iota(jnp.int32, (chunk, gl), 0)
    s_idx = lax.broadcasted_iota(jnp.int32, (chunk, gl), 1) % RWKV_DIM
    strict = s_idx < t_idx
    incl = s_idx <= t_idx
    tril_b = jnp.where(_tril_mask(chunk), 1.0, 0.0).astype(BF16)
    zeros_c = jnp.zeros((chunk, gl), F32)

    chunks = range(nch)
    blk = lambda ref, c: ref[c * chunk:(c + 1) * chunk, :]
    pad = lambda x: jnp.concatenate([x, zeros_c], axis=0)
    each = lambda fn, *lists: [fn(*args) for args in zip(*lists)]

    kr = kr_ref[...]
    kkn_all = kr * lax.rsqrt(jnp.maximum(headsum(kr * kr), 1e-24))

    lw = [blk(lw_ref, c) for c in chunks]
    cum = each(lambda x: _cumsum_rows(x, tril_b, pieces=2), lw)
    c_last = [x[chunk - 1:chunk, :] for x in cum]
    kkn = [kkn_all[c * chunk:(c + 1) * chunk, :] for c in chunks]
    kka = [kkn[c] * blk(a_ref, c) for c in chunks]
    km = [blk(km_ref, c) for c in chunks]
    v = [blk(v_ref, c) for c in chunks]
    e_neg = [jnp.exp(-x) for x in cum]
    e_end = each(lambda cl, x: jnp.exp(cl - x), c_last, cum)
    at = each(lambda k, x, l: -k * jnp.exp(x - l), kkn, cum, lw)
    bt = each(jnp.multiply, kka, e_neg)
    kt = each(jnp.multiply, km, e_neg)
    rt = [blk(r_ref, c) * jnp.exp(cum[c]) for c in chunks]
    bw = each(jnp.multiply, kka, e_end)
    kw = each(jnp.multiply, km, e_end)
    first_head = lax.broadcasted_iota(jnp.int32, (RWKV_DIM, LANES), 1) < RWKV_DIM
    for c in chunks:
        wt = jnp.broadcast_to(jnp.exp(c_last[c]), (LANES, gl)).T
        wc_ref[grp, c] = jnp.concatenate(
            [jnp.where(first_head, wt[2 * p * RWKV_DIM:(2 * p + 1) * RWKV_DIM],
                       wt[(2 * p + 1) * RWKV_DIM:(2 * p + 2) * RWKV_DIM])
             for p in range(RWKV_GROUP // 2)], axis=1)

    lhs = each(lambda a, r: jnp.concatenate([a, r], axis=0).astype(BF16), at, rt)
    pb = each(lambda l, b: _dot_nt(l, bd(b)), lhs, bt)
    pk = each(lambda l, k: _dot_nt(l, bd(k)), lhs, kt)
    a_ab = [jnp.where(strict, x[:chunk], 0.0) for x in pb]
    a_rb = [jnp.where(incl, x[chunk:], 0.0) for x in pb]
    a_ak = [jnp.where(strict, x[:chunk], 0.0) for x in pk]
    a_rk = [jnp.where(incl, x[chunk:], 0.0) for x in pk]

    e = [jnp.where((t_idx % 2 == 1) & (s_idx == t_idx - 1), x, 0.0) for x in a_ab]
    size = 2
    while size < chunk:
        off = ((t_idx // size) % 2 == 1) & (s_idx // size == t_idx // size - 1)
        a_off = [jnp.where(off, x, 0.0) for x in a_ab]
        t1 = each(lambda ao, ee: ao + hmm(ao, ee), a_off, e)
        e = each(lambda ee, tt: ee + tt + hmm(ee, tt), e, t1)
        size *= 2

    av = each(lambda ak, rk, vv: hmm(jnp.concatenate([ak, rk], axis=0), vv), a_ak, a_rk, v)
    akv = [x[:chunk] for x in av]
    p1 = each(lambda x, ee: x + hmm(ee, x), akv, e)
    mat = each(lambda x, ee: x + hmm(ee, x), at, e)
    q1 = each(lambda x, arb, pp: x[chunk:] + hmm(arb, pp), av, a_rb, p1)
    r2 = each(lambda r, arb, mm: r + hmm(arb, mm), rt, a_rb, mat)
    pct = each(lambda b, mm: tn_blocks(pad(b), pad(mm)), bw, mat)
    z0 = each(lambda b, k, pp, vv: tn_blocks(jnp.concatenate([b, k], axis=0),
                                             jnp.concatenate([pp, vv], axis=0)), bw, kw, p1, v)
    def fold(x):
        out = x[:RWKV_DIM]
        for h in range(1, RWKV_GROUP):
            out = out + x[h * RWKV_DIM:(h + 1) * RWKV_DIM]
        return out

    for c in chunks:
        q1_ref[grp, c * chunk:(c + 1) * chunk, :] = q1[c]
        lhs_ref[grp, c, :chunk, :] = r2[c].astype(BF16)
        lhs_ref[grp, c, chunk:, :] = fold(pct[c]).astype(BF16)
        z0_ref[grp, c] = fold(z0[c])
    bonus_ref[grp] = headsum(r_ref[...] * km_ref[...] * rk_ref[...], pieces=1) * v_ref[...]
    gate_ref[grp] = g_ref[...]

    @pl.when(grp == groups - 1)
    def _():
        gs = range(groups)

        def body(c, carry):
            rows = pl.ds(pl.multiple_of(c * chunk, chunk), chunk)
            st = [st_ref[j] for j in gs]
            res = [_dot(lhs_ref[j, c], bd(st[j])) for j in gs]
            for j in gs:
                y_ref[j, rows, :] = q1_ref[j, rows, :] + res[j][:chunk]
                st_ref[j] = st[j] * wc_ref[j, c] + res[j][chunk:] + z0_ref[j, c]
            return carry

        lax.fori_loop(0, nch, body, 0)

        inv_n = 1.0 / RWKV_DIM
        for j in gs:
            cols = slice(j * gl, (j + 1) * gl)
            y = y_ref[j]
            mean = headsum(y, pieces=1) * inv_n
            dlt = y - mean
            var = headsum(dlt * dlt, pieces=1) * inv_n
            yn = dlt * lax.rsqrt(var + GN_EPS) * lg_ref[:, cols] + lb_ref[:, cols]
            o_ref[:, cols] = ((yn + bonus_ref[j]) * gate_ref[j]).astype(o_ref.dtype)


def _wkv(r, lw, km, v, kr, a, g, r_k, lnx_g, lnx_b, batch, seq, ts, chunk):
    m, d = r.shape
    gl = GROUP_LANES
    groups = d // gl
    nt = seq // ts
    nch = ts // chunk
    row = pl.BlockSpec((ts, gl), lambda b, t, j: (b * nt + t, j))
    vec = pl.BlockSpec((1, gl), lambda b, t, j: (0, j))
    full = pl.BlockSpec((1, d), lambda b, t, j: (0, 0))
    return pl.pallas_call(
        functools.partial(_wkv_kernel, chunk=chunk, groups=groups),
        grid=(batch, nt, groups),
        in_specs=[row] * 7 + [vec, full, full],
        out_specs=pl.BlockSpec((ts, d), lambda b, t, j: (b * nt + t, 0)),
        out_shape=jax.ShapeDtypeStruct((m, d), BF16),
        scratch_shapes=[pltpu.VMEM((groups, RWKV_DIM, gl), F32),
                        pltpu.VMEM((groups, ts, gl), F32),
                        pltpu.VMEM((groups, ts, gl), F32),
                        pltpu.VMEM((groups, nch, chunk + RWKV_DIM, gl), BF16),
                        pltpu.VMEM((groups, nch, RWKV_DIM, gl), F32),
                        pltpu.VMEM((groups, nch, RWKV_DIM, gl), F32),
                        pltpu.VMEM((groups, ts, gl), F32),
                        pltpu.VMEM((groups, ts, gl), F32)],
        compiler_params=_cparams(("parallel", "arbitrary", "arbitrary"), VMEM_LARGE_MIB),
        name="wkv7",
    )(r, lw, km, v, kr, a, g, r_k, lnx_g, lnx_b)


def kernel(x, norm_mix_g, norm_ffn_g, ab_w_in, hgrn_lower_bounds, hgrn_norm_g, fox_forget_bias,
           fox_q_norm_g, fox_k_norm_g, ab_w_out, rwkv_mu, rwkv_w_rkv, rwkv_w0, rwkv_w1, rwkv_w2,
           rwkv_a0, rwkv_a1, rwkv_a2, rwkv_g1, rwkv_g2, rwkv_k_k, rwkv_k_a, rwkv_r_k,
           rwkv_lnx_g, rwkv_lnx_b, rwkv_w_o, mlp_w_up, mlp_w_down):
    batch, seq, d = x.shape
    m = batch * seq
    t = _tiles(seq)
    row = lambda a: a.reshape(1, -1).astype(F32)
    bf = lambda a: a.astype(BF16)

    lb_all = jnp.cumsum(jax.nn.softmax(hgrn_lower_bounds.astype(F32), axis=0), axis=0)
    h = x.reshape(m, d)

    n_wide = ab_w_in.shape[-1] - FOX_HEADS
    gate_w = jnp.tile(ab_w_in[0][:, n_wide:], (1, 3))
    w_in = bf(jnp.pad(jnp.concatenate([ab_w_in[0][:, :n_wide], gate_w], axis=1),
                      ((0, 0), (0, LANES - 3 * FOX_HEADS))))
    fb = jnp.pad(jnp.tile(row(fox_forget_bias[0]), (1, 3)), ((0, 0), (0, LANES - 3 * FOX_HEADS)))
    ha, gate, qt, ka, vt = _inproj_fox(h, row(norm_mix_g[0]), w_in, fb, row(fox_q_norm_g[0]),
                                       row(fox_k_norm_g[0]), batch, seq, t.time)
    ya = _hgrn(ha, row(lb_all[0]), row(hgrn_norm_g[0]), batch, seq, t.time, t.chunk)
    yb = _fox(qt, ka, vt, gate, batch, seq, t.time, t.fox_keys, FOX_HEADS)
    h = _mix_mlp([ya, yb], bf(ab_w_out[0]), h, row(norm_ffn_g[0]), bf(mlp_w_up[0]), bf(mlp_w_down[0]),
                 t.mlp_rows, t.ff_chunk, "mix_mlp0")

    outs = _rwkvproj(h, row(norm_mix_g[1]), rwkv_mu[0].astype(F32),
                     bf(rwkv_w_rkv[0, 0]), bf(rwkv_w_rkv[0, 1]), bf(rwkv_w_rkv[0, 2]),
                     bf(rwkv_w1[0]), bf(rwkv_w2[0]), bf(rwkv_a1[0]), bf(rwkv_a2[0]),
                     bf(rwkv_g1[0]), bf(rwkv_g2[0]), row(rwkv_w0[0]), row(rwkv_a0[0]),
                     row(rwkv_k_k[0]), row(rwkv_k_a[0]), seq, t.rows)
    z = _wkv(*outs, row(rwkv_r_k[0]), row(rwkv_lnx_g[0]), row(rwkv_lnx_b[0]),
             batch, seq, t.time, t.chunk)
    h = _mix_mlp([z], bf(rwkv_w_o[0]), h, row(norm_ffn_g[1]), bf(mlp_w_up[1]), bf(mlp_w_down[1]),
                 t.mlp_rows, t.ff_chunk, "mix_mlp1")
    return h.reshape(batch, seq, d)
```

```python
import functools
import math
from typing import NamedTuple

import jax
import jax.numpy as jnp
import numpy as np
from jax import lax
from jax.experimental import pallas as pl
from jax.experimental.pallas import tpu as pltpu

F32 = jnp.float32
BF16 = jnp.bfloat16

RMS_EPS = 1e-6
GN_EPS = 64e-5

HGRN_HEADS = 4
HGRN_DIM = 128
HGRN_SUB = 16
FOX_HEADS = 8
FOX_DIM = 64
FOX_VROWS = FOX_DIM + 16
RWKV_DIM = 64
RWKV_GROUP = 4
GROUP_LANES = RWKV_GROUP * RWKV_DIM
LANES = 128
VMEM_V7X_MIB = 64
VMEM_SMALL_MIB = VMEM_V7X_MIB // 2
VMEM_MEDIUM_MIB = 3 * VMEM_V7X_MIB // 4
VMEM_LARGE_MIB = 7 * VMEM_V7X_MIB // 8
NEG_BIG = -1e30
LOG2E = 1.4426950408889634

NT_DIMS = (((1,), (1,)), ((), ()))


class _Tiles(NamedTuple):
    rows: int
    mlp_rows: int
    time: int
    fox_keys: int
    chunk: int
    ff_chunk: int


def _tiles(seq):
    return _Tiles(rows=min(512, seq), mlp_rows=min(512, seq), time=min(512, seq),
                  fox_keys=min(256, seq), chunk=64, ff_chunk=1024)


def _cparams(sem, vmem_mb):
    return pltpu.CompilerParams(dimension_semantics=sem, vmem_limit_bytes=vmem_mb * 1024 * 1024)


def _dot(a, b):
    return jnp.dot(a, b, preferred_element_type=F32)


def _dot_nt(a, b):
    return lax.dot_general(a, b, NT_DIMS, preferred_element_type=F32)


def _rms(x, g):
    return x * lax.rsqrt(jnp.mean(x * x, axis=-1, keepdims=True) + RMS_EPS) * g


def _sigmoid(x):
    return 1.0 / (1.0 + jnp.exp(-x))


def _log_sigmoid(x):
    return jnp.minimum(x, 0.0) - jnp.log(1.0 + jnp.exp(-jnp.abs(x)))


def _tril_mask(n, strict=False):
    r = lax.broadcasted_iota(jnp.int32, (n, n), 0)
    c = lax.broadcasted_iota(jnp.int32, (n, n), 1)
    return (c < r) if strict else (c <= r)


def _split3(x):
    hi = x.astype(BF16)
    r1 = x - hi.astype(F32)
    mid = r1.astype(BF16)
    lo = (r1 - mid.astype(F32)).astype(BF16)
    return hi, mid, lo


def _cumsum_rows(x, tril_bf16, pieces=3):
    parts = _split3(x)[:pieces]
    out = _dot(tril_bf16, parts[0])
    for part in parts[1:]:
        out = out + _dot(tril_bf16, part)
    return out


def _hgrn_kernel(q_ref, f_ref, i_ref, g_ref, lb_ref, ng_ref, o_ref, st_ref, *, chunk):
    @pl.when(pl.program_id(1) == 0)
    def _():
        st_ref[...] = jnp.zeros_like(st_ref)

    ts = q_ref.shape[0]
    tril_b = jnp.where(_tril_mask(chunk), 1.0, 0.0).astype(BF16)

    hs = range(HGRN_HEADS)
    nch = ts // chunk
    tiles = [(c, h) for c in range(nch) for h in hs]
    blk = lambda ref, c, h: ref[c * chunk:(c + 1) * chunk, h * HGRN_DIM:(h + 1) * HGRN_DIM]
    lbs = [lb_ref[:, h * HGRN_DIM:(h + 1) * HGRN_DIM] for h in hs]

    f = [lbs[h] + (1.0 - lbs[h]) * _sigmoid(blk(f_ref, c, h)) for c, h in tiles]
    b = [_cumsum_rows(jnp.log(x), tril_b) for x in f]
    b_last = [x[chunk - 1:chunk, :] for x in b]
    q = [blk(q_ref, c, h) * _sigmoid(blk(q_ref, c, h)) for c, h in tiles]
    k = [1.0 - x for x in f]
    vb = [blk(i_ref, c, h).astype(BF16) for c, h in tiles]
    n = range(len(tiles))

    sub = HGRN_SUB
    tril_sub = _tril_mask(sub)
    score_cols = [[] for _ in n]
    for j in range(chunk // sub):
        lo, hi = j * sub, (j + 1) * sub
        for i in n:
            bj = b[i][lo:hi]
            b_mid = b[i][lo + sub // 2 - 1:lo + sub // 2]
            diag = _dot_nt((q[i][lo:hi] * jnp.exp(bj - b_mid)).astype(BF16),
                           (k[i][lo:hi] * jnp.exp(b_mid - bj)).astype(BF16))
            parts = [jnp.where(tril_sub, diag, 0.0)]
            if lo > 0:
                parts.insert(0, jnp.zeros((lo, sub), F32))
            if hi < chunk:
                b_end = b[i][hi - 1:hi]
                parts.append(_dot_nt((q[i][hi:] * jnp.exp(b[i][hi:] - b_end)).astype(BF16),
                                     (k[i][lo:hi] * jnp.exp(b_end - bj)).astype(BF16)))
            score_cols[i].append(jnp.concatenate(parts, axis=0).astype(BF16))
    o = []
    for i in n:
        acc = _dot(score_cols[i][0], vb[i][:sub])
        for j in range(1, chunk // sub):
            acc = acc + _dot(score_cols[i][j], vb[i][j * sub:(j + 1) * sub])
        o.append(acc)
    inc = [_dot(blk(i_ref, c, h).T.astype(BF16), (k[i] * jnp.exp(b_last[i] - b[i])).astype(BF16))
           for i, (c, h) in enumerate(tiles)]
    dec = [jnp.exp(x) for x in b_last]

    st = [st_ref[h] for h in hs]
    st_in = []
    for i, (c, h) in enumerate(tiles):
        st_in.append(st[h].astype(BF16))
        st[h] = st[h] * dec[i] + inc[i]
    for h in hs:
        st_ref[h] = st[h]

    for i, (c, h) in enumerate(tiles):
        oi = o[i] + _dot_nt((q[i] * jnp.exp(b[i])).astype(BF16), st_in[i])
        ag = blk(g_ref, c, h)
        on = _rms(oi, ng_ref[:, h * HGRN_DIM:(h + 1) * HGRN_DIM])
        o_ref[c * chunk:(c + 1) * chunk, h * HGRN_DIM:(h + 1) * HGRN_DIM] = (
            on * (ag * _sigmoid(ag))).astype(o_ref.dtype)


def _hgrn(proj, lb, ng, batch, seq, ts, chunk):
    m = proj.shape[0]
    w = HGRN_HEADS * HGRN_DIM
    nt = seq // ts
    spec = lambda j: pl.BlockSpec((ts, w), lambda b, t, j=j: (b * nt + t, j))
    vec = pl.BlockSpec((1, w), lambda b, t: (0, 0))
    return pl.pallas_call(
        functools.partial(_hgrn_kernel, chunk=chunk),
        grid=(batch, nt),
        in_specs=[spec(0), spec(1), spec(2), spec(3), vec, vec],
        out_specs=pl.BlockSpec((ts, w), lambda b, t: (b * nt + t, 0)),
        out_shape=jax.ShapeDtypeStruct((m, w), BF16),
        scratch_shapes=[pltpu.VMEM((HGRN_HEADS, HGRN_DIM, HGRN_DIM), F32)],
        compiler_params=_cparams(("parallel", "arbitrary"), VMEM_SMALL_MIB),
        name="hgrn2",
    )(proj, proj, proj, proj, lb, ng)


def _inproj_fox_kernel(x_ref, g_ref, w_ref, fb_ref, qg_ref, kg_ref, hsum_ref, wq_ref, wk_ref,
                       oq_ref, ok_ref, ha_ref, go_ref, qt_ref, ka_ref, vt_ref, carry_ref):
    @pl.when(pl.program_id(1) == 0)
    def _():
        carry_ref[...] = jnp.zeros_like(carry_ref)

    ts = x_ref.shape[0]
    pairs = FOX_HEADS // 2
    hn = _rms(x_ref[...], g_ref[...]).astype(BF16)
    n_hgrn = ha_ref.shape[1]
    w = FOX_HEADS * FOX_DIM
    part = lambda j: _dot(hn, w_ref[:, n_hgrn + j * w:n_hgrn + (j + 1) * w])
    ha_ref[...] = _dot(hn, w_ref[:, :n_hgrn])
    q, k, v = part(0), part(1), part(2)
    go_ref[...] = part(3)
    f = _dot(hn, w_ref[:, n_hgrn + 4 * w:])

    tril_b = jnp.where(_tril_mask(ts), 1.0, 0.0).astype(BF16)
    lf = _log_sigmoid(f + fb_ref[...])
    c = _cumsum_rows(lf, tril_b) + carry_ref[...]
    carry_ref[...] = c[ts - 1:ts, :]
    c2 = c * LOG2E
    hi = c2.astype(BF16).astype(F32)
    rest = c2 - hi
    mid = rest.astype(BF16).astype(F32)
    lane = lax.broadcasted_iota(jnp.int32, c.shape, 1)
    pieces = jnp.where(lane < FOX_HEADS, hi,
                       jnp.where(lane < 2 * FOX_HEADS, mid, rest - mid)).astype(BF16)

    def headnorm(x_all, g_ref, r, scale):
        cols = slice(r * 2 * LANES, (r + 1) * 2 * LANES)
        x = x_all[:, cols]
        xx = x * x
        xh = xx.astype(BF16)
        ss = _dot(xh, hsum_ref[...]) + _dot((xx - xh.astype(F32)).astype(BF16), hsum_ref[...])
        return (x * lax.rsqrt(ss * (1.0 / FOX_DIM) + RMS_EPS) * (g_ref[:, cols] * scale)).astype(BF16)

    qn = [headnorm(q, qg_ref, r, FOX_DIM ** -0.5 * LOG2E) for r in range(pairs // 2)]
    kn = [headnorm(k, kg_ref, r, 1.0) for r in range(pairs // 2)]
    for r in range(pairs):
        src = slice((r % 2) * LANES, (r % 2 + 1) * LANES)
        out = slice(r * 2 * LANES, (r + 1) * 2 * LANES)
        qa = _dot(jnp.concatenate([qn[r // 2][:, src], pieces], axis=1), wq_ref[r]) + oq_ref[:, out]
        ka = _dot(jnp.concatenate([kn[r // 2][:, src], pieces], axis=1), wk_ref[r]) + ok_ref[:, out]
        ka_ref[:, out] = ka.astype(BF16)
        for hh in range(2):
            qt_ref[0, 2 * r + hh] = qa[:, hh * LANES:(hh + 1) * LANES].T.astype(BF16)
        vt = v[:, r * LANES:(r + 1) * LANES].T
        extra = jnp.where(lax.broadcasted_iota(jnp.int32, (FOX_VROWS - FOX_DIM, ts), 0) == 0, 1.0, 0.0)
        for hh in range(2):
            vt_ref[0, 2 * r + hh, 0] = jnp.concatenate(
                [vt[hh * FOX_DIM:(hh + 1) * FOX_DIM], extra], axis=0).astype(BF16)


def _inproj_fox(x2, g, w_in, fb, qg, kg, batch, seq, ts):
    m, d = x2.shape
    w = FOX_HEADS * FOX_DIM
    n_hgrn = 4 * HGRN_HEADS * HGRN_DIM
    nt = seq // ts
    pairs = FOX_HEADS // 2
    const = lambda a: pl.BlockSpec(a.shape, lambda b, t: (0,) * a.ndim)
    rows = lambda width: pl.BlockSpec((ts, width), lambda b, t: (b * nt + t, 0))

    wa = FOX_HEADS * LANES
    ch = np.arange(2 * LANES)
    hsum = (ch[:, None] // FOX_DIM == ch[None, :] // FOX_DIM).astype(np.float32)
    wq = np.zeros((pairs, 2 * LANES, 2 * LANES), np.float32)
    wk = np.zeros((pairs, 2 * LANES, 2 * LANES), np.float32)
    oq = np.zeros((1, wa), np.float32)
    ok = np.zeros((1, wa), np.float32)
    for r in range(pairs):
        for hh in range(2):
            h = 2 * r + hh
            ch = np.arange(FOX_DIM)
            wq[r, hh * FOX_DIM + ch, hh * LANES + ch] = 1.0
            wk[r, hh * FOX_DIM + ch, hh * LANES + ch] = 1.0
            for p in range(3):
                wq[r, LANES + p * FOX_HEADS + h, hh * LANES + FOX_DIM + p] = 1.0
                wk[r, LANES + p * FOX_HEADS + h, hh * LANES + FOX_DIM + 3 + p] = -1.0
                oq[0, h * LANES + FOX_DIM + 3 + p] = 1.0
                ok[0, h * LANES + FOX_DIM + p] = 1.0
    consts = [jnp.asarray(a, BF16) for a in (hsum, wq, wk)] + [jnp.asarray(oq), jnp.asarray(ok)]
    qg = jnp.tile(qg, (1, FOX_HEADS))
    kg = jnp.tile(kg, (1, FOX_HEADS))
    return pl.pallas_call(
        _inproj_fox_kernel,
        grid=(batch, nt),
        in_specs=[rows(d), const(g),
                  pl.BlockSpec(w_in.shape, lambda b, t: (0, 0), pipeline_mode=pl.Buffered(1)),
                  const(fb), const(qg), const(kg)] + [const(a) for a in consts],
        out_specs=[rows(n_hgrn), rows(w),
                   pl.BlockSpec((1, FOX_HEADS, LANES, ts), lambda b, t: (b, 0, 0, t)),
                   rows(FOX_HEADS * LANES),
                   pl.BlockSpec((1, FOX_HEADS, 1, FOX_VROWS, ts), lambda b, t: (b, 0, t, 0, 0))],
        out_shape=[jax.ShapeDtypeStruct((m, n_hgrn), F32),
                   jax.ShapeDtypeStruct((m, w), F32),
                   jax.ShapeDtypeStruct((batch, FOX_HEADS, LANES, seq), BF16),
                   jax.ShapeDtypeStruct((m, FOX_HEADS * LANES), BF16),
                   jax.ShapeDtypeStruct((batch, FOX_HEADS, nt, FOX_VROWS, ts), BF16)],
        scratch_shapes=[pltpu.VMEM((1, LANES), F32)],
        compiler_params=_cparams(("parallel", "arbitrary"), VMEM_MEDIUM_MIB),
        name="inproj_fox",
    )(x2, g, w_in, fb, qg, kg, *consts)


def _fox_kernel(qt_ref, k_ref, vt_ref, g_ref, o_ref, *, tq, tk, heads):
    i = pl.program_id(2)
    hs = range(heads)
    ratio = tq // tk

    def step(j, carry, q0):
        ms, accs = carry
        diagonal = q0 is not None
        q0 = q0 or 0
        nq = tq - q0
        nkeys = tk if diagonal else tq
        rows = pl.ds(pl.multiple_of(j * tq, tq) + q0, nkeys)
        keys = slice(q0, q0 + nkeys)
        if diagonal:
            visible = (lax.broadcasted_iota(jnp.int32, (tk, nq), 1)
                       >= lax.broadcasted_iota(jnp.int32, (tk, nq), 0))

        def scores(h):
            s = _dot(k_ref[rows, h * LANES:(h + 1) * LANES], qt_ref[0, h, :, q0:])
            return jnp.where(visible, s, NEG_BIG) if diagonal else s

        def softmax(h, s):
            m_old = ms[h][:, q0:]
            m_new = jnp.maximum(m_old, jnp.max(s, axis=0, keepdims=True))
            return m_new, jnp.exp2(m_old - m_new), jnp.exp2(s - m_new).astype(BF16)

        def values(h, alpha, p):
            return accs[h][:, q0:] * alpha + _dot(vt_ref[0, h, j, :, keys], p)

        s, sm, out = {}, {}, {}
        for t in range(heads + 2):
            if t < heads:
                s[t] = scores(t)
            if 0 <= t - 1 < heads:
                sm[t - 1] = softmax(t - 1, s[t - 1])
            if 0 <= t - 2 < heads:
                out[t - 2] = values(t - 2, sm[t - 2][1], sm[t - 2][2])
        keep = lambda old, new: new if q0 == 0 else jnp.concatenate([old[:, :q0], new], axis=1)
        return (tuple(keep(ms[h], sm[h][0]) for h in hs), tuple(keep(accs[h], out[h]) for h in hs))

    neg = jnp.full((1, tq), NEG_BIG, F32)
    carry = ((neg,) * heads, (jnp.zeros((FOX_VROWS, tq), F32),) * heads)
    carry = lax.fori_loop(0, i, lambda j, c: step(j, c, None), carry)
    for d in range(ratio):
        carry = step(i, carry, d * tk)
    _, accs = carry
    norm = [a[:FOX_DIM] / a[FOX_DIM:FOX_DIM + 1] for a in accs]
    for r in range(heads // 2):
        cols = slice(r * LANES, (r + 1) * LANES)
        out = jnp.concatenate([norm[2 * r], norm[2 * r + 1]], axis=0)
        o_ref[:, cols] = (out.T * _sigmoid(g_ref[:, cols])).astype(o_ref.dtype)


def _fox(qt, ka, vt, gate, batch, seq, tq, tk, heads):
    m = ka.shape[0]
    nq = seq // tq
    groups = FOX_HEADS // heads
    wv = heads * FOX_DIM
    return pl.pallas_call(
        functools.partial(_fox_kernel, tq=tq, tk=tk, heads=heads),
        grid=(batch, groups, nq),
        in_specs=[pl.BlockSpec((1, heads, LANES, tq), lambda b, p, i: (b, p, 0, i)),
                  pl.BlockSpec((seq, heads * LANES), lambda b, p, i: (b, p)),
                  pl.BlockSpec((1, heads, nq, FOX_VROWS, tq), lambda b, p, i: (b, p, 0, 0, 0)),
                  pl.BlockSpec((tq, wv), lambda b, p, i: (b * nq + i, p))],
        out_specs=pl.BlockSpec((tq, wv), lambda b, p, i: (b * nq + i, p)),
        out_shape=jax.ShapeDtypeStruct((m, FOX_HEADS * FOX_DIM), BF16),
        compiler_params=_cparams(("parallel", "parallel", "arbitrary"), VMEM_MEDIUM_MIB),
        name="fox_attention",
    )(qt, ka, vt, gate)


def _mix_mlp_kernel(*refs, n_mix, ck):
    ys = refs[:n_mix]
    w_ref, h_ref, g_ref, wu_ref, wd_ref, o_ref = refs[n_mix:]
    y = ys[0][...] if n_mix == 1 else jnp.concatenate([r[...] for r in ys], axis=1)
    x = h_ref[...] + _dot(y, w_ref[...])
    hn = _rms(x, g_ref[...]).astype(BF16)
    acc = x
    for c in range(wu_ref.shape[1] // ck):
        u = jnp.maximum(_dot(hn, wu_ref[:, c * ck:(c + 1) * ck]), 0.0)
        acc = acc + _dot((u * u).astype(BF16), wd_ref[c * ck:(c + 1) * ck, :])
    o_ref[...] = acc


def _mix_mlp(ys, w, h, g, wu, wd, tm, ck, name):
    m, d = h.shape
    const = lambda a: pl.BlockSpec(a.shape, lambda i: (0, 0), pipeline_mode=pl.Buffered(1))
    return pl.pallas_call(
        functools.partial(_mix_mlp_kernel, n_mix=len(ys), ck=ck),
        grid=(m // tm,),
        in_specs=([pl.BlockSpec((tm, y.shape[1]), lambda i: (i, 0)) for y in ys]
                  + [const(w), pl.BlockSpec((tm, d), lambda i: (i, 0)), const(g), const(wu), const(wd)]),
        out_specs=pl.BlockSpec((tm, d), lambda i: (i, 0)),
        out_shape=jax.ShapeDtypeStruct((m, d), F32),
        compiler_params=_cparams(("parallel",), VMEM_LARGE_MIB),
        name=name,
    )(*ys, w, h, g, wu, wd)


def _rwkvproj_kernel(h_ref, hp_ref, g_ref, mu_ref, wr_ref, wk_ref, wv_ref, w1_ref, w2_ref,
                     a1_ref, a2_ref, g1_ref, g2_ref, w0_ref, a0_ref, kk_ref, ka_ref,
                     r_ref, lw_ref, km_ref, v_ref, kr_ref, a_ref, go_ref, *, tiles_per_seq):
    i = pl.program_id(0)
    tm = h_ref.shape[0]
    gn = g_ref[...]
    hn = _rms(h_ref[...], gn)
    prev = _rms(hp_ref[7:8, :], gn)
    prev = jnp.where(i % tiles_per_seq == 0, jnp.zeros_like(prev), prev)
    row = lax.broadcasted_iota(jnp.int32, hn.shape, 0)
    shifted = jnp.where(row == 0, jnp.broadcast_to(prev, hn.shape), pltpu.roll(hn, 1, 0))
    xx = shifted - hn
    hn_b = hn.astype(BF16)
    xx_b = xx.astype(BF16)
    mix = lambda j: hn_b + xx_b * mu_ref[j:j + 1, :].astype(BF16)
    r = _dot(mix(0), wr_ref[...])
    k = _dot(mix(2), wk_ref[...])
    v = _dot(mix(3), wv_ref[...])
    z = w0_ref[...] + _dot(jnp.tanh(_dot(mix(1), w1_ref[...])).astype(BF16), w2_ref[...])
    a = _sigmoid(a0_ref[...] + _dot(_dot(mix(4), a1_ref[...]).astype(BF16), a2_ref[...]))
    g = _dot(_sigmoid(_dot(mix(5), g1_ref[...])).astype(BF16), g2_ref[...])
    r_ref[...] = r
    lw_ref[...] = _sigmoid(z) * (-math.exp(-0.5))
    km_ref[...] = k * (1.0 + (a - 1.0) * ka_ref[...])
    v_ref[...] = v
    kr_ref[...] = k * kk_ref[...]
    a_ref[...] = a
    go_ref[...] = g


def _rwkvproj(h, g, mu, wr, wk, wv, w1, w2, a1, a2, g1, g2, w0, a0, k_k, k_a, seq, tm):
    m, d = h.shape
    tiles_per_seq = seq // tm
    full = lambda a: pl.BlockSpec(a.shape, lambda i: (0,) * a.ndim, pipeline_mode=pl.Buffered(1))
    row = pl.BlockSpec((tm, d), lambda i: (i, 0))
    prev = pl.BlockSpec((8, d), lambda i: (jnp.maximum(i * (tm // 8) - 1, 0), 0))
    consts = (g, mu, wr, wk, wv, w1, w2, a1, a2, g1, g2, w0, a0, k_k, k_a)
    return pl.pallas_call(
        functools.partial(_rwkvproj_kernel, tiles_per_seq=tiles_per_seq),
        grid=(m // tm,),
        in_specs=[row, prev] + [full(a) for a in consts],
        out_specs=[row] * 7,
        out_shape=[jax.ShapeDtypeStruct((m, d), F32)] * 7,
        compiler_params=_cparams(("parallel",), VMEM_LARGE_MIB),
        name="rwkv_proj",
    )(h, h, *consts)


def _wkv_kernel(r_ref, lw_ref, km_ref, v_ref, kr_ref, a_ref, g_ref, rk_ref, lg_ref, lb_ref,
                o_ref, st_ref, y_ref, q1_ref, lhs_ref, z0_ref, wc_ref, bonus_ref, gate_ref,
                *, chunk, groups):
    grp = pl.program_id(2)

    @pl.when(pl.program_id(1) == 0)
    def _():
        st_ref[grp] = jnp.zeros(st_ref.shape[1:], F32)

    ts = r_ref.shape[0]
    nch = ts // chunk
    gl = GROUP_LANES
    rb = lax.broadcasted_iota(jnp.int32, (gl, gl), 0) // RWKV_DIM
    cb = lax.broadcasted_iota(jnp.int32, (gl, gl), 1) // RWKV_DIM
    blockmask = rb == cb
    ones_bd = jnp.where(blockmask, 1.0, 0.0).astype(BF16)

    def headsum(x, pieces=2):
        hi = x.astype(BF16)
        out = _dot(hi, ones_bd)
        if pieces == 2:
            out = out + _dot((x - hi.astype(F32)).astype(BF16), ones_bd)
        return out

    def bd(y):
        reps = gl // y.shape[0]
        return jnp.where(blockmask, jnp.concatenate([y] * reps, axis=0), 0.0).astype(BF16)

    def hmm(x, y):
        return _dot(x.astype(BF16), bd(y))

    def tn_blocks(x, y):
        return jnp.where(blockmask, _dot(x.T.astype(BF16), y.astype(BF16)), 0.0)

    t_idx = lax.broadcasted_iota(jnp.int32, (chunk, gl), 0)
    s_idx = lax.broadcasted_iota(jnp.int32, (chunk, gl), 1) % RWKV_DIM
    strict = s_idx < t_idx
    incl = s_idx <= t_idx
    tril_b = jnp.where(_tril_mask(chunk), 1.0, 0.0).astype(BF16)
    zeros_c = jnp.zeros((chunk, gl), F32)

    chunks = range(nch)
    blk = lambda ref, c: ref[c * chunk:(c + 1) * chunk, :]
    pad = lambda x: jnp.concatenate([x, zeros_c], axis=0)
    each = lambda fn, *lists: [fn(*args) for args in zip(*lists)]

    kr = kr_ref[...]
    kkn_all = kr * lax.rsqrt(jnp.maximum(headsum(kr * kr), 1e-24))

    lw = [blk(lw_ref, c) for c in chunks]
    cum = each(lambda x: _cumsum_rows(x, tril_b, pieces=2), lw)
    c_last = [x[chunk - 1:chunk, :] for x in cum]
    kkn = [kkn_all[c * chunk:(c + 1) * chunk, :] for c in chunks]
    kka = [kkn[c] * blk(a_ref, c) for c in chunks]
    km = [blk(km_ref, c) for c in chunks]
    v = [blk(v_ref, c) for c in chunks]
    e_neg = [jnp.exp(-x) for x in cum]
    e_end = each(lambda cl, x: jnp.exp(cl - x), c_last, cum)
    at = each(lambda k, x, l: -k * jnp.exp(x - l), kkn, cum, lw)
    bt = each(jnp.multiply, kka, e_neg)
    kt = each(jnp.multiply, km, e_neg)
    rt = [blk(r_ref, c) * jnp.exp(cum[c]) for c in chunks]
    bw = each(jnp.multiply, kka, e_end)
    kw = each(jnp.multiply, km, e_end)
    first_head = lax.broadcasted_iota(jnp.int32, (RWKV_DIM, LANES), 1) < RWKV_DIM
    for c in chunks:
        wt = jnp.broadcast_to(jnp.exp(c_last[c]), (LANES, gl)).T
        wc_ref[grp, c] = jnp.concatenate(
            [jnp.where(first_head, wt[2 * p * RWKV_DIM:(2 * p + 1) * RWKV_DIM],
                       wt[(2 * p + 1) * RWKV_DIM:(2 * p + 2) * RWKV_DIM])
             for p in range(RWKV_GROUP // 2)], axis=1)

    lhs = each(lambda a, r: jnp.concatenate([a, r], axis=0).astype(BF16), at, rt)
    pb = each(lambda l, b: _dot_nt(l, bd(b)), lhs, bt)
    pk = each(lambda l, k: _dot_nt(l, bd(k)), lhs, kt)
    a_ab = [jnp.where(strict, x[:chunk], 0.0) for x in pb]
    a_rb = [jnp.where(incl, x[chunk:], 0.0) for x in pb]
    a_ak = [jnp.where(strict, x[:chunk], 0.0) for x in pk]
    a_rk = [jnp.where(incl, x[chunk:], 0.0) for x in pk]

    e = [jnp.where((t_idx % 2 == 1) & (s_idx == t_idx - 1), x, 0.0) for x in a_ab]
    size = 2
    while size < chunk:
        off = ((t_idx // size) % 2 == 1) & (s_idx // size == t_idx // size - 1)
        a_off = [jnp.where(off, x, 0.0) for x in a_ab]
        t1 = each(lambda ao, ee: ao + hmm(ao, ee), a_off, e)
        e = each(lambda ee, tt: ee + tt + hmm(ee, tt), e, t1)
        size *= 2

    av = each(lambda ak, rk, vv: hmm(jnp.concatenate([ak, rk], axis=0), vv), a_ak, a_rk, v)
    akv = [x[:chunk] for x in av]
    p1 = each(lambda x, ee: x + hmm(ee, x), akv, e)
    mat = each(lambda x, ee: x + hmm(ee, x), at, e)
    q1 = each(lambda x, arb, pp: x[chunk:] + hmm(arb, pp), av, a_rb, p1)
    r2 = each(lambda r, arb, mm: r + hmm(arb, mm), rt, a_rb, mat)
    pct = each(lambda b, mm: tn_blocks(pad(b), pad(mm)), bw, mat)
    z0 = each(lambda b, k, pp, vv: tn_blocks(jnp.concatenate([b, k], axis=0),
                                             jnp.concatenate([pp, vv], axis=0)), bw, kw, p1, v)
    def fold(x):
        out = x[:RWKV_DIM]
        for h in range(1, RWKV_GROUP):
            out = out + x[h * RWKV_DIM:(h + 1) * RWKV_DIM]
        return out

    for c in chunks:
        q1_ref[grp, c * chunk:(c + 1) * chunk, :] = q1[c]
        lhs_ref[grp, c, :chunk, :] = r2[c].astype(BF16)
        lhs_ref[grp, c, chunk:, :] = fold(pct[c]).astype(BF16)
        z0_ref[grp, c] = fold(z0[c])
    bonus_ref[grp] = headsum(r_ref[...] * km_ref[...] * rk_ref[...], pieces=1) * v_ref[...]
    gate_ref[grp] = g_ref[...]

    @pl.when(grp == groups - 1)
    def _():
        gs = range(groups)

        def body(c, carry):
            rows = pl.ds(pl.multiple_of(c * chunk, chunk), chunk)
            st = [st_ref[j] for j in gs]
            res = [_dot(lhs_ref[j, c], bd(st[j])) for j in gs]
            for j in gs:
                y_ref[j, rows, :] = q1_ref[j, rows, :] + res[j][:chunk]
                st_ref[j] = st[j] * wc_ref[j, c] + res[j][chunk:] + z0_ref[j, c]
            return carry

        lax.fori_loop(0, nch, body, 0)

        inv_n = 1.0 / RWKV_DIM
        for j in gs:
            cols = slice(j * gl, (j + 1) * gl)
            y = y_ref[j]
            mean = headsum(y, pieces=1) * inv_n
            dlt = y - mean
            var = headsum(dlt * dlt, pieces=1) * inv_n
            yn = dlt * lax.rsqrt(var + GN_EPS) * lg_ref[:, cols] + lb_ref[:, cols]
            o_ref[:, cols] = ((yn + bonus_ref[j]) * gate_ref[j]).astype(o_ref.dtype)


def _wkv(r, lw, km, v, kr, a, g, r_k, lnx_g, lnx_b, batch, seq, ts, chunk):
    m, d = r.shape
    gl = GROUP_LANES
    groups = d // gl
    nt = seq // ts
    nch = ts // chunk
    row = pl.BlockSpec((ts, gl), lambda b, t, j: (b * nt + t, j))
    vec = pl.BlockSpec((1, gl), lambda b, t, j: (0, j))
    full = pl.BlockSpec((1, d), lambda b, t, j: (0, 0))
    return pl.pallas_call(
        functools.partial(_wkv_kernel, chunk=chunk, groups=groups),
        grid=(batch, nt, groups),
        in_specs=[row] * 7 + [vec, full, full],
        out_specs=pl.BlockSpec((ts, d), lambda b, t, j: (b * nt + t, 0)),
        out_shape=jax.ShapeDtypeStruct((m, d), BF16),
        scratch_shapes=[pltpu.VMEM((groups, RWKV_DIM, gl), F32),
                        pltpu.VMEM((groups, ts, gl), F32),
                        pltpu.VMEM((groups, ts, gl), F32),
                        pltpu.VMEM((groups, nch, chunk + RWKV_DIM, gl), BF16),
                        pltpu.VMEM((groups, nch, RWKV_DIM, gl), F32),
                        pltpu.VMEM((groups, nch, RWKV_DIM, gl), F32),
                        pltpu.VMEM((groups, ts, gl), F32),
                        pltpu.VMEM((groups, ts, gl), F32)],
        compiler_params=_cparams(("parallel", "arbitrary", "arbitrary"), VMEM_LARGE_MIB),
        name="wkv7",
    )(r, lw, km, v, kr, a, g, r_k, lnx_g, lnx_b)


def kernel(x, norm_mix_g, norm_ffn_g, ab_w_in, hgrn_lower_bounds, hgrn_norm_g, fox_forget_bias,
           fox_q_norm_g, fox_k_norm_g, ab_w_out, rwkv_mu, rwkv_w_rkv, rwkv_w0, rwkv_w1, rwkv_w2,
           rwkv_a0, rwkv_a1, rwkv_a2, rwkv_g1, rwkv_g2, rwkv_k_k, rwkv_k_a, rwkv_r_k,
           rwkv_lnx_g, rwkv_lnx_b, rwkv_w_o, mlp_w_up, mlp_w_down):
    batch, seq, d = x.shape
    m = batch * seq
    t = _tiles(seq)
    assert d == 2 * HGRN_HEADS * HGRN_DIM == 2 * FOX_HEADS * FOX_DIM and d % GROUP_LANES == 0
    assert ab_w_in.shape == (1, d, 4 * (HGRN_HEADS * HGRN_DIM + FOX_HEADS * FOX_DIM) + FOX_HEADS)
    assert seq % t.time == 0 and t.time % t.chunk == 0 and t.time % t.fox_keys == 0
    assert seq % t.rows == 0 and seq % t.mlp_rows == 0 and mlp_w_up.shape[-1] % t.ff_chunk == 0
    row = lambda a: a.reshape(1, -1).astype(F32)
    bf = lambda a: a.astype(BF16)

    lb_all = jnp.cumsum(jax.nn.softmax(hgrn_lower_bounds.astype(F32), axis=0), axis=0)
    h = x.reshape(m, d)

    n_wide = ab_w_in.shape[-1] - FOX_HEADS
    gate_w = jnp.tile(ab_w_in[0][:, n_wide:], (1, 3))
    w_in = bf(jnp.pad(jnp.concatenate([ab_w_in[0][:, :n_wide], gate_w], axis=1),
                      ((0, 0), (0, LANES - 3 * FOX_HEADS))))
    fb = jnp.pad(jnp.tile(row(fox_forget_bias[0]), (1, 3)), ((0, 0), (0, LANES - 3 * FOX_HEADS)))
    ha, gate, qt, ka, vt = _inproj_fox(h, row(norm_mix_g[0]), w_in, fb, row(fox_q_norm_g[0]),
                                       row(fox_k_norm_g[0]), batch, seq, t.time)
    ya = _hgrn(ha, row(lb_all[0]), row(hgrn_norm_g[0]), batch, seq, t.time, t.chunk)
    yb = _fox(qt, ka, vt, gate, batch, seq, t.time, t.fox_keys, FOX_HEADS)
    h = _mix_mlp([ya, yb], bf(ab_w_out[0]), h, row(norm_ffn_g[0]), bf(mlp_w_up[0]), bf(mlp_w_down[0]),
                 t.mlp_rows, t.ff_chunk, "mix_mlp0")

    outs = _rwkvproj(h, row(norm_mix_g[1]), rwkv_mu[0].astype(F32),
                     bf(rwkv_w_rkv[0, 0]), bf(rwkv_w_rkv[0, 1]), bf(rwkv_w_rkv[0, 2]),
                     bf(rwkv_w1[0]), bf(rwkv_w2[0]), bf(rwkv_a1[0]), bf(rwkv_a2[0]),
                     bf(rwkv_g1[0]), bf(rwkv_g2[0]), row(rwkv_w0[0]), row(rwkv_a0[0]),
                     row(rwkv_k_k[0]), row(rwkv_k_a[0]), seq, t.rows)
    z = _wkv(*outs, row(rwkv_r_k[0]), row(rwkv_lnx_g[0]), row(rwkv_lnx_b[0]),
             batch, seq, t.time, t.chunk)
    h = _mix_mlp([z], bf(rwkv_w_o[0]), h, row(norm_ffn_g[1]), bf(mlp_w_up[1]), bf(mlp_w_down[1]),
                 t.mlp_rows, t.ff_chunk, "mix_mlp1")
    return h.reshape(batch, seq, d)
```

```python
import functools
import math
from typing import NamedTuple

import jax
import jax.numpy as jnp
import numpy as np
from jax import lax
from jax.experimental import pallas as pl
from jax.experimental.pallas import tpu as pltpu

F32 = jnp.float32
BF16 = jnp.bfloat16

RMS_EPS = 1e-6
GN_EPS = 64e-5

HGRN_HEADS = 4
HGRN_DIM = 128
HGRN_SUB = 16
FOX_HEADS = 8
FOX_DIM = 64
FOX_VROWS = FOX_DIM + 16
RWKV_DIM = 64
RWKV_GROUP = 4
GROUP_LANES = RWKV_GROUP * RWKV_DIM
LANES = 128
VMEM_V7X_MIB = 64
VMEM_SMALL_MIB = VMEM_V7X_MIB // 2
VMEM_MEDIUM_MIB = 3 * VMEM_V7X_MIB // 4
VMEM_LARGE_MIB = 7 * VMEM_V7X_MIB // 8
NEG_BIG = -1e30
LOG2E = 1.4426950408889634

NT_DIMS = (((1,), (1,)), ((), ()))


class _Tiles(NamedTuple):
    rows: int
    mlp_rows: int
    time: int
    fox_keys: int
    chunk: int
    ff_chunk: int


def _tiles(seq):
    return _Tiles(rows=min(512, seq), mlp_rows=min(512, seq), time=min(512, seq),
                  fox_keys=min(256, seq), chunk=64, ff_chunk=1024)


def _cparams(sem, vmem_mb):
    return pltpu.CompilerParams(dimension_semantics=sem, vmem_limit_bytes=vmem_mb * 1024 * 1024)


def _dot(a, b):
    return jnp.dot(a, b, preferred_element_type=F32)


def _dot_nt(a, b):
    return lax.dot_general(a, b, NT_DIMS, preferred_element_type=F32)


def _rms(x, g):
    return x * lax.rsqrt(jnp.mean(x * x, axis=-1, keepdims=True) + RMS_EPS) * g


def _sigmoid(x):
    return 1.0 / (1.0 + jnp.exp(-x))


def _log_sigmoid(x):
    return jnp.minimum(x, 0.0) - jnp.log(1.0 + jnp.exp(-jnp.abs(x)))


def _tril_mask(n, strict=False):
    r = lax.broadcasted_iota(jnp.int32, (n, n), 0)
    c = lax.broadcasted_iota(jnp.int32, (n, n), 1)
    return (c < r) if strict else (c <= r)


def _split3(x):
    hi = x.astype(BF16)
    r1 = x - hi.astype(F32)
    mid = r1.astype(BF16)
    lo = (r1 - mid.astype(F32)).astype(BF16)
    return hi, mid, lo


def _cumsum_rows(x, tril_bf16, pieces=3):
    parts = _split3(x)[:pieces]
    out = _dot(tril_bf16, parts[0])
    for part in parts[1:]:
        out = out + _dot(tril_bf16, part)
    return out


def _hgrn_kernel(q_ref, f_ref, i_ref, g_ref, lb_ref, ng_ref, o_ref, st_ref, *, chunk):
    @pl.when(pl.program_id(1) == 0)
    def _():
        st_ref[...] = jnp.zeros_like(st_ref)

    ts = q_ref.shape[0]
    tril_b = jnp.where(_tril_mask(chunk), 1.0, 0.0).astype(BF16)

    hs = range(HGRN_HEADS)
    nch = ts // chunk
    tiles = [(c, h) for c in range(nch) for h in hs]
    blk = lambda ref, c, h: ref[c * chunk:(c + 1) * chunk, h * HGRN_DIM:(h + 1) * HGRN_DIM]
    lbs = [lb_ref[:, h * HGRN_DIM:(h + 1) * HGRN_DIM] for h in hs]

    f = [lbs[h] + (1.0 - lbs[h]) * _sigmoid(blk(f_ref, c, h)) for c, h in tiles]
    b = [_cumsum_rows(jnp.log(x), tril_b) for x in f]
    b_last = [x[chunk - 1:chunk, :] for x in b]
    q = [blk(q_ref, c, h) * _sigmoid(blk(q_ref, c, h)) for c, h in tiles]
    k = [1.0 - x for x in f]
    vb = [blk(i_ref, c, h).astype(BF16) for c, h in tiles]
    n = range(len(tiles))

    sub = HGRN_SUB
    tril_sub = _tril_mask(sub)
    score_cols = [[] for _ in n]
    for j in range(chunk // sub):
        lo, hi = j * sub, (j + 1) * sub
        for i in n:
            bj = b[i][lo:hi]
            b_mid = b[i][lo + sub // 2 - 1:lo + sub // 2]
            diag = _dot_nt((q[i][lo:hi] * jnp.exp(bj - b_mid)).astype(BF16),
                           (k[i][lo:hi] * jnp.exp(b_mid - bj)).astype(BF16))
            parts = [jnp.where(tril_sub, diag, 0.0)]
            if lo > 0:
                parts.insert(0, jnp.zeros((lo, sub), F32))
            if hi < chunk:
                b_end = b[i][hi - 1:hi]
                parts.append(_dot_nt((q[i][hi:] * jnp.exp(b[i][hi:] - b_end)).astype(BF16),
                                     (k[i][lo:hi] * jnp.exp(b_end - bj)).astype(BF16)))
            score_cols[i].append(jnp.concatenate(parts, axis=0).astype(BF16))
    o = []
    for i in n:
        acc = _dot(score_cols[i][0], vb[i][:sub])
        for j in range(1, chunk // sub):
            acc = acc + _dot(score_cols[i][j], vb[i][j * sub:(j + 1) * sub])
        o.append(acc)
    inc = [_dot(blk(i_ref, c, h).T.astype(BF16), (k[i] * jnp.exp(b_last[i] - b[i])).astype(BF16))
           for i, (c, h) in enumerate(tiles)]
    dec = [jnp.exp(x) for x in b_last]

    st = [st_ref[h] for h in hs]
    st_in = []
    for i, (c, h) in enumerate(tiles):
        st_in.append(st[h].astype(BF16))
        st[h] = st[h] * dec[i] + inc[i]
    for h in hs:
        st_ref[h] = st[h]

    for i, (c, h) in enumerate(tiles):
        oi = o[i] + _dot_nt((q[i] * jnp.exp(b[i])).astype(BF16), st_in[i])
        ag = blk(g_ref, c, h)
        on = _rms(oi, ng_ref[:, h * HGRN_DIM:(h + 1) * HGRN_DIM])
        o_ref[c * chunk:(c + 1) * chunk, h * HGRN_DIM:(h + 1) * HGRN_DIM] = (
            on * (ag * _sigmoid(ag))).astype(o_ref.dtype)


def _hgrn(proj, lb, ng, batch, seq, ts, chunk):
    m = proj.shape[0]
    w = HGRN_HEADS * HGRN_DIM
    nt = seq // ts
    spec = lambda j: pl.BlockSpec((ts, w), lambda b, t, j=j: (b * nt + t, j))
    vec = pl.BlockSpec((1, w), lambda b, t: (0, 0))
    return pl.pallas_call(
        functools.partial(_hgrn_kernel, chunk=chunk),
        grid=(batch, nt),
        in_specs=[spec(0), spec(1), spec(2), spec(3), vec, vec],
        out_specs=pl.BlockSpec((ts, w), lambda b, t: (b * nt + t, 0)),
        out_shape=jax.ShapeDtypeStruct((m, w), BF16),
        scratch_shapes=[pltpu.VMEM((HGRN_HEADS, HGRN_DIM, HGRN_DIM), F32)],
        compiler_params=_cparams(("parallel", "arbitrary"), VMEM_SMALL_MIB),
        name="hgrn2",
    )(proj, proj, proj, proj, lb, ng)


def _inproj_fox_kernel(x_ref, g_ref, w_ref, fb_ref, qg_ref, kg_ref, hsum_ref, wq_ref, wk_ref,
                       oq_ref, ok_ref, ha_ref, go_ref, qt_ref, ka_ref, vt_ref, carry_ref):
    @pl.when(pl.program_id(1) == 0)
    def _():
        carry_ref[...] = jnp.zeros_like(carry_ref)

    ts = x_ref.shape[0]
    pairs = FOX_HEADS // 2
    hn = _rms(x_ref[...], g_ref[...]).astype(BF16)
    n_hgrn = ha_ref.shape[1]
    w = FOX_HEADS * FOX_DIM
    part = lambda j: _dot(hn, w_ref[:, n_hgrn + j * w:n_hgrn + (j + 1) * w])
    ha_ref[...] = _dot(hn, w_ref[:, :n_hgrn])
    q, k, v = part(0), part(1), part(2)
    go_ref[...] = part(3)
    f = _dot(hn, w_ref[:, n_hgrn + 4 * w:])

    tril_b = jnp.where(_tril_mask(ts), 1.0, 0.0).astype(BF16)
    lf = _log_sigmoid(f + fb_ref[...])
    c = _cumsum_rows(lf, tril_b) + carry_ref[...]
    carry_ref[...] = c[ts - 1:ts, :]
    c2 = c * LOG2E
    hi = c2.astype(BF16).astype(F32)
    rest = c2 - hi
    mid = rest.astype(BF16).astype(F32)
    lane = lax.broadcasted_iota(jnp.int32, c.shape, 1)
    pieces = jnp.where(lane < FOX_HEADS, hi,
                       jnp.where(lane < 2 * FOX_HEADS, mid, rest - mid)).astype(BF16)

    def headnorm(x_all, g_ref, r, scale):
        cols = slice(r * 2 * LANES, (r + 1) * 2 * LANES)
        x = x_all[:, cols]
        xx = x * x
        ss = _dot(xx.astype(BF16), hsum_ref[...])
        return (x * lax.rsqrt(ss * (1.0 / FOX_DIM) + RMS_EPS) * (g_ref[:, cols] * scale)).astype(BF16)

    qn = [headnorm(q, qg_ref, r, FOX_DIM ** -0.5 * LOG2E) for r in range(pairs // 2)]
    kn = [headnorm(k, kg_ref, r, 1.0) for r in range(pairs // 2)]
    for r in range(pairs):
        src = slice((r % 2) * LANES, (r % 2 + 1) * LANES)
        out = slice(r * 2 * LANES, (r + 1) * 2 * LANES)
        qa = _dot(jnp.concatenate([qn[r // 2][:, src], pieces], axis=1), wq_ref[r]) + oq_ref[:, out]
        ka = _dot(jnp.concatenate([kn[r // 2][:, src], pieces], axis=1), wk_ref[r]) + ok_ref[:, out]
        ka_ref[:, out] = ka.astype(BF16)
        for hh in range(2):
            qt_ref[0, 2 * r + hh] = qa[:, hh * LANES:(hh + 1) * LANES].T.astype(BF16)
        vt = v[:, r * LANES:(r + 1) * LANES].T
        extra = jnp.where(lax.broadcasted_iota(jnp.int32, (FOX_VROWS - FOX_DIM, ts), 0) == 0, 1.0, 0.0)
        for hh in range(2):
            vt_ref[0, 2 * r + hh, 0] = jnp.concatenate(
                [vt[hh * FOX_DIM:(hh + 1) * FOX_DIM], extra], axis=0).astype(BF16)


def _inproj_fox(x2, g, w_in, fb, qg, kg, batch, seq, ts):
    m, d = x2.shape
    w = FOX_HEADS * FOX_DIM
    n_hgrn = 4 * HGRN_HEADS * HGRN_DIM
    nt = seq // ts
    pairs = FOX_HEADS // 2
    const = lambda a: pl.BlockSpec(a.shape, lambda b, t: (0,) * a.ndim)
    rows = lambda width: pl.BlockSpec((ts, width), lambda b, t: (b * nt + t, 0))

    wa = FOX_HEADS * LANES
    ch = np.arange(2 * LANES)
    hsum = (ch[:, None] // FOX_DIM == ch[None, :] // FOX_DIM).astype(np.float32)
    wq = np.zeros((pairs, 2 * LANES, 2 * LANES), np.float32)
    wk = np.zeros((pairs, 2 * LANES, 2 * LANES), np.float32)
    oq = np.zeros((1, wa), np.float32)
    ok = np.zeros((1, wa), np.float32)
    for r in range(pairs):
        for hh in range(2):
            h = 2 * r + hh
            ch = np.arange(FOX_DIM)
            wq[r, hh * FOX_DIM + ch, hh * LANES + ch] = 1.0
            wk[r, hh * FOX_DIM + ch, hh * LANES + ch] = 1.0
            for p in range(3):
                wq[r, LANES + p * FOX_HEADS + h, hh * LANES + FOX_DIM + p] = 1.0
                wk[r, LANES + p * FOX_HEADS + h, hh * LANES + FOX_DIM + 3 + p] = -1.0
                oq[0, h * LANES + FOX_DIM + 3 + p] = 1.0
                ok[0, h * LANES + FOX_DIM + p] = 1.0
    consts = [jnp.asarray(a, BF16) for a in (hsum, wq, wk)] + [jnp.asarray(oq), jnp.asarray(ok)]
    qg = jnp.tile(qg, (1, FOX_HEADS))
    kg = jnp.tile(kg, (1, FOX_HEADS))
    return pl.pallas_call(
        _inproj_fox_kernel,
        grid=(batch, nt),
        in_specs=[rows(d), const(g),
                  pl.BlockSpec(w_in.shape, lambda b, t: (0, 0), pipeline_mode=pl.Buffered(1)),
                  const(fb), const(qg), const(kg)] + [const(a) for a in consts],
        out_specs=[rows(n_hgrn), rows(w),
                   pl.BlockSpec((1, FOX_HEADS, LANES, ts), lambda b, t: (b, 0, 0, t)),
                   rows(FOX_HEADS * LANES),
                   pl.BlockSpec((1, FOX_HEADS, 1, FOX_VROWS, ts), lambda b, t: (b, 0, t, 0, 0))],
        out_shape=[jax.ShapeDtypeStruct((m, n_hgrn), F32),
                   jax.ShapeDtypeStruct((m, w), F32),
                   jax.ShapeDtypeStruct((batch, FOX_HEADS, LANES, seq), BF16),
                   jax.ShapeDtypeStruct((m, FOX_HEADS * LANES), BF16),
                   jax.ShapeDtypeStruct((batch, FOX_HEADS, nt, FOX_VROWS, ts), BF16)],
        scratch_shapes=[pltpu.VMEM((1, LANES), F32)],
        compiler_params=_cparams(("parallel", "arbitrary"), VMEM_MEDIUM_MIB),
        name="inproj_fox",
    )(x2, g, w_in, fb, qg, kg, *consts)


def _fox_kernel(qt_ref, k_ref, vt_ref, g_ref, o_ref, *, tq, tk, heads):
    i = pl.program_id(2)
    hs = range(heads)
    ratio = tq // tk

    def step(j, carry, q0):
        ms, accs = carry
        diagonal = q0 is not None
        q0 = q0 or 0
        nq = tq - q0
        nkeys = tk if diagonal else tq
        rows = pl.ds(pl.multiple_of(j * tq, tq) + q0, nkeys)
        keys = slice(q0, q0 + nkeys)
        if diagonal:
            visible = (lax.broadcasted_iota(jnp.int32, (tk, nq), 1)
                       >= lax.broadcasted_iota(jnp.int32, (tk, nq), 0))

        def scores(h):
            s = _dot(k_ref[rows, h * LANES:(h + 1) * LANES], qt_ref[0, h, :, q0:])
            return jnp.where(visible, s, NEG_BIG) if diagonal else s

        def softmax(h, s):
            m_old = ms[h][:, q0:]
            m_new = jnp.maximum(m_old, jnp.max(s, axis=0, keepdims=True))
            return m_new, jnp.exp2(m_old - m_new), jnp.exp2(s - m_new).astype(BF16)

        def values(h, alpha, p):
            return accs[h][:, q0:] * alpha + _dot(vt_ref[0, h, j, :, keys], p)

        s, sm, out = {}, {}, {}
        for t in range(heads + 2):
            if t < heads:
                s[t] = scores(t)
            if 0 <= t - 1 < heads:
                sm[t - 1] = softmax(t - 1, s[t - 1])
            if 0 <= t - 2 < heads:
                out[t - 2] = values(t - 2, sm[t - 2][1], sm[t - 2][2])
        keep = lambda old, new: new if q0 == 0 else jnp.concatenate([old[:, :q0], new], axis=1)
        return (tuple(keep(ms[h], sm[h][0]) for h in hs), tuple(keep(accs[h], out[h]) for h in hs))

    neg = jnp.full((1, tq), NEG_BIG, F32)
    carry = ((neg,) * heads, (jnp.zeros((FOX_VROWS, tq), F32),) * heads)
    carry = lax.fori_loop(0, i, lambda j, c: step(j, c, None), carry)
    for d in range(ratio):
        carry = step(i, carry, d * tk)
    _, accs = carry
    norm = [a[:FOX_DIM] / a[FOX_DIM:FOX_DIM + 1] for a in accs]
    for r in range(heads // 2):
        cols = slice(r * LANES, (r + 1) * LANES)
        out = jnp.concatenate([norm[2 * r], norm[2 * r + 1]], axis=0)
        o_ref[:, cols] = (out.T * _sigmoid(g_ref[:, cols])).astype(o_ref.dtype)


def _fox(qt, ka, vt, gate, batch, seq, tq, tk, heads):
    m = ka.shape[0]
    nq = seq // tq
    groups = FOX_HEADS // heads
    wv = heads * FOX_DIM
    return pl.pallas_call(
        functools.partial(_fox_kernel, tq=tq, tk=tk, heads=heads),
        grid=(batch, groups, nq),
        in_specs=[pl.BlockSpec((1, heads, LANES, tq), lambda b, p, i: (b, p, 0, i)),
                  pl.BlockSpec((seq, heads * LANES), lambda b, p, i: (b, p)),
                  pl.BlockSpec((1, heads, nq, FOX_VROWS, tq), lambda b, p, i: (b, p, 0, 0, 0)),
                  pl.BlockSpec((tq, wv), lambda b, p, i: (b * nq + i, p))],
        out_specs=pl.BlockSpec((tq, wv), lambda b, p, i: (b * nq + i, p)),
        out_shape=jax.ShapeDtypeStruct((m, FOX_HEADS * FOX_DIM), BF16),
        compiler_params=_cparams(("parallel", "parallel", "arbitrary"), VMEM_MEDIUM_MIB),
        name="fox_attention",
    )(qt, ka, vt, gate)


def _mix_mlp_kernel(*refs, n_mix, ck):
    ys = refs[:n_mix]
    w_ref, h_ref, g_ref, wu_ref, wd_ref, o_ref = refs[n_mix:]
    y = ys[0][...] if n_mix == 1 else jnp.concatenate([r[...] for r in ys], axis=1)
    x = h_ref[...] + _dot(y, w_ref[...])
    hn = _rms(x, g_ref[...]).astype(BF16)
    acc = x
    for c in range(wu_ref.shape[1] // ck):
        u = jnp.maximum(_dot(hn, wu_ref[:, c * ck:(c + 1) * ck]), 0.0)
        acc = acc + _dot((u * u).astype(BF16), wd_ref[c * ck:(c + 1) * ck, :])
    o_ref[...] = acc


def _mix_mlp(ys, w, h, g, wu, wd, tm, ck, name):
    m, d = h.shape
    const = lambda a: pl.BlockSpec(a.shape, lambda i: (0, 0), pipeline_mode=pl.Buffered(1))
    return pl.pallas_call(
        functools.partial(_mix_mlp_kernel, n_mix=len(ys), ck=ck),
        grid=(m // tm,),
        in_specs=([pl.BlockSpec((tm, y.shape[1]), lambda i: (i, 0)) for y in ys]
                  + [const(w), pl.BlockSpec((tm, d), lambda i: (i, 0)), const(g), const(wu), const(wd)]),
        out_specs=pl.BlockSpec((tm, d), lambda i: (i, 0)),
        out_shape=jax.ShapeDtypeStruct((m, d), F32),
        compiler_params=_cparams(("parallel",), VMEM_LARGE_MIB),
        name=name,
    )(*ys, w, h, g, wu, wd)


def _rwkvproj_kernel(h_ref, hp_ref, g_ref, mu_ref, wr_ref, wk_ref, wv_ref, w1_ref, w2_ref,
                     a1_ref, a2_ref, g1_ref, g2_ref, w0_ref, a0_ref, kk_ref, ka_ref,
                     r_ref, lw_ref, km_ref, v_ref, kr_ref, a_ref, go_ref, *, tiles_per_seq):
    i = pl.program_id(0)
    tm = h_ref.shape[0]
    gn = g_ref[...]
    hn = _rms(h_ref[...], gn)
    prev = _rms(hp_ref[7:8, :], gn)
    prev = jnp.where(i % tiles_per_seq == 0, jnp.zeros_like(prev), prev)
    row = lax.broadcasted_iota(jnp.int32, hn.shape, 0)
    shifted = jnp.where(row == 0, jnp.broadcast_to(prev, hn.shape), pltpu.roll(hn, 1, 0))
    xx = shifted - hn
    hn_b = hn.astype(BF16)
    xx_b = xx.astype(BF16)
    mix = lambda j: hn_b + xx_b * mu_ref[j:j + 1, :].astype(BF16)
    r = _dot(mix(0), wr_ref[...])
    k = _dot(mix(2), wk_ref[...])
    v = _dot(mix(3), wv_ref[...])
    z = w0_ref[...] + _dot(jnp.tanh(_dot(mix(1), w1_ref[...])).astype(BF16), w2_ref[...])
    a = _sigmoid(a0_ref[...] + _dot(_dot(mix(4), a1_ref[...]).astype(BF16), a2_ref[...]))
    g = _dot(_sigmoid(_dot(mix(5), g1_ref[...])).astype(BF16), g2_ref[...])
    r_ref[...] = r
    lw_ref[...] = _sigmoid(z) * (-math.exp(-0.5))
    km_ref[...] = k * (1.0 + (a - 1.0) * ka_ref[...])
    v_ref[...] = v
    kr_ref[...] = k * kk_ref[...]
    a_ref[...] = a
    go_ref[...] = g


def _rwkvproj(h, g, mu, wr, wk, wv, w1, w2, a1, a2, g1, g2, w0, a0, k_k, k_a, seq, tm):
    m, d = h.shape
    tiles_per_seq = seq // tm
    full = lambda a: pl.BlockSpec(a.shape, lambda i: (0,) * a.ndim, pipeline_mode=pl.Buffered(1))
    row = pl.BlockSpec((tm, d), lambda i: (i, 0))
    prev = pl.BlockSpec((8, d), lambda i: (jnp.maximum(i * (tm // 8) - 1, 0), 0))
    consts = (g, mu, wr, wk, wv, w1, w2, a1, a2, g1, g2, w0, a0, k_k, k_a)
    return pl.pallas_call(
        functools.partial(_rwkvproj_kernel, tiles_per_seq=tiles_per_seq),
        grid=(m // tm,),
        in_specs=[row, prev] + [full(a) for a in consts],
        out_specs=[row] * 7,
        out_shape=[jax.ShapeDtypeStruct((m, d), F32)] * 7,
        compiler_params=_cparams(("parallel",), VMEM_LARGE_MIB),
        name="rwkv_proj",
    )(h, h, *consts)


def _wkv_kernel(r_ref, lw_ref, km_ref, v_ref, kr_ref, a_ref, g_ref, rk_ref, lg_ref, lb_ref,
                o_ref, st_ref, y_ref, q1_ref, lhs_ref, z0_ref, wc_ref, bonus_ref, gate_ref,
                *, chunk, groups):
    grp = pl.program_id(2)

    @pl.when(pl.program_id(1) == 0)
    def _():
        st_ref[grp] = jnp.zeros(st_ref.shape[1:], F32)

    ts = r_ref.shape[0]
    nch = ts // chunk
    gl = GROUP_LANES
    rb = lax.broadcasted_iota(jnp.int32, (gl, gl), 0) // RWKV_DIM
    cb = lax.broadcasted_iota(jnp.int32, (gl, gl), 1) // RWKV_DIM
    blockmask = rb == cb
    ones_bd = jnp.where(blockmask, 1.0, 0.0).astype(BF16)

    def headsum(x, pieces=1):
        hi = x.astype(BF16)
        out = _dot(hi, ones_bd)
        if pieces == 2:
            out = out + _dot((x - hi.astype(F32)).astype(BF16), ones_bd)
        return out

    def bd(y):
        reps = gl // y.shape[0]
        return jnp.where(blockmask, jnp.concatenate([y] * reps, axis=0), 0.0).astype(BF16)

    def hmm(x, y):
        return _dot(x.astype(BF16), bd(y))

    def tn_blocks(x, y):
        return jnp.where(blockmask, _dot(x.T.astype(BF16), y.astype(BF16)), 0.0)

    t_idx = lax.broadcasted_iota(jnp.int32, (chunk, gl), 0)
    s_idx = lax.broadcasted_iota(jnp.int32, (chunk, gl), 1) % RWKV_DIM
    strict = s_idx < t_idx
    incl = s_idx <= t_idx
    tril_b = jnp.where(_tril_mask(chunk), 1.0, 0.0).astype(BF16)
    zeros_c = jnp.zeros((chunk, gl), F32)

    chunks = range(nch)
    blk = lambda ref, c: ref[c * chunk:(c + 1) * chunk, :]
    pad = lambda x: jnp.concatenate([x, zeros_c], axis=0)
    each = lambda fn, *lists: [fn(*args) for args in zip(*lists)]

    kr = kr_ref[...]
    kkn_all = kr * lax.rsqrt(jnp.maximum(headsum(kr * kr, pieces=2), 1e-24))

    lw = [blk(lw_ref, c) for c in chunks]
    cum = each(lambda x: _cumsum_rows(x, tril_b, pieces=2), lw)
    c_last = [x[chunk - 1:chunk, :] for x in cum]
    kkn = [kkn_all[c * chunk:(c + 1) * chunk, :] for c in chunks]
    kka = [kkn[c] * blk(a_ref, c) for c in chunks]
    km = [blk(km_ref, c) for c in chunks]
    v = [blk(v_ref, c) for c in chunks]
    e_neg = [jnp.exp(-x) for x in cum]
    e_end = each(lambda cl, x: jnp.exp(cl - x), c_last, cum)
    at = each(lambda k, x, l: -k * jnp.exp(x - l), kkn, cum, lw)
    bt = each(jnp.multiply, kka, e_neg)
    kt = each(jnp.multiply, km, e_neg)
    rt = [blk(r_ref, c) * jnp.exp(cum[c]) for c in chunks]
    bw = each(jnp.multiply, kka, e_end)
    kw = each(jnp.multiply, km, e_end)
    first_head = lax.broadcasted_iota(jnp.int32, (RWKV_DIM, LANES), 1) < RWKV_DIM
    for c in chunks:
        wt = jnp.broadcast_to(jnp.exp(c_last[c]), (LANES, gl)).T
        wc_ref[grp, c] = jnp.concatenate(
            [jnp.where(first_head, wt[2 * p * RWKV_DIM:(2 * p + 1) * RWKV_DIM],
                       wt[(2 * p + 1) * RWKV_DIM:(2 * p + 2) * RWKV_DIM])
             for p in range(RWKV_GROUP // 2)], axis=1)

    lhs = each(lambda a, r: jnp.concatenate([a, r], axis=0).astype(BF16), at, rt)
    pb = each(lambda l, b: _dot_nt(l, bd(b)), lhs, bt)
    pk = each(lambda l, k: _dot_nt(l, bd(k)), lhs, kt)
    a_ab = [jnp.where(strict, x[:chunk], 0.0) for x in pb]
    a_rb = [jnp.where(incl, x[chunk:], 0.0) for x in pb]
    a_ak = [jnp.where(strict, x[:chunk], 0.0) for x in pk]
    a_rk = [jnp.where(incl, x[chunk:], 0.0) for x in pk]

    e = [jnp.where((t_idx % 2 == 1) & (s_idx == t_idx - 1), x, 0.0) for x in a_ab]
    size = 2
    while size < chunk:
        off = ((t_idx // size) % 2 == 1) & (s_idx // size == t_idx // size - 1)
        a_off = [jnp.where(off, x, 0.0) for x in a_ab]
        t1 = each(lambda ao, ee: ao + hmm(ao, ee), a_off, e)
        e = each(lambda ee, tt: ee + tt + hmm(ee, tt), e, t1)
        size *= 2

    av = each(lambda ak, rk, vv: hmm(jnp.concatenate([ak, rk], axis=0), vv), a_ak, a_rk, v)
    akv = [x[:chunk] for x in av]
    p1 = each(lambda x, ee: x + hmm(ee, x), akv, e)
    mat = each(lambda x, ee: x + hmm(ee, x), at, e)
    q1 = each(lambda x, arb, pp: x[chunk:] + hmm(arb, pp), av, a_rb, p1)
    r2 = each(lambda r, arb, mm: r + hmm(arb, mm), rt, a_rb, mat)
    pct = each(lambda b, mm: tn_blocks(pad(b), pad(mm)), bw, mat)
    z0 = each(lambda b, k, pp, vv: tn_blocks(jnp.concatenate([b, k], axis=0),
                                             jnp.concatenate([pp, vv], axis=0)), bw, kw, p1, v)
    def fold(x):
        out = x[:RWKV_DIM]
        for h in range(1, RWKV_GROUP):
            out = out + x[h * RWKV_DIM:(h + 1) * RWKV_DIM]
        return out

    for c in chunks:
        q1_ref[grp, c * chunk:(c + 1) * chunk, :] = q1[c]
        lhs_ref[grp, c, :chunk, :] = r2[c].astype(BF16)
        lhs_ref[grp, c, chunk:, :] = fold(pct[c]).astype(BF16)
        z0_ref[grp, c] = fold(z0[c])
    bonus_ref[grp] = headsum(r_ref[...] * km_ref[...] * rk_ref[...]) * v_ref[...]
    gate_ref[grp] = g_ref[...]

    @pl.when(grp == groups - 1)
    def _():
        gs = range(groups)

        def body(c, carry):
            rows = pl.ds(pl.multiple_of(c * chunk, chunk), chunk)
            st = [st_ref[j] for j in gs]
            res = [_dot(lhs_ref[j, c], bd(st[j])) for j in gs]
            for j in gs:
                y_ref[j, rows, :] = q1_ref[j, rows, :] + res[j][:chunk]
                st_ref[j] = st[j] * wc_ref[j, c] + res[j][chunk:] + z0_ref[j, c]
            return carry

        lax.fori_loop(0, nch, body, 0)

        inv_n = 1.0 / RWKV_DIM
        for j in gs:
            cols = slice(j * gl, (j + 1) * gl)
            y = y_ref[j]
            mean = headsum(y) * inv_n
            dlt = y - mean
            var = headsum(dlt * dlt) * inv_n
            yn = dlt * lax.rsqrt(var + GN_EPS) * lg_ref[:, cols] + lb_ref[:, cols]
            o_ref[:, cols] = ((yn + bonus_ref[j]) * gate_ref[j]).astype(o_ref.dtype)


def _wkv(r, lw, km, v, kr, a, g, r_k, lnx_g, lnx_b, batch, seq, ts, chunk):
    m, d = r.shape
    gl = GROUP_LANES
    groups = d // gl
    nt = seq // ts
    nch = ts // chunk
    row = pl.BlockSpec((ts, gl), lambda b, t, j: (b * nt + t, j))
    vec = pl.BlockSpec((1, gl), lambda b, t, j: (0, j))
    full = pl.BlockSpec((1, d), lambda b, t, j: (0, 0))
    return pl.pallas_call(
        functools.partial(_wkv_kernel, chunk=chunk, groups=groups),
        grid=(batch, nt, groups),
        in_specs=[row] * 7 + [vec, full, full],
        out_specs=pl.BlockSpec((ts, d), lambda b, t, j: (b * nt + t, 0)),
        out_shape=jax.ShapeDtypeStruct((m, d), BF16),
        scratch_shapes=[pltpu.VMEM((groups, RWKV_DIM, gl), F32),
                        pltpu.VMEM((groups, ts, gl), F32),
                        pltpu.VMEM((groups, ts, gl), F32),
                        pltpu.VMEM((groups, nch, chunk + RWKV_DIM, gl), BF16),
                        pltpu.VMEM((groups, nch, RWKV_DIM, gl), F32),
                        pltpu.VMEM((groups, nch, RWKV_DIM, gl), F32),
                        pltpu.VMEM((groups, ts, gl), F32),
                        pltpu.VMEM((groups, ts, gl), F32)],
        compiler_params=_cparams(("parallel", "arbitrary", "arbitrary"), VMEM_LARGE_MIB),
        name="wkv7",
    )(r, lw, km, v, kr, a, g, r_k, lnx_g, lnx_b)


def kernel(x, norm_mix_g, norm_ffn_g, ab_w_in, hgrn_lower_bounds, hgrn_norm_g, fox_forget_bias,
           fox_q_norm_g, fox_k_norm_g, ab_w_out, rwkv_mu, rwkv_w_rkv, rwkv_w0, rwkv_w1, rwkv_w2,
           rwkv_a0, rwkv_a1, rwkv_a2, rwkv_g1, rwkv_g2, rwkv_k_k, rwkv_k_a, rwkv_r_k,
           rwkv_lnx_g, rwkv_lnx_b, rwkv_w_o, mlp_w_up, mlp_w_down):
    batch, seq, d = x.shape
    m = batch * seq
    t = _tiles(seq)
    assert d == 2 * HGRN_HEADS * HGRN_DIM == 2 * FOX_HEADS * FOX_DIM and d % GROUP_LANES == 0
    assert ab_w_in.shape == (1, d, 4 * (HGRN_HEADS * HGRN_DIM + FOX_HEADS * FOX_DIM) + FOX_HEADS)
    assert seq % t.time == 0 and t.time % t.chunk == 0 and t.time % t.fox_keys == 0
    assert seq % t.rows == 0 and seq % t.mlp_rows == 0 and mlp_w_up.shape[-1] % t.ff_chunk == 0
    row = lambda a: a.reshape(1, -1).astype(F32)
    bf = lambda a: a.astype(BF16)

    lb_all = jnp.cumsum(jax.nn.softmax(hgrn_lower_bounds.astype(F32), axis=0), axis=0)
    h = x.reshape(m, d)

    n_wide = ab_w_in.shape[-1] - FOX_HEADS
    gate_w = jnp.tile(ab_w_in[0][:, n_wide:], (1, 3))
    w_in = bf(jnp.pad(jnp.concatenate([ab_w_in[0][:, :n_wide], gate_w], axis=1),
                      ((0, 0), (0, LANES - 3 * FOX_HEADS))))
    fb = jnp.pad(jnp.tile(row(fox_forget_bias[0]), (1, 3)), ((0, 0), (0, LANES - 3 * FOX_HEADS)))
    ha, gate, qt, ka, vt = _inproj_fox(h, row(norm_mix_g[0]), w_in, fb, row(fox_q_norm_g[0]),
                                       row(fox_k_norm_g[0]), batch, seq, t.time)
    ya = _hgrn(ha, row(lb_all[0]), row(hgrn_norm_g[0]), batch, seq, t.time, t.chunk)
    yb = _fox(qt, ka, vt, gate, batch, seq, t.time, t.fox_keys, FOX_HEADS)
    h = _mix_mlp([ya, yb], bf(ab_w_out[0]), h, row(norm_ffn_g[0]), bf(mlp_w_up[0]), bf(mlp_w_down[0]),
                 t.mlp_rows, t.ff_chunk, "mix_mlp0")

    outs = _rwkvproj(h, row(norm_mix_g[1]), rwkv_mu[0].astype(F32),
                     bf(rwkv_w_rkv[0, 0]), bf(rwkv_w_rkv[0, 1]), bf(rwkv_w_rkv[0, 2]),
                     bf(rwkv_w1[0]), bf(rwkv_w2[0]), bf(rwkv_a1[0]), bf(rwkv_a2[0]),
                     bf(rwkv_g1[0]), bf(rwkv_g2[0]), row(rwkv_w0[0]), row(rwkv_a0[0]),
                     row(rwkv_k_k[0]), row(rwkv_k_a[0]), seq, t.rows)
    z = _wkv(*outs, row(rwkv_r_k[0]), row(rwkv_lnx_g[0]), row(rwkv_lnx_b[0]),
             batch, seq, t.time, t.chunk)
    h = _mix_mlp([z], bf(rwkv_w_o[0]), h, row(norm_ffn_g[1]), bf(mlp_w_up[1]), bf(mlp_w_down[1]),
                 t.mlp_rows, t.ff_chunk, "mix_mlp1")
    return h.reshape(batch, seq, d)
```

```python
import functools
import math
from typing import NamedTuple

import jax
import jax.numpy as jnp
import numpy as np
from jax import lax
from jax.experimental import pallas as pl
from jax.experimental.pallas import tpu as pltpu

F32 = jnp.float32
BF16 = jnp.bfloat16

RMS_EPS = 1e-6
GN_EPS = 64e-5

HGRN_HEADS = 4
HGRN_DIM = 128
HGRN_SUB = 16
FOX_HEADS = 8
FOX_DIM = 64
FOX_VROWS = FOX_DIM + 16
RWKV_DIM = 64
RWKV_GROUP = 4
GROUP_LANES = RWKV_GROUP * RWKV_DIM
LANES = 128
VMEM_V7X_MIB = 64
VMEM_SMALL_MIB = VMEM_V7X_MIB // 2
VMEM_MEDIUM_MIB = 3 * VMEM_V7X_MIB // 4
VMEM_LARGE_MIB = 7 * VMEM_V7X_MIB // 8
NEG_BIG = -1e30
LOG2E = 1.4426950408889634

NT_DIMS = (((1,), (1,)), ((), ()))


class _Tiles(NamedTuple):
    rows: int
    mlp_rows: int
    time: int
    fox_keys: int
    chunk: int
    ff_chunk: int


def _tiles(seq):
    return _Tiles(rows=min(512, seq), mlp_rows=min(512, seq), time=min(512, seq),
                  fox_keys=min(256, seq), chunk=64, ff_chunk=1024)


def _cparams(sem, vmem_mb):
    return pltpu.CompilerParams(dimension_semantics=sem, vmem_limit_bytes=vmem_mb * 1024 * 1024)


def _dot(a, b):
    return jnp.dot(a, b, preferred_element_type=F32)


def _dot_nt(a, b):
    return lax.dot_general(a, b, NT_DIMS, preferred_element_type=F32)


def _rms(x, g):
    return x * lax.rsqrt(jnp.mean(x * x, axis=-1, keepdims=True) + RMS_EPS) * g


def _sigmoid(x):
    return 1.0 / (1.0 + jnp.exp(-x))


def _log_sigmoid(x):
    return jnp.minimum(x, 0.0) - jnp.log(1.0 + jnp.exp(-jnp.abs(x)))


def _tril_mask(n, strict=False):
    r = lax.broadcasted_iota(jnp.int32, (n, n), 0)
    c = lax.broadcasted_iota(jnp.int32, (n, n), 1)
    return (c < r) if strict else (c <= r)


def _split3(x):
    hi = x.astype(BF16)
    r1 = x - hi.astype(F32)
    mid = r1.astype(BF16)
    lo = (r1 - mid.astype(F32)).astype(BF16)
    return hi, mid, lo


def _cumsum_rows(x, tril_bf16, pieces=3):
    parts = _split3(x)[:pieces]
    out = _dot(tril_bf16, parts[0])
    for part in parts[1:]:
        out = out + _dot(tril_bf16, part)
    return out


def _hgrn_kernel(q_ref, f_ref, i_ref, g_ref, lb_ref, ng_ref, o_ref, st_ref, *, chunk):
    @pl.when(pl.program_id(1) == 0)
    def _():
        st_ref[...] = jnp.zeros_like(st_ref)

    ts = q_ref.shape[0]
    tril_b = jnp.where(_tril_mask(chunk), 1.0, 0.0).astype(BF16)

    hs = range(HGRN_HEADS)
    nch = ts // chunk
    tiles = [(c, h) for c in range(nch) for h in hs]
    blk = lambda ref, c, h: ref[c * chunk:(c + 1) * chunk, h * HGRN_DIM:(h + 1) * HGRN_DIM]
    lbs = [lb_ref[:, h * HGRN_DIM:(h + 1) * HGRN_DIM] for h in hs]

    f = [lbs[h] + (1.0 - lbs[h]) * _sigmoid(blk(f_ref, c, h)) for c, h in tiles]
    b = [_cumsum_rows(jnp.log(x), tril_b) for x in f]
    b_last = [x[chunk - 1:chunk, :] for x in b]
    q = [blk(q_ref, c, h) * _sigmoid(blk(q_ref, c, h)) for c, h in tiles]
    k = [1.0 - x for x in f]
    vb = [blk(i_ref, c, h).astype(BF16) for c, h in tiles]
    n = range(len(tiles))

    sub = HGRN_SUB
    tril_sub = _tril_mask(sub)
    score_cols = [[] for _ in n]
    for j in range(chunk // sub):
        lo, hi = j * sub, (j + 1) * sub
        for i in n:
            bj = b[i][lo:hi]
            b_mid = b[i][lo + sub // 2 - 1:lo + sub // 2]
            diag = _dot_nt((q[i][lo:hi] * jnp.exp(bj - b_mid)).astype(BF16),
                           (k[i][lo:hi] * jnp.exp(b_mid - bj)).astype(BF16))
            parts = [jnp.where(tril_sub, diag, 0.0)]
            if lo > 0:
                parts.insert(0, jnp.zeros((lo, sub), F32))
            if hi < chunk:
                b_end = b[i][hi - 1:hi]
                parts.append(_dot_nt((q[i][hi:] * jnp.exp(b[i][hi:] - b_end)).astype(BF16),
                                     (k[i][lo:hi] * jnp.exp(b_end - bj)).astype(BF16)))
            score_cols[i].append(jnp.concatenate(parts, axis=0).astype(BF16))
    o = []
    for i in n:
        acc = _dot(score_cols[i][0], vb[i][:sub])
        for j in range(1, chunk // sub):
            acc = acc + _dot(score_cols[i][j], vb[i][j * sub:(j + 1) * sub])
        o.append(acc)
    inc = [_dot(blk(i_ref, c, h).T.astype(BF16), (k[i] * jnp.exp(b_last[i] - b[i])).astype(BF16))
           for i, (c, h) in enumerate(tiles)]
    dec = [jnp.exp(x) for x in b_last]

    st = [st_ref[h] for h in hs]
    st_in = []
    for i, (c, h) in enumerate(tiles):
        st_in.append(st[h].astype(BF16))
        st[h] = st[h] * dec[i] + inc[i]
    for h in hs:
        st_ref[h] = st[h]

    for i, (c, h) in enumerate(tiles):
        oi = o[i] + _dot_nt((q[i] * jnp.exp(b[i])).astype(BF16), st_in[i])
        ag = blk(g_ref, c, h)
        on = _rms(oi, ng_ref[:, h * HGRN_DIM:(h + 1) * HGRN_DIM])
        o_ref[c * chunk:(c + 1) * chunk, h * HGRN_DIM:(h + 1) * HGRN_DIM] = (
            on * (ag * _sigmoid(ag))).astype(o_ref.dtype)


def _hgrn(proj, lb, ng, batch, seq, ts, chunk):
    m = proj.shape[0]
    w = HGRN_HEADS * HGRN_DIM
    nt = seq // ts
    spec = lambda j: pl.BlockSpec((ts, w), lambda b, t, j=j: (b * nt + t, j))
    vec = pl.BlockSpec((1, w), lambda b, t: (0, 0))
    return pl.pallas_call(
        functools.partial(_hgrn_kernel, chunk=chunk),
        grid=(batch, nt),
        in_specs=[spec(0), spec(1), spec(2), spec(3), vec, vec],
        out_specs=pl.BlockSpec((ts, w), lambda b, t: (b * nt + t, 0)),
        out_shape=jax.ShapeDtypeStruct((m, w), BF16),
        scratch_shapes=[pltpu.VMEM((HGRN_HEADS, HGRN_DIM, HGRN_DIM), F32)],
        compiler_params=_cparams(("parallel", "arbitrary"), VMEM_SMALL_MIB),
        name="hgrn2",
    )(proj, proj, proj, proj, lb, ng)


def _inproj_fox_kernel(x_ref, g_ref, w_ref, fb_ref, qg_ref, kg_ref, hsum_ref, wq_ref, wk_ref,
                       oq_ref, ok_ref, ha_ref, go_ref, qt_ref, ka_ref, vt_ref, carry_ref):
    @pl.when(pl.program_id(1) == 0)
    def _():
        carry_ref[...] = jnp.zeros_like(carry_ref)

    ts = x_ref.shape[0]
    pairs = FOX_HEADS // 2
    hn = _rms(x_ref[...], g_ref[...]).astype(BF16)
    n_hgrn = ha_ref.shape[1]
    w = FOX_HEADS * FOX_DIM
    part = lambda j: _dot(hn, w_ref[:, n_hgrn + j * w:n_hgrn + (j + 1) * w])
    ha_ref[...] = _dot(hn, w_ref[:, :n_hgrn])
    q, k, v = part(0), part(1), part(2)
    go_ref[...] = part(3)
    f = _dot(hn, w_ref[:, n_hgrn + 4 * w:])

    tril_b = jnp.where(_tril_mask(ts), 1.0, 0.0).astype(BF16)
    lf = _log_sigmoid(f + fb_ref[...])
    c = _cumsum_rows(lf, tril_b) + carry_ref[...]
    carry_ref[...] = c[ts - 1:ts, :]
    c2 = c * LOG2E
    hi = c2.astype(BF16).astype(F32)
    rest = c2 - hi
    mid = rest.astype(BF16).astype(F32)
    lane = lax.broadcasted_iota(jnp.int32, c.shape, 1)
    pieces = jnp.where(lane < FOX_HEADS, hi,
                       jnp.where(lane < 2 * FOX_HEADS, mid, rest - mid)).astype(BF16)

    def headnorm(x_all, g_ref, r, scale):
        cols = slice(r * 2 * LANES, (r + 1) * 2 * LANES)
        x = x_all[:, cols]
        xx = x * x
        ss = _dot(xx.astype(BF16), hsum_ref[...])
        return (x * lax.rsqrt(ss * (1.0 / FOX_DIM) + RMS_EPS) * (g_ref[:, cols] * scale)).astype(BF16)

    qn = [headnorm(q, qg_ref, r, FOX_DIM ** -0.5 * LOG2E) for r in range(pairs // 2)]
    kn = [headnorm(k, kg_ref, r, 1.0) for r in range(pairs // 2)]
    for r in range(pairs):
        src = slice((r % 2) * LANES, (r % 2 + 1) * LANES)
        out = slice(r * 2 * LANES, (r + 1) * 2 * LANES)
        qa = _dot(jnp.concatenate([qn[r // 2][:, src], pieces], axis=1), wq_ref[r]) + oq_ref[:, out]
        ka = _dot(jnp.concatenate([kn[r // 2][:, src], pieces], axis=1), wk_ref[r]) + ok_ref[:, out]
        ka_ref[:, out] = ka.astype(BF16)
        for hh in range(2):
            qt_ref[0, 2 * r + hh] = qa[:, hh * LANES:(hh + 1) * LANES].T.astype(BF16)
        vt = v[:, r * LANES:(r + 1) * LANES].T
        extra = jnp.where(lax.broadcasted_iota(jnp.int32, (FOX_VROWS - FOX_DIM, ts), 0) == 0, 1.0, 0.0)
        for hh in range(2):
            vt_ref[0, 2 * r + hh, 0] = jnp.concatenate(
                [vt[hh * FOX_DIM:(hh + 1) * FOX_DIM], extra], axis=0).astype(BF16)


def _inproj_fox(x2, g, w_in, fb, qg, kg, batch, seq, ts):
    m, d = x2.shape
    w = FOX_HEADS * FOX_DIM
    n_hgrn = 4 * HGRN_HEADS * HGRN_DIM
    nt = seq // ts
    pairs = FOX_HEADS // 2
    const = lambda a: pl.BlockSpec(a.shape, lambda b, t: (0,) * a.ndim)
    rows = lambda width: pl.BlockSpec((ts, width), lambda b, t: (b * nt + t, 0))

    wa = FOX_HEADS * LANES
    ch = np.arange(2 * LANES)
    hsum = (ch[:, None] // FOX_DIM == ch[None, :] // FOX_DIM).astype(np.float32)
    wq = np.zeros((pairs, 2 * LANES, 2 * LANES), np.float32)
    wk = np.zeros((pairs, 2 * LANES, 2 * LANES), np.float32)
    oq = np.zeros((1, wa), np.float32)
    ok = np.zeros((1, wa), np.float32)
    for r in range(pairs):
        for hh in range(2):
            h = 2 * r + hh
            ch = np.arange(FOX_DIM)
            wq[r, hh * FOX_DIM + ch, hh * LANES + ch] = 1.0
            wk[r, hh * FOX_DIM + ch, hh * LANES + ch] = 1.0
            for p in range(3):
                wq[r, LANES + p * FOX_HEADS + h, hh * LANES + FOX_DIM + p] = 1.0
                wk[r, LANES + p * FOX_HEADS + h, hh * LANES + FOX_DIM + 3 + p] = -1.0
                oq[0, h * LANES + FOX_DIM + 3 + p] = 1.0
                ok[0, h * LANES + FOX_DIM + p] = 1.0
    consts = [jnp.asarray(a, BF16) for a in (hsum, wq, wk)] + [jnp.asarray(oq), jnp.asarray(ok)]
    qg = jnp.tile(qg, (1, FOX_HEADS))
    kg = jnp.tile(kg, (1, FOX_HEADS))
    return pl.pallas_call(
        _inproj_fox_kernel,
        grid=(batch, nt),
        in_specs=[rows(d), const(g),
                  pl.BlockSpec(w_in.shape, lambda b, t: (0, 0), pipeline_mode=pl.Buffered(1)),
                  const(fb), const(qg), const(kg)] + [const(a) for a in consts],
        out_specs=[rows(n_hgrn), rows(w),
                   pl.BlockSpec((1, FOX_HEADS, LANES, ts), lambda b, t: (b, 0, 0, t)),
                   rows(FOX_HEADS * LANES),
                   pl.BlockSpec((1, FOX_HEADS, 1, FOX_VROWS, ts), lambda b, t: (b, 0, t, 0, 0))],
        out_shape=[jax.ShapeDtypeStruct((m, n_hgrn), F32),
                   jax.ShapeDtypeStruct((m, w), F32),
                   jax.ShapeDtypeStruct((batch, FOX_HEADS, LANES, seq), BF16),
                   jax.ShapeDtypeStruct((m, FOX_HEADS * LANES), BF16),
                   jax.ShapeDtypeStruct((batch, FOX_HEADS, nt, FOX_VROWS, ts), BF16)],
        scratch_shapes=[pltpu.VMEM((1, LANES), F32)],
        compiler_params=_cparams(("parallel", "arbitrary"), VMEM_MEDIUM_MIB),
        name="inproj_fox",
    )(x2, g, w_in, fb, qg, kg, *consts)


def _fox_kernel(qt_ref, k_ref, vt_ref, g_ref, o_ref, m_ref, acc_ref, *, tq, tk, heads):
    i = pl.program_id(2)
    hs = range(heads)
    ratio = tq // tk

    def step(j, q0):
        diagonal = q0 is not None
        q0 = q0 or 0
        nq = tq - q0
        nkeys = tk if diagonal else tq
        rows = pl.ds(pl.multiple_of(j * tq, tq) + q0, nkeys)
        keys = slice(q0, q0 + nkeys)
        if diagonal:
            visible = (lax.broadcasted_iota(jnp.int32, (tk, nq), 1)
                       >= lax.broadcasted_iota(jnp.int32, (tk, nq), 0))

        def scores(h):
            s = _dot(k_ref[rows, h * LANES:(h + 1) * LANES], qt_ref[0, h, :, q0:])
            return jnp.where(visible, s, NEG_BIG) if diagonal else s

        def softmax(h, s):
            m_old = m_ref[h, :, q0:]
            m_new = jnp.maximum(m_old, jnp.max(s, axis=0, keepdims=True))
            return m_new, jnp.exp2(m_old - m_new), jnp.exp2(s - m_new).astype(BF16)

        def values(h, alpha, p):
            return acc_ref[h, :, q0:] * alpha + _dot(vt_ref[0, h, j, :, keys], p)

        s, sm, out = {}, {}, {}
        for t in range(heads + 2):
            if t < heads:
                s[t] = scores(t)
            if 0 <= t - 1 < heads:
                sm[t - 1] = softmax(t - 1, s[t - 1])
            if 0 <= t - 2 < heads:
                out[t - 2] = values(t - 2, sm[t - 2][1], sm[t - 2][2])
        for h in hs:
            m_ref[h, :, q0:] = sm[h][0]
            acc_ref[h, :, q0:] = out[h]

    m_ref[...] = jnp.full(m_ref.shape, NEG_BIG, F32)
    acc_ref[...] = jnp.zeros(acc_ref.shape, F32)

    def body(j, carry):
        step(j, None)
        return carry

    lax.fori_loop(0, i, body, 0)
    for d in range(ratio):
        step(i, d * tk)
    norm = [acc_ref[h, :FOX_DIM, :] / acc_ref[h, FOX_DIM:FOX_DIM + 1, :] for h in hs]
    for r in range(heads // 2):
        cols = slice(r * LANES, (r + 1) * LANES)
        out = jnp.concatenate([norm[2 * r], norm[2 * r + 1]], axis=0)
        o_ref[:, cols] = (out.T * _sigmoid(g_ref[:, cols])).astype(o_ref.dtype)


def _fox(qt, ka, vt, gate, batch, seq, tq, tk, heads):
    m = ka.shape[0]
    nq = seq // tq
    groups = FOX_HEADS // heads
    wv = heads * FOX_DIM
    return pl.pallas_call(
        functools.partial(_fox_kernel, tq=tq, tk=tk, heads=heads),
        grid=(batch, groups, nq),
        in_specs=[pl.BlockSpec((1, heads, LANES, tq), lambda b, p, i: (b, p, 0, i)),
                  pl.BlockSpec((seq, heads * LANES), lambda b, p, i: (b, p)),
                  pl.BlockSpec((1, heads, nq, FOX_VROWS, tq), lambda b, p, i: (b, p, 0, 0, 0)),
                  pl.BlockSpec((tq, wv), lambda b, p, i: (b * nq + i, p))],
        out_specs=pl.BlockSpec((tq, wv), lambda b, p, i: (b * nq + i, p)),
        out_shape=jax.ShapeDtypeStruct((m, FOX_HEADS * FOX_DIM), BF16),
        scratch_shapes=[pltpu.VMEM((heads, 1, tq), F32),
                        pltpu.VMEM((heads, FOX_VROWS, tq), F32)],
        compiler_params=_cparams(("parallel", "parallel", "arbitrary"), VMEM_MEDIUM_MIB),
        name="fox_attention",
    )(qt, ka, vt, gate)


def _mix_mlp_kernel(*refs, n_mix, ck):
    ys = refs[:n_mix]
    w_ref, h_ref, g_ref, wu_ref, wd_ref, o_ref = refs[n_mix:]
    y = ys[0][...] if n_mix == 1 else jnp.concatenate([r[...] for r in ys], axis=1)
    x = h_ref[...] + _dot(y, w_ref[...])
    hn = _rms(x, g_ref[...]).astype(BF16)
    acc = x
    for c in range(wu_ref.shape[1] // ck):
        u = jnp.maximum(_dot(hn, wu_ref[:, c * ck:(c + 1) * ck]), 0.0)
        acc = acc + _dot((u * u).astype(BF16), wd_ref[c * ck:(c + 1) * ck, :])
    o_ref[...] = acc


def _mix_mlp(ys, w, h, g, wu, wd, tm, ck, name):
    m, d = h.shape
    const = lambda a: pl.BlockSpec(a.shape, lambda i: (0, 0), pipeline_mode=pl.Buffered(1))
    return pl.pallas_call(
        functools.partial(_mix_mlp_kernel, n_mix=len(ys), ck=ck),
        grid=(m // tm,),
        in_specs=([pl.BlockSpec((tm, y.shape[1]), lambda i: (i, 0)) for y in ys]
                  + [const(w), pl.BlockSpec((tm, d), lambda i: (i, 0)), const(g), const(wu), const(wd)]),
        out_specs=pl.BlockSpec((tm, d), lambda i: (i, 0)),
        out_shape=jax.ShapeDtypeStruct((m, d), F32),
        compiler_params=_cparams(("parallel",), VMEM_LARGE_MIB),
        name=name,
    )(*ys, w, h, g, wu, wd)


def _rwkvproj_kernel(h_ref, hp_ref, g_ref, mu_ref, wr_ref, wk_ref, wv_ref, w1_ref, w2_ref,
                     a1_ref, a2_ref, g1_ref, g2_ref, w0_ref, a0_ref, kk_ref, ka_ref,
                     r_ref, lw_ref, km_ref, v_ref, kr_ref, a_ref, go_ref, *, tiles_per_seq):
    i = pl.program_id(0)
    tm = h_ref.shape[0]
    gn = g_ref[...]
    hn = _rms(h_ref[...], gn)
    prev = _rms(hp_ref[7:8, :], gn)
    prev = jnp.where(i % tiles_per_seq == 0, jnp.zeros_like(prev), prev)
    row = lax.broadcasted_iota(jnp.int32, hn.shape, 0)
    shifted = jnp.where(row == 0, jnp.broadcast_to(prev, hn.shape), pltpu.roll(hn, 1, 0))
    xx = shifted - hn
    hn_b = hn.astype(BF16)
    xx_b = xx.astype(BF16)
    mix = lambda j: hn_b + xx_b * mu_ref[j:j + 1, :].astype(BF16)
    r = _dot(mix(0), wr_ref[...])
    k = _dot(mix(2), wk_ref[...])
    v = _dot(mix(3), wv_ref[...])
    z = w0_ref[...] + _dot(jnp.tanh(_dot(mix(1), w1_ref[...])).astype(BF16), w2_ref[...])
    a = _sigmoid(a0_ref[...] + _dot(_dot(mix(4), a1_ref[...]).astype(BF16), a2_ref[...]))
    g = _dot(_sigmoid(_dot(mix(5), g1_ref[...])).astype(BF16), g2_ref[...])
    r_ref[...] = r
    lw_ref[...] = _sigmoid(z) * (-math.exp(-0.5))
    km_ref[...] = k * (1.0 + (a - 1.0) * ka_ref[...])
    v_ref[...] = v
    kr_ref[...] = k * kk_ref[...]
    a_ref[...] = a
    go_ref[...] = g


def _rwkvproj(h, g, mu, wr, wk, wv, w1, w2, a1, a2, g1, g2, w0, a0, k_k, k_a, seq, tm):
    m, d = h.shape
    tiles_per_seq = seq // tm
    full = lambda a: pl.BlockSpec(a.shape, lambda i: (0,) * a.ndim, pipeline_mode=pl.Buffered(1))
    row = pl.BlockSpec((tm, d), lambda i: (i, 0))
    prev = pl.BlockSpec((8, d), lambda i: (jnp.maximum(i * (tm // 8) - 1, 0), 0))
    consts = (g, mu, wr, wk, wv, w1, w2, a1, a2, g1, g2, w0, a0, k_k, k_a)
    return pl.pallas_call(
        functools.partial(_rwkvproj_kernel, tiles_per_seq=tiles_per_seq),
        grid=(m // tm,),
        in_specs=[row, prev] + [full(a) for a in consts],
        out_specs=[row] * 7,
        out_shape=[jax.ShapeDtypeStruct((m, d), F32)] * 7,
        compiler_params=_cparams(("parallel",), VMEM_LARGE_MIB),
        name="rwkv_proj",
    )(h, h, *consts)


def _wkv_kernel(r_ref, lw_ref, km_ref, v_ref, kr_ref, a_ref, g_ref, rk_ref, lg_ref, lb_ref,
                o_ref, st_ref, y_ref, q1_ref, lhs_ref, z0_ref, wc_ref, bonus_ref, gate_ref,
                *, chunk, groups):
    grp = pl.program_id(2)

    @pl.when(pl.program_id(1) == 0)
    def _():
        st_ref[grp] = jnp.zeros(st_ref.shape[1:], F32)

    ts = r_ref.shape[0]
    nch = ts // chunk
    gl = GROUP_LANES
    rb = lax.broadcasted_iota(jnp.int32, (gl, gl), 0) // RWKV_DIM
    cb = lax.broadcasted_iota(jnp.int32, (gl, gl), 1) // RWKV_DIM
    blockmask = rb == cb
    ones_bd = jnp.where(blockmask, 1.0, 0.0).astype(BF16)

    def headsum(x, pieces=1):
        hi = x.astype(BF16)
        out = _dot(hi, ones_bd)
        if pieces == 2:
            out = out + _dot((x - hi.astype(F32)).astype(BF16), ones_bd)
        return out

    def bd(y):
        reps = gl // y.shape[0]
        return jnp.where(blockmask, jnp.concatenate([y] * reps, axis=0), 0.0).astype(BF16)

    def hmm(x, y):
        return _dot(x.astype(BF16), bd(y))

    def tn_blocks(x, y):
        return jnp.where(blockmask, _dot(x.T.astype(BF16), y.astype(BF16)), 0.0)

    t_idx = lax.broadcasted_iota(jnp.int32, (chunk, gl), 0)
    s_idx = lax.broadcasted_iota(jnp.int32, (chunk, gl), 1) % RWKV_DIM
    strict = s_idx < t_idx
    incl = s_idx <= t_idx
    tril_b = jnp.where(_tril_mask(chunk), 1.0, 0.0).astype(BF16)
    zeros_c = jnp.zeros((chunk, gl), F32)

    chunks = range(nch)
    blk = lambda ref, c: ref[c * chunk:(c + 1) * chunk, :]
    pad = lambda x: jnp.concatenate([x, zeros_c], axis=0)
    each = lambda fn, *lists: [fn(*args) for args in zip(*lists)]

    kr = kr_ref[...]
    kkn_all = kr * lax.rsqrt(jnp.maximum(headsum(kr * kr, pieces=2), 1e-24))

    lw = [blk(lw_ref, c) for c in chunks]
    cum = each(lambda x: _cumsum_rows(x, tril_b, pieces=2), lw)
    c_last = [x[chunk - 1:chunk, :] for x in cum]
    kkn = [kkn_all[c * chunk:(c + 1) * chunk, :] for c in chunks]
    kka = [kkn[c] * blk(a_ref, c) for c in chunks]
    km = [blk(km_ref, c) for c in chunks]
    v = [blk(v_ref, c) for c in chunks]
    e_neg = [jnp.exp(-x) for x in cum]
    e_end = each(lambda cl, x: jnp.exp(cl - x), c_last, cum)
    at = each(lambda k, x, l: -k * jnp.exp(x - l), kkn, cum, lw)
    bt = each(jnp.multiply, kka, e_neg)
    kt = each(jnp.multiply, km, e_neg)
    rt = [blk(r_ref, c) * jnp.exp(cum[c]) for c in chunks]
    bw = each(jnp.multiply, kka, e_end)
    kw = each(jnp.multiply, km, e_end)
    first_head = lax.broadcasted_iota(jnp.int32, (RWKV_DIM, LANES), 1) < RWKV_DIM
    for c in chunks:
        wt = jnp.broadcast_to(jnp.exp(c_last[c]), (LANES, gl)).T
        wc_ref[grp, c] = jnp.concatenate(
            [jnp.where(first_head, wt[2 * p * RWKV_DIM:(2 * p + 1) * RWKV_DIM],
                       wt[(2 * p + 1) * RWKV_DIM:(2 * p + 2) * RWKV_DIM])
             for p in range(RWKV_GROUP // 2)], axis=1)

    lhs = each(lambda a, r: jnp.concatenate([a, r], axis=0).astype(BF16), at, rt)
    pb = each(lambda l, b: _dot_nt(l, bd(b)), lhs, bt)
    pk = each(lambda l, k: _dot_nt(l, bd(k)), lhs, kt)
    a_ab = [jnp.where(strict, x[:chunk], 0.0) for x in pb]
    a_rb = [jnp.where(incl, x[chunk:], 0.0) for x in pb]
    a_ak = [jnp.where(strict, x[:chunk], 0.0) for x in pk]
    a_rk = [jnp.where(incl, x[chunk:], 0.0) for x in pk]

    e = [jnp.where((t_idx % 2 == 1) & (s_idx == t_idx - 1), x, 0.0) for x in a_ab]
    size = 2
    while size < chunk:
        off = ((t_idx // size) % 2 == 1) & (s_idx // size == t_idx // size - 1)
        a_off = [jnp.where(off, x, 0.0) for x in a_ab]
        t1 = each(lambda ao, ee: ao + hmm(ao, ee), a_off, e)
        e = each(lambda ee, tt: ee + tt + hmm(ee, tt), e, t1)
        size *= 2

    av = each(lambda ak, rk, vv: hmm(jnp.concatenate([ak, rk], axis=0), vv), a_ak, a_rk, v)
    akv = [x[:chunk] for x in av]
    p1 = each(lambda x, ee: x + hmm(ee, x), akv, e)
    mat = each(lambda x, ee: x + hmm(ee, x), at, e)
    q1 = each(lambda x, arb, pp: x[chunk:] + hmm(arb, pp), av, a_rb, p1)
    r2 = each(lambda r, arb, mm: r + hmm(arb, mm), rt, a_rb, mat)
    pct = each(lambda b, mm: tn_blocks(pad(b), pad(mm)), bw, mat)
    z0 = each(lambda b, k, pp, vv: tn_blocks(jnp.concatenate([b, k], axis=0),
                                             jnp.concatenate([pp, vv], axis=0)), bw, kw, p1, v)
    def fold(x):
        out = x[:RWKV_DIM]
        for h in range(1, RWKV_GROUP):
            out = out + x[h * RWKV_DIM:(h + 1) * RWKV_DIM]
        return out

    for c in chunks:
        q1_ref[grp, c * chunk:(c + 1) * chunk, :] = q1[c]
        lhs_ref[grp, c, :chunk, :] = r2[c].astype(BF16)
        lhs_ref[grp, c, chunk:, :] = fold(pct[c]).astype(BF16)
        z0_ref[grp, c] = fold(z0[c])
    bonus_ref[grp] = headsum(r_ref[...] * km_ref[...] * rk_ref[...]) * v_ref[...]
    gate_ref[grp] = g_ref[...]

    @pl.when(grp == groups - 1)
    def _():
        gs = range(groups)

        def body(c, carry):
            rows = pl.ds(pl.multiple_of(c * chunk, chunk), chunk)
            st = [st_ref[j] for j in gs]
            res = [_dot(lhs_ref[j, c], bd(st[j])) for j in gs]
            for j in gs:
                y_ref[j, rows, :] = q1_ref[j, rows, :] + res[j][:chunk]
                st_ref[j] = st[j] * wc_ref[j, c] + res[j][chunk:] + z0_ref[j, c]
            return carry

        lax.fori_loop(0, nch, body, 0)

        inv_n = 1.0 / RWKV_DIM
        for j in gs:
            cols = slice(j * gl, (j + 1) * gl)
            y = y_ref[j]
            mean = headsum(y) * inv_n
            dlt = y - mean
            var = headsum(dlt * dlt) * inv_n
            yn = dlt * lax.rsqrt(var + GN_EPS) * lg_ref[:, cols] + lb_ref[:, cols]
            o_ref[:, cols] = ((yn + bonus_ref[j]) * gate_ref[j]).astype(o_ref.dtype)


def _wkv(r, lw, km, v, kr, a, g, r_k, lnx_g, lnx_b, batch, seq, ts, chunk):
    m, d = r.shape
    gl = GROUP_LANES
    groups = d // gl
    nt = seq // ts
    nch = ts // chunk
    row = pl.BlockSpec((ts, gl), lambda b, t, j: (b * nt + t, j))
    vec = pl.BlockSpec((1, gl), lambda b, t, j: (0, j))
    full = pl.BlockSpec((1, d), lambda b, t, j: (0, 0))
    return pl.pallas_call(
        functools.partial(_wkv_kernel, chunk=chunk, groups=groups),
        grid=(batch, nt, groups),
        in_specs=[row] * 7 + [vec, full, full],
        out_specs=pl.BlockSpec((ts, d), lambda b, t, j: (b * nt + t, 0)),
        out_shape=jax.ShapeDtypeStruct((m, d), BF16),
        scratch_shapes=[pltpu.VMEM((groups, RWKV_DIM, gl), F32),
                        pltpu.VMEM((groups, ts, gl), F32),
                        pltpu.VMEM((groups, ts, gl), F32),
                        pltpu.VMEM((groups, nch, chunk + RWKV_DIM, gl), BF16),
                        pltpu.VMEM((groups, nch, RWKV_DIM, gl), F32),
                        pltpu.VMEM((groups, nch, RWKV_DIM, gl), F32),
                        pltpu.VMEM((groups, ts, gl), F32),
                        pltpu.VMEM((groups, ts, gl), F32)],
        compiler_params=_cparams(("parallel", "arbitrary", "arbitrary"), VMEM_LARGE_MIB),
        name="wkv7",
    )(r, lw, km, v, kr, a, g, r_k, lnx_g, lnx_b)


def kernel(x, norm_mix_g, norm_ffn_g, ab_w_in, hgrn_lower_bounds, hgrn_norm_g, fox_forget_bias,
           fox_q_norm_g, fox_k_norm_g, ab_w_out, rwkv_mu, rwkv_w_rkv, rwkv_w0, rwkv_w1, rwkv_w2,
           rwkv_a0, rwkv_a1, rwkv_a2, rwkv_g1, rwkv_g2, rwkv_k_k, rwkv_k_a, rwkv_r_k,
           rwkv_lnx_g, rwkv_lnx_b, rwkv_w_o, mlp_w_up, mlp_w_down):
    batch, seq, d = x.shape
    m = batch * seq
    t = _tiles(seq)
    assert d == 2 * HGRN_HEADS * HGRN_DIM == 2 * FOX_HEADS * FOX_DIM and d % GROUP_LANES == 0
    assert ab_w_in.shape == (1, d, 4 * (HGRN_HEADS * HGRN_DIM + FOX_HEADS * FOX_DIM) + FOX_HEADS)
    assert seq % t.time == 0 and t.time % t.chunk == 0 and t.time % t.fox_keys == 0
    assert seq % t.rows == 0 and seq % t.mlp_rows == 0 and mlp_w_up.shape[-1] % t.ff_chunk == 0
    row = lambda a: a.reshape(1, -1).astype(F32)
    bf = lambda a: a.astype(BF16)

    lb_all = jnp.cumsum(jax.nn.softmax(hgrn_lower_bounds.astype(F32), axis=0), axis=0)
    h = x.reshape(m, d)

    n_wide = ab_w_in.shape[-1] - FOX_HEADS
    gate_w = jnp.tile(ab_w_in[0][:, n_wide:], (1, 3))
    w_in = bf(jnp.pad(jnp.concatenate([ab_w_in[0][:, :n_wide], gate_w], axis=1),
                      ((0, 0), (0, LANES - 3 * FOX_HEADS))))
    fb = jnp.pad(jnp.tile(row(fox_forget_bias[0]), (1, 3)), ((0, 0), (0, LANES - 3 * FOX_HEADS)))
    ha, gate, qt, ka, vt = _inproj_fox(h, row(norm_mix_g[0]), w_in, fb, row(fox_q_norm_g[0]),
                                       row(fox_k_norm_g[0]), batch, seq, t.time)
    ya = _hgrn(ha, row(lb_all[0]), row(hgrn_norm_g[0]), batch, seq, t.time, t.chunk)
    yb = _fox(qt, ka, vt, gate, batch, seq, t.time, t.fox_keys, FOX_HEADS)
    h = _mix_mlp([ya, yb], bf(ab_w_out[0]), h, row(norm_ffn_g[0]), bf(mlp_w_up[0]), bf(mlp_w_down[0]),
                 t.mlp_rows, t.ff_chunk, "mix_mlp0")

    outs = _rwkvproj(h, row(norm_mix_g[1]), rwkv_mu[0].astype(F32),
                     bf(rwkv_w_rkv[0, 0]), bf(rwkv_w_rkv[0, 1]), bf(rwkv_w_rkv[0, 2]),
                     bf(rwkv_w1[0]), bf(rwkv_w2[0]), bf(rwkv_a1[0]), bf(rwkv_a2[0]),
                     bf(rwkv_g1[0]), bf(rwkv_g2[0]), row(rwkv_w0[0]), row(rwkv_a0[0]),
                     row(rwkv_k_k[0]), row(rwkv_k_a[0]), seq, t.rows)
    z = _wkv(*outs, row(rwkv_r_k[0]), row(rwkv_lnx_g[0]), row(rwkv_lnx_b[0]),
             batch, seq, t.time, t.chunk)
    h = _mix_mlp([z], bf(rwkv_w_o[0]), h, row(norm_ffn_g[1]), bf(mlp_w_up[1]), bf(mlp_w_down[1]),
                 t.mlp_rows, t.ff_chunk, "mix_mlp1")
    return h.reshape(batch, seq, d)
```

```python
import functools
import math
from typing import NamedTuple

import jax
import jax.numpy as jnp
import numpy as np
from jax import lax
from jax.experimental import pallas as pl
from jax.experimental.pallas import tpu as pltpu

F32 = jnp.float32
BF16 = jnp.bfloat16

RMS_EPS = 1e-6
GN_EPS = 64e-5

HGRN_HEADS = 4
HGRN_DIM = 128
HGRN_SUB = 16
FOX_HEADS = 8
FOX_DIM = 64
FOX_VROWS = FOX_DIM + 16
RWKV_DIM = 64
RWKV_GROUP = 4
GROUP_LANES = RWKV_GROUP * RWKV_DIM
LANES = 128
VMEM_V7X_MIB = 64
VMEM_SMALL_MIB = VMEM_V7X_MIB // 2
VMEM_MEDIUM_MIB = 3 * VMEM_V7X_MIB // 4
VMEM_LARGE_MIB = 7 * VMEM_V7X_MIB // 8
NEG_BIG = -1e30
LOG2E = 1.4426950408889634

NT_DIMS = (((1,), (1,)), ((), ()))


class _Tiles(NamedTuple):
    rows: int
    mlp_rows: int
    time: int
    fox_keys: int
    chunk: int
    ff_chunk: int


def _tiles(seq):
    return _Tiles(rows=min(512, seq), mlp_rows=min(1024, seq), time=min(512, seq),
                  fox_keys=min(256, seq), chunk=64, ff_chunk=1024)


def _cparams(sem, vmem_mb):
    return pltpu.CompilerParams(dimension_semantics=sem, vmem_limit_bytes=vmem_mb * 1024 * 1024)


def _dot(a, b):
    return jnp.dot(a, b, preferred_element_type=F32)


def _dot_nt(a, b):
    return lax.dot_general(a, b, NT_DIMS, preferred_element_type=F32)


def _rms(x, g):
    return x * lax.rsqrt(jnp.mean(x * x, axis=-1, keepdims=True) + RMS_EPS) * g


def _sigmoid(x):
    return 1.0 / (1.0 + jnp.exp(-x))


def _log_sigmoid(x):
    return jnp.minimum(x, 0.0) - jnp.log(1.0 + jnp.exp(-jnp.abs(x)))


def _tril_mask(n, strict=False):
    r = lax.broadcasted_iota(jnp.int32, (n, n), 0)
    c = lax.broadcasted_iota(jnp.int32, (n, n), 1)
    return (c < r) if strict else (c <= r)


def _split3(x):
    hi = x.astype(BF16)
    r1 = x - hi.astype(F32)
    mid = r1.astype(BF16)
    lo = (r1 - mid.astype(F32)).astype(BF16)
    return hi, mid, lo


def _cumsum_rows(x, tril_bf16, pieces=3):
    parts = _split3(x)[:pieces]
    out = _dot(tril_bf16, parts[0])
    for part in parts[1:]:
        out = out + _dot(tril_bf16, part)
    return out


def _hgrn_kernel(q_ref, f_ref, i_ref, g_ref, lb_ref, ng_ref, o_ref, st_ref, *, chunk):
    @pl.when(pl.program_id(1) == 0)
    def _():
        st_ref[...] = jnp.zeros_like(st_ref)

    ts = q_ref.shape[0]
    tril_b = jnp.where(_tril_mask(chunk), 1.0, 0.0).astype(BF16)

    hs = range(HGRN_HEADS)
    nch = ts // chunk
    tiles = [(c, h) for c in range(nch) for h in hs]
    blk = lambda ref, c, h: ref[c * chunk:(c + 1) * chunk, h * HGRN_DIM:(h + 1) * HGRN_DIM]
    lbs = [lb_ref[:, h * HGRN_DIM:(h + 1) * HGRN_DIM] for h in hs]

    f = [lbs[h] + (1.0 - lbs[h]) * _sigmoid(blk(f_ref, c, h)) for c, h in tiles]
    b = [_cumsum_rows(jnp.log(x), tril_b) for x in f]
    b_last = [x[chunk - 1:chunk, :] for x in b]
    q = [blk(q_ref, c, h) * _sigmoid(blk(q_ref, c, h)) for c, h in tiles]
    k = [1.0 - x for x in f]
    vb = [blk(i_ref, c, h).astype(BF16) for c, h in tiles]
    n = range(len(tiles))

    sub = HGRN_SUB
    tril_sub = _tril_mask(sub)
    score_cols = [[] for _ in n]
    for j in range(chunk // sub):
        lo, hi = j * sub, (j + 1) * sub
        for i in n:
            bj = b[i][lo:hi]
            b_mid = b[i][lo + sub // 2 - 1:lo + sub // 2]
            diag = _dot_nt((q[i][lo:hi] * jnp.exp(bj - b_mid)).astype(BF16),
                           (k[i][lo:hi] * jnp.exp(b_mid - bj)).astype(BF16))
            parts = [jnp.where(tril_sub, diag, 0.0)]
            if lo > 0:
                parts.insert(0, jnp.zeros((lo, sub), F32))
            if hi < chunk:
                b_end = b[i][hi - 1:hi]
                parts.append(_dot_nt((q[i][hi:] * jnp.exp(b[i][hi:] - b_end)).astype(BF16),
                                     (k[i][lo:hi] * jnp.exp(b_end - bj)).astype(BF16)))
            score_cols[i].append(jnp.concatenate(parts, axis=0).astype(BF16))
    o = []
    for i in n:
        acc = _dot(score_cols[i][0], vb[i][:sub])
        for j in range(1, chunk // sub):
            acc = acc + _dot(score_cols[i][j], vb[i][j * sub:(j + 1) * sub])
        o.append(acc)
    inc = [_dot(blk(i_ref, c, h).T.astype(BF16), (k[i] * jnp.exp(b_last[i] - b[i])).astype(BF16))
           for i, (c, h) in enumerate(tiles)]
    dec = [jnp.exp(x) for x in b_last]

    st = [st_ref[h] for h in hs]
    st_in = []
    for i, (c, h) in enumerate(tiles):
        st_in.append(st[h].astype(BF16))
        st[h] = st[h] * dec[i] + inc[i]
    for h in hs:
        st_ref[h] = st[h]

    for i, (c, h) in enumerate(tiles):
        oi = o[i] + _dot_nt((q[i] * jnp.exp(b[i])).astype(BF16), st_in[i])
        ag = blk(g_ref, c, h)
        on = _rms(oi, ng_ref[:, h * HGRN_DIM:(h + 1) * HGRN_DIM])
        o_ref[c * chunk:(c + 1) * chunk, h * HGRN_DIM:(h + 1) * HGRN_DIM] = (
            on * (ag * _sigmoid(ag))).astype(o_ref.dtype)


def _hgrn(proj, lb, ng, batch, seq, ts, chunk):
    m = proj.shape[0]
    w = HGRN_HEADS * HGRN_DIM
    nt = seq // ts
    spec = lambda j: pl.BlockSpec((ts, w), lambda b, t, j=j: (b * nt + t, j))
    vec = pl.BlockSpec((1, w), lambda b, t: (0, 0))
    return pl.pallas_call(
        functools.partial(_hgrn_kernel, chunk=chunk),
        grid=(batch, nt),
        in_specs=[spec(0), spec(1), spec(2), spec(3), vec, vec],
        out_specs=pl.BlockSpec((ts, w), lambda b, t: (b * nt + t, 0)),
        out_shape=jax.ShapeDtypeStruct((m, w), BF16),
        scratch_shapes=[pltpu.VMEM((HGRN_HEADS, HGRN_DIM, HGRN_DIM), F32)],
        compiler_params=_cparams(("parallel", "arbitrary"), VMEM_SMALL_MIB),
        name="hgrn2",
    )(proj, proj, proj, proj, lb, ng)


def _inproj_fox_kernel(x_ref, g_ref, w_ref, fb_ref, qg_ref, kg_ref, hsum_ref, wq_ref, wk_ref,
                       oq_ref, ok_ref, ha_ref, go_ref, qt_ref, ka_ref, vt_ref, carry_ref):
    @pl.when(pl.program_id(1) == 0)
    def _():
        carry_ref[...] = jnp.zeros_like(carry_ref)

    ts = x_ref.shape[0]
    pairs = FOX_HEADS // 2
    hn = _rms(x_ref[...], g_ref[...]).astype(BF16)
    n_hgrn = ha_ref.shape[1]
    w = FOX_HEADS * FOX_DIM
    part = lambda j: _dot(hn, w_ref[:, n_hgrn + j * w:n_hgrn + (j + 1) * w])
    ha_ref[...] = _dot(hn, w_ref[:, :n_hgrn])
    q, k, v = part(0), part(1), part(2)
    go_ref[...] = part(3)
    f = _dot(hn, w_ref[:, n_hgrn + 4 * w:])

    tril_b = jnp.where(_tril_mask(ts), 1.0, 0.0).astype(BF16)
    lf = _log_sigmoid(f + fb_ref[...])
    c = _cumsum_rows(lf, tril_b) + carry_ref[...]
    carry_ref[...] = c[ts - 1:ts, :]
    c2 = c * LOG2E
    hi = c2.astype(BF16).astype(F32)
    rest = c2 - hi
    mid = rest.astype(BF16).astype(F32)
    lane = lax.broadcasted_iota(jnp.int32, c.shape, 1)
    pieces = jnp.where(lane < FOX_HEADS, hi,
                       jnp.where(lane < 2 * FOX_HEADS, mid, rest - mid)).astype(BF16)

    def headnorm(x_all, g_ref, r, scale):
        cols = slice(r * 2 * LANES, (r + 1) * 2 * LANES)
        x = x_all[:, cols]
        xx = x * x
        ss = _dot(xx.astype(BF16), hsum_ref[...])
        return (x * lax.rsqrt(ss * (1.0 / FOX_DIM) + RMS_EPS) * (g_ref[:, cols] * scale)).astype(BF16)

    qn = [headnorm(q, qg_ref, r, FOX_DIM ** -0.5 * LOG2E) for r in range(pairs // 2)]
    kn = [headnorm(k, kg_ref, r, 1.0) for r in range(pairs // 2)]
    for r in range(pairs):
        src = slice((r % 2) * LANES, (r % 2 + 1) * LANES)
        out = slice(r * 2 * LANES, (r + 1) * 2 * LANES)
        qa = _dot(jnp.concatenate([qn[r // 2][:, src], pieces], axis=1), wq_ref[r]) + oq_ref[:, out]
        ka = _dot(jnp.concatenate([kn[r // 2][:, src], pieces], axis=1), wk_ref[r]) + ok_ref[:, out]
        ka_ref[:, out] = ka.astype(BF16)
        for hh in range(2):
            qt_ref[0, 2 * r + hh] = qa[:, hh * LANES:(hh + 1) * LANES].T.astype(BF16)
        vt = v[:, r * LANES:(r + 1) * LANES].T
        extra = jnp.where(lax.broadcasted_iota(jnp.int32, (FOX_VROWS - FOX_DIM, ts), 0) == 0, 1.0, 0.0)
        for hh in range(2):
            vt_ref[0, 2 * r + hh, 0] = jnp.concatenate(
                [vt[hh * FOX_DIM:(hh + 1) * FOX_DIM], extra], axis=0).astype(BF16)


def _inproj_fox(x2, g, w_in, fb, qg, kg, batch, seq, ts):
    m, d = x2.shape
    w = FOX_HEADS * FOX_DIM
    n_hgrn = 4 * HGRN_HEADS * HGRN_DIM
    nt = seq // ts
    pairs = FOX_HEADS // 2
    const = lambda a: pl.BlockSpec(a.shape, lambda b, t: (0,) * a.ndim)
    rows = lambda width: pl.BlockSpec((ts, width), lambda b, t: (b * nt + t, 0))

    wa = FOX_HEADS * LANES
    ch = np.arange(2 * LANES)
    hsum = (ch[:, None] // FOX_DIM == ch[None, :] // FOX_DIM).astype(np.float32)
    wq = np.zeros((pairs, 2 * LANES, 2 * LANES), np.float32)
    wk = np.zeros((pairs, 2 * LANES, 2 * LANES), np.float32)
    oq = np.zeros((1, wa), np.float32)
    ok = np.zeros((1, wa), np.float32)
    for r in range(pairs):
        for hh in range(2):
            h = 2 * r + hh
            ch = np.arange(FOX_DIM)
            wq[r, hh * FOX_DIM + ch, hh * LANES + ch] = 1.0
            wk[r, hh * FOX_DIM + ch, hh * LANES + ch] = 1.0
            for p in range(3):
                wq[r, LANES + p * FOX_HEADS + h, hh * LANES + FOX_DIM + p] = 1.0
                wk[r, LANES + p * FOX_HEADS + h, hh * LANES + FOX_DIM + 3 + p] = -1.0
                oq[0, h * LANES + FOX_DIM + 3 + p] = 1.0
                ok[0, h * LANES + FOX_DIM + p] = 1.0
    consts = [jnp.asarray(a, BF16) for a in (hsum, wq, wk)] + [jnp.asarray(oq), jnp.asarray(ok)]
    qg = jnp.tile(qg, (1, FOX_HEADS))
    kg = jnp.tile(kg, (1, FOX_HEADS))
    return pl.pallas_call(
        _inproj_fox_kernel,
        grid=(batch, nt),
        in_specs=[rows(d), const(g),
                  pl.BlockSpec(w_in.shape, lambda b, t: (0, 0), pipeline_mode=pl.Buffered(1)),
                  const(fb), const(qg), const(kg)] + [const(a) for a in consts],
        out_specs=[rows(n_hgrn), rows(w),
                   pl.BlockSpec((1, FOX_HEADS, LANES, ts), lambda b, t: (b, 0, 0, t)),
                   rows(FOX_HEADS * LANES),
                   pl.BlockSpec((1, FOX_HEADS, 1, FOX_VROWS, ts), lambda b, t: (b, 0, t, 0, 0))],
        out_shape=[jax.ShapeDtypeStruct((m, n_hgrn), F32),
                   jax.ShapeDtypeStruct((m, w), F32),
                   jax.ShapeDtypeStruct((batch, FOX_HEADS, LANES, seq), BF16),
                   jax.ShapeDtypeStruct((m, FOX_HEADS * LANES), BF16),
                   jax.ShapeDtypeStruct((batch, FOX_HEADS, nt, FOX_VROWS, ts), BF16)],
        scratch_shapes=[pltpu.VMEM((1, LANES), F32)],
        compiler_params=_cparams(("parallel", "arbitrary"), VMEM_MEDIUM_MIB),
        name="inproj_fox",
    )(x2, g, w_in, fb, qg, kg, *consts)


def _fox_kernel(qt_ref, k_ref, vt_ref, g_ref, o_ref, *, tq, tk, heads):
    i = pl.program_id(2)
    hs = range(heads)
    ratio = tq // tk

    def step(j, carry, q0):
        ms, accs = carry
        diagonal = q0 is not None
        q0 = q0 or 0
        nq = tq - q0
        nkeys = tk if diagonal else tq
        rows = pl.ds(pl.multiple_of(j * tq, tq) + q0, nkeys)
        keys = slice(q0, q0 + nkeys)
        if diagonal:
            visible = (lax.broadcasted_iota(jnp.int32, (tk, nq), 1)
                       >= lax.broadcasted_iota(jnp.int32, (tk, nq), 0))

        def scores(h):
            s = _dot(k_ref[rows, h * LANES:(h + 1) * LANES], qt_ref[0, h, :, q0:])
            return jnp.where(visible, s, NEG_BIG) if diagonal else s

        def softmax(h, s):
            m_old = ms[h][:, q0:]
            m_new = jnp.maximum(m_old, jnp.max(s, axis=0, keepdims=True))
            return m_new, jnp.exp2(m_old - m_new), jnp.exp2(s - m_new).astype(BF16)

        def values(h, alpha, p):
            return accs[h][:, q0:] * alpha + _dot(vt_ref[0, h, j, :, keys], p)

        s, sm, out = {}, {}, {}
        for t in range(heads + 2):
            if t < heads:
                s[t] = scores(t)
            if 0 <= t - 1 < heads:
                sm[t - 1] = softmax(t - 1, s[t - 1])
            if 0 <= t - 2 < heads:
                out[t - 2] = values(t - 2, sm[t - 2][1], sm[t - 2][2])
        keep = lambda old, new: new if q0 == 0 else jnp.concatenate([old[:, :q0], new], axis=1)
        return (tuple(keep(ms[h], sm[h][0]) for h in hs), tuple(keep(accs[h], out[h]) for h in hs))

    neg = jnp.full((1, tq), NEG_BIG, F32)
    carry = ((neg,) * heads, (jnp.zeros((FOX_VROWS, tq), F32),) * heads)
    carry = lax.fori_loop(0, i, lambda j, c: step(j, c, None), carry)
    for d in range(ratio):
        carry = step(i, carry, d * tk)
    _, accs = carry
    norm = [a[:FOX_DIM] / a[FOX_DIM:FOX_DIM + 1] for a in accs]
    for r in range(heads // 2):
        cols = slice(r * LANES, (r + 1) * LANES)
        out = jnp.concatenate([norm[2 * r], norm[2 * r + 1]], axis=0)
        o_ref[:, cols] = (out.T * _sigmoid(g_ref[:, cols])).astype(o_ref.dtype)


def _fox(qt, ka, vt, gate, batch, seq, tq, tk, heads):
    m = ka.shape[0]
    nq = seq // tq
    groups = FOX_HEADS // heads
    wv = heads * FOX_DIM
    return pl.pallas_call(
        functools.partial(_fox_kernel, tq=tq, tk=tk, heads=heads),
        grid=(batch, groups, nq),
        in_specs=[pl.BlockSpec((1, heads, LANES, tq), lambda b, p, i: (b, p, 0, i)),
                  pl.BlockSpec((seq, heads * LANES), lambda b, p, i: (b, p)),
                  pl.BlockSpec((1, heads, nq, FOX_VROWS, tq), lambda b, p, i: (b, p, 0, 0, 0)),
                  pl.BlockSpec((tq, wv), lambda b, p, i: (b * nq + i, p))],
        out_specs=pl.BlockSpec((tq, wv), lambda b, p, i: (b * nq + i, p)),
        out_shape=jax.ShapeDtypeStruct((m, FOX_HEADS * FOX_DIM), BF16),
        compiler_params=_cparams(("parallel", "parallel", "arbitrary"), VMEM_MEDIUM_MIB),
        name="fox_attention",
    )(qt, ka, vt, gate)


def _mix_mlp_kernel(*refs, n_mix, ck):
    ys = refs[:n_mix]
    w_ref, h_ref, g_ref, wu_ref, wd_ref, o_ref = refs[n_mix:]
    y = ys[0][...] if n_mix == 1 else jnp.concatenate([r[...] for r in ys], axis=1)
    x = h_ref[...] + _dot(y, w_ref[...])
    hn = _rms(x, g_ref[...]).astype(BF16)
    acc = x
    for c in range(wu_ref.shape[1] // ck):
        u = jnp.maximum(_dot(hn, wu_ref[:, c * ck:(c + 1) * ck]), 0.0)
        acc = acc + _dot((u * u).astype(BF16), wd_ref[c * ck:(c + 1) * ck, :])
    o_ref[...] = acc


def _mix_mlp(ys, w, h, g, wu, wd, tm, ck, name):
    m, d = h.shape
    const = lambda a: pl.BlockSpec(a.shape, lambda i: (0, 0), pipeline_mode=pl.Buffered(1))
    return pl.pallas_call(
        functools.partial(_mix_mlp_kernel, n_mix=len(ys), ck=ck),
        grid=(m // tm,),
        in_specs=([pl.BlockSpec((tm, y.shape[1]), lambda i: (i, 0)) for y in ys]
                  + [const(w), pl.BlockSpec((tm, d), lambda i: (i, 0)), const(g), const(wu), const(wd)]),
        out_specs=pl.BlockSpec((tm, d), lambda i: (i, 0)),
        out_shape=jax.ShapeDtypeStruct((m, d), F32),
        compiler_params=_cparams(("parallel",), VMEM_LARGE_MIB),
        name=name,
    )(*ys, w, h, g, wu, wd)


def _rwkvproj_kernel(h_ref, hp_ref, g_ref, mu_ref, wr_ref, wk_ref, wv_ref, w1_ref, w2_ref,
                     a1_ref, a2_ref, g1_ref, g2_ref, w0_ref, a0_ref, kk_ref, ka_ref,
                     r_ref, lw_ref, km_ref, v_ref, kr_ref, a_ref, go_ref, *, tiles_per_seq):
    i = pl.program_id(0)
    tm = h_ref.shape[0]
    gn = g_ref[...]
    hn = _rms(h_ref[...], gn)
    prev = _rms(hp_ref[7:8, :], gn)
    prev = jnp.where(i % tiles_per_seq == 0, jnp.zeros_like(prev), prev)
    row = lax.broadcasted_iota(jnp.int32, hn.shape, 0)
    shifted = jnp.where(row == 0, jnp.broadcast_to(prev, hn.shape), pltpu.roll(hn, 1, 0))
    xx = shifted - hn
    hn_b = hn.astype(BF16)
    xx_b = xx.astype(BF16)
    mix = lambda j: hn_b + xx_b * mu_ref[j:j + 1, :].astype(BF16)
    r = _dot(mix(0), wr_ref[...])
    k = _dot(mix(2), wk_ref[...])
    v = _dot(mix(3), wv_ref[...])
    z = w0_ref[...] + _dot(jnp.tanh(_dot(mix(1), w1_ref[...])).astype(BF16), w2_ref[...])
    a = _sigmoid(a0_ref[...] + _dot(_dot(mix(4), a1_ref[...]).astype(BF16), a2_ref[...]))
    g = _dot(_sigmoid(_dot(mix(5), g1_ref[...])).astype(BF16), g2_ref[...])
    r_ref[...] = r
    lw_ref[...] = _sigmoid(z) * (-math.exp(-0.5))
    km_ref[...] = k * (1.0 + (a - 1.0) * ka_ref[...])
    v_ref[...] = v
    kr_ref[...] = k * kk_ref[...]
    a_ref[...] = a
    go_ref[...] = g


def _rwkvproj(h, g, mu, wr, wk, wv, w1, w2, a1, a2, g1, g2, w0, a0, k_k, k_a, seq, tm):
    m, d = h.shape
    tiles_per_seq = seq // tm
    full = lambda a: pl.BlockSpec(a.shape, lambda i: (0,) * a.ndim, pipeline_mode=pl.Buffered(1))
    row = pl.BlockSpec((tm, d), lambda i: (i, 0))
    prev = pl.BlockSpec((8, d), lambda i: (jnp.maximum(i * (tm // 8) - 1, 0), 0))
    consts = (g, mu, wr, wk, wv, w1, w2, a1, a2, g1, g2, w0, a0, k_k, k_a)
    return pl.pallas_call(
        functools.partial(_rwkvproj_kernel, tiles_per_seq=tiles_per_seq),
        grid=(m // tm,),
        in_specs=[row, prev] + [full(a) for a in consts],
        out_specs=[row] * 7,
        out_shape=[jax.ShapeDtypeStruct((m, d), F32)] * 7,
        compiler_params=_cparams(("parallel",), VMEM_LARGE_MIB),
        name="rwkv_proj",
    )(h, h, *consts)


def _wkv_kernel(r_ref, lw_ref, km_ref, v_ref, kr_ref, a_ref, g_ref, rk_ref, lg_ref, lb_ref,
                o_ref, st_ref, y_ref, q1_ref, lhs_ref, z0_ref, wc_ref, bonus_ref, gate_ref,
                *, chunk, groups):
    grp = pl.program_id(2)

    @pl.when(pl.program_id(1) == 0)
    def _():
        st_ref[grp] = jnp.zeros(st_ref.shape[1:], F32)

    ts = r_ref.shape[0]
    nch = ts // chunk
    gl = GROUP_LANES
    rb = lax.broadcasted_iota(jnp.int32, (gl, gl), 0) // RWKV_DIM
    cb = lax.broadcasted_iota(jnp.int32, (gl, gl), 1) // RWKV_DIM
    blockmask = rb == cb
    ones_bd = jnp.where(blockmask, 1.0, 0.0).astype(BF16)

    def headsum(x, pieces=1):
        hi = x.astype(BF16)
        out = _dot(hi, ones_bd)
        if pieces == 2:
            out = out + _dot((x - hi.astype(F32)).astype(BF16), ones_bd)
        return out

    def bd(y):
        reps = gl // y.shape[0]
        return jnp.where(blockmask, jnp.concatenate([y] * reps, axis=0), 0.0).astype(BF16)

    def hmm(x, y):
        return _dot(x.astype(BF16), bd(y))

    def tn_blocks(x, y):
        return jnp.where(blockmask, _dot(x.T.astype(BF16), y.astype(BF16)), 0.0)

    t_idx = lax.broadcasted_iota(jnp.int32, (chunk, gl), 0)
    s_idx = lax.broadcasted_iota(jnp.int32, (chunk, gl), 1) % RWKV_DIM
    strict = s_idx < t_idx
    incl = s_idx <= t_idx
    tril_b = jnp.where(_tril_mask(chunk), 1.0, 0.0).astype(BF16)
    zeros_c = jnp.zeros((chunk, gl), F32)

    chunks = range(nch)
    blk = lambda ref, c: ref[c * chunk:(c + 1) * chunk, :]
    pad = lambda x: jnp.concatenate([x, zeros_c], axis=0)
    each = lambda fn, *lists: [fn(*args) for args in zip(*lists)]

    kr = kr_ref[...]
    kkn_all = kr * lax.rsqrt(jnp.maximum(headsum(kr * kr, pieces=2), 1e-24))

    lw = [blk(lw_ref, c) for c in chunks]
    cum = each(lambda x: _cumsum_rows(x, tril_b, pieces=2), lw)
    c_last = [x[chunk - 1:chunk, :] for x in cum]
    kkn = [kkn_all[c * chunk:(c + 1) * chunk, :] for c in chunks]
    kka = [kkn[c] * blk(a_ref, c) for c in chunks]
    km = [blk(km_ref, c) for c in chunks]
    v = [blk(v_ref, c) for c in chunks]
    e_neg = [jnp.exp(-x) for x in cum]
    e_end = each(lambda cl, x: jnp.exp(cl - x), c_last, cum)
    at = each(lambda k, x, l: -k * jnp.exp(x - l), kkn, cum, lw)
    bt = each(jnp.multiply, kka, e_neg)
    kt = each(jnp.multiply, km, e_neg)
    rt = [blk(r_ref, c) * jnp.exp(cum[c]) for c in chunks]
    bw = each(jnp.multiply, kka, e_end)
    kw = each(jnp.multiply, km, e_end)
    first_head = lax.broadcasted_iota(jnp.int32, (RWKV_DIM, LANES), 1) < RWKV_DIM
    for c in chunks:
        wt = jnp.broadcast_to(jnp.exp(c_last[c]), (LANES, gl)).T
        wc_ref[grp, c] = jnp.concatenate(
            [jnp.where(first_head, wt[2 * p * RWKV_DIM:(2 * p + 1) * RWKV_DIM],
                       wt[(2 * p + 1) * RWKV_DIM:(2 * p + 2) * RWKV_DIM])
             for p in range(RWKV_GROUP // 2)], axis=1)

    lhs = each(lambda a, r: jnp.concatenate([a, r], axis=0).astype(BF16), at, rt)
    pb = each(lambda l, b: _dot_nt(l, bd(b)), lhs, bt)
    pk = each(lambda l, k: _dot_nt(l, bd(k)), lhs, kt)
    a_ab = [jnp.where(strict, x[:chunk], 0.0) for x in pb]
    a_rb = [jnp.where(incl, x[chunk:], 0.0) for x in pb]
    a_ak = [jnp.where(strict, x[:chunk], 0.0) for x in pk]
    a_rk = [jnp.where(incl, x[chunk:], 0.0) for x in pk]

    e = [jnp.where((t_idx % 2 == 1) & (s_idx == t_idx - 1), x, 0.0) for x in a_ab]
    size = 2
    while size < chunk:
        off = ((t_idx // size) % 2 == 1) & (s_idx // size == t_idx // size - 1)
        a_off = [jnp.where(off, x, 0.0) for x in a_ab]
        t1 = each(lambda ao, ee: ao + hmm(ao, ee), a_off, e)
        e = each(lambda ee, tt: ee + tt + hmm(ee, tt), e, t1)
        size *= 2

    av = each(lambda ak, rk, vv: hmm(jnp.concatenate([ak, rk], axis=0), vv), a_ak, a_rk, v)
    akv = [x[:chunk] for x in av]
    p1 = each(lambda x, ee: x + hmm(ee, x), akv, e)
    mat = each(lambda x, ee: x + hmm(ee, x), at, e)
    q1 = each(lambda x, arb, pp: x[chunk:] + hmm(arb, pp), av, a_rb, p1)
    r2 = each(lambda r, arb, mm: r + hmm(arb, mm), rt, a_rb, mat)
    pct = each(lambda b, mm: tn_blocks(pad(b), pad(mm)), bw, mat)
    z0 = each(lambda b, k, pp, vv: tn_blocks(jnp.concatenate([b, k], axis=0),
                                             jnp.concatenate([pp, vv], axis=0)), bw, kw, p1, v)
    def fold(x):
        out = x[:RWKV_DIM]
        for h in range(1, RWKV_GROUP):
            out = out + x[h * RWKV_DIM:(h + 1) * RWKV_DIM]
        return out

    for c in chunks:
        q1_ref[grp, c * chunk:(c + 1) * chunk, :] = q1[c]
        lhs_ref[grp, c, :chunk, :] = r2[c].astype(BF16)
        lhs_ref[grp, c, chunk:, :] = fold(pct[c]).astype(BF16)
        z0_ref[grp, c] = fold(z0[c])
    bonus_ref[grp] = headsum(r_ref[...] * km_ref[...] * rk_ref[...]) * v_ref[...]
    gate_ref[grp] = g_ref[...]

    @pl.when(grp == groups - 1)
    def _():
        gs = range(groups)

        def body(c, carry):
            rows = pl.ds(pl.multiple_of(c * chunk, chunk), chunk)
            st = [st_ref[j] for j in gs]
            res = [_dot(lhs_ref[j, c], bd(st[j])) for j in gs]
            for j in gs:
                y_ref[j, rows, :] = q1_ref[j, rows, :] + res[j][:chunk]
                st_ref[j] = st[j] * wc_ref[j, c] + res[j][chunk:] + z0_ref[j, c]
            return carry

        lax.fori_loop(0, nch, body, 0)

        inv_n = 1.0 / RWKV_DIM
        for j in gs:
            cols = slice(j * gl, (j + 1) * gl)
            y = y_ref[j]
            mean = headsum(y) * inv_n
            dlt = y - mean
            var = headsum(dlt * dlt) * inv_n
            yn = dlt * lax.rsqrt(var + GN_EPS) * lg_ref[:, cols] + lb_ref[:, cols]
            o_ref[:, cols] = ((yn + bonus_ref[j]) * gate_ref[j]).astype(o_ref.dtype)


def _wkv(r, lw, km, v, kr, a, g, r_k, lnx_g, lnx_b, batch, seq, ts, chunk):
    m, d = r.shape
    gl = GROUP_LANES
    groups = d // gl
    nt = seq // ts
    nch = ts // chunk
    row = pl.BlockSpec((ts, gl), lambda b, t, j: (b * nt + t, j))
    vec = pl.BlockSpec((1, gl), lambda b, t, j: (0, j))
    full = pl.BlockSpec((1, d), lambda b, t, j: (0, 0))
    return pl.pallas_call(
        functools.partial(_wkv_kernel, chunk=chunk, groups=groups),
        grid=(batch, nt, groups),
        in_specs=[row] * 7 + [vec, full, full],
        out_specs=pl.BlockSpec((ts, d), lambda b, t, j: (b * nt + t, 0)),
        out_shape=jax.ShapeDtypeStruct((m, d), BF16),
        scratch_shapes=[pltpu.VMEM((groups, RWKV_DIM, gl), F32),
                        pltpu.VMEM((groups, ts, gl), F32),
                        pltpu.VMEM((groups, ts, gl), F32),
                        pltpu.VMEM((groups, nch, chunk + RWKV_DIM, gl), BF16),
                        pltpu.VMEM((groups, nch, RWKV_DIM, gl), F32),
                        pltpu.VMEM((groups, nch, RWKV_DIM, gl), F32),
                        pltpu.VMEM((groups, ts, gl), F32),
                        pltpu.VMEM((groups, ts, gl), F32)],
        compiler_params=_cparams(("parallel", "arbitrary", "arbitrary"), VMEM_LARGE_MIB),
        name="wkv7",
    )(r, lw, km, v, kr, a, g, r_k, lnx_g, lnx_b)


def kernel(x, norm_mix_g, norm_ffn_g, ab_w_in, hgrn_lower_bounds, hgrn_norm_g, fox_forget_bias,
           fox_q_norm_g, fox_k_norm_g, ab_w_out, rwkv_mu, rwkv_w_rkv, rwkv_w0, rwkv_w1, rwkv_w2,
           rwkv_a0, rwkv_a1, rwkv_a2, rwkv_g1, rwkv_g2, rwkv_k_k, rwkv_k_a, rwkv_r_k,
           rwkv_lnx_g, rwkv_lnx_b, rwkv_w_o, mlp_w_up, mlp_w_down):
    batch, seq, d = x.shape
    m = batch * seq
    t = _tiles(seq)
    assert d == 2 * HGRN_HEADS * HGRN_DIM == 2 * FOX_HEADS * FOX_DIM and d % GROUP_LANES == 0
    assert ab_w_in.shape == (1, d, 4 * (HGRN_HEADS * HGRN_DIM + FOX_HEADS * FOX_DIM) + FOX_HEADS)
    assert seq % t.time == 0 and t.time % t.chunk == 0 and t.time % t.fox_keys == 0
    assert seq % t.rows == 0 and seq % t.mlp_rows == 0 and mlp_w_up.shape[-1] % t.ff_chunk == 0
    row = lambda a: a.reshape(1, -1).astype(F32)
    bf = lambda a: a.astype(BF16)

    lb_all = jnp.cumsum(jax.nn.softmax(hgrn_lower_bounds.astype(F32), axis=0), axis=0)
    h = x.reshape(m, d)

    n_wide = ab_w_in.shape[-1] - FOX_HEADS
    gate_w = jnp.tile(ab_w_in[0][:, n_wide:], (1, 3))
    w_in = bf(jnp.pad(jnp.concatenate([ab_w_in[0][:, :n_wide], gate_w], axis=1),
                      ((0, 0), (0, LANES - 3 * FOX_HEADS))))
    fb = jnp.pad(jnp.tile(row(fox_forget_bias[0]), (1, 3)), ((0, 0), (0, LANES - 3 * FOX_HEADS)))
    ha, gate, qt, ka, vt = _inproj_fox(h, row(norm_mix_g[0]), w_in, fb, row(fox_q_norm_g[0]),
                                       row(fox_k_norm_g[0]), batch, seq, t.time)
    ya = _hgrn(ha, row(lb_all[0]), row(hgrn_norm_g[0]), batch, seq, t.time, t.chunk)
    yb = _fox(qt, ka, vt, gate, batch, seq, t.time, t.fox_keys, FOX_HEADS)
    h = _mix_mlp([ya, yb], bf(ab_w_out[0]), h, row(norm_ffn_g[0]), bf(mlp_w_up[0]), bf(mlp_w_down[0]),
                 t.mlp_rows, t.ff_chunk, "mix_mlp0")

    outs = _rwkvproj(h, row(norm_mix_g[1]), rwkv_mu[0].astype(F32),
                     bf(rwkv_w_rkv[0, 0]), bf(rwkv_w_rkv[0, 1]), bf(rwkv_w_rkv[0, 2]),
                     bf(rwkv_w1[0]), bf(rwkv_w2[0]), bf(rwkv_a1[0]), bf(rwkv_a2[0]),
                     bf(rwkv_g1[0]), bf(rwkv_g2[0]), row(rwkv_w0[0]), row(rwkv_a0[0]),
                     row(rwkv_k_k[0]), row(rwkv_k_a[0]), seq, t.rows)
    z = _wkv(*outs, row(rwkv_r_k[0]), row(rwkv_lnx_g[0]), row(rwkv_lnx_b[0]),
             batch, seq, t.time, t.chunk)
    h = _mix_mlp([z], bf(rwkv_w_o[0]), h, row(norm_ffn_g[1]), bf(mlp_w_up[1]), bf(mlp_w_down[1]),
                 t.mlp_rows, t.ff_chunk, "mix_mlp1")
    return h.reshape(batch, seq, d)
```
